```python
import jax, jax.numpy as jnp
from jax import lax
import numpy as np

D_MODEL = 1024
BATCH = 8
SEQ = 2048
DEPTH = 1
DEC_BATCH = 128
DEC_SEQ = 1
PAST_LEN = 16384
PAGE_SIZE = 128

MIX_WIDTH = 2 * D_MODEL
CONV_CH = MIX_WIDTH // 2
CONV_GROUPS = 16
SHORT_CONV_W = 3
SSM_CH = MIX_WIDTH - CONV_CH
SSM_HEAD_DIM = 64
SSM_HEADS = SSM_CH // SSM_HEAD_DIM
SSM_GROUPS = 2
SSM_STATE = 128
SSM_CONV_W = 4
SSM_CHUNK = 128
XBC_CH = SSM_CH + 2 * SSM_GROUPS * SSM_STATE
IN_COLS = 3 * CONV_CH + SSM_CH + XBC_CH + SSM_HEADS
D_FF = 4 * D_MODEL
ALPHA = (2 * DEPTH) ** 0.25
BETA = (8 * DEPTH) ** -0.25
LN_EPS = 1e-5
RMS_EPS = 1e-5

kernel_name = 'hymba_style_shortconv_mamba2_deepnorm_adaln_step'


def layer_norm(x, g, b):
    xf = x.astype(jnp.float32)
    mu = jnp.mean(xf, axis=-1, keepdims=True)
    var = jnp.mean(jnp.square(xf - mu), axis=-1, keepdims=True)
    y = (xf - mu) * lax.rsqrt(var + LN_EPS) * g.astype(jnp.float32) + b.astype(jnp.float32)
    return y.astype(x.dtype)


def group_rms_norm(x, w, n_groups):
    shape = x.shape
    xf = x.astype(jnp.float32).reshape(*shape[:-1], n_groups, shape[-1] // n_groups)
    xf = xf * lax.rsqrt(jnp.mean(jnp.square(xf), axis=-1, keepdims=True) + RMS_EPS)
    return (xf.reshape(shape) * w.astype(jnp.float32)).astype(x.dtype)


def causal_dwconv(inp, buf, w):
    k_w = w.shape[0]
    l_ = inp.shape[1]
    full = jnp.concatenate([buf.astype(inp.dtype), inp], axis=1)
    out = sum(full[:, k:k + l_] * w[k] for k in range(k_w))
    return out, full[:, l_:]


def ssd_chunked(xh, dt, a, bm, cm, s0):
    b_, l_, h_, p_ = xh.shape
    g_, n_ = bm.shape[2], bm.shape[3]
    e_ = h_ // g_
    cl = min(SSM_CHUNK, l_)
    nc = -(-l_ // cl)
    pad = nc * cl - l_
    padf = lambda t: jnp.pad(t, [(0, 0), (0, pad)] + [(0, 0)] * (t.ndim - 2))
    xdt = padf(xh.astype(jnp.float32) * dt[..., None]).reshape(b_, nc, cl, g_, e_, p_)
    dta = padf(dt * a).reshape(b_, nc, cl, g_, e_).transpose(0, 1, 3, 4, 2)
    bm = padf(bm.astype(jnp.float32)).reshape(b_, nc, cl, g_, n_)
    cm = padf(cm.astype(jnp.float32)).reshape(b_, nc, cl, g_, n_)
    acs = jnp.cumsum(dta, axis=-1)
    causal = jnp.tril(jnp.ones((cl, cl), dtype=bool))
    seg = acs[..., :, None] - acs[..., None, :]
    lmat = jnp.exp(jnp.where(causal, seg, -jnp.inf))
    cb = jnp.einsum('bcsgn,bctgn->bcgst', cm, bm)
    y_diag = jnp.einsum('bcgest,bctgep->bcsgep', cb[:, :, :, None] * lmat, xdt)
    decay_out = jnp.exp(acs[..., -1:] - acs)
    chunk_states = jnp.einsum('bcsgn,bcsgep->bcgepn', bm, xdt * decay_out.transpose(0, 1, 4, 2, 3)[..., None])
    chunk_decay = jnp.exp(acs[..., -1])

    def step(carry, inp):
        st, dec = inp
        return carry * dec[..., None, None] + st, carry

    s0g = s0.astype(jnp.float32).reshape(b_, g_, e_, p_, n_)
    s_final, s_prev = lax.scan(step, s0g, (jnp.moveaxis(chunk_states, 1, 0), jnp.moveaxis(chunk_decay, 1, 0)))
    s_prev = jnp.moveaxis(s_prev, 0, 1)
    c_dec = cm[:, :, :, :, None, :] * jnp.exp(acs).transpose(0, 1, 4, 2, 3)[..., None]
    y_off = jnp.einsum('bcsgen,bcgepn->bcsgep', c_dec, s_prev)
    y = (y_diag + y_off).reshape(b_, nc * cl, h_, p_)[:, :l_]
    return y, s_final.reshape(b_, h_, p_, n_)


def mixer(u, conv_buf, ssm_conv_buf, ssm_state, w_in, conv_w, conv_norm_w, ssm_conv_w, ssm_conv_b,
          dt_bias, a_log, d_skip, ssm_norm_w, w_out):
    b_, l_ = u.shape[0], u.shape[1]
    proj = jnp.einsum('bld,dk->blk', u, w_in)
    cuts = [CONV_CH, 2 * CONV_CH, 3 * CONV_CH, 3 * CONV_CH + SSM_CH, 3 * CONV_CH + SSM_CH + XBC_CH]
    gb, gc, hv, z, xbc, dt_raw = jnp.split(proj, cuts, axis=-1)
    cv, new_conv_buf = causal_dwconv(gc * hv, conv_buf, conv_w)
    y_conv = group_rms_norm(gb * cv, conv_norm_w, CONV_GROUPS)
    xbc_c, new_ssm_conv_buf = causal_dwconv(xbc, ssm_conv_buf, ssm_conv_w)
    xbc_c = jax.nn.silu(xbc_c + ssm_conv_b)
    xs, bm, cm = jnp.split(xbc_c, [SSM_CH, SSM_CH + SSM_GROUPS * SSM_STATE], axis=-1)
    xh = xs.reshape(b_, l_, SSM_HEADS, SSM_HEAD_DIM)
    bm = bm.reshape(b_, l_, SSM_GROUPS, SSM_STATE)
    cm = cm.reshape(b_, l_, SSM_GROUPS, SSM_STATE)
    dt = jax.nn.softplus(dt_raw.astype(jnp.float32) + dt_bias.astype(jnp.float32))
    a = -jnp.exp(a_log.astype(jnp.float32))
    y, new_state = ssd_chunked(xh, dt, a, bm, cm, ssm_state)
    y = y + xh.astype(jnp.float32) * d_skip.astype(jnp.float32)[:, None]
    y = y.reshape(b_, l_, SSM_CH) * jax.nn.silu(z.astype(jnp.float32))
    y_ssm = group_rms_norm(y, ssm_norm_w, SSM_GROUPS).astype(u.dtype)
    out = jnp.einsum('blk,kd->bld', jnp.concatenate([y_conv, y_ssm], axis=-1), w_out)
    return out, new_conv_buf, new_ssm_conv_buf, new_state.astype(ssm_state.dtype)


def decoder_layer(x, c, conv_buf, ssm_conv_buf, ssm_state, w_ada, b_ada, w_in, conv_w, conv_norm_w,
                  ssm_conv_w, ssm_conv_b, dt_bias, a_log, d_skip, ssm_norm_w, w_out,
                  ln1_g, ln1_b, w_up, w_down, ln2_g, ln2_b):
    mod = (c @ w_ada + b_ada)[:, None, :]
    sh1, sc1, g1, sh2, sc2, g2 = jnp.split(mod, 6, axis=-1)
    u = x * (1 + sc1) + sh1
    m, new_conv, new_ssm_conv, new_ssm = mixer(u, conv_buf, ssm_conv_buf, ssm_state, w_in, conv_w, conv_norm_w,
                                               ssm_conv_w, ssm_conv_b, dt_bias, a_log, d_skip, ssm_norm_w, w_out)
    x = layer_norm(ALPHA * x + (1 + g1) * m, ln1_g, ln1_b)
    v = x * (1 + sc2) + sh2
    hid = jnp.square(jax.nn.relu(jnp.einsum('bld,df->blf', v, w_up)))
    x = layer_norm(ALPHA * x + (1 + g2) * jnp.einsum('blf,fd->bld', hid, w_down), ln2_g, ln2_b)
    return x, new_conv, new_ssm_conv, new_ssm


def setup_inputs(seed: int = 0) -> dict:
    key = jax.random.key(seed)
    ks = jax.random.split(key, 32)
    f32 = jnp.float32
    nrm = lambda k, shape, s: jax.random.normal(k, shape, f32) * s
    dt0 = jnp.exp(jax.random.uniform(ks[20], (DEPTH, SSM_HEADS), f32, np.log(1e-3), np.log(1e-1)))
    return {
        'x_prompt': nrm(ks[0], (BATCH, SEQ, D_MODEL), 1.0),
        'x_sample': nrm(ks[1], (DEC_BATCH, DEC_SEQ, D_MODEL), 1.0),
        'state_conv': nrm(ks[2], (DEPTH, DEC_BATCH, SHORT_CONV_W - 1, CONV_CH), 1.0),
        'state_ssm_conv': nrm(ks[3], (DEPTH, DEC_BATCH, SSM_CONV_W - 1, XBC_CH), 1.0),
        'state_ssm': nrm(ks[4], (DEPTH, DEC_BATCH, SSM_HEADS, SSM_HEAD_DIM, SSM_STATE), 0.1),
        'c_prompt': nrm(ks[5], (BATCH, D_MODEL), 1.0),
        'c_sample': nrm(ks[6], (DEC_BATCH, D_MODEL), 1.0),
        'w_ada': nrm(ks[7], (DEPTH, D_MODEL, 6 * D_MODEL), 0.1 * D_MODEL ** -0.5),
        'b_ada': nrm(ks[8], (DEPTH, 6 * D_MODEL), 0.01),
        'w_in': nrm(ks[9], (DEPTH, D_MODEL, IN_COLS), D_MODEL ** -0.5),
        'conv_w': nrm(ks[10], (DEPTH, SHORT_CONV_W, CONV_CH), SHORT_CONV_W ** -0.5),
        'conv_norm_w': 1.0 + nrm(ks[11], (DEPTH, CONV_CH), 0.02),
        'ssm_conv_w': nrm(ks[12], (DEPTH, SSM_CONV_W, XBC_CH), SSM_CONV_W ** -0.5),
        'ssm_conv_b': nrm(ks[13], (DEPTH, XBC_CH), 0.01),
        'dt_bias': dt0 + jnp.log(-jnp.expm1(-dt0)),
        'a_log': jnp.log(jax.random.uniform(ks[14], (DEPTH, SSM_HEADS), f32, 1.0, 16.0)),
        'd_skip': 1.0 + nrm(ks[15], (DEPTH, SSM_HEADS), 0.02),
        'ssm_norm_w': 1.0 + nrm(ks[16], (DEPTH, SSM_CH), 0.02),
        'w_out': nrm(ks[17], (DEPTH, MIX_WIDTH, D_MODEL), BETA * MIX_WIDTH ** -0.5),
        'ln1_g': 1.0 + nrm(ks[18], (DEPTH, D_MODEL), 0.02),
        'ln1_b': nrm(ks[19], (DEPTH, D_MODEL), 0.01),
        'w_up': nrm(ks[21], (DEPTH, D_MODEL, D_FF), D_MODEL ** -0.5),
        'w_down': nrm(ks[22], (DEPTH, D_FF, D_MODEL), BETA * D_FF ** -0.5),
        'ln2_g': 1.0 + nrm(ks[23], (DEPTH, D_MODEL), 0.02),
        'ln2_b': nrm(ks[24], (DEPTH, D_MODEL), 0.01),
    }


def reference(x_prompt, x_sample, state_conv, state_ssm_conv, state_ssm, c_prompt, c_sample,
              w_ada, b_ada, w_in, conv_w, conv_norm_w, ssm_conv_w, ssm_conv_b, dt_bias, a_log, d_skip,
              ssm_norm_w, w_out, ln1_g, ln1_b, w_up, w_down, ln2_g, ln2_b):
    bp = x_prompt.shape[0]
    zero_conv = jnp.zeros((bp, SHORT_CONV_W - 1, CONV_CH), x_prompt.dtype)
    zero_ssm_conv = jnp.zeros((bp, SSM_CONV_W - 1, XBC_CH), x_prompt.dtype)
    zero_ssm = jnp.zeros((bp, SSM_HEADS, SSM_HEAD_DIM, SSM_STATE), state_ssm.dtype)
    hp, hs = x_prompt, x_sample
    p_conv, p_sconv, p_ssm, s_conv, s_sconv, s_ssm = [], [], [], [], [], []
    for i in range(DEPTH):
        lw = (w_ada[i], b_ada[i], w_in[i], conv_w[i], conv_norm_w[i], ssm_conv_w[i], ssm_conv_b[i],
              dt_bias[i], a_log[i], d_skip[i], ssm_norm_w[i], w_out[i], ln1_g[i], ln1_b[i],
              w_up[i], w_down[i], ln2_g[i], ln2_b[i])
        hp, pc, psc, pss = decoder_layer(hp, c_prompt, zero_conv, zero_ssm_conv, zero_ssm, *lw)
        hs, sc, ssc, sss = decoder_layer(hs, c_sample, state_conv[i], state_ssm_conv[i], state_ssm[i], *lw)
        p_conv.append(pc); p_sconv.append(psc); p_ssm.append(pss)
        s_conv.append(sc); s_sconv.append(ssc); s_ssm.append(sss)
    return (hp, hs, jnp.stack(p_conv), jnp.stack(p_sconv), jnp.stack(p_ssm),
            jnp.stack(s_conv), jnp.stack(s_sconv), jnp.stack(s_ssm))
```

```python
import functools

import jax
import jax.numpy as jnp
from jax import lax
from jax.experimental import pallas as pl
from jax.experimental.pallas import tpu as pltpu

F32 = jnp.float32
BF16 = jnp.bfloat16

D_MODEL = 1024
CONV_CH = 1024
CONV_GROUP = 64
SSM_CH = 1024
SSM_HEADS = 16
SSM_HEAD_DIM = 64
SSM_GROUPS = 2
SSM_GROUP_CH = SSM_CH // SSM_GROUPS
SSM_STATE = 128
SSM_CHUNK = 128
XBC_CH = SSM_CH + 2 * SSM_GROUPS * SSM_STATE
D_FF = 4 * D_MODEL
LANES = 128
SUBLANES = 8
COL_GB, COL_GC, COL_HV, COL_Z, COL_XBC = 0, 1024, 2048, 3072, 4096
COL_DT = COL_XBC + XBC_CH
IN_COLS = COL_DT + SSM_HEADS
IN_PAD = COL_DT + LANES
ALPHA = 2.0 ** 0.25
LN_EPS = 1e-5
RMS_EPS = 1e-5
VMEM_LIMIT = 56 * 1024 * 1024


def _dot(a, b):
    return jnp.dot(a, b, preferred_element_type=F32)


def _split3(a):
    hi = a.astype(BF16)
    r1 = a - hi.astype(F32)
    mid = r1.astype(BF16)
    lo = (r1 - mid.astype(F32)).astype(BF16)
    return hi, mid, lo


def _dot_f32_lhs(a, b_exact):
    hi, mid, lo = _split3(a)
    return _dot(hi, b_exact) + _dot(mid, b_exact) + _dot(lo, b_exact)


def _dot_f32_rhs(a_exact, b):
    hi, mid, lo = _split3(b)
    return _dot(a_exact, hi) + _dot(a_exact, mid) + _dot(a_exact, lo)


def _head_expand():
    h = lax.broadcasted_iota(jnp.int32, (LANES, SSM_CH), 0)
    c = lax.broadcasted_iota(jnp.int32, (LANES, SSM_CH), 1)
    return (c // SSM_HEAD_DIM == h).astype(BF16)


def _group_reduce():
    c = lax.broadcasted_iota(jnp.int32, (CONV_CH, LANES), 0)
    k = lax.broadcasted_iota(jnp.int32, (CONV_CH, LANES), 1)
    return (c // CONV_GROUP == k).astype(BF16)


def _sigmoid(x):
    return 1.0 / (1.0 + jnp.exp(-x))


def _silu(x):
    return x * _sigmoid(x)


def _softplus(x):
    return jnp.maximum(x, 0.0) + jnp.log1p(jnp.exp(-jnp.abs(x)))


def _layer_norm(r, g, b):
    mu = jnp.mean(r, axis=-1, keepdims=True)
    d = r - mu
    var = jnp.mean(d * d, axis=-1, keepdims=True)
    return d * lax.rsqrt(var + LN_EPS) * g + b


def _conv_group_norm(prod, w, expand, reduce):
    ssum = _dot_f32_lhs(prod * prod, reduce)
    rstd = lax.rsqrt(ssum * (1.0 / CONV_GROUP) + RMS_EPS)
    return prod * _dot_f32_lhs(rstd, expand) * w


def _ssm_group_norm(y, w):
    outs = []
    for g in range(SSM_GROUPS):
        yg = y[:, g * SSM_GROUP_CH:(g + 1) * SSM_GROUP_CH]
        ms = jnp.mean(yg * yg, axis=-1, keepdims=True)
        outs.append((yg * lax.rsqrt(ms + RMS_EPS) * w[:, g * SSM_GROUP_CH:(g + 1) * SSM_GROUP_CH]).astype(BF16))
    return outs


def _mix_out(y_conv, y_ssm_groups, w_out_ref):
    m = _dot(y_conv.astype(BF16), w_out_ref[0:CONV_CH, :])
    for g, yg in enumerate(y_ssm_groups):
        lo = CONV_CH + g * SSM_GROUP_CH
        m = m + _dot(yg, w_out_ref[lo:lo + SSM_GROUP_CH, :])
    return m


def _ada_kernel(c_ref, w_ref, b_ref, o_ref):
    c = c_ref[...]
    w = w_ref[...]
    c_hi = c.astype(BF16)
    c_lo = (c - c_hi.astype(F32)).astype(BF16)
    w_hi = w.astype(BF16)
    w_lo = (w - w_hi.astype(F32)).astype(BF16)
    o_ref[...] = _dot(c_hi, w_hi) + _dot(c_hi, w_lo) + _dot(c_lo, w_hi) + b_ref[...]


def _ada(c_all, w_ada, b_ada, tile_n=512):
    rows = c_all.shape[0]
    n = w_ada.shape[1]
    return pl.pallas_call(
        _ada_kernel,
        grid=(n // tile_n,),
        in_specs=[pl.BlockSpec((rows, D_MODEL), lambda i: (0, 0)),
                  pl.BlockSpec((D_MODEL, tile_n), lambda i: (0, i)),
                  pl.BlockSpec((1, tile_n), lambda i: (0, i))],
        out_specs=pl.BlockSpec((rows, tile_n), lambda i: (0, i)),
        out_shape=jax.ShapeDtypeStruct((rows, n), F32),
        name="ada_mod",
    )(c_all, w_ada, b_ada)


def _mixer_prompt_kernel(x_ref, mod_ref, w_in_ref, conv_w_ref, conv_nw_ref, sconv_w_ref, sconv_b_ref,
                         dtb_ref, alog_ref, dexp_ref, snw_ref, w_out_ref, ln_g_ref, ln_b_ref,
                         x1_ref, cst_ref, scst_ref, sst_ref,
                         cbuf, xbuf, st_ref, xs_ref, bc_ref, xdt_ref, dta_ref, y_ref, *, tile):
    j = pl.program_id(1)
    last = pl.num_programs(1) - 1

    @pl.when(j == 0)
    def _():
        cbuf[0:SUBLANES, :] = jnp.zeros((SUBLANES, CONV_CH), F32)
        xbuf[0:SUBLANES, :] = jnp.zeros((SUBLANES, XBC_CH), F32)
        st_ref[...] = jnp.zeros_like(st_ref)

    expand = _head_expand()
    reduce = _group_reduce()

    x = x_ref[...]
    sh1 = mod_ref[:, 0:D_MODEL]
    sc1 = mod_ref[:, D_MODEL:2 * D_MODEL]
    g1 = mod_ref[:, 2 * D_MODEL:3 * D_MODEL]
    u = (x * (1.0 + sc1) + sh1).astype(BF16)

    def proj(lo, width):
        return _dot(u, w_in_ref[:, lo:lo + width])

    ch = proj(COL_GC, CONV_CH) * proj(COL_HV, CONV_CH)
    cbuf[SUBLANES:SUBLANES + tile, :] = ch
    cw = conv_w_ref[...]
    cv = (cw[0:1, :] * cbuf[SUBLANES - 2:SUBLANES - 2 + tile, :]
          + cw[1:2, :] * cbuf[SUBLANES - 1:SUBLANES - 1 + tile, :]
          + cw[2:3, :] * ch)
    cbuf[0:SUBLANES, :] = cbuf[tile:tile + SUBLANES, :]

    @pl.when(j == last)
    def _():
        cst_ref[...] = cbuf[SUBLANES - 2:SUBLANES, :]

    y_conv = _conv_group_norm(proj(COL_GB, CONV_CH) * cv, conv_nw_ref[...], expand, reduce)

    xbc = proj(COL_XBC, XBC_CH)
    xbuf[SUBLANES:SUBLANES + tile, :] = xbc
    sw = sconv_w_ref[...]
    xc = (sw[0:1, :] * xbuf[SUBLANES - 3:SUBLANES - 3 + tile, :]
          + sw[1:2, :] * xbuf[SUBLANES - 2:SUBLANES - 2 + tile, :]
          + sw[2:3, :] * xbuf[SUBLANES - 1:SUBLANES - 1 + tile, :]
          + sw[3:4, :] * xbc + sconv_b_ref[...])
    xbuf[0:SUBLANES, :] = xbuf[tile:tile + SUBLANES, :]

    @pl.when(j == last)
    def _():
        scst_ref[...] = xbuf[SUBLANES - 3:SUBLANES, :]

    xc = _silu(xc)
    xs_ref[...] = xc[:, 0:SSM_CH]
    bc_ref[...] = xc[:, SSM_CH:XBC_CH]

    dt = _softplus(proj(COL_DT, LANES) + dtb_ref[...])
    dta_ref[...] = dt * (-jnp.exp(alog_ref[...]))
    xdt_ref[...] = xs_ref[...] * _dot_f32_lhs(dt, expand)

    row = lax.broadcasted_iota(jnp.int32, (SSM_CHUNK, SSM_CHUNK), 0)
    col = lax.broadcasted_iota(jnp.int32, (SSM_CHUNK, SSM_CHUNK), 1)
    causal = row >= col
    tri = causal.astype(BF16)
    first_half = col < SSM_HEAD_DIM

    def chunk(c, carry):
        r0 = pl.multiple_of(c * SSM_CHUNK, SSM_CHUNK)
        rows = pl.ds(r0, SSM_CHUNK)
        acs = _dot_f32_rhs(tri, dta_ref[rows, :])
        acs_t = acs.T
        acs_x = _dot_f32_lhs(acs, expand)
        end_x = acs_x[SSM_CHUNK - 1:SSM_CHUNK, :]
        xdt_c = xdt_ref[rows, :]
        xdec = (xdt_c * jnp.exp(end_x - acs_x)).astype(BF16)
        e_acs = jnp.exp(acs_x)
        e_end = jnp.exp(end_x)
        for g in range(SSM_GROUPS):
            gl = g * SSM_GROUP_CH
            bm = bc_ref[rows, g * SSM_STATE:(g + 1) * SSM_STATE]
            cm = bc_ref[rows, (SSM_GROUPS + g) * SSM_STATE:(SSM_GROUPS + g + 1) * SSM_STATE].astype(BF16)
            cb = lax.dot_general(cm, bm.astype(BF16), (((1,), (1,)), ((), ())), preferred_element_type=F32)
            st = st_ref[:, gl:gl + SSM_GROUP_CH]
            y_off = _dot(cm, st.astype(BF16)) * e_acs[:, gl:gl + SSM_GROUP_CH]
            st_ref[:, gl:gl + SSM_GROUP_CH] = (st * e_end[:, gl:gl + SSM_GROUP_CH]
                                               + _dot(bm.T.astype(BF16), xdec[:, gl:gl + SSM_GROUP_CH]))
            for q in range(SSM_GROUP_CH // LANES):
                lo = gl + q * LANES
                h0 = lo // SSM_HEAD_DIM
                slab = acs_x[:, lo:lo + LANES]
                rolled = pltpu.roll(slab, SSM_HEAD_DIM, axis=1)
                a0 = jnp.where(first_half, slab, rolled)
                a1 = jnp.where(first_half, rolled, slab)
                l0 = jnp.exp(jnp.where(causal, a0 - acs_t[h0:h0 + 1, :], -jnp.inf))
                l1 = jnp.exp(jnp.where(causal, a1 - acs_t[h0 + 1:h0 + 2, :], -jnp.inf))
                m = jnp.concatenate([(cb * l0).astype(BF16), (cb * l1).astype(BF16)], axis=1)
                xp = xdt_c[:, lo:lo + LANES]
                rhs = jnp.concatenate([jnp.where(first_half, xp, 0.0), jnp.where(first_half, 0.0, xp)],
                                      axis=0).astype(BF16)
                y_ref[rows, lo:lo + LANES] = _dot(m, rhs) + y_off[:, q * LANES:(q + 1) * LANES]
        return carry

    lax.fori_loop(0, tile // SSM_CHUNK, chunk, 0)

    @pl.when(j == last)
    def _():
        sst_ref[...] = st_ref[...].T

    y = y_ref[...] + xs_ref[...] * dexp_ref[...]
    y = y * _silu(proj(COL_Z, SSM_CH))
    m = _mix_out(y_conv, _ssm_group_norm(y, snw_ref[...]), w_out_ref)
    x1_ref[...] = _layer_norm(ALPHA * x + (1.0 + g1) * m, ln_g_ref[...], ln_b_ref[...])


def _const_spec(shape):
    return pl.BlockSpec(shape, lambda *_: (0,) * len(shape), pipeline_mode=pl.Buffered(1))


def _mixer_prompt(x, mod, w_in, conv_w, conv_nw, sconv_w, sconv_b, dtb, alog, dexp, snw, w_out, ln_g, ln_b,
                  tile=256):
    nb, seq, _ = x.shape
    kern = functools.partial(_mixer_prompt_kernel, tile=tile)
    small = [conv_w, conv_nw, sconv_w, sconv_b, dtb, alog, dexp, snw]
    return pl.pallas_call(
        kern,
        grid=(nb, seq // tile),
        in_specs=[pl.BlockSpec((None, tile, D_MODEL), lambda b, j: (b, j, 0)),
                  pl.BlockSpec((None, 1, 6 * D_MODEL), lambda b, j: (b, 0, 0)),
                  _const_spec(w_in.shape)]
                 + [_const_spec(a.shape) for a in small]
                 + [_const_spec(w_out.shape), _const_spec(ln_g.shape), _const_spec(ln_b.shape)],
        out_specs=[pl.BlockSpec((None, tile, D_MODEL), lambda b, j: (b, j, 0)),
                   pl.BlockSpec((None, 2, CONV_CH), lambda b, j: (b, 0, 0)),
                   pl.BlockSpec((None, 3, XBC_CH), lambda b, j: (b, 0, 0)),
                   pl.BlockSpec((None, SSM_CH, SSM_STATE), lambda b, j: (b, 0, 0))],
        out_shape=[jax.ShapeDtypeStruct((nb, seq, D_MODEL), F32),
                   jax.ShapeDtypeStruct((nb, 2, CONV_CH), F32),
                   jax.ShapeDtypeStruct((nb, 3, XBC_CH), F32),
                   jax.ShapeDtypeStruct((nb, SSM_CH, SSM_STATE), F32)],
        scratch_shapes=[pltpu.VMEM((tile + SUBLANES, CONV_CH), F32),
                        pltpu.VMEM((tile + SUBLANES, XBC_CH), F32),
                        pltpu.VMEM((SSM_STATE, SSM_CH), F32),
                        pltpu.VMEM((tile, SSM_CH), F32),
                        pltpu.VMEM((tile, 2 * SSM_GROUPS * SSM_STATE), F32),
                        pltpu.VMEM((tile, SSM_CH), F32),
                        pltpu.VMEM((tile, LANES), F32),
                        pltpu.VMEM((tile, SSM_CH), F32)],
        compiler_params=pltpu.CompilerParams(dimension_semantics=("arbitrary", "arbitrary"),
                                             vmem_limit_bytes=VMEM_LIMIT),
        name="mixer_prompt",
    )(x, mod, w_in, *small, w_out, ln_g, ln_b)


def _ffn_kernel(x_ref, mod_ref, w_up_ref, w_down_ref, ln_g_ref, ln_b_ref, o_ref, *, ff_tile):
    x = x_ref[...]
    sh2 = mod_ref[:, 3 * D_MODEL:4 * D_MODEL]
    sc2 = mod_ref[:, 4 * D_MODEL:5 * D_MODEL]
    g2 = mod_ref[:, 5 * D_MODEL:6 * D_MODEL]
    v = (x * (1.0 + sc2) + sh2).astype(BF16)
    acc = jnp.zeros(x.shape, F32)
    for k in range(D_FF // ff_tile):
        h = jnp.maximum(_dot(v, w_up_ref[:, k * ff_tile:(k + 1) * ff_tile]), 0.0)
        acc = acc + _dot((h * h).astype(BF16), w_down_ref[k * ff_tile:(k + 1) * ff_tile, :])
    o_ref[...] = _layer_norm(ALPHA * x + (1.0 + g2) * acc, ln_g_ref[...], ln_b_ref[...])


def _ffn(x, mod, rows_per_mod, w_up, w_down, ln_g, ln_b, tile, ff_tile=1024):
    rows = x.shape[0]
    mod_rows = mod.shape[1]
    tiles_per_mod = rows_per_mod // tile
    kern = functools.partial(_ffn_kernel, ff_tile=ff_tile)
    return pl.pallas_call(
        kern,
        grid=(rows // tile,),
        in_specs=[pl.BlockSpec((tile, D_MODEL), lambda i: (i, 0)),
                  pl.BlockSpec((None, mod_rows, 6 * D_MODEL), lambda i: (i // tiles_per_mod, 0, 0)),
                  _const_spec(w_up.shape), _const_spec(w_down.shape),
                  _const_spec(ln_g.shape), _const_spec(ln_b.shape)],
        out_specs=pl.BlockSpec((tile, D_MODEL), lambda i: (i, 0)),
        out_shape=jax.ShapeDtypeStruct((rows, D_MODEL), F32),
        compiler_params=pltpu.CompilerParams(dimension_semantics=("arbitrary",),
                                             vmem_limit_bytes=VMEM_LIMIT),
        name="ffn",
    )(x, mod, w_up, w_down, ln_g, ln_b)


def _sample_pre_kernel(x_ref, mod_ref, w_in_ref, conv_w_ref, conv_nw_ref, sconv_w_ref, sconv_b_ref,
                       dtb_ref, alog_ref, cb0_ref, cb1_ref, sb0_ref, sb1_ref, sb2_ref,
                       yconv_ref, ch_ref, xbc_ref, z_ref, xs_ref, ydiag_ref, bc_ref, xdt_t_ref, dec_t_ref):
    expand = _head_expand()
    reduce = _group_reduce()
    x = x_ref[...]
    sh1 = mod_ref[:, 0:D_MODEL]
    sc1 = mod_ref[:, D_MODEL:2 * D_MODEL]
    u = (x * (1.0 + sc1) + sh1).astype(BF16)

    def proj(lo, width):
        return _dot(u, w_in_ref[:, lo:lo + width])

    ch = proj(COL_GC, CONV_CH) * proj(COL_HV, CONV_CH)
    ch_ref[...] = ch
    cw = conv_w_ref[...]
    cv = cw[0:1, :] * cb0_ref[...] + cw[1:2, :] * cb1_ref[...] + cw[2:3, :] * ch
    yconv_ref[...] = _conv_group_norm(proj(COL_GB, CONV_CH) * cv, conv_nw_ref[...], expand, reduce)

    xbc = proj(COL_XBC, XBC_CH)
    xbc_ref[...] = xbc
    sw = sconv_w_ref[...]
    xc = _silu(sw[0:1, :] * sb0_ref[...] + sw[1:2, :] * sb1_ref[...] + sw[2:3, :] * sb2_ref[...]
               + sw[3:4, :] * xbc + sconv_b_ref[...])
    xs = xc[:, 0:SSM_CH]
    xs_ref[...] = xs
    bc_ref[...] = xc[:, SSM_CH:XBC_CH]
    z_ref[...] = proj(COL_Z, SSM_CH)

    dt = _softplus(proj(COL_DT, LANES) + dtb_ref[...])
    dta = dt * (-jnp.exp(alog_ref[...]))
    xdt = xs * _dot_f32_lhs(dt, expand)
    dec_t_ref[...] = jnp.exp(_dot_f32_lhs(dta, expand)).T
    xdt_t_ref[...] = xdt.T
    for g in range(SSM_GROUPS):
        bm = xc[:, SSM_CH + g * SSM_STATE:SSM_CH + (g + 1) * SSM_STATE]
        cm = xc[:, SSM_CH + (SSM_GROUPS + g) * SSM_STATE:SSM_CH + (SSM_GROUPS + g + 1) * SSM_STATE]
        cb = jnp.sum(cm * bm, axis=-1, keepdims=True)
        gl = g * SSM_GROUP_CH
        ydiag_ref[:, gl:gl + SSM_GROUP_CH] = cb * xdt[:, gl:gl + SSM_GROUP_CH]


def _sample_pre(x, mod, w_in, conv_w, conv_nw, sconv_w, sconv_b, dtb, alog, cb0, cb1, sb0, sb1, sb2):
    n = x.shape[0]
    args = (x, mod, w_in, conv_w, conv_nw, sconv_w, sconv_b, dtb, alog, cb0, cb1, sb0, sb1, sb2)
    out_shapes = [(n, CONV_CH), (n, CONV_CH), (n, XBC_CH), (n, SSM_CH), (n, SSM_CH), (n, SSM_CH),
                  (n, 2 * SSM_GROUPS * SSM_STATE), (SSM_CH, n), (SSM_CH, n)]
    return pl.pallas_call(
        _sample_pre_kernel,
        out_shape=[jax.ShapeDtypeStruct(s, F32) for s in out_shapes],
        compiler_params=pltpu.CompilerParams(vmem_limit_bytes=VMEM_LIMIT),
        name="sample_pre",
    )(*args)


def _sample_state_kernel(s_ref, xdt_t_ref, dec_t_ref, bc_ref, o_ref, yt_ref, *, block):
    i = pl.program_id(0)

    @pl.when(i == 0)
    def _():
        yt_ref[...] = jnp.zeros_like(yt_ref)

    lane = lax.broadcasted_iota(jnp.int32, yt_ref.shape, 1)

    def body(k, carry):
        b = i * block + k
        s = s_ref[k]
        here = lane == b
        xcol = jnp.sum(jnp.where(here, xdt_t_ref[...], 0.0), axis=1, keepdims=True)
        dcol = jnp.sum(jnp.where(here, dec_t_ref[...], 0.0), axis=1, keepdims=True)
        bc = bc_ref[pl.ds(b, 1), :]
        ycols, outer = [], []
        for g in range(SSM_GROUPS):
            gl = g * SSM_GROUP_CH
            bm = bc[:, g * SSM_STATE:(g + 1) * SSM_STATE]
            cm = bc[:, (SSM_GROUPS + g) * SSM_STATE:(SSM_GROUPS + g + 1) * SSM_STATE]
            ycols.append(jnp.sum(s[gl:gl + SSM_GROUP_CH, :] * cm, axis=1, keepdims=True))
            outer.append(xcol[gl:gl + SSM_GROUP_CH, :] * bm)
        ycol = jnp.concatenate(ycols, axis=0) * dcol
        yt_ref[...] = jnp.where(here, ycol, yt_ref[...])
        o_ref[k] = s * dcol + jnp.concatenate(outer, axis=0)
        return carry

    lax.fori_loop(0, block, body, 0)


def _sample_state(state, xdt_t, dec_t, bc, block=8):
    n = state.shape[0]
    kern = functools.partial(_sample_state_kernel, block=block)
    return pl.pallas_call(
        kern,
        grid=(n // block,),
        in_specs=[pl.BlockSpec((block, SSM_CH, SSM_STATE), lambda i: (i, 0, 0)),
                  _const_spec(xdt_t.shape), _const_spec(dec_t.shape), _const_spec(bc.shape)],
        out_specs=[pl.BlockSpec((block, SSM_CH, SSM_STATE), lambda i: (i, 0, 0)),
                   pl.BlockSpec((SSM_CH, n), lambda i: (0, 0))],
        out_shape=[jax.ShapeDtypeStruct(state.shape, F32), jax.ShapeDtypeStruct((SSM_CH, n), F32)],
        compiler_params=pltpu.CompilerParams(dimension_semantics=("arbitrary",),
                                             vmem_limit_bytes=VMEM_LIMIT),
        name="sample_state",
    )(state, xdt_t, dec_t, bc)


def _sample_post_kernel(x_ref, mod_ref, yconv_ref, ydiag_ref, yoff_t_ref, xs_ref, z_ref, dexp_ref, snw_ref,
                        w_out_ref, ln_g_ref, ln_b_ref, x1_ref):
    g1 = mod_ref[:, 2 * D_MODEL:3 * D_MODEL]
    y = ydiag_ref[...] + yoff_t_ref[...].T + xs_ref[...] * dexp_ref[...]
    y = y * _silu(z_ref[...])
    m = _mix_out(yconv_ref[...], _ssm_group_norm(y, snw_ref[...]), w_out_ref)
    x1_ref[...] = _layer_norm(ALPHA * x_ref[...] + (1.0 + g1) * m, ln_g_ref[...], ln_b_ref[...])


def _sample_post(x, mod, yconv, ydiag, yoff_t, xs, z, dexp, snw, w_out, ln_g, ln_b):
    return pl.pallas_call(
        _sample_post_kernel,
        out_shape=jax.ShapeDtypeStruct(x.shape, F32),
        compiler_params=pltpu.CompilerParams(vmem_limit_bytes=VMEM_LIMIT),
        name="sample_post",
    )(x, mod, yconv, ydiag, yoff_t, xs, z, dexp, snw, w_out, ln_g, ln_b)


def kernel(x_prompt, x_sample, state_conv, state_ssm_conv, state_ssm, c_prompt, c_sample, w_ada, b_ada, w_in, conv_w, conv_norm_w, ssm_conv_w, ssm_conv_b, dt_bias, a_log, d_skip, ssm_norm_w, w_out, ln1_g, ln1_b, w_up, w_down, ln2_g, ln2_b):
    assert w_ada.shape[0] == 1, "single-layer trunk"
    nb, seq, _ = x_prompt.shape
    ns = x_sample.shape[0]
    row = lambda a: a.reshape(1, -1)
    pad_heads = lambda a: jnp.pad(a.reshape(1, -1), ((0, 0), (0, LANES - SSM_HEADS)))

    w_in_b = jnp.pad(w_in[0], ((0, 0), (0, IN_PAD - IN_COLS))).astype(BF16)
    w_out_b = w_out[0].astype(BF16)
    w_up_b = w_up[0].astype(BF16)
    w_down_b = w_down[0].astype(BF16)
    conv_nw, sconv_b, snw = row(conv_norm_w[0]), row(ssm_conv_b[0]), row(ssm_norm_w[0])
    dtb, alog = pad_heads(dt_bias[0]), pad_heads(a_log[0])
    dexp = row(jnp.repeat(d_skip[0], SSM_HEAD_DIM))
    g1, b1, g2, b2 = row(ln1_g[0]), row(ln1_b[0]), row(ln2_g[0]), row(ln2_b[0])

    mod = _ada(jnp.concatenate([c_prompt, c_sample], axis=0), w_ada[0], row(b_ada[0]))
    mod_p = mod[:nb].reshape(nb, 1, 6 * D_MODEL)
    mod_s = mod[nb:].reshape(1, ns, 6 * D_MODEL)

    x1_p, cst_p, scst_p, sst_p = _mixer_prompt(x_prompt, mod_p, w_in_b, conv_w[0], conv_nw, ssm_conv_w[0],
                                               sconv_b, dtb, alog, dexp, snw, w_out_b, g1, b1)
    y_p = _ffn(x1_p.reshape(nb * seq, D_MODEL), mod_p, seq, w_up_b, w_down_b, g2, b2, tile=512)

    xs2 = x_sample.reshape(ns, D_MODEL)
    (yconv_s, ch_s, xbc_s, z_s, xs_s, ydiag_s, bc_s, xdt_t, dec_t) = _sample_pre(
        xs2, mod_s[0], w_in_b, conv_w[0], conv_nw, ssm_conv_w[0], sconv_b, dtb, alog,
        state_conv[0, :, 0], state_conv[0, :, 1],
        state_ssm_conv[0, :, 0], state_ssm_conv[0, :, 1], state_ssm_conv[0, :, 2])
    new_state_s, yoff_t = _sample_state(state_ssm[0].reshape(ns, SSM_CH, SSM_STATE), xdt_t, dec_t, bc_s)
    x1_s = _sample_post(xs2, mod_s[0], yconv_s, ydiag_s, yoff_t, xs_s, z_s, dexp, snw, w_out_b, g1, b1)
    y_s = _ffn(x1_s, mod_s, ns, w_up_b, w_down_b, g2, b2, tile=ns)

    return (y_p.reshape(nb, seq, D_MODEL),
            y_s.reshape(ns, 1, D_MODEL),
            cst_p[None],
            scst_p[None],
            sst_p.reshape(1, nb, SSM_HEADS, SSM_HEAD_DIM, SSM_STATE),
            jnp.stack([state_conv[0, :, 1], ch_s], axis=1)[None],
            jnp.stack([state_ssm_conv[0, :, 1], state_ssm_conv[0, :, 2], xbc_s], axis=1)[None],
            new_state_s.reshape(1, ns, SSM_HEADS, SSM_HEAD_DIM, SSM_STATE))
```

```python
import functools

import jax
import jax.numpy as jnp
from jax import lax
from jax.experimental import pallas as pl
from jax.experimental.pallas import tpu as pltpu

F32 = jnp.float32
BF16 = jnp.bfloat16

D_MODEL = 1024
CONV_CH = 1024
CONV_GROUP = 64
SSM_CH = 1024
SSM_HEADS = 16
SSM_HEAD_DIM = 64
SSM_GROUPS = 2
SSM_GROUP_CH = SSM_CH // SSM_GROUPS
SSM_STATE = 128
SSM_CHUNK = 128
XBC_CH = SSM_CH + 2 * SSM_GROUPS * SSM_STATE
D_FF = 4 * D_MODEL
LANES = 128
SUBLANES = 8
MXU_COLS = 256
COL_GB, COL_GC, COL_HV, COL_Z, COL_XBC = 0, 1024, 2048, 3072, 4096
COL_DT = COL_XBC + XBC_CH
IN_COLS = COL_DT + SSM_HEADS
IN_PAD = COL_DT + LANES
ALPHA = 2.0 ** 0.25
LN_EPS = 1e-5
RMS_EPS = 1e-5
VMEM_LIMIT = 56 * 1024 * 1024


def _dot(a, b):
    return jnp.dot(a, b, preferred_element_type=F32)


def _split(a, terms):
    parts = []
    r = a
    for t in range(terms):
        p = r.astype(BF16)
        parts.append(p)
        if t + 1 < terms:
            r = r - p.astype(F32)
    return parts


def _dot_f32_lhs(a, b_exact, terms=3):
    parts = _split(a, terms)
    out = _dot(parts[0], b_exact)
    for p in parts[1:]:
        out = out + _dot(p, b_exact)
    return out


def _dot_f32_rhs(a_exact, b, terms=3):
    parts = _split(b, terms)
    out = _dot(a_exact, parts[0])
    for p in parts[1:]:
        out = out + _dot(a_exact, p)
    return out


def _head_expand():
    h = lax.broadcasted_iota(jnp.int32, (LANES, SSM_CH), 0)
    c = lax.broadcasted_iota(jnp.int32, (LANES, SSM_CH), 1)
    return (c // SSM_HEAD_DIM == h).astype(BF16)


def _group_reduce():
    c = lax.broadcasted_iota(jnp.int32, (CONV_CH, LANES), 0)
    k = lax.broadcasted_iota(jnp.int32, (CONV_CH, LANES), 1)
    return (c // CONV_GROUP == k).astype(BF16)


def _sigmoid(x):
    return 1.0 / (1.0 + jnp.exp(-x))


def _silu(x):
    return x * _sigmoid(x)


def _softplus(x):
    return jnp.maximum(x, 0.0) + jnp.log1p(jnp.exp(-jnp.abs(x)))


def _layer_norm(r, g, b):
    mu = jnp.mean(r, axis=-1, keepdims=True)
    d = r - mu
    var = jnp.mean(d * d, axis=-1, keepdims=True)
    return d * lax.rsqrt(var + LN_EPS) * g + b


def _conv_group_norm(prod, w, expand, reduce):
    ssum = _dot_f32_lhs(prod * prod, reduce, terms=2)
    rstd = lax.rsqrt(ssum * (1.0 / CONV_GROUP) + RMS_EPS)
    return prod * _dot_f32_lhs(rstd, expand, terms=2) * w


def _ssm_group_norm(y, w):
    outs = []
    for g in range(SSM_GROUPS):
        yg = y[:, g * SSM_GROUP_CH:(g + 1) * SSM_GROUP_CH]
        ms = jnp.mean(yg * yg, axis=-1, keepdims=True)
        outs.append((yg * lax.rsqrt(ms + RMS_EPS) * w[:, g * SSM_GROUP_CH:(g + 1) * SSM_GROUP_CH]).astype(BF16))
    return outs


def _mix_out(y_conv, y_ssm_groups, w_out_ref):
    m = _dot(y_conv.astype(BF16), w_out_ref[0:CONV_CH, :])
    for g, yg in enumerate(y_ssm_groups):
        lo = CONV_CH + g * SSM_GROUP_CH
        m = m + _dot(yg, w_out_ref[lo:lo + SSM_GROUP_CH, :])
    return m


def _ada_kernel(c_ref, w_ref, b_ref, o_ref):
    c = c_ref[...]
    w = w_ref[...]
    c_hi = c.astype(BF16)
    c_lo = (c - c_hi.astype(F32)).astype(BF16)
    w_hi = w.astype(BF16)
    w_lo = (w - w_hi.astype(F32)).astype(BF16)
    o_ref[...] = _dot(c_hi, w_hi) + _dot(c_hi, w_lo) + _dot(c_lo, w_hi) + b_ref[...]


def _ada(c_all, w_ada, b_ada, tile_n=512):
    rows = c_all.shape[0]
    n = w_ada.shape[1]
    return pl.pallas_call(
        _ada_kernel,
        grid=(n // tile_n,),
        in_specs=[pl.BlockSpec((rows, D_MODEL), lambda i: (0, 0)),
                  pl.BlockSpec((D_MODEL, tile_n), lambda i: (0, i)),
                  pl.BlockSpec((1, tile_n), lambda i: (0, i))],
        out_specs=pl.BlockSpec((rows, tile_n), lambda i: (0, i)),
        out_shape=jax.ShapeDtypeStruct((rows, n), F32),
        name="ada_mod",
    )(c_all, w_ada, b_ada)


def _mixer_prompt_kernel(xa_ref, moda_ref, modb_ref, w_in_ref, expand_ref, reduce_ref,
                         conv_w_ref, conv_nw_ref, sconv_w_ref, sconv_b_ref,
                         dtb_ref, alog_ref, dexp_ref, snw_ref, w_out_ref, ln_g_ref, ln_b_ref,
                         x1_ref, cst_ref, scst_ref, sst_ref,
                         p0, p1, xk0, xk1, cbuf, xbuf, st_ref, xs_ref, bc_ref, xdt_ref, dta_ref, y_ref, yc_ref,
                         *, tile, tiles_per_seq, sched):
    s = pl.program_id(0)
    jb = lax.rem(s + (tiles_per_seq - 1), tiles_per_seq)

    @pl.when(s == 0)
    def _():
        p1[...] = jnp.zeros_like(p1)
        xk1[...] = jnp.zeros_like(xk1)

    @pl.when((jb == 0) | (s == 0))
    def _():
        cbuf[0:SUBLANES, :] = jnp.zeros((SUBLANES, CONV_CH), F32)
        xbuf[0:SUBLANES, :] = jnp.zeros((SUBLANES, XBC_CH), F32)
        st_ref[...] = jnp.zeros_like(st_ref)

    def stages(pa, xka, pb, xkb):
        xa = xa_ref[...]
        xka[...] = xa
        u = (xa * (1.0 + moda_ref[:, D_MODEL:2 * D_MODEL]) + moda_ref[:, 0:D_MODEL]).astype(BF16)
        pieces = [(lo, MXU_COLS) for lo in range(0, IN_PAD - MXU_COLS - LANES, MXU_COLS)]
        pieces.append((pieces[-1][0] + MXU_COLS, IN_PAD - pieces[-1][0] - MXU_COLS))
        pieces = iter(pieces)

        def first_stage(n):
            for _ in range(n):
                piece = next(pieces, None)
                if piece is not None:
                    lo, width = piece
                    pa[:, lo:lo + width] = _dot(u, w_in_ref[:, lo:lo + width])

        expand = expand_ref[...]
        x = xkb[...]
        g1 = modb_ref[:, 2 * D_MODEL:3 * D_MODEL]

        def proj(lo, width):
            return pb[:, lo:lo + width]

        for k in range(CONV_CH // MXU_COLS):
            first_stage(sched[0])
            c0 = k * MXU_COLS
            cs = slice(c0, c0 + MXU_COLS)
            ch = proj(COL_GC + c0, MXU_COLS) * proj(COL_HV + c0, MXU_COLS)
            cbuf[SUBLANES:SUBLANES + tile, cs] = ch
            cv = (conv_w_ref[0:1, cs] * cbuf[SUBLANES - 2:SUBLANES - 2 + tile, cs]
                  + conv_w_ref[1:2, cs] * cbuf[SUBLANES - 1:SUBLANES - 1 + tile, cs]
                  + conv_w_ref[2:3, cs] * ch)
            cbuf[0:SUBLANES, cs] = cbuf[tile:tile + SUBLANES, cs]
            prod = proj(COL_GB + c0, MXU_COLS) * cv
            ssum = _dot_f32_lhs(prod * prod, reduce_ref[cs, :], terms=2)
            rstd = lax.rsqrt(ssum * (1.0 / CONV_GROUP) + RMS_EPS)
            yc_ref[:, cs] = (prod * _dot_f32_lhs(rstd, expand_ref[:, cs], terms=2)
                             * conv_nw_ref[:, cs]).astype(BF16)

        for k in range(XBC_CH // MXU_COLS):
            first_stage(sched[1])
            c0 = k * MXU_COLS
            cs = slice(c0, c0 + MXU_COLS)
            xbc = proj(COL_XBC + c0, MXU_COLS)
            xbuf[SUBLANES:SUBLANES + tile, cs] = xbc
            xc = (sconv_w_ref[0:1, cs] * xbuf[SUBLANES - 3:SUBLANES - 3 + tile, cs]
                  + sconv_w_ref[1:2, cs] * xbuf[SUBLANES - 2:SUBLANES - 2 + tile, cs]
                  + sconv_w_ref[2:3, cs] * xbuf[SUBLANES - 1:SUBLANES - 1 + tile, cs]
                  + sconv_w_ref[3:4, cs] * xbc + sconv_b_ref[:, cs])
            xbuf[0:SUBLANES, cs] = xbuf[tile:tile + SUBLANES, cs]
            xc = _silu(xc)
            if c0 < SSM_CH:
                xs_ref[:, cs] = xc
            else:
                bc_ref[:, c0 - SSM_CH:c0 - SSM_CH + MXU_COLS] = xc

        first_stage(sched[2])
        dt = _softplus(proj(COL_DT, LANES) + dtb_ref[...])
        dta_ref[...] = dt * (-jnp.exp(alog_ref[...]))
        xdt_ref[...] = xs_ref[...] * _dot_f32_lhs(dt, expand, terms=2)

        row = lax.broadcasted_iota(jnp.int32, (SSM_CHUNK, SSM_CHUNK), 0)
        col = lax.broadcasted_iota(jnp.int32, (SSM_CHUNK, SSM_CHUNK), 1)
        causal = row >= col
        tri = causal.astype(BF16)
        first_half = col < SSM_HEAD_DIM

        for c in range(tile // SSM_CHUNK):
            first_stage(sched[3])
            rows = slice(c * SSM_CHUNK, (c + 1) * SSM_CHUNK)
            acs = _dot_f32_rhs(tri, dta_ref[rows, :])
            acs_t = acs.T
            acs_x = _dot_f32_lhs(acs, expand)
            end_x = acs_x[SSM_CHUNK - 1:SSM_CHUNK, :]
            xdt_c = xdt_ref[rows, :]
            xdec = (xdt_c * jnp.exp(end_x - acs_x)).astype(BF16)
            e_acs = jnp.exp(acs_x)
            e_end = jnp.exp(end_x)
            for g in range(SSM_GROUPS):
                first_stage(sched[4])
                gl = g * SSM_GROUP_CH
                bm = bc_ref[rows, g * SSM_STATE:(g + 1) * SSM_STATE]
                cm = bc_ref[rows, (SSM_GROUPS + g) * SSM_STATE:(SSM_GROUPS + g + 1) * SSM_STATE].astype(BF16)
                cb = lax.dot_general(cm, bm.astype(BF16), (((1,), (1,)), ((), ())), preferred_element_type=F32)
                st = st_ref[:, gl:gl + SSM_GROUP_CH]
                y_off = _dot(cm, st.astype(BF16)) * e_acs[:, gl:gl + SSM_GROUP_CH]
                st_ref[:, gl:gl + SSM_GROUP_CH] = (st * e_end[:, gl:gl + SSM_GROUP_CH]
                                                   + _dot(bm.T.astype(BF16), xdec[:, gl:gl + SSM_GROUP_CH]))
                for q in range(SSM_GROUP_CH // LANES):
                    lo = gl + q * LANES
                    h0 = lo // SSM_HEAD_DIM
                    slab = acs_x[:, lo:lo + LANES]
                    rolled = pltpu.roll(slab, SSM_HEAD_DIM, axis=1)
                    a0 = jnp.where(first_half, slab, rolled)
                    a1 = jnp.where(first_half, rolled, slab)
                    l0 = jnp.exp(jnp.where(causal, a0 - acs_t[h0:h0 + 1, :], -jnp.inf))
                    l1 = jnp.exp(jnp.where(causal, a1 - acs_t[h0 + 1:h0 + 2, :], -jnp.inf))
                    m = jnp.concatenate([(cb * l0).astype(BF16), (cb * l1).astype(BF16)], axis=1)
                    xp = xdt_c[:, lo:lo + LANES]
                    rhs = jnp.concatenate([jnp.where(first_half, xp, 0.0), jnp.where(first_half, 0.0, xp)],
                                          axis=0).astype(BF16)
                    y_ref[rows, lo:lo + LANES] = _dot(m, rhs) + y_off[:, q * LANES:(q + 1) * LANES]

        for k in range(SSM_CH // MXU_COLS):
            first_stage(sched[5])
            c0 = k * MXU_COLS
            cs = slice(c0, c0 + MXU_COLS)
            y_ref[:, cs] = (y_ref[:, cs] + xs_ref[:, cs] * dexp_ref[:, cs]) * _silu(proj(COL_Z + c0, MXU_COLS))
        first_stage(IN_PAD // MXU_COLS)
        m = _mix_out(yc_ref[...], _ssm_group_norm(y_ref[...], snw_ref[...]), w_out_ref)
        x1_ref[...] = _layer_norm(ALPHA * x + (1.0 + g1) * m, ln_g_ref[...], ln_b_ref[...])

    @pl.when(lax.rem(s, 2) == 0)
    def _():
        stages(p0, xk0, p1, xk1)

    @pl.when(lax.rem(s, 2) == 1)
    def _():
        stages(p1, xk1, p0, xk0)

    @pl.when((jb == tiles_per_seq - 1) & (s > 0))
    def _():
        cst_ref[...] = cbuf[SUBLANES - 2:SUBLANES, :]
        scst_ref[...] = xbuf[SUBLANES - 3:SUBLANES, :]
        sst_ref[...] = st_ref[...].T


def _const_spec(shape):
    return pl.BlockSpec(shape, lambda *_: (0,) * len(shape), pipeline_mode=pl.Buffered(1))


def _mixer_prompt(x, mod, w_in, conv_w, conv_nw, sconv_w, sconv_b, dtb, alog, dexp, snw, w_out, ln_g, ln_b,
                  tile=256, sched=(1, 1, 1, 1, 1, 1)):
    assert CONV_GROUP == SSM_HEAD_DIM and CONV_CH == SSM_CH
    nb, seq, _ = x.shape
    tiles_per_seq = seq // tile
    n_tiles = nb * tiles_per_seq
    kern = functools.partial(_mixer_prompt_kernel, tile=tile, tiles_per_seq=tiles_per_seq, sched=sched)
    consts = [w_in, _head_expand(), _group_reduce(), conv_w, conv_nw, sconv_w, sconv_b, dtb, alog, dexp, snw,
              w_out, ln_g, ln_b]
    first = lambda s: jnp.minimum(s, n_tiles - 1)
    second = lambda s: jnp.maximum(s - 1, 0)
    return pl.pallas_call(
        kern,
        grid=(n_tiles + 1,),
        in_specs=[pl.BlockSpec((tile, D_MODEL), lambda s: (first(s), 0)),
                  pl.BlockSpec((None, 1, 6 * D_MODEL), lambda s: (first(s) // tiles_per_seq, 0, 0)),
                  pl.BlockSpec((None, 1, 6 * D_MODEL), lambda s: (second(s) // tiles_per_seq, 0, 0))]
                 + [_const_spec(a.shape) for a in consts],
        out_specs=[pl.BlockSpec((tile, D_MODEL), lambda s: (second(s), 0)),
                   pl.BlockSpec((None, 2, CONV_CH), lambda s: (second(s) // tiles_per_seq, 0, 0)),
                   pl.BlockSpec((None, 3, XBC_CH), lambda s: (second(s) // tiles_per_seq, 0, 0)),
                   pl.BlockSpec((None, SSM_CH, SSM_STATE), lambda s: (second(s) // tiles_per_seq, 0, 0))],
        out_shape=[jax.ShapeDtypeStruct((nb * seq, D_MODEL), F32),
                   jax.ShapeDtypeStruct((nb, 2, CONV_CH), F32),
                   jax.ShapeDtypeStruct((nb, 3, XBC_CH), F32),
                   jax.ShapeDtypeStruct((nb, SSM_CH, SSM_STATE), F32)],
        scratch_shapes=[pltpu.VMEM((tile, IN_PAD), F32),
                        pltpu.VMEM((tile, IN_PAD), F32),
                        pltpu.VMEM((tile, D_MODEL), F32),
                        pltpu.VMEM((tile, D_MODEL), F32),
                        pltpu.VMEM((tile + SUBLANES, CONV_CH), F32),
                        pltpu.VMEM((tile + SUBLANES, XBC_CH), F32),
                        pltpu.VMEM((SSM_STATE, SSM_CH), F32),
                        pltpu.VMEM((tile, SSM_CH), F32),
                        pltpu.VMEM((tile, 2 * SSM_GROUPS * SSM_STATE), F32),
                        pltpu.VMEM((tile, SSM_CH), F32),
                        pltpu.VMEM((tile, LANES), F32),
                        pltpu.VMEM((tile, SSM_CH), F32),
                        pltpu.VMEM((tile, CONV_CH), BF16)],
        compiler_params=pltpu.CompilerParams(dimension_semantics=("arbitrary",),
                                             vmem_limit_bytes=VMEM_LIMIT),
        name="mixer_prompt",
    )(x.reshape(nb * seq, D_MODEL), mod, mod, *consts)


def _ffn_kernel(x_ref, mod_ref, w_up_ref, w_down_ref, ln_g_ref, ln_b_ref, o_ref, *, ff_tile):
    x = x_ref[...]
    sh2 = mod_ref[:, 3 * D_MODEL:4 * D_MODEL]
    sc2 = mod_ref[:, 4 * D_MODEL:5 * D_MODEL]
    g2 = mod_ref[:, 5 * D_MODEL:6 * D_MODEL]
    v = (x * (1.0 + sc2) + sh2).astype(BF16)
    acc = jnp.zeros(x.shape, F32)
    for k in range(D_FF // ff_tile):
        h = jnp.maximum(_dot(v, w_up_ref[:, k * ff_tile:(k + 1) * ff_tile]), 0.0)
        acc = acc + _dot((h * h).astype(BF16), w_down_ref[k * ff_tile:(k + 1) * ff_tile, :])
    o_ref[...] = _layer_norm(ALPHA * x + (1.0 + g2) * acc, ln_g_ref[...], ln_b_ref[...])


def _ffn(x, mod, rows_per_mod, w_up, w_down, ln_g, ln_b, tile, ff_tile=1024):
    rows = x.shape[0]
    mod_rows = mod.shape[1]
    tiles_per_mod = rows_per_mod // tile
    kern = functools.partial(_ffn_kernel, ff_tile=ff_tile)
    return pl.pallas_call(
        kern,
        grid=(rows // tile,),
        in_specs=[pl.BlockSpec((tile, D_MODEL), lambda i: (i, 0)),
                  pl.BlockSpec((None, mod_rows, 6 * D_MODEL), lambda i: (i // tiles_per_mod, 0, 0)),
                  _const_spec(w_up.shape), _const_spec(w_down.shape),
                  _const_spec(ln_g.shape), _const_spec(ln_b.shape)],
        out_specs=pl.BlockSpec((tile, D_MODEL), lambda i: (i, 0)),
        out_shape=jax.ShapeDtypeStruct((rows, D_MODEL), F32),
        compiler_params=pltpu.CompilerParams(dimension_semantics=("arbitrary",),
                                             vmem_limit_bytes=VMEM_LIMIT),
        name="ffn",
    )(x, mod, w_up, w_down, ln_g, ln_b)


def _sample_pre_kernel(x_ref, mod_ref, w_in_ref, conv_w_ref, conv_nw_ref, sconv_w_ref, sconv_b_ref,
                       dtb_ref, alog_ref, cb0_ref, cb1_ref, sb0_ref, sb1_ref, sb2_ref,
                       yconv_ref, ch_ref, xbc_ref, z_ref, xs_ref, ydiag_ref, bc_ref, xdt_t_ref, dec_t_ref):
    expand = _head_expand()
    reduce = _group_reduce()
    x = x_ref[...]
    sh1 = mod_ref[:, 0:D_MODEL]
    sc1 = mod_ref[:, D_MODEL:2 * D_MODEL]
    u = (x * (1.0 + sc1) + sh1).astype(BF16)

    def proj(lo, width):
        return _dot(u, w_in_ref[:, lo:lo + width])

    ch = proj(COL_GC, CONV_CH) * proj(COL_HV, CONV_CH)
    ch_ref[...] = ch
    cw = conv_w_ref[...]
    cv = cw[0:1, :] * cb0_ref[...] + cw[1:2, :] * cb1_ref[...] + cw[2:3, :] * ch
    yconv_ref[...] = _conv_group_norm(proj(COL_GB, CONV_CH) * cv, conv_nw_ref[...], expand, reduce)

    xbc = proj(COL_XBC, XBC_CH)
    xbc_ref[...] = xbc
    sw = sconv_w_ref[...]
    xc = _silu(sw[0:1, :] * sb0_ref[...] + sw[1:2, :] * sb1_ref[...] + sw[2:3, :] * sb2_ref[...]
               + sw[3:4, :] * xbc + sconv_b_ref[...])
    xs = xc[:, 0:SSM_CH]
    xs_ref[...] = xs
    bc_ref[...] = xc[:, SSM_CH:XBC_CH]
    z_ref[...] = proj(COL_Z, SSM_CH)

    dt = _softplus(proj(COL_DT, LANES) + dtb_ref[...])
    dta = dt * (-jnp.exp(alog_ref[...]))
    xdt = xs * _dot_f32_lhs(dt, expand)
    dec_t_ref[...] = jnp.exp(_dot_f32_lhs(dta, expand)).T
    xdt_t_ref[...] = xdt.T
    for g in range(SSM_GROUPS):
        bm = xc[:, SSM_CH + g * SSM_STATE:SSM_CH + (g + 1) * SSM_STATE]
        cm = xc[:, SSM_CH + (SSM_GROUPS + g) * SSM_STATE:SSM_CH + (SSM_GROUPS + g + 1) * SSM_STATE]
        cb = jnp.sum(cm * bm, axis=-1, keepdims=True)
        gl = g * SSM_GROUP_CH
        ydiag_ref[:, gl:gl + SSM_GROUP_CH] = cb * xdt[:, gl:gl + SSM_GROUP_CH]


def _sample_pre(x, mod, w_in, conv_w, conv_nw, sconv_w, sconv_b, dtb, alog, cb0, cb1, sb0, sb1, sb2):
    n = x.shape[0]
    args = (x, mod, w_in, conv_w, conv_nw, sconv_w, sconv_b, dtb, alog, cb0, cb1, sb0, sb1, sb2)
    out_shapes = [(n, CONV_CH), (n, CONV_CH), (n, XBC_CH), (n, SSM_CH), (n, SSM_CH), (n, SSM_CH),
                  (n, 2 * SSM_GROUPS * SSM_STATE), (SSM_CH, n), (SSM_CH, n)]
    return pl.pallas_call(
        _sample_pre_kernel,
        out_shape=[jax.ShapeDtypeStruct(s, F32) for s in out_shapes],
        compiler_params=pltpu.CompilerParams(vmem_limit_bytes=VMEM_LIMIT),
        name="sample_pre",
    )(*args)


def _sample_state_kernel(s_ref, xdt_t_ref, dec_t_ref, bc_ref, o_ref, yt_ref, *, block):
    i = pl.program_id(0)

    @pl.when(i == 0)
    def _():
        yt_ref[...] = jnp.zeros_like(yt_ref)

    lane = lax.broadcasted_iota(jnp.int32, yt_ref.shape, 1)

    def body(k, carry):
        b = i * block + k
        s = s_ref[k]
        here = lane == b
        xcol = jnp.sum(jnp.where(here, xdt_t_ref[...], 0.0), axis=1, keepdims=True)
        dcol = jnp.sum(jnp.where(here, dec_t_ref[...], 0.0), axis=1, keepdims=True)
        bc = bc_ref[pl.ds(b, 1), :]
        ycols, outer = [], []
        for g in range(SSM_GROUPS):
            gl = g * SSM_GROUP_CH
            bm = bc[:, g * SSM_STATE:(g + 1) * SSM_STATE]
            cm = bc[:, (SSM_GROUPS + g) * SSM_STATE:(SSM_GROUPS + g + 1) * SSM_STATE]
            ycols.append(jnp.sum(s[gl:gl + SSM_GROUP_CH, :] * cm, axis=1, keepdims=True))
            outer.append(xcol[gl:gl + SSM_GROUP_CH, :] * bm)
        ycol = jnp.concatenate(ycols, axis=0) * dcol
        yt_ref[...] = jnp.where(here, ycol, yt_ref[...])
        o_ref[k] = s * dcol + jnp.concatenate(outer, axis=0)
        return carry

    lax.fori_loop(0, block, body, 0)


def _sample_state(state, xdt_t, dec_t, bc, block=8):
    n = state.shape[0]
    kern = functools.partial(_sample_state_kernel, block=block)
    return pl.pallas_call(
        kern,
        grid=(n // block,),
        in_specs=[pl.BlockSpec((block, SSM_CH, SSM_STATE), lambda i: (i, 0, 0)),
                  _const_spec(xdt_t.shape), _const_spec(dec_t.shape), _const_spec(bc.shape)],
        out_specs=[pl.BlockSpec((block, SSM_CH, SSM_STATE), lambda i: (i, 0, 0)),
                   pl.BlockSpec((SSM_CH, n), lambda i: (0, 0))],
        out_shape=[jax.ShapeDtypeStruct(state.shape, F32), jax.ShapeDtypeStruct((SSM_CH, n), F32)],
        compiler_params=pltpu.CompilerParams(dimension_semantics=("arbitrary",),
                                             vmem_limit_bytes=VMEM_LIMIT),
        name="sample_state",
    )(state, xdt_t, dec_t, bc)


def _sample_post_kernel(x_ref, mod_ref, yconv_ref, ydiag_ref, yoff_t_ref, xs_ref, z_ref, dexp_ref, snw_ref,
                        w_out_ref, ln_g_ref, ln_b_ref, x1_ref):
    g1 = mod_ref[:, 2 * D_MODEL:3 * D_MODEL]
    y = ydiag_ref[...] + yoff_t_ref[...].T + xs_ref[...] * dexp_ref[...]
    y = y * _silu(z_ref[...])
    m = _mix_out(yconv_ref[...], _ssm_group_norm(y, snw_ref[...]), w_out_ref)
    x1_ref[...] = _layer_norm(ALPHA * x_ref[...] + (1.0 + g1) * m, ln_g_ref[...], ln_b_ref[...])


def _sample_post(x, mod, yconv, ydiag, yoff_t, xs, z, dexp, snw, w_out, ln_g, ln_b):
    return pl.pallas_call(
        _sample_post_kernel,
        out_shape=jax.ShapeDtypeStruct(x.shape, F32),
        compiler_params=pltpu.CompilerParams(vmem_limit_bytes=VMEM_LIMIT),
        name="sample_post",
    )(x, mod, yconv, ydiag, yoff_t, xs, z, dexp, snw, w_out, ln_g, ln_b)


def kernel(x_prompt, x_sample, state_conv, state_ssm_conv, state_ssm, c_prompt, c_sample, w_ada, b_ada, w_in, conv_w, conv_norm_w, ssm_conv_w, ssm_conv_b, dt_bias, a_log, d_skip, ssm_norm_w, w_out, ln1_g, ln1_b, w_up, w_down, ln2_g, ln2_b):
    assert w_ada.shape[0] == 1, "single-layer trunk"
    nb, seq, _ = x_prompt.shape
    ns = x_sample.shape[0]
    row = lambda a: a.reshape(1, -1)
    pad_heads = lambda a: jnp.pad(a.reshape(1, -1), ((0, 0), (0, LANES - SSM_HEADS)))

    w_in_b = jnp.pad(w_in[0], ((0, 0), (0, IN_PAD - IN_COLS))).astype(BF16)
    w_out_b = w_out[0].astype(BF16)
    w_up_b = w_up[0].astype(BF16)
    w_down_b = w_down[0].astype(BF16)
    conv_nw, sconv_b, snw = row(conv_norm_w[0]), row(ssm_conv_b[0]), row(ssm_norm_w[0])
    dtb, alog = pad_heads(dt_bias[0]), pad_heads(a_log[0])
    dexp = row(jnp.repeat(d_skip[0], SSM_HEAD_DIM))
    g1, b1, g2, b2 = row(ln1_g[0]), row(ln1_b[0]), row(ln2_g[0]), row(ln2_b[0])

    mod = _ada(jnp.concatenate([c_prompt, c_sample], axis=0), w_ada[0], row(b_ada[0]))
    mod_p = mod[:nb].reshape(nb, 1, 6 * D_MODEL)
    mod_s = mod[nb:].reshape(1, ns, 6 * D_MODEL)

    x1_p, cst_p, scst_p, sst_p = _mixer_prompt(x_prompt, mod_p, w_in_b, conv_w[0], conv_nw, ssm_conv_w[0],
                                               sconv_b, dtb, alog, dexp, snw, w_out_b, g1, b1)
    y_p = _ffn(x1_p, mod_p, seq, w_up_b, w_down_b, g2, b2, tile=512)

    xs2 = x_sample.reshape(ns, D_MODEL)
    (yconv_s, ch_s, xbc_s, z_s, xs_s, ydiag_s, bc_s, xdt_t, dec_t) = _sample_pre(
        xs2, mod_s[0], w_in_b, conv_w[0], conv_nw, ssm_conv_w[0], sconv_b, dtb, alog,
        state_conv[0, :, 0], state_conv[0, :, 1],
        state_ssm_conv[0, :, 0], state_ssm_conv[0, :, 1], state_ssm_conv[0, :, 2])
    new_state_s, yoff_t = _sample_state(state_ssm[0].reshape(ns, SSM_CH, SSM_STATE), xdt_t, dec_t, bc_s)
    x1_s = _sample_post(xs2, mod_s[0], yconv_s, ydiag_s, yoff_t, xs_s, z_s, dexp, snw, w_out_b, g1, b1)
    y_s = _ffn(x1_s, mod_s, ns, w_up_b, w_down_b, g2, b2, tile=ns)

    return (y_p.reshape(nb, seq, D_MODEL),
            y_s.reshape(ns, 1, D_MODEL),
            cst_p[None],
            scst_p[None],
            sst_p.reshape(1, nb, SSM_HEADS, SSM_HEAD_DIM, SSM_STATE),
            jnp.stack([state_conv[0, :, 1], ch_s], axis=1)[None],
            jnp.stack([state_ssm_conv[0, :, 1], state_ssm_conv[0, :, 2], xbc_s], axis=1)[None],
            new_state_s.reshape(1, ns, SSM_HEADS, SSM_HEAD_DIM, SSM_STATE))
```

```python
import functools

import jax
import jax.numpy as jnp
from jax import lax
from jax.experimental import pallas as pl
from jax.experimental.pallas import tpu as pltpu

F32 = jnp.float32
BF16 = jnp.bfloat16

D_MODEL = 1024
CONV_CH = 1024
CONV_GROUP = 64
SSM_CH = 1024
SSM_HEADS = 16
SSM_HEAD_DIM = 64
SSM_GROUPS = 2
SSM_GROUP_CH = SSM_CH // SSM_GROUPS
SSM_STATE = 128
SSM_CHUNK = 128
XBC_CH = SSM_CH + 2 * SSM_GROUPS * SSM_STATE
D_FF = 4 * D_MODEL
LANES = 128
SUBLANES = 8
MXU_COLS = 256
COL_GB, COL_GC, COL_HV, COL_Z, COL_XBC = 0, 1024, 2048, 3072, 4096
COL_DT = COL_XBC + XBC_CH
IN_COLS = COL_DT + SSM_HEADS
IN_PAD = COL_DT + LANES
ALPHA = 2.0 ** 0.25
LN_EPS = 1e-5
RMS_EPS = 1e-5
VMEM_LIMIT = 56 * 1024 * 1024


def _dot(a, b):
    return jnp.dot(a, b, preferred_element_type=F32)


def _split(a, terms):
    parts = []
    r = a
    for t in range(terms):
        p = r.astype(BF16)
        parts.append(p)
        if t + 1 < terms:
            r = r - p.astype(F32)
    return parts


def _dot_f32_lhs(a, b_exact, terms=3):
    parts = _split(a, terms)
    out = _dot(parts[0], b_exact)
    for p in parts[1:]:
        out = out + _dot(p, b_exact)
    return out


def _dot_f32_rhs(a_exact, b, terms=3):
    parts = _split(b, terms)
    out = _dot(a_exact, parts[0])
    for p in parts[1:]:
        out = out + _dot(a_exact, p)
    return out


def _head_expand():
    h = lax.broadcasted_iota(jnp.int32, (LANES, SSM_CH), 0)
    c = lax.broadcasted_iota(jnp.int32, (LANES, SSM_CH), 1)
    return (c // SSM_HEAD_DIM == h).astype(BF16)


def _group_reduce():
    c = lax.broadcasted_iota(jnp.int32, (CONV_CH, LANES), 0)
    k = lax.broadcasted_iota(jnp.int32, (CONV_CH, LANES), 1)
    return (c // CONV_GROUP == k).astype(BF16)


def _sigmoid(x):
    return 1.0 / (1.0 + jnp.exp(-x))


def _silu(x):
    return x * _sigmoid(x)


def _softplus(x):
    return jnp.maximum(x, 0.0) + jnp.log1p(jnp.exp(-jnp.abs(x)))


def _layer_norm(r, g, b):
    mu = jnp.mean(r, axis=-1, keepdims=True)
    d = r - mu
    var = jnp.mean(d * d, axis=-1, keepdims=True)
    return d * lax.rsqrt(var + LN_EPS) * g + b


def _conv_group_norm(prod, w, expand, reduce):
    ssum = _dot_f32_lhs(prod * prod, reduce, terms=2)
    rstd = lax.rsqrt(ssum * (1.0 / CONV_GROUP) + RMS_EPS)
    return prod * _dot_f32_lhs(rstd, expand, terms=2) * w


def _ssm_group_norm(y, w):
    outs = []
    for g in range(SSM_GROUPS):
        yg = y[:, g * SSM_GROUP_CH:(g + 1) * SSM_GROUP_CH]
        ms = jnp.mean(yg * yg, axis=-1, keepdims=True)
        outs.append((yg * lax.rsqrt(ms + RMS_EPS) * w[:, g * SSM_GROUP_CH:(g + 1) * SSM_GROUP_CH]).astype(BF16))
    return outs


def _mix_out(y_conv, y_ssm_groups, w_out_ref):
    m = _dot(y_conv.astype(BF16), w_out_ref[0:CONV_CH, :])
    for g, yg in enumerate(y_ssm_groups):
        lo = CONV_CH + g * SSM_GROUP_CH
        m = m + _dot(yg, w_out_ref[lo:lo + SSM_GROUP_CH, :])
    return m


def _ada_kernel(c_ref, w_ref, b_ref, o_ref):
    c = c_ref[...]
    w = w_ref[...]
    c_hi = c.astype(BF16)
    c_lo = (c - c_hi.astype(F32)).astype(BF16)
    w_hi = w.astype(BF16)
    w_lo = (w - w_hi.astype(F32)).astype(BF16)
    o_ref[...] = _dot(c_hi, w_hi) + _dot(c_hi, w_lo) + _dot(c_lo, w_hi) + b_ref[...]


def _ada(c_all, w_ada, b_ada, tile_n=512):
    rows = c_all.shape[0]
    n = w_ada.shape[1]
    return pl.pallas_call(
        _ada_kernel,
        grid=(n // tile_n,),
        in_specs=[pl.BlockSpec((rows, D_MODEL), lambda i: (0, 0)),
                  pl.BlockSpec((D_MODEL, tile_n), lambda i: (0, i)),
                  pl.BlockSpec((1, tile_n), lambda i: (0, i))],
        out_specs=pl.BlockSpec((rows, tile_n), lambda i: (0, i)),
        out_shape=jax.ShapeDtypeStruct((rows, n), F32),
        name="ada_mod",
    )(c_all, w_ada, b_ada)


def _mixer_prompt_kernel(xa_ref, moda_ref, modb_ref, w_in_ref, expand_ref, reduce_ref,
                         conv_w_ref, conv_nw_ref, sconv_w_ref, sconv_b_ref,
                         dtb_ref, alog_ref, dexp_ref, snw_ref, w_out_ref, ln_g_ref, ln_b_ref,
                         x1_ref, cst_ref, scst_ref, sst_ref,
                         p0, p1, xk0, xk1, cbuf, xbuf, st_ref, xs_ref, bc_ref, dtx_ref, acsx_ref, endx_ref,
                         acst_ref, cb_ref, bmt_ref, y_ref, yc_ref,
                         *, tile, tiles_per_seq, sched):
    s = pl.program_id(0)
    jb = lax.rem(s + (tiles_per_seq - 1), tiles_per_seq)

    @pl.when(s == 0)
    def _():
        p1[...] = jnp.zeros_like(p1)
        xk1[...] = jnp.zeros_like(xk1)

    @pl.when((jb == 0) | (s == 0))
    def _():
        cbuf[...] = jnp.zeros_like(cbuf)
        xbuf[...] = jnp.zeros_like(xbuf)
        st_ref[...] = jnp.zeros_like(st_ref)

    def stages(pa, xka, pb, xkb):
        xa = xa_ref[...]
        xka[...] = xa
        u = (xa * (1.0 + moda_ref[:, D_MODEL:2 * D_MODEL]) + moda_ref[:, 0:D_MODEL]).astype(BF16)
        pieces = [(lo, MXU_COLS) for lo in range(0, IN_PAD - MXU_COLS - LANES, MXU_COLS)]
        pieces.append((pieces[-1][0] + MXU_COLS, IN_PAD - pieces[-1][0] - MXU_COLS))
        pieces = iter(pieces)

        def first_stage(n):
            for _ in range(n):
                piece = next(pieces, None)
                if piece is not None:
                    lo, width = piece
                    pa[:, lo:lo + width] = _dot(u, w_in_ref[:, lo:lo + width])

        expand = expand_ref[...]
        x = xkb[...]
        g1 = modb_ref[:, 2 * D_MODEL:3 * D_MODEL]

        def proj(lo, width):
            return pb[:, lo:lo + width]

        def delayed(tail_ref, cs, cur, taps):
            seq = jnp.concatenate([tail_ref[:, cs], cur], axis=0)
            tail_ref[:, cs] = cur[tile - SUBLANES:, :]
            return [pltpu.roll(seq, k, axis=0)[SUBLANES:, :] for k in range(1, taps + 1)]

        for k in range(CONV_CH // MXU_COLS):
            first_stage(sched[0])
            c0 = k * MXU_COLS
            cs = slice(c0, c0 + MXU_COLS)
            ch = proj(COL_GC + c0, MXU_COLS) * proj(COL_HV + c0, MXU_COLS)
            ch1, ch2 = delayed(cbuf, cs, ch, 2)
            cv = conv_w_ref[0:1, cs] * ch2 + conv_w_ref[1:2, cs] * ch1 + conv_w_ref[2:3, cs] * ch
            prod = proj(COL_GB + c0, MXU_COLS) * cv
            ssum = _dot_f32_lhs(prod * prod, reduce_ref[cs, :], terms=2)
            rstd = lax.rsqrt(ssum * (1.0 / CONV_GROUP) + RMS_EPS)
            yc_ref[:, cs] = (prod * _dot_f32_lhs(rstd, expand_ref[:, cs], terms=2)
                             * conv_nw_ref[:, cs]).astype(BF16)

        def pre_conv(c0):
            cs = slice(c0, c0 + MXU_COLS)
            xbc = proj(COL_XBC + c0, MXU_COLS)
            x1, x2, x3 = delayed(xbuf, cs, xbc, 3)
            return _silu(sconv_w_ref[0:1, cs] * x3 + sconv_w_ref[1:2, cs] * x2 + sconv_w_ref[2:3, cs] * x1
                         + sconv_w_ref[3:4, cs] * xbc + sconv_b_ref[:, cs])

        row = lax.broadcasted_iota(jnp.int32, (SSM_CHUNK, SSM_CHUNK), 0)
        col = lax.broadcasted_iota(jnp.int32, (SSM_CHUNK, SSM_CHUNK), 1)
        causal = row >= col
        tri = causal.astype(BF16)
        groups = SSM_CHUNK // SUBLANES
        causal_bias = jnp.where(causal, 0.0, -jnp.inf).reshape(groups, SUBLANES, SSM_CHUNK)
        first_half = (col < SSM_HEAD_DIM).reshape(groups, SUBLANES, SSM_CHUNK)
        half_rows = col < SSM_HEAD_DIM
        chunks = [slice(c * SSM_CHUNK, (c + 1) * SSM_CHUNK) for c in range(tile // SSM_CHUNK)]

        first_stage(sched[1])
        dt = _softplus(proj(COL_DT, LANES) + dtb_ref[...])
        dta = dt * (-jnp.exp(alog_ref[...]))
        dtx_ref[...] = _dot_f32_lhs(dt, expand, terms=2)
        for c, rows in enumerate(chunks):
            acs = _dot_f32_rhs(tri, dta[rows, :])
            acs_t = acs.T
            for h in range(SSM_HEADS):
                r8 = (c * SSM_HEADS + h) * SUBLANES
                acst_ref[r8:r8 + SUBLANES, :] = jnp.broadcast_to(acs_t[h:h + 1, :], (SUBLANES, SSM_CHUNK))
            acs_x = _dot_f32_lhs(acs, expand)
            acsx_ref[rows, :] = acs_x
            endx_ref[c * SUBLANES:(c + 1) * SUBLANES, :] = jnp.broadcast_to(acs_x[SSM_CHUNK - 1:SSM_CHUNK, :],
                                                                             (SUBLANES, SSM_CH))
        for c0 in range(SSM_CH, XBC_CH, MXU_COLS):
            first_stage(sched[2])
            bc_ref[:, c0 - SSM_CH:c0 - SSM_CH + MXU_COLS] = pre_conv(c0)
        for rows in chunks:
            for g in range(SSM_GROUPS):
                gs = slice(g * SSM_STATE, (g + 1) * SSM_STATE)
                bm = bc_ref[rows, gs]
                cm = bc_ref[rows, (SSM_GROUPS + g) * SSM_STATE:(SSM_GROUPS + g + 1) * SSM_STATE]
                cb_ref[rows, gs] = lax.dot_general(cm.astype(BF16), bm.astype(BF16), (((1,), (1,)), ((), ())),
                                                   preferred_element_type=F32)
                bmt_ref[rows, gs] = bm.T.astype(BF16)

        for c0 in range(0, SSM_CH, MXU_COLS):
            first_stage(sched[3])
            cs = slice(c0, c0 + MXU_COLS)
            g = c0 // SSM_GROUP_CH
            gs = slice(g * SSM_STATE, (g + 1) * SSM_STATE)
            xs = pre_conv(c0)
            xs_ref[:, cs] = xs
            xdt = xs * dtx_ref[:, cs]
            for c, rows in enumerate(chunks):
                first_stage(sched[4])
                acs_x = acsx_ref[rows, cs].reshape(groups, SUBLANES, MXU_COLS)
                end_x = endx_ref[c * SUBLANES:(c + 1) * SUBLANES, cs]
                xdt_c = xdt[rows, :]
                xdec = (xdt_c * jnp.exp(end_x[None] - acs_x).reshape(SSM_CHUNK, MXU_COLS)).astype(BF16)
                cm = bc_ref[rows, (SSM_GROUPS + g) * SSM_STATE:(SSM_GROUPS + g + 1) * SSM_STATE].astype(BF16)
                cb = cb_ref[rows, gs]
                st = st_ref[:, cs]
                y_off = _dot(cm, st.astype(BF16)) * jnp.exp(acs_x).reshape(SSM_CHUNK, MXU_COLS)
                st_ref[:, cs] = ((st.reshape(groups, SUBLANES, MXU_COLS) * jnp.exp(end_x)[None])
                                 .reshape(SSM_STATE, MXU_COLS) + _dot(bmt_ref[rows, gs], xdec))
                for lo in range(0, MXU_COLS, LANES):
                    h0 = (c * SSM_HEADS + (c0 + lo) // SSM_HEAD_DIM) * SUBLANES
                    slab = acs_x[:, :, lo:lo + LANES]
                    rolled = pltpu.roll(slab, SSM_HEAD_DIM, axis=2)
                    a0 = jnp.where(first_half, slab, rolled) - acst_ref[h0:h0 + SUBLANES, :][None]
                    a1 = jnp.where(first_half, rolled, slab) - acst_ref[h0 + SUBLANES:h0 + 2 * SUBLANES, :][None]
                    l0 = jnp.exp(a0 + causal_bias).reshape(SSM_CHUNK, SSM_CHUNK)
                    l1 = jnp.exp(a1 + causal_bias).reshape(SSM_CHUNK, SSM_CHUNK)
                    m = jnp.concatenate([(cb * l0).astype(BF16), (cb * l1).astype(BF16)], axis=1)
                    xp = xdt_c[:, lo:lo + LANES]
                    rhs = jnp.concatenate([jnp.where(half_rows, xp, 0.0), jnp.where(half_rows, 0.0, xp)],
                                          axis=0).astype(BF16)
                    y_ref[rows, c0 + lo:c0 + lo + LANES] = _dot(m, rhs) + y_off[:, lo:lo + LANES]

        for k in range(SSM_CH // MXU_COLS):
            first_stage(sched[5])
            c0 = k * MXU_COLS
            cs = slice(c0, c0 + MXU_COLS)
            y_ref[:, cs] = (y_ref[:, cs] + xs_ref[:, cs] * dexp_ref[:, cs]) * _silu(proj(COL_Z + c0, MXU_COLS))
        first_stage(sched[6])
        m = _mix_out(yc_ref[...], _ssm_group_norm(y_ref[...], snw_ref[...]), w_out_ref)
        x1_ref[...] = _layer_norm(ALPHA * x + (1.0 + g1) * m, ln_g_ref[...], ln_b_ref[...])
        first_stage(IN_PAD // MXU_COLS)

    @pl.when(lax.rem(s, 2) == 0)
    def _():
        stages(p0, xk0, p1, xk1)

    @pl.when(lax.rem(s, 2) == 1)
    def _():
        stages(p1, xk1, p0, xk0)

    @pl.when((jb == tiles_per_seq - 1) & (s > 0))
    def _():
        cst_ref[...] = cbuf[SUBLANES - 2:SUBLANES, :]
        scst_ref[...] = xbuf[SUBLANES - 3:SUBLANES, :]
        sst_ref[...] = st_ref[...].T


def _const_spec(shape):
    return pl.BlockSpec(shape, lambda *_: (0,) * len(shape), pipeline_mode=pl.Buffered(1))


def _mixer_prompt(x, mod, w_in, conv_w, conv_nw, sconv_w, sconv_b, dtb, alog, dexp, snw, w_out, ln_g, ln_b,
                  tile=256, sched=(1, 0, 0, 2, 1, 0, 0)):
    assert CONV_GROUP == SSM_HEAD_DIM and CONV_CH == SSM_CH
    nb, seq, _ = x.shape
    tiles_per_seq = seq // tile
    n_tiles = nb * tiles_per_seq
    kern = functools.partial(_mixer_prompt_kernel, tile=tile, tiles_per_seq=tiles_per_seq, sched=sched)
    consts = [w_in, _head_expand(), _group_reduce(), conv_w, conv_nw, sconv_w, sconv_b, dtb, alog, dexp, snw,
              w_out, ln_g, ln_b]
    first = lambda s: jnp.minimum(s, n_tiles - 1)
    second = lambda s: jnp.maximum(s - 1, 0)
    return pl.pallas_call(
        kern,
        grid=(n_tiles + 1,),
        in_specs=[pl.BlockSpec((tile, D_MODEL), lambda s: (first(s), 0)),
                  pl.BlockSpec((None, 1, 6 * D_MODEL), lambda s: (first(s) // tiles_per_seq, 0, 0)),
                  pl.BlockSpec((None, 1, 6 * D_MODEL), lambda s: (second(s) // tiles_per_seq, 0, 0))]
                 + [_const_spec(a.shape) for a in consts],
        out_specs=[pl.BlockSpec((tile, D_MODEL), lambda s: (second(s), 0)),
                   pl.BlockSpec((None, 2, CONV_CH), lambda s: (second(s) // tiles_per_seq, 0, 0)),
                   pl.BlockSpec((None, 3, XBC_CH), lambda s: (second(s) // tiles_per_seq, 0, 0)),
                   pl.BlockSpec((None, SSM_CH, SSM_STATE), lambda s: (second(s) // tiles_per_seq, 0, 0))],
        out_shape=[jax.ShapeDtypeStruct((nb * seq, D_MODEL), F32),
                   jax.ShapeDtypeStruct((nb, 2, CONV_CH), F32),
                   jax.ShapeDtypeStruct((nb, 3, XBC_CH), F32),
                   jax.ShapeDtypeStruct((nb, SSM_CH, SSM_STATE), F32)],
        scratch_shapes=[pltpu.VMEM((tile, IN_PAD), F32),
                        pltpu.VMEM((tile, IN_PAD), F32),
                        pltpu.VMEM((tile, D_MODEL), F32),
                        pltpu.VMEM((tile, D_MODEL), F32),
                        pltpu.VMEM((SUBLANES, CONV_CH), F32),
                        pltpu.VMEM((SUBLANES, XBC_CH), F32),
                        pltpu.VMEM((SSM_STATE, SSM_CH), F32),
                        pltpu.VMEM((tile, SSM_CH), F32),
                        pltpu.VMEM((tile, 2 * SSM_GROUPS * SSM_STATE), F32),
                        pltpu.VMEM((tile, SSM_CH), F32),
                        pltpu.VMEM((tile, SSM_CH), F32),
                        pltpu.VMEM((tile // SSM_CHUNK * SUBLANES, SSM_CH), F32),
                        pltpu.VMEM((tile // SSM_CHUNK * SSM_HEADS * SUBLANES, SSM_CHUNK), F32),
                        pltpu.VMEM((tile, SSM_GROUPS * SSM_STATE), F32),
                        pltpu.VMEM((tile, SSM_GROUPS * SSM_STATE), BF16),
                        pltpu.VMEM((tile, SSM_CH), F32),
                        pltpu.VMEM((tile, CONV_CH), BF16)],
        compiler_params=pltpu.CompilerParams(dimension_semantics=("arbitrary",),
                                             vmem_limit_bytes=VMEM_LIMIT),
        name="mixer_prompt",
    )(x.reshape(nb * seq, D_MODEL), mod, mod, *consts)


def _ffn_kernel(x_ref, mod_ref, w_up_ref, w_down_ref, ln_g_ref, ln_b_ref, o_ref, *, ff_tile):
    x = x_ref[...]
    sh2 = mod_ref[:, 3 * D_MODEL:4 * D_MODEL]
    sc2 = mod_ref[:, 4 * D_MODEL:5 * D_MODEL]
    g2 = mod_ref[:, 5 * D_MODEL:6 * D_MODEL]
    v = (x * (1.0 + sc2) + sh2).astype(BF16)
    acc = jnp.zeros(x.shape, F32)
    for k in range(D_FF // ff_tile):
        h = jnp.maximum(_dot(v, w_up_ref[:, k * ff_tile:(k + 1) * ff_tile]), 0.0)
        acc = acc + _dot((h * h).astype(BF16), w_down_ref[k * ff_tile:(k + 1) * ff_tile, :])
    o_ref[...] = _layer_norm(ALPHA * x + (1.0 + g2) * acc, ln_g_ref[...], ln_b_ref[...])


def _ffn(x, mod, rows_per_mod, w_up, w_down, ln_g, ln_b, tile, ff_tile=1024):
    rows = x.shape[0]
    mod_rows = mod.shape[1]
    tiles_per_mod = rows_per_mod // tile
    kern = functools.partial(_ffn_kernel, ff_tile=ff_tile)
    return pl.pallas_call(
        kern,
        grid=(rows // tile,),
        in_specs=[pl.BlockSpec((tile, D_MODEL), lambda i: (i, 0)),
                  pl.BlockSpec((None, mod_rows, 6 * D_MODEL), lambda i: (i // tiles_per_mod, 0, 0)),
                  _const_spec(w_up.shape), _const_spec(w_down.shape),
                  _const_spec(ln_g.shape), _const_spec(ln_b.shape)],
        out_specs=pl.BlockSpec((tile, D_MODEL), lambda i: (i, 0)),
        out_shape=jax.ShapeDtypeStruct((rows, D_MODEL), F32),
        compiler_params=pltpu.CompilerParams(dimension_semantics=("arbitrary",),
                                             vmem_limit_bytes=VMEM_LIMIT),
        name="ffn",
    )(x, mod, w_up, w_down, ln_g, ln_b)


def _sample_pre_kernel(x_ref, mod_ref, w_in_ref, conv_w_ref, conv_nw_ref, sconv_w_ref, sconv_b_ref,
                       dtb_ref, alog_ref, cb0_ref, cb1_ref, sb0_ref, sb1_ref, sb2_ref,
                       yconv_ref, ch_ref, xbc_ref, z_ref, xs_ref, ydiag_ref, bc_ref, xdt_t_ref, dec_t_ref):
    expand = _head_expand()
    reduce = _group_reduce()
    x = x_ref[...]
    sh1 = mod_ref[:, 0:D_MODEL]
    sc1 = mod_ref[:, D_MODEL:2 * D_MODEL]
    u = (x * (1.0 + sc1) + sh1).astype(BF16)

    def proj(lo, width):
        return _dot(u, w_in_ref[:, lo:lo + width])

    ch = proj(COL_GC, CONV_CH) * proj(COL_HV, CONV_CH)
    ch_ref[...] = ch
    cw = conv_w_ref[...]
    cv = cw[0:1, :] * cb0_ref[...] + cw[1:2, :] * cb1_ref[...] + cw[2:3, :] * ch
    yconv_ref[...] = _conv_group_norm(proj(COL_GB, CONV_CH) * cv, conv_nw_ref[...], expand, reduce)

    xbc = proj(COL_XBC, XBC_CH)
    xbc_ref[...] = xbc
    sw = sconv_w_ref[...]
    xc = _silu(sw[0:1, :] * sb0_ref[...] + sw[1:2, :] * sb1_ref[...] + sw[2:3, :] * sb2_ref[...]
               + sw[3:4, :] * xbc + sconv_b_ref[...])
    xs = xc[:, 0:SSM_CH]
    xs_ref[...] = xs
    bc_ref[...] = xc[:, SSM_CH:XBC_CH]
    z_ref[...] = proj(COL_Z, SSM_CH)

    dt = _softplus(proj(COL_DT, LANES) + dtb_ref[...])
    dta = dt * (-jnp.exp(alog_ref[...]))
    xdt = xs * _dot_f32_lhs(dt, expand)
    dec_t_ref[...] = jnp.exp(_dot_f32_lhs(dta, expand)).T
    xdt_t_ref[...] = xdt.T
    for g in range(SSM_GROUPS):
        bm = xc[:, SSM_CH + g * SSM_STATE:SSM_CH + (g + 1) * SSM_STATE]
        cm = xc[:, SSM_CH + (SSM_GROUPS + g) * SSM_STATE:SSM_CH + (SSM_GROUPS + g + 1) * SSM_STATE]
        cb = jnp.sum(cm * bm, axis=-1, keepdims=True)
        gl = g * SSM_GROUP_CH
        ydiag_ref[:, gl:gl + SSM_GROUP_CH] = cb * xdt[:, gl:gl + SSM_GROUP_CH]


def _sample_pre(x, mod, w_in, conv_w, conv_nw, sconv_w, sconv_b, dtb, alog, cb0, cb1, sb0, sb1, sb2):
    n = x.shape[0]
    args = (x, mod, w_in, conv_w, conv_nw, sconv_w, sconv_b, dtb, alog, cb0, cb1, sb0, sb1, sb2)
    out_shapes = [(n, CONV_CH), (n, CONV_CH), (n, XBC_CH), (n, SSM_CH), (n, SSM_CH), (n, SSM_CH),
                  (n, 2 * SSM_GROUPS * SSM_STATE), (SSM_CH, n), (SSM_CH, n)]
    return pl.pallas_call(
        _sample_pre_kernel,
        out_shape=[jax.ShapeDtypeStruct(s, F32) for s in out_shapes],
        compiler_params=pltpu.CompilerParams(vmem_limit_bytes=VMEM_LIMIT),
        name="sample_pre",
    )(*args)


def _sample_state_kernel(s_ref, xdt_t_ref, dec_t_ref, bc_ref, o_ref, yt_ref, *, block):
    i = pl.program_id(0)

    @pl.when(i == 0)
    def _():
        yt_ref[...] = jnp.zeros_like(yt_ref)

    lane = lax.broadcasted_iota(jnp.int32, yt_ref.shape, 1)

    def body(k, carry):
        b = i * block + k
        s = s_ref[k]
        here = lane == b
        xcol = jnp.sum(jnp.where(here, xdt_t_ref[...], 0.0), axis=1, keepdims=True)
        dcol = jnp.sum(jnp.where(here, dec_t_ref[...], 0.0), axis=1, keepdims=True)
        bc = bc_ref[pl.ds(b, 1), :]
        ycols, outer = [], []
        for g in range(SSM_GROUPS):
            gl = g * SSM_GROUP_CH
            bm = bc[:, g * SSM_STATE:(g + 1) * SSM_STATE]
            cm = bc[:, (SSM_GROUPS + g) * SSM_STATE:(SSM_GROUPS + g + 1) * SSM_STATE]
            ycols.append(jnp.sum(s[gl:gl + SSM_GROUP_CH, :] * cm, axis=1, keepdims=True))
            outer.append(xcol[gl:gl + SSM_GROUP_CH, :] * bm)
        ycol = jnp.concatenate(ycols, axis=0) * dcol
        yt_ref[...] = jnp.where(here, ycol, yt_ref[...])
        o_ref[k] = s * dcol + jnp.concatenate(outer, axis=0)
        return carry

    lax.fori_loop(0, block, body, 0)


def _sample_state(state, xdt_t, dec_t, bc, block=8):
    n = state.shape[0]
    kern = functools.partial(_sample_state_kernel, block=block)
    return pl.pallas_call(
        kern,
        grid=(n // block,),
        in_specs=[pl.BlockSpec((block, SSM_CH, SSM_STATE), lambda i: (i, 0, 0)),
                  _const_spec(xdt_t.shape), _const_spec(dec_t.shape), _const_spec(bc.shape)],
        out_specs=[pl.BlockSpec((block, SSM_CH, SSM_STATE), lambda i: (i, 0, 0)),
                   pl.BlockSpec((SSM_CH, n), lambda i: (0, 0))],
        out_shape=[jax.ShapeDtypeStruct(state.shape, F32), jax.ShapeDtypeStruct((SSM_CH, n), F32)],
        compiler_params=pltpu.CompilerParams(dimension_semantics=("arbitrary",),
                                             vmem_limit_bytes=VMEM_LIMIT),
        name="sample_state",
    )(state, xdt_t, dec_t, bc)


def _sample_post_kernel(x_ref, mod_ref, yconv_ref, ydiag_ref, yoff_t_ref, xs_ref, z_ref, dexp_ref, snw_ref,
                        w_out_ref, ln_g_ref, ln_b_ref, x1_ref):
    g1 = mod_ref[:, 2 * D_MODEL:3 * D_MODEL]
    y = ydiag_ref[...] + yoff_t_ref[...].T + xs_ref[...] * dexp_ref[...]
    y = y * _silu(z_ref[...])
    m = _mix_out(yconv_ref[...], _ssm_group_norm(y, snw_ref[...]), w_out_ref)
    x1_ref[...] = _layer_norm(ALPHA * x_ref[...] + (1.0 + g1) * m, ln_g_ref[...], ln_b_ref[...])


def _sample_post(x, mod, yconv, ydiag, yoff_t, xs, z, dexp, snw, w_out, ln_g, ln_b):
    return pl.pallas_call(
        _sample_post_kernel,
        out_shape=jax.ShapeDtypeStruct(x.shape, F32),
        compiler_params=pltpu.CompilerParams(vmem_limit_bytes=VMEM_LIMIT),
        name="sample_post",
    )(x, mod, yconv, ydiag, yoff_t, xs, z, dexp, snw, w_out, ln_g, ln_b)


def kernel(x_prompt, x_sample, state_conv, state_ssm_conv, state_ssm, c_prompt, c_sample, w_ada, b_ada, w_in, conv_w, conv_norm_w, ssm_conv_w, ssm_conv_b, dt_bias, a_log, d_skip, ssm_norm_w, w_out, ln1_g, ln1_b, w_up, w_down, ln2_g, ln2_b):
    assert w_ada.shape[0] == 1, "single-layer trunk"
    nb, seq, _ = x_prompt.shape
    ns = x_sample.shape[0]
    row = lambda a: a.reshape(1, -1)
    pad_heads = lambda a: jnp.pad(a.reshape(1, -1), ((0, 0), (0, LANES - SSM_HEADS)))

    w_in_b = jnp.pad(w_in[0], ((0, 0), (0, IN_PAD - IN_COLS))).astype(BF16)
    w_out_b = w_out[0].astype(BF16)
    w_up_b = w_up[0].astype(BF16)
    w_down_b = w_down[0].astype(BF16)
    conv_nw, sconv_b, snw = row(conv_norm_w[0]), row(ssm_conv_b[0]), row(ssm_norm_w[0])
    dtb, alog = pad_heads(dt_bias[0]), pad_heads(a_log[0])
    dexp = row(jnp.repeat(d_skip[0], SSM_HEAD_DIM))
    g1, b1, g2, b2 = row(ln1_g[0]), row(ln1_b[0]), row(ln2_g[0]), row(ln2_b[0])

    mod = _ada(jnp.concatenate([c_prompt, c_sample], axis=0), w_ada[0], row(b_ada[0]))
    mod_p = mod[:nb].reshape(nb, 1, 6 * D_MODEL)
    mod_s = mod[nb:].reshape(1, ns, 6 * D_MODEL)

    x1_p, cst_p, scst_p, sst_p = _mixer_prompt(x_prompt, mod_p, w_in_b, conv_w[0], conv_nw, ssm_conv_w[0],
                                               sconv_b, dtb, alog, dexp, snw, w_out_b, g1, b1)
    y_p = _ffn(x1_p, mod_p, seq, w_up_b, w_down_b, g2, b2, tile=512)

    xs2 = x_sample.reshape(ns, D_MODEL)
    (yconv_s, ch_s, xbc_s, z_s, xs_s, ydiag_s, bc_s, xdt_t, dec_t) = _sample_pre(
        xs2, mod_s[0], w_in_b, conv_w[0], conv_nw, ssm_conv_w[0], sconv_b, dtb, alog,
        state_conv[0, :, 0], state_conv[0, :, 1],
        state_ssm_conv[0, :, 0], state_ssm_conv[0, :, 1], state_ssm_conv[0, :, 2])
    new_state_s, yoff_t = _sample_state(state_ssm[0].reshape(ns, SSM_CH, SSM_STATE), xdt_t, dec_t, bc_s)
    x1_s = _sample_post(xs2, mod_s[0], yconv_s, ydiag_s, yoff_t, xs_s, z_s, dexp, snw, w_out_b, g1, b1)
    y_s = _ffn(x1_s, mod_s, ns, w_up_b, w_down_b, g2, b2, tile=ns)

    return (y_p.reshape(nb, seq, D_MODEL),
            y_s.reshape(ns, 1, D_MODEL),
            cst_p[None],
            scst_p[None],
            sst_p.reshape(1, nb, SSM_HEADS, SSM_HEAD_DIM, SSM_STATE),
            jnp.stack([state_conv[0, :, 1], ch_s], axis=1)[None],
            jnp.stack([state_ssm_conv[0, :, 1], state_ssm_conv[0, :, 2], xbc_s], axis=1)[None],
            new_state_s.reshape(1, ns, SSM_HEADS, SSM_HEAD_DIM, SSM_STATE))
```

```python
import functools

import jax
import jax.numpy as jnp
from jax import lax
from jax.experimental import pallas as pl
from jax.experimental.pallas import tpu as pltpu

F32 = jnp.float32
BF16 = jnp.bfloat16

D_MODEL = 1024
CONV_CH = 1024
CONV_GROUP = 64
SSM_CH = 1024
SSM_HEADS = 16
SSM_HEAD_DIM = 64
SSM_GROUPS = 2
SSM_GROUP_CH = SSM_CH // SSM_GROUPS
SSM_STATE = 128
SSM_CHUNK = 128
XBC_CH = SSM_CH + 2 * SSM_GROUPS * SSM_STATE
D_FF = 4 * D_MODEL
LANES = 128
SUBLANES = 8
MXU_COLS = 256
COL_GB, COL_GC, COL_HV, COL_Z, COL_XBC = 0, 1024, 2048, 3072, 4096
COL_DT = COL_XBC + XBC_CH
IN_COLS = COL_DT + SSM_HEADS
IN_PAD = COL_DT + LANES
ALPHA = 2.0 ** 0.25
LN_EPS = 1e-5
RMS_EPS = 1e-5
VMEM_LIMIT = 56 * 1024 * 1024


def _dot(a, b):
    return jnp.dot(a, b, preferred_element_type=F32)


def _split(a, terms):
    parts = []
    r = a
    for t in range(terms):
        p = r.astype(BF16)
        parts.append(p)
        if t + 1 < terms:
            r = r - p.astype(F32)
    return parts


def _dot_f32_lhs(a, b_exact, terms=3):
    parts = _split(a, terms)
    out = _dot(parts[0], b_exact)
    for p in parts[1:]:
        out = out + _dot(p, b_exact)
    return out


def _dot_f32_rhs(a_exact, b, terms=3):
    parts = _split(b, terms)
    out = _dot(a_exact, parts[0])
    for p in parts[1:]:
        out = out + _dot(a_exact, p)
    return out


def _head_expand():
    h = lax.broadcasted_iota(jnp.int32, (LANES, SSM_CH), 0)
    c = lax.broadcasted_iota(jnp.int32, (LANES, SSM_CH), 1)
    return (c // SSM_HEAD_DIM == h).astype(BF16)


def _group_reduce():
    c = lax.broadcasted_iota(jnp.int32, (CONV_CH, LANES), 0)
    k = lax.broadcasted_iota(jnp.int32, (CONV_CH, LANES), 1)
    return (c // CONV_GROUP == k).astype(BF16)


def _sigmoid(x):
    return 1.0 / (1.0 + jnp.exp(-x))


def _silu(x):
    return x * _sigmoid(x)


def _softplus(x):
    return jnp.maximum(x, 0.0) + jnp.log1p(jnp.exp(-jnp.abs(x)))


def _layer_norm(r, g, b):
    mu = jnp.mean(r, axis=-1, keepdims=True)
    d = r - mu
    var = jnp.mean(d * d, axis=-1, keepdims=True)
    return d * lax.rsqrt(var + LN_EPS) * g + b


def _conv_group_norm(prod, w, expand, reduce):
    ssum = _dot_f32_lhs(prod * prod, reduce, terms=2)
    rstd = lax.rsqrt(ssum * (1.0 / CONV_GROUP) + RMS_EPS)
    return prod * _dot_f32_lhs(rstd, expand, terms=2) * w


def _ssm_group_norm(y, w):
    outs = []
    for g in range(SSM_GROUPS):
        yg = y[:, g * SSM_GROUP_CH:(g + 1) * SSM_GROUP_CH]
        ms = jnp.mean(yg * yg, axis=-1, keepdims=True)
        outs.append((yg * lax.rsqrt(ms + RMS_EPS) * w[:, g * SSM_GROUP_CH:(g + 1) * SSM_GROUP_CH]).astype(BF16))
    return outs


def _mix_out(y_conv, y_ssm_groups, w_out_ref):
    m = _dot(y_conv.astype(BF16), w_out_ref[0:CONV_CH, :])
    for g, yg in enumerate(y_ssm_groups):
        lo = CONV_CH + g * SSM_GROUP_CH
        m = m + _dot(yg, w_out_ref[lo:lo + SSM_GROUP_CH, :])
    return m


def _ada_kernel(c_ref, w_ref, b_ref, op_ref, os_ref):
    c = c_ref[...]
    w = w_ref[...]
    c_hi = c.astype(BF16)
    c_lo = (c - c_hi.astype(F32)).astype(BF16)
    w_hi = w.astype(BF16)
    w_lo = (w - w_hi.astype(F32)).astype(BF16)
    mod = _dot(c_hi, w_hi) + _dot(c_hi, w_lo) + _dot(c_lo, w_hi) + b_ref[...]
    n_sample = os_ref.shape[0]
    os_ref[...] = mod[0:n_sample, :]
    op_ref[...] = mod[n_sample:, :]


def _ada(c_sample, c_prompt, w_ada, b_ada, tile_n=1024):
    ns, nb = c_sample.shape[0], c_prompt.shape[0]
    n = w_ada.shape[1]
    return pl.pallas_call(
        _ada_kernel,
        grid=(n // tile_n,),
        in_specs=[pl.BlockSpec((ns + nb, D_MODEL), lambda i: (0, 0)),
                  pl.BlockSpec((D_MODEL, tile_n), lambda i: (0, i)),
                  pl.BlockSpec((1, tile_n), lambda i: (0, i))],
        out_specs=[pl.BlockSpec((nb, tile_n), lambda i: (0, i)),
                   pl.BlockSpec((ns, tile_n), lambda i: (0, i))],
        out_shape=[jax.ShapeDtypeStruct((nb, n), F32), jax.ShapeDtypeStruct((ns, n), F32)],
        name="ada_mod",
    )(jnp.concatenate([c_sample, c_prompt], axis=0), w_ada, b_ada)


def _mixer_prompt_kernel(xa_ref, moda_ref, modb_ref, w_in_ref, w_dt_ref, expand_ref, reduce_ref,
                         conv_w_ref, conv_nw_ref, sconv_w_ref, sconv_b_ref,
                         dtb_ref, alog_ref, dexp_ref, snw_ref, w_out_ref, ln_g_ref, ln_b_ref,
                         x1_ref, cst_ref, scst_ref, sst_ref,
                         p0, p1, xk0, xk1, cbuf, xbuf, st_ref, xs_ref, bc_ref, dtx_ref, acsx_ref, endx_ref,
                         acst_ref, cb_ref, bmt_ref, y_ref, yc_ref,
                         *, tile, tiles_per_seq, sched):
    s = pl.program_id(0)
    jb = lax.rem(s + (tiles_per_seq - 1), tiles_per_seq)

    @pl.when(s == 0)
    def _():
        p1[...] = jnp.zeros_like(p1)
        xk1[...] = jnp.zeros_like(xk1)

    @pl.when((jb == 0) | (s == 0))
    def _():
        cbuf[...] = jnp.zeros_like(cbuf)
        xbuf[...] = jnp.zeros_like(xbuf)
        st_ref[...] = jnp.zeros_like(st_ref)

    def stages(pa, xka, pb, xkb):
        xa = xa_ref[...]
        xka[...] = xa
        u = (xa * (1.0 + moda_ref[:, D_MODEL:2 * D_MODEL]) + moda_ref[:, 0:D_MODEL]).astype(BF16)
        pieces = iter(list(range(0, COL_DT, MXU_COLS)) + [COL_DT])

        def first_stage(n):
            for _ in range(n):
                lo = next(pieces, None)
                if lo == COL_DT:
                    pa[:, COL_DT:IN_PAD] = _dot(u, w_dt_ref[...])
                elif lo is not None:
                    pa[:, lo:lo + MXU_COLS] = _dot(u, w_in_ref[:, lo:lo + MXU_COLS])

        expand = expand_ref[...]
        x = xkb[...]
        g1 = modb_ref[:, 2 * D_MODEL:3 * D_MODEL]

        def proj(lo, width):
            return pb[:, lo:lo + width]

        def delayed(tail_ref, cs, cur, taps):
            seq = jnp.concatenate([tail_ref[:, cs], cur], axis=0)
            tail_ref[:, cs] = cur[tile - SUBLANES:, :]
            return [pltpu.roll(seq, k, axis=0)[SUBLANES:, :] for k in range(1, taps + 1)]

        for k in range(CONV_CH // MXU_COLS):
            first_stage(sched[0])
            c0 = k * MXU_COLS
            cs = slice(c0, c0 + MXU_COLS)
            ch = proj(COL_GC + c0, MXU_COLS) * proj(COL_HV + c0, MXU_COLS)
            ch1, ch2 = delayed(cbuf, cs, ch, 2)
            cv = conv_w_ref[0:1, cs] * ch2 + conv_w_ref[1:2, cs] * ch1 + conv_w_ref[2:3, cs] * ch
            prod = proj(COL_GB + c0, MXU_COLS) * cv
            ssum = _dot_f32_lhs(prod * prod, reduce_ref[cs, :], terms=2)
            rstd = lax.rsqrt(ssum * (1.0 / CONV_GROUP) + RMS_EPS)
            yc_ref[:, cs] = (prod * _dot_f32_lhs(rstd, expand_ref[:, cs], terms=2)
                             * conv_nw_ref[:, cs]).astype(BF16)

        def pre_conv(c0):
            cs = slice(c0, c0 + MXU_COLS)
            xbc = proj(COL_XBC + c0, MXU_COLS)
            x1, x2, x3 = delayed(xbuf, cs, xbc, 3)
            return _silu(sconv_w_ref[0:1, cs] * x3 + sconv_w_ref[1:2, cs] * x2 + sconv_w_ref[2:3, cs] * x1
                         + sconv_w_ref[3:4, cs] * xbc + sconv_b_ref[:, cs])

        row = lax.broadcasted_iota(jnp.int32, (SSM_CHUNK, SSM_CHUNK), 0)
        col = lax.broadcasted_iota(jnp.int32, (SSM_CHUNK, SSM_CHUNK), 1)
        causal = row >= col
        tri = causal.astype(BF16)
        groups = SSM_CHUNK // SUBLANES
        causal_bias = jnp.where(causal, 0.0, -jnp.inf).reshape(groups, SUBLANES, SSM_CHUNK)
        first_half = (col < SSM_HEAD_DIM).reshape(groups, SUBLANES, SSM_CHUNK)
        half_rows = col < SSM_HEAD_DIM
        chunks = [slice(c * SSM_CHUNK, (c + 1) * SSM_CHUNK) for c in range(tile // SSM_CHUNK)]

        first_stage(sched[1])
        dt = _softplus(proj(COL_DT, LANES) + dtb_ref[...])
        dta = dt * (-jnp.exp(alog_ref[...]))
        dtx_ref[...] = _dot_f32_lhs(dt, expand, terms=2)
        for c, rows in enumerate(chunks):
            acs = _dot_f32_rhs(tri, dta[rows, :])
            acs_t = acs.T
            for h in range(SSM_HEADS):
                r8 = (c * SSM_HEADS + h) * SUBLANES
                acst_ref[r8:r8 + SUBLANES, :] = jnp.broadcast_to(acs_t[h:h + 1, :], (SUBLANES, SSM_CHUNK))
            acs_x = _dot_f32_lhs(acs, expand)
            acsx_ref[rows, :] = acs_x
            endx_ref[c * SUBLANES:(c + 1) * SUBLANES, :] = jnp.broadcast_to(acs_x[SSM_CHUNK - 1:SSM_CHUNK, :],
                                                                             (SUBLANES, SSM_CH))
        for c0 in range(SSM_CH, XBC_CH, MXU_COLS):
            first_stage(sched[2])
            bc_ref[:, c0 - SSM_CH:c0 - SSM_CH + MXU_COLS] = pre_conv(c0)
        for rows in chunks:
            for g in range(SSM_GROUPS):
                gs = slice(g * SSM_STATE, (g + 1) * SSM_STATE)
                bm = bc_ref[rows, gs]
                cm = bc_ref[rows, (SSM_GROUPS + g) * SSM_STATE:(SSM_GROUPS + g + 1) * SSM_STATE]
                cb_ref[rows, gs] = lax.dot_general(cm.astype(BF16), bm.astype(BF16), (((1,), (1,)), ((), ())),
                                                   preferred_element_type=F32)
                bmt_ref[rows, gs] = bm.T.astype(BF16)

        for c0 in range(0, SSM_CH, MXU_COLS):
            first_stage(sched[3])
            cs = slice(c0, c0 + MXU_COLS)
            g = c0 // SSM_GROUP_CH
            gs = slice(g * SSM_STATE, (g + 1) * SSM_STATE)
            xs = pre_conv(c0)
            xs_ref[:, cs] = xs
            xdt = xs * dtx_ref[:, cs]
            for c, rows in enumerate(chunks):
                first_stage(sched[4])
                acs_x = acsx_ref[rows, cs].reshape(groups, SUBLANES, MXU_COLS)
                end_x = endx_ref[c * SUBLANES:(c + 1) * SUBLANES, cs]
                xdt_c = xdt[rows, :]
                xdec = (xdt_c * jnp.exp(end_x[None] - acs_x).reshape(SSM_CHUNK, MXU_COLS)).astype(BF16)
                cm = bc_ref[rows, (SSM_GROUPS + g) * SSM_STATE:(SSM_GROUPS + g + 1) * SSM_STATE].astype(BF16)
                cb = cb_ref[rows, gs]
                st = st_ref[:, cs]
                y_off = _dot(cm, st.astype(BF16)) * jnp.exp(acs_x).reshape(SSM_CHUNK, MXU_COLS)
                st_ref[:, cs] = ((st.reshape(groups, SUBLANES, MXU_COLS) * jnp.exp(end_x)[None])
                                 .reshape(SSM_STATE, MXU_COLS) + _dot(bmt_ref[rows, gs], xdec))
                for lo in range(0, MXU_COLS, LANES):
                    h0 = (c * SSM_HEADS + (c0 + lo) // SSM_HEAD_DIM) * SUBLANES
                    slab = acs_x[:, :, lo:lo + LANES]
                    rolled = pltpu.roll(slab, SSM_HEAD_DIM, axis=2)
                    a0 = jnp.where(first_half, slab, rolled) - acst_ref[h0:h0 + SUBLANES, :][None]
                    a1 = jnp.where(first_half, rolled, slab) - acst_ref[h0 + SUBLANES:h0 + 2 * SUBLANES, :][None]
                    l0 = jnp.exp(a0 + causal_bias).reshape(SSM_CHUNK, SSM_CHUNK)
                    l1 = jnp.exp(a1 + causal_bias).reshape(SSM_CHUNK, SSM_CHUNK)
                    m = jnp.concatenate([(cb * l0).astype(BF16), (cb * l1).astype(BF16)], axis=1)
                    xp = xdt_c[:, lo:lo + LANES]
                    rhs = jnp.concatenate([jnp.where(half_rows, xp, 0.0), jnp.where(half_rows, 0.0, xp)],
                                          axis=0).astype(BF16)
                    y_ref[rows, c0 + lo:c0 + lo + LANES] = _dot(m, rhs) + y_off[:, lo:lo + LANES]

        for k in range(SSM_CH // MXU_COLS):
            first_stage(sched[5])
            c0 = k * MXU_COLS
            cs = slice(c0, c0 + MXU_COLS)
            y_ref[:, cs] = (y_ref[:, cs] + xs_ref[:, cs] * dexp_ref[:, cs]) * _silu(proj(COL_Z + c0, MXU_COLS))
        first_stage(sched[6])
        m = _mix_out(yc_ref[...], _ssm_group_norm(y_ref[...], snw_ref[...]), w_out_ref)
        x1_ref[...] = _layer_norm(ALPHA * x + (1.0 + g1) * m, ln_g_ref[...], ln_b_ref[...])
        first_stage(IN_PAD // MXU_COLS)

    @pl.when(lax.rem(s, 2) == 0)
    def _():
        stages(p0, xk0, p1, xk1)

    @pl.when(lax.rem(s, 2) == 1)
    def _():
        stages(p1, xk1, p0, xk0)

    @pl.when((jb == tiles_per_seq - 1) & (s > 0))
    def _():
        cst_ref[...] = cbuf[SUBLANES - 2:SUBLANES, :]
        scst_ref[...] = xbuf[SUBLANES - 3:SUBLANES, :]
        sst_ref[...] = st_ref[...].T


def _const_spec(shape):
    return pl.BlockSpec(shape, lambda *_: (0,) * len(shape), pipeline_mode=pl.Buffered(1))


def _mixer_prompt(x, mod, w_in, w_dt, conv_w, conv_nw, sconv_w, sconv_b, dtb, alog, dexp, snw, w_out, ln_g, ln_b,
                  tile=256, sched=(1, 0, 0, 2, 1, 0, 0)):
    assert CONV_GROUP == SSM_HEAD_DIM and CONV_CH == SSM_CH
    nb, seq, _ = x.shape
    tiles_per_seq = seq // tile
    n_tiles = nb * tiles_per_seq
    kern = functools.partial(_mixer_prompt_kernel, tile=tile, tiles_per_seq=tiles_per_seq, sched=sched)
    consts = [w_in, w_dt, _head_expand(), _group_reduce(), conv_w, conv_nw, sconv_w, sconv_b, dtb, alog, dexp, snw,
              w_out, ln_g, ln_b]
    first = lambda s: jnp.minimum(s, n_tiles - 1)
    second = lambda s: jnp.maximum(s - 1, 0)
    return pl.pallas_call(
        kern,
        grid=(n_tiles + 1,),
        in_specs=[pl.BlockSpec((tile, D_MODEL), lambda s: (first(s), 0)),
                  pl.BlockSpec((None, 1, 6 * D_MODEL), lambda s: (first(s) // tiles_per_seq, 0, 0)),
                  pl.BlockSpec((None, 1, 6 * D_MODEL), lambda s: (second(s) // tiles_per_seq, 0, 0))]
                 + [_const_spec(a.shape) for a in consts],
        out_specs=[pl.BlockSpec((tile, D_MODEL), lambda s: (second(s), 0)),
                   pl.BlockSpec((None, 2, CONV_CH), lambda s: (second(s) // tiles_per_seq, 0, 0)),
                   pl.BlockSpec((None, 3, XBC_CH), lambda s: (second(s) // tiles_per_seq, 0, 0)),
                   pl.BlockSpec((None, SSM_CH, SSM_STATE), lambda s: (second(s) // tiles_per_seq, 0, 0))],
        out_shape=[jax.ShapeDtypeStruct((nb * seq, D_MODEL), F32),
                   jax.ShapeDtypeStruct((nb, 2, CONV_CH), F32),
                   jax.ShapeDtypeStruct((nb, 3, XBC_CH), F32),
                   jax.ShapeDtypeStruct((nb, SSM_CH, SSM_STATE), F32)],
        scratch_shapes=[pltpu.VMEM((tile, IN_PAD), F32),
                        pltpu.VMEM((tile, IN_PAD), F32),
                        pltpu.VMEM((tile, D_MODEL), F32),
                        pltpu.VMEM((tile, D_MODEL), F32),
                        pltpu.VMEM((SUBLANES, CONV_CH), F32),
                        pltpu.VMEM((SUBLANES, XBC_CH), F32),
                        pltpu.VMEM((SSM_STATE, SSM_CH), F32),
                        pltpu.VMEM((tile, SSM_CH), F32),
                        pltpu.VMEM((tile, 2 * SSM_GROUPS * SSM_STATE), F32),
                        pltpu.VMEM((tile, SSM_CH), F32),
                        pltpu.VMEM((tile, SSM_CH), F32),
                        pltpu.VMEM((tile // SSM_CHUNK * SUBLANES, SSM_CH), F32),
                        pltpu.VMEM((tile // SSM_CHUNK * SSM_HEADS * SUBLANES, SSM_CHUNK), F32),
                        pltpu.VMEM((tile, SSM_GROUPS * SSM_STATE), F32),
                        pltpu.VMEM((tile, SSM_GROUPS * SSM_STATE), BF16),
                        pltpu.VMEM((tile, SSM_CH), F32),
                        pltpu.VMEM((tile, CONV_CH), BF16)],
        compiler_params=pltpu.CompilerParams(dimension_semantics=("arbitrary",),
                                             vmem_limit_bytes=VMEM_LIMIT),
        name="mixer_prompt",
    )(x.reshape(nb * seq, D_MODEL), mod, mod, *consts)


def _ffn_kernel(x_ref, mod_ref, w_up_ref, w_down_ref, ln_g_ref, ln_b_ref, o_ref, *, ff_tile):
    x = x_ref[...]
    sh2 = mod_ref[:, 3 * D_MODEL:4 * D_MODEL]
    sc2 = mod_ref[:, 4 * D_MODEL:5 * D_MODEL]
    g2 = mod_ref[:, 5 * D_MODEL:6 * D_MODEL]
    v = (x * (1.0 + sc2) + sh2).astype(BF16)
    acc = jnp.zeros(x.shape, F32)
    for k in range(D_FF // ff_tile):
        h = jnp.maximum(_dot(v, w_up_ref[:, k * ff_tile:(k + 1) * ff_tile]), 0.0)
        acc = acc + _dot((h * h).astype(BF16), w_down_ref[k * ff_tile:(k + 1) * ff_tile, :])
    o_ref[...] = _layer_norm(ALPHA * x + (1.0 + g2) * acc, ln_g_ref[...], ln_b_ref[...])


def _ffn(x, mod, rows_per_mod, w_up, w_down, ln_g, ln_b, tile, ff_tile=1024):
    rows = x.shape[0]
    mod_rows = mod.shape[1]
    tiles_per_mod = rows_per_mod // tile
    kern = functools.partial(_ffn_kernel, ff_tile=ff_tile)
    return pl.pallas_call(
        kern,
        grid=(rows // tile,),
        in_specs=[pl.BlockSpec((tile, D_MODEL), lambda i: (i, 0)),
                  pl.BlockSpec((None, mod_rows, 6 * D_MODEL), lambda i: (i // tiles_per_mod, 0, 0)),
                  _const_spec(w_up.shape), _const_spec(w_down.shape),
                  _const_spec(ln_g.shape), _const_spec(ln_b.shape)],
        out_specs=pl.BlockSpec((tile, D_MODEL), lambda i: (i, 0)),
        out_shape=jax.ShapeDtypeStruct((rows, D_MODEL), F32),
        compiler_params=pltpu.CompilerParams(dimension_semantics=("arbitrary",),
                                             vmem_limit_bytes=VMEM_LIMIT),
        name="ffn",
    )(x, mod, w_up, w_down, ln_g, ln_b)


def _sample_pre_kernel(x_ref, mod_ref, w_in_ref, w_dt_ref, conv_w_ref, conv_nw_ref, sconv_w_ref, sconv_b_ref,
                       dtb_ref, alog_ref, cb0_ref, cb1_ref, sb0_ref, sb1_ref, sb2_ref,
                       yconv_ref, ch_ref, xbc_ref, z_ref, xs_ref, ydiag_ref, decx_ref, cm_ref, xdt3_ref, dec3_ref, bm3_ref):
    expand = _head_expand()
    reduce = _group_reduce()
    x = x_ref[...]
    sh1 = mod_ref[:, 0:D_MODEL]
    sc1 = mod_ref[:, D_MODEL:2 * D_MODEL]
    u = (x * (1.0 + sc1) + sh1).astype(BF16)

    def proj(lo, width):
        return _dot(u, w_in_ref[:, lo:lo + width])

    ch = proj(COL_GC, CONV_CH) * proj(COL_HV, CONV_CH)
    ch_ref[...] = ch
    cw = conv_w_ref[...]
    cv = cw[0:1, :] * cb0_ref[...] + cw[1:2, :] * cb1_ref[...] + cw[2:3, :] * ch
    yconv_ref[...] = _conv_group_norm(proj(COL_GB, CONV_CH) * cv, conv_nw_ref[...], expand, reduce)

    xbc = proj(COL_XBC, XBC_CH)
    xbc_ref[...] = xbc
    sw = sconv_w_ref[...]
    xc = _silu(sw[0:1, :] * sb0_ref[...] + sw[1:2, :] * sb1_ref[...] + sw[2:3, :] * sb2_ref[...]
               + sw[3:4, :] * xbc + sconv_b_ref[...])
    xs = xc[:, 0:SSM_CH]
    xs_ref[...] = xs
    cm_ref[...] = xc[:, SSM_CH + SSM_GROUPS * SSM_STATE:XBC_CH]
    z_ref[...] = proj(COL_Z, SSM_CH)

    dt = _softplus(_dot(u, w_dt_ref[...]) + dtb_ref[...])
    dta = dt * (-jnp.exp(alog_ref[...]))
    xdt = xs * _dot_f32_lhs(dt, expand)
    decx = jnp.exp(_dot_f32_lhs(dta, expand))
    decx_ref[...] = decx
    for ref, val in ((xdt3_ref, xdt), (dec3_ref, decx), (bm3_ref, xc[:, SSM_CH:SSM_CH + SSM_GROUPS * SSM_STATE])):
        for t, part in enumerate(_split(val, 3)):
            ref[t] = part
    for g in range(SSM_GROUPS):
        bm = xc[:, SSM_CH + g * SSM_STATE:SSM_CH + (g + 1) * SSM_STATE]
        cm = xc[:, SSM_CH + (SSM_GROUPS + g) * SSM_STATE:SSM_CH + (SSM_GROUPS + g + 1) * SSM_STATE]
        cb = jnp.sum(cm * bm, axis=-1, keepdims=True)
        gl = g * SSM_GROUP_CH
        ydiag_ref[:, gl:gl + SSM_GROUP_CH] = cb * xdt[:, gl:gl + SSM_GROUP_CH]


def _sample_pre(x, mod, w_in, w_dt, conv_w, conv_nw, sconv_w, sconv_b, dtb, alog, cb0, cb1, sb0, sb1, sb2):
    n = x.shape[0]
    args = (x, mod, w_in, w_dt, conv_w, conv_nw, sconv_w, sconv_b, dtb, alog, cb0, cb1, sb0, sb1, sb2)
    f32_shapes = [(n, CONV_CH), (n, CONV_CH), (n, XBC_CH), (n, SSM_CH), (n, SSM_CH), (n, SSM_CH), (n, SSM_CH),
                  (n, SSM_GROUPS * SSM_STATE)]
    bf16_shapes = [(3, n, SSM_CH), (3, n, SSM_CH), (3, n, SSM_GROUPS * SSM_STATE)]
    return pl.pallas_call(
        _sample_pre_kernel,
        out_shape=[jax.ShapeDtypeStruct(s, F32) for s in f32_shapes]
                  + [jax.ShapeDtypeStruct(s, BF16) for s in bf16_shapes],
        compiler_params=pltpu.CompilerParams(vmem_limit_bytes=VMEM_LIMIT),
        name="sample_pre",
    )(*args)


def _sample_state_kernel(s_ref, lhs_ref, rhs_ref, cm_ref, decx_ref, o_ref, yoff_ref, *, block):
    i = pl.program_id(0)

    def body(k, carry):
        b = i * block + k
        s = s_ref[k]
        upd = lax.dot_general(lhs_ref[k], rhs_ref[k], (((0,), (0,)), ((), ())), preferred_element_type=F32)
        o_ref[k] = s * upd[:, SSM_STATE:2 * SSM_STATE] + upd[:, 0:SSM_STATE]
        cm = cm_ref[pl.ds(b, 1), :]
        sums = []
        for c0 in range(0, SSM_CH, LANES):
            g = c0 // SSM_GROUP_CH
            prod = s[c0:c0 + LANES, :] * cm[:, g * SSM_STATE:(g + 1) * SSM_STATE]
            sums.append(jnp.sum(prod.T, axis=0, keepdims=True))
        yoff_ref[pl.ds(b, 1), :] = jnp.concatenate(sums, axis=1) * decx_ref[pl.ds(b, 1), :]
        return carry

    lax.fori_loop(0, block, body, 0, unroll=True)


_PRODUCT_TERMS = ((0, 0), (0, 1), (1, 0), (0, 2), (2, 0), (1, 1))
UPDATE_TERMS = 16


def _sample_state_operands(xdt3, dec3, bm3):
    n = xdt3.shape[1]
    ch = jnp.arange(SSM_CH) // SSM_GROUP_CH
    zero_b = jnp.zeros((n, SSM_STATE), BF16)
    lhs, rhs = [], []
    for g in range(SSM_GROUPS):
        in_group = (ch == g)[None, :]
        for tx, tb in _PRODUCT_TERMS:
            lhs.append(jnp.where(in_group, xdt3[tx], 0))
            rhs.append(jnp.concatenate([bm3[tb][:, g * SSM_STATE:(g + 1) * SSM_STATE], zero_b], axis=1))
    for t in range(3):
        lhs.append(dec3[t])
        rhs.append(jnp.concatenate([zero_b, jnp.ones((n, SSM_STATE), BF16)], axis=1))
    while len(lhs) < UPDATE_TERMS:
        lhs.append(jnp.zeros((n, SSM_CH), BF16))
        rhs.append(jnp.zeros((n, 2 * SSM_STATE), BF16))
    return jnp.stack(lhs, axis=1), jnp.stack(rhs, axis=1)


def _sample_state(state, lhs, rhs, cm, decx, block=8):
    n = state.shape[0]
    kern = functools.partial(_sample_state_kernel, block=block)
    return pl.pallas_call(
        kern,
        grid=(n // block,),
        in_specs=[pl.BlockSpec((block, SSM_CH, SSM_STATE), lambda i: (i, 0, 0)),
                  pl.BlockSpec((block, UPDATE_TERMS, SSM_CH), lambda i: (i, 0, 0)),
                  pl.BlockSpec((block, UPDATE_TERMS, 2 * SSM_STATE), lambda i: (i, 0, 0)),
                  _const_spec(cm.shape), _const_spec(decx.shape)],
        out_specs=[pl.BlockSpec((block, SSM_CH, SSM_STATE), lambda i: (i, 0, 0)),
                   pl.BlockSpec((n, SSM_CH), lambda i: (0, 0))],
        out_shape=[jax.ShapeDtypeStruct(state.shape, F32), jax.ShapeDtypeStruct((n, SSM_CH), F32)],
        compiler_params=pltpu.CompilerParams(dimension_semantics=("arbitrary",),
                                             vmem_limit_bytes=VMEM_LIMIT),
        name="sample_state",
    )(state, lhs, rhs, cm, decx)


def _sample_post_kernel(x_ref, mod_ref, yconv_ref, ydiag_ref, yoff_ref, xs_ref, z_ref, dexp_ref, snw_ref,
                        w_out_ref, ln_g_ref, ln_b_ref, x1_ref):
    g1 = mod_ref[:, 2 * D_MODEL:3 * D_MODEL]
    y = ydiag_ref[...] + yoff_ref[...] + xs_ref[...] * dexp_ref[...]
    y = y * _silu(z_ref[...])
    m = _mix_out(yconv_ref[...], _ssm_group_norm(y, snw_ref[...]), w_out_ref)
    x1_ref[...] = _layer_norm(ALPHA * x_ref[...] + (1.0 + g1) * m, ln_g_ref[...], ln_b_ref[...])


def _sample_post(x, mod, yconv, ydiag, yoff, xs, z, dexp, snw, w_out, ln_g, ln_b):
    return pl.pallas_call(
        _sample_post_kernel,
        out_shape=jax.ShapeDtypeStruct(x.shape, F32),
        compiler_params=pltpu.CompilerParams(vmem_limit_bytes=VMEM_LIMIT),
        name="sample_post",
    )(x, mod, yconv, ydiag, yoff, xs, z, dexp, snw, w_out, ln_g, ln_b)


def kernel(x_prompt, x_sample, state_conv, state_ssm_conv, state_ssm, c_prompt, c_sample, w_ada, b_ada, w_in, conv_w, conv_norm_w, ssm_conv_w, ssm_conv_b, dt_bias, a_log, d_skip, ssm_norm_w, w_out, ln1_g, ln1_b, w_up, w_down, ln2_g, ln2_b):
    assert w_ada.shape[0] == 1, "single-layer trunk"
    nb, seq, _ = x_prompt.shape
    ns = x_sample.shape[0]
    row = lambda a: a.reshape(1, -1)
    pad_heads = lambda a: jnp.pad(a.reshape(1, -1), ((0, 0), (0, LANES - SSM_HEADS)))

    w_in_b = w_in[0, :, :COL_DT].astype(BF16)
    w_dt_b = jnp.pad(w_in[0, :, COL_DT:], ((0, 0), (0, IN_PAD - IN_COLS))).astype(BF16)
    w_out_b = w_out[0].astype(BF16)
    w_up_b = w_up[0].astype(BF16)
    w_down_b = w_down[0].astype(BF16)
    conv_nw, sconv_b, snw = row(conv_norm_w[0]), row(ssm_conv_b[0]), row(ssm_norm_w[0])
    dtb, alog = pad_heads(dt_bias[0]), pad_heads(a_log[0])
    dexp = row(jnp.repeat(d_skip[0], SSM_HEAD_DIM))
    g1, b1, g2, b2 = row(ln1_g[0]), row(ln1_b[0]), row(ln2_g[0]), row(ln2_b[0])

    mod_p, mod_s = _ada(c_sample, c_prompt, w_ada[0], row(b_ada[0]))
    mod_p = mod_p.reshape(nb, 1, 6 * D_MODEL)

    x1_p, cst_p, scst_p, sst_p = _mixer_prompt(x_prompt, mod_p, w_in_b, w_dt_b, conv_w[0], conv_nw, ssm_conv_w[0],
                                               sconv_b, dtb, alog, dexp, snw, w_out_b, g1, b1)
    y_p = _ffn(x1_p, mod_p, seq, w_up_b, w_down_b, g2, b2, tile=512)

    xs2 = x_sample.reshape(ns, D_MODEL)
    (yconv_s, ch_s, xbc_s, z_s, xs_s, ydiag_s, decx_s, cm_s, xdt3, dec3, bm3) = _sample_pre(
        xs2, mod_s, w_in_b, w_dt_b, conv_w[0], conv_nw, ssm_conv_w[0], sconv_b, dtb, alog,
        state_conv[0, :, 0], state_conv[0, :, 1],
        state_ssm_conv[0, :, 0], state_ssm_conv[0, :, 1], state_ssm_conv[0, :, 2])
    new_state_s, yoff_s = _sample_state(state_ssm[0].reshape(ns, SSM_CH, SSM_STATE),
                                        *_sample_state_operands(xdt3, dec3, bm3), cm_s, decx_s)
    x1_s = _sample_post(xs2, mod_s, yconv_s, ydiag_s, yoff_s, xs_s, z_s, dexp, snw, w_out_b, g1, b1)
    y_s = _ffn(x1_s, mod_s.reshape(1, ns, 6 * D_MODEL), ns, w_up_b, w_down_b, g2, b2, tile=ns)

    return (y_p.reshape(nb, seq, D_MODEL),
            y_s.reshape(ns, 1, D_MODEL),
            cst_p[None],
            scst_p[None],
            sst_p.reshape(1, nb, SSM_HEADS, SSM_HEAD_DIM, SSM_STATE),
            jnp.stack([state_conv[0, :, 1], ch_s], axis=1)[None],
            jnp.stack([state_ssm_conv[0, :, 1], state_ssm_conv[0, :, 2], xbc_s], axis=1)[None],
            new_state_s.reshape(1, ns, SSM_HEADS, SSM_HEAD_DIM, SSM_STATE))
```

```python
import functools

import jax
import jax.numpy as jnp
from jax import lax
from jax.experimental import pallas as pl
from jax.experimental.pallas import tpu as pltpu

F32 = jnp.float32
BF16 = jnp.bfloat16

D_MODEL = 1024
CONV_CH = 1024
CONV_GROUP = 64
SSM_CH = 1024
SSM_HEADS = 16
SSM_HEAD_DIM = 64
SSM_GROUPS = 2
SSM_GROUP_CH = SSM_CH // SSM_GROUPS
SSM_STATE = 128
SSM_CHUNK = 128
XBC_CH = SSM_CH + 2 * SSM_GROUPS * SSM_STATE
D_FF = 4 * D_MODEL
LANES = 128
SUBLANES = 8
MXU_COLS = 256
COL_GB, COL_GC, COL_HV, COL_Z, COL_XBC = 0, 1024, 2048, 3072, 4096
COL_DT = COL_XBC + XBC_CH
IN_COLS = COL_DT + SSM_HEADS
IN_PAD = COL_DT + LANES
ALPHA = 2.0 ** 0.25
LN_EPS = 1e-5
RMS_EPS = 1e-5
VMEM_LIMIT = 56 * 1024 * 1024


def _dot(a, b):
    return jnp.dot(a, b, preferred_element_type=F32)


def _split(a, terms):
    parts = []
    r = a
    for t in range(terms):
        p = r.astype(BF16)
        parts.append(p)
        if t + 1 < terms:
            r = r - p.astype(F32)
    return parts


def _dot_f32_lhs(a, b_exact, terms=3):
    parts = _split(a, terms)
    out = _dot(parts[0], b_exact)
    for p in parts[1:]:
        out = out + _dot(p, b_exact)
    return out


def _dot_f32_rhs(a_exact, b, terms=3):
    parts = _split(b, terms)
    out = _dot(a_exact, parts[0])
    for p in parts[1:]:
        out = out + _dot(a_exact, p)
    return out


def _head_expand():
    h = lax.broadcasted_iota(jnp.int32, (LANES, SSM_CH), 0)
    c = lax.broadcasted_iota(jnp.int32, (LANES, SSM_CH), 1)
    return (c // SSM_HEAD_DIM == h).astype(BF16)


def _group_reduce():
    c = lax.broadcasted_iota(jnp.int32, (CONV_CH, LANES), 0)
    k = lax.broadcasted_iota(jnp.int32, (CONV_CH, LANES), 1)
    return (c // CONV_GROUP == k).astype(BF16)


def _sigmoid(x):
    return 1.0 / (1.0 + jnp.exp(-x))


def _silu(x):
    return x * _sigmoid(x)


def _softplus(x):
    return jnp.maximum(x, 0.0) + jnp.log1p(jnp.exp(-jnp.abs(x)))


def _layer_norm(r, g, b):
    mu = jnp.mean(r, axis=-1, keepdims=True)
    d = r - mu
    var = jnp.mean(d * d, axis=-1, keepdims=True)
    return d * lax.rsqrt(var + LN_EPS) * g + b


def _conv_group_norm(prod, w, expand, reduce):
    ssum = _dot_f32_lhs(prod * prod, reduce, terms=2)
    rstd = lax.rsqrt(ssum * (1.0 / CONV_GROUP) + RMS_EPS)
    return prod * _dot_f32_lhs(rstd, expand, terms=2) * w


def _ssm_group_norm(y, w):
    outs = []
    for g in range(SSM_GROUPS):
        yg = y[:, g * SSM_GROUP_CH:(g + 1) * SSM_GROUP_CH]
        ms = jnp.mean(yg * yg, axis=-1, keepdims=True)
        outs.append((yg * lax.rsqrt(ms + RMS_EPS) * w[:, g * SSM_GROUP_CH:(g + 1) * SSM_GROUP_CH]).astype(BF16))
    return outs


def _mix_out(y_conv, y_ssm_groups, w_out_ref):
    m = _dot(y_conv.astype(BF16), w_out_ref[0:CONV_CH, :])
    for g, yg in enumerate(y_ssm_groups):
        lo = CONV_CH + g * SSM_GROUP_CH
        m = m + _dot(yg, w_out_ref[lo:lo + SSM_GROUP_CH, :])
    return m


def _ada_kernel(c_ref, w_ref, b_ref, op_ref, os_ref):
    c = c_ref[...]
    w = w_ref[...]
    c_hi = c.astype(BF16)
    c_lo = (c - c_hi.astype(F32)).astype(BF16)
    w_hi = w.astype(BF16)
    w_lo = (w - w_hi.astype(F32)).astype(BF16)
    mod = _dot(c_hi, w_hi) + _dot(c_hi, w_lo) + _dot(c_lo, w_hi) + b_ref[...]
    n_sample = os_ref.shape[0]
    os_ref[...] = mod[0:n_sample, :]
    op_ref[...] = mod[n_sample:, :]


def _ada(c_sample, c_prompt, w_ada, b_ada, tile_n=1024):
    ns, nb = c_sample.shape[0], c_prompt.shape[0]
    n = w_ada.shape[1]
    return pl.pallas_call(
        _ada_kernel,
        grid=(n // tile_n,),
        in_specs=[pl.BlockSpec((ns + nb, D_MODEL), lambda i: (0, 0)),
                  pl.BlockSpec((D_MODEL, tile_n), lambda i: (0, i)),
                  pl.BlockSpec((1, tile_n), lambda i: (0, i))],
        out_specs=[pl.BlockSpec((nb, tile_n), lambda i: (0, i)),
                   pl.BlockSpec((ns, tile_n), lambda i: (0, i))],
        out_shape=[jax.ShapeDtypeStruct((nb, n), F32), jax.ShapeDtypeStruct((ns, n), F32)],
        name="ada_mod",
    )(jnp.concatenate([c_sample, c_prompt], axis=0), w_ada, b_ada)


def _cast_cols_kernel(w_ref, o_ref):
    o_ref[...] = w_ref[:, 0:o_ref.shape[1]].astype(o_ref.dtype)


def _cast_cols(w, cols, tile_rows=128):
    rows, width = w.shape
    return pl.pallas_call(
        _cast_cols_kernel,
        grid=(rows // tile_rows,),
        in_specs=[pl.BlockSpec((tile_rows, width), lambda i: (i, 0))],
        out_specs=pl.BlockSpec((tile_rows, cols), lambda i: (i, 0)),
        out_shape=jax.ShapeDtypeStruct((rows, cols), BF16),
        name="cast_cols",
    )(w)


def _mixer_prompt_kernel(xa_ref, moda_ref, modb_ref, w_in_ref, w_dt_ref, expand_ref, reduce_ref,
                         conv_w_ref, conv_nw_ref, sconv_w_ref, sconv_b_ref,
                         dtb_ref, alog_ref, dexp_ref, snw_ref, w_out_ref, ln_g_ref, ln_b_ref,
                         x1_ref, cst_ref, scst_ref, sst_ref,
                         p0, p1, xk0, xk1, cbuf, xbuf, st_ref, xs_ref, bc_ref, dtx_ref, acsx_ref, endx_ref,
                         acst_ref, cb_ref, bmt_ref, y_ref, yc_ref,
                         *, tile, tiles_per_seq, sched):
    s = pl.program_id(0)
    jb = lax.rem(s + (tiles_per_seq - 1), tiles_per_seq)

    @pl.when(s == 0)
    def _():
        p1[...] = jnp.zeros_like(p1)
        xk1[...] = jnp.zeros_like(xk1)

    @pl.when((jb == 0) | (s == 0))
    def _():
        cbuf[...] = jnp.zeros_like(cbuf)
        xbuf[...] = jnp.zeros_like(xbuf)
        st_ref[...] = jnp.zeros_like(st_ref)

    def stages(pa, xka, pb, xkb):
        xa = xa_ref[...]
        xka[...] = xa
        u = (xa * (1.0 + moda_ref[:, D_MODEL:2 * D_MODEL]) + moda_ref[:, 0:D_MODEL]).astype(BF16)
        pieces = iter(list(range(0, COL_DT, MXU_COLS)) + [COL_DT])

        def first_stage(n):
            for _ in range(n):
                lo = next(pieces, None)
                if lo == COL_DT:
                    pa[:, COL_DT:IN_PAD] = _dot(u, w_dt_ref[...])
                elif lo is not None:
                    pa[:, lo:lo + MXU_COLS] = _dot(u, w_in_ref[:, lo:lo + MXU_COLS])

        expand = expand_ref[...]
        x = xkb[...]
        g1 = modb_ref[:, 2 * D_MODEL:3 * D_MODEL]

        def proj(lo, width):
            return pb[:, lo:lo + width]

        def delayed(tail_ref, cs, cur, taps):
            seq = jnp.concatenate([tail_ref[:, cs], cur], axis=0)
            tail_ref[:, cs] = cur[tile - SUBLANES:, :]
            return [pltpu.roll(seq, k, axis=0)[SUBLANES:, :] for k in range(1, taps + 1)]

        for k in range(CONV_CH // MXU_COLS):
            first_stage(sched[0])
            c0 = k * MXU_COLS
            cs = slice(c0, c0 + MXU_COLS)
            ch = proj(COL_GC + c0, MXU_COLS) * proj(COL_HV + c0, MXU_COLS)
            ch1, ch2 = delayed(cbuf, cs, ch, 2)
            cv = conv_w_ref[0:1, cs] * ch2 + conv_w_ref[1:2, cs] * ch1 + conv_w_ref[2:3, cs] * ch
            prod = proj(COL_GB + c0, MXU_COLS) * cv
            ssum = _dot_f32_lhs(prod * prod, reduce_ref[cs, :], terms=2)
            rstd = lax.rsqrt(ssum * (1.0 / CONV_GROUP) + RMS_EPS)
            yc_ref[:, cs] = (prod * _dot_f32_lhs(rstd, expand_ref[:, cs], terms=2)
                             * conv_nw_ref[:, cs]).astype(BF16)

        def pre_conv(c0):
            cs = slice(c0, c0 + MXU_COLS)
            xbc = proj(COL_XBC + c0, MXU_COLS)
            x1, x2, x3 = delayed(xbuf, cs, xbc, 3)
            return _silu(sconv_w_ref[0:1, cs] * x3 + sconv_w_ref[1:2, cs] * x2 + sconv_w_ref[2:3, cs] * x1
                         + sconv_w_ref[3:4, cs] * xbc + sconv_b_ref[:, cs])

        row = lax.broadcasted_iota(jnp.int32, (SSM_CHUNK, SSM_CHUNK), 0)
        col = lax.broadcasted_iota(jnp.int32, (SSM_CHUNK, SSM_CHUNK), 1)
        causal = row >= col
        tri = causal.astype(BF16)
        groups = SSM_CHUNK // SUBLANES
        causal_bias = jnp.where(causal, 0.0, -jnp.inf).reshape(groups, SUBLANES, SSM_CHUNK)
        first_half = (col < SSM_HEAD_DIM).reshape(groups, SUBLANES, SSM_CHUNK)
        half_rows = col < SSM_HEAD_DIM
        chunks = [slice(c * SSM_CHUNK, (c + 1) * SSM_CHUNK) for c in range(tile // SSM_CHUNK)]

        first_stage(sched[1])
        dt = _softplus(proj(COL_DT, LANES) + dtb_ref[...])
        dta = dt * (-jnp.exp(alog_ref[...]))
        dtx_ref[...] = _dot_f32_lhs(dt, expand, terms=2)
        for c, rows in enumerate(chunks):
            acs = _dot_f32_rhs(tri, dta[rows, :])
            acs_t = acs.T
            for h in range(SSM_HEADS):
                r8 = (c * SSM_HEADS + h) * SUBLANES
                acst_ref[r8:r8 + SUBLANES, :] = jnp.broadcast_to(acs_t[h:h + 1, :], (SUBLANES, SSM_CHUNK))
            acs_x = _dot_f32_lhs(acs, expand)
            acsx_ref[rows, :] = acs_x
            endx_ref[c * SUBLANES:(c + 1) * SUBLANES, :] = jnp.broadcast_to(acs_x[SSM_CHUNK - 1:SSM_CHUNK, :],
                                                                             (SUBLANES, SSM_CH))
        for c0 in range(SSM_CH, XBC_CH, MXU_COLS):
            first_stage(sched[2])
            bc_ref[:, c0 - SSM_CH:c0 - SSM_CH + MXU_COLS] = pre_conv(c0)
        for rows in chunks:
            for g in range(SSM_GROUPS):
                gs = slice(g * SSM_STATE, (g + 1) * SSM_STATE)
                bm = bc_ref[rows, gs]
                cm = bc_ref[rows, (SSM_GROUPS + g) * SSM_STATE:(SSM_GROUPS + g + 1) * SSM_STATE]
                cb_ref[rows, gs] = lax.dot_general(cm.astype(BF16), bm.astype(BF16), (((1,), (1,)), ((), ())),
                                                   preferred_element_type=F32)
                bmt_ref[rows, gs] = bm.T.astype(BF16)

        for c0 in range(0, SSM_CH, MXU_COLS):
            first_stage(sched[3])
            cs = slice(c0, c0 + MXU_COLS)
            g = c0 // SSM_GROUP_CH
            gs = slice(g * SSM_STATE, (g + 1) * SSM_STATE)
            xs = pre_conv(c0)
            xs_ref[:, cs] = xs
            xdt = xs * dtx_ref[:, cs]
            for c, rows in enumerate(chunks):
                first_stage(sched[4])
                acs_x = acsx_ref[rows, cs].reshape(groups, SUBLANES, MXU_COLS)
                end_x = endx_ref[c * SUBLANES:(c + 1) * SUBLANES, cs]
                xdt_c = xdt[rows, :]
                xdec = (xdt_c * jnp.exp(end_x[None] - acs_x).reshape(SSM_CHUNK, MXU_COLS)).astype(BF16)
                cm = bc_ref[rows, (SSM_GROUPS + g) * SSM_STATE:(SSM_GROUPS + g + 1) * SSM_STATE].astype(BF16)
                cb = cb_ref[rows, gs]
                st = st_ref[:, cs]
                y_off = _dot(cm, st.astype(BF16)) * jnp.exp(acs_x).reshape(SSM_CHUNK, MXU_COLS)
                st_ref[:, cs] = ((st.reshape(groups, SUBLANES, MXU_COLS) * jnp.exp(end_x)[None])
                                 .reshape(SSM_STATE, MXU_COLS) + _dot(bmt_ref[rows, gs], xdec))
                for lo in range(0, MXU_COLS, LANES):
                    h0 = (c * SSM_HEADS + (c0 + lo) // SSM_HEAD_DIM) * SUBLANES
                    slab = acs_x[:, :, lo:lo + LANES]
                    rolled = pltpu.roll(slab, SSM_HEAD_DIM, axis=2)
                    a0 = jnp.where(first_half, slab, rolled) - acst_ref[h0:h0 + SUBLANES, :][None]
                    a1 = jnp.where(first_half, rolled, slab) - acst_ref[h0 + SUBLANES:h0 + 2 * SUBLANES, :][None]
                    l0 = jnp.exp(a0 + causal_bias).reshape(SSM_CHUNK, SSM_CHUNK)
                    l1 = jnp.exp(a1 + causal_bias).reshape(SSM_CHUNK, SSM_CHUNK)
                    m = jnp.concatenate([(cb * l0).astype(BF16), (cb * l1).astype(BF16)], axis=1)
                    xp = xdt_c[:, lo:lo + LANES]
                    rhs = jnp.concatenate([jnp.where(half_rows, xp, 0.0), jnp.where(half_rows, 0.0, xp)],
                                          axis=0).astype(BF16)
                    y_ref[rows, c0 + lo:c0 + lo + LANES] = _dot(m, rhs) + y_off[:, lo:lo + LANES]

        for k in range(SSM_CH // MXU_COLS):
            first_stage(sched[5])
            c0 = k * MXU_COLS
            cs = slice(c0, c0 + MXU_COLS)
            y_ref[:, cs] = (y_ref[:, cs] + xs_ref[:, cs] * dexp_ref[:, cs]) * _silu(proj(COL_Z + c0, MXU_COLS))
        first_stage(sched[6])
        m = _mix_out(yc_ref[...], _ssm_group_norm(y_ref[...], snw_ref[...]), w_out_ref)
        x1_ref[...] = _layer_norm(ALPHA * x + (1.0 + g1) * m, ln_g_ref[...], ln_b_ref[...])
        first_stage(IN_PAD // MXU_COLS)

    @pl.when(lax.rem(s, 2) == 0)
    def _():
        stages(p0, xk0, p1, xk1)

    @pl.when(lax.rem(s, 2) == 1)
    def _():
        stages(p1, xk1, p0, xk0)

    @pl.when((jb == tiles_per_seq - 1) & (s > 0))
    def _():
        cst_ref[...] = cbuf[SUBLANES - 2:SUBLANES, :]
        scst_ref[...] = xbuf[SUBLANES - 3:SUBLANES, :]
        sst_ref[...] = st_ref[...].T


def _const_spec(shape):
    return pl.BlockSpec(shape, lambda *_: (0,) * len(shape), pipeline_mode=pl.Buffered(1))


def _mixer_prompt(x, mod, w_in, w_dt, conv_w, conv_nw, sconv_w, sconv_b, dtb, alog, dexp, snw, w_out, ln_g, ln_b,
                  tile=256, sched=(1, 0, 0, 2, 1, 0, 0)):
    assert CONV_GROUP == SSM_HEAD_DIM and CONV_CH == SSM_CH
    nb, seq, _ = x.shape
    tiles_per_seq = seq // tile
    n_tiles = nb * tiles_per_seq
    kern = functools.partial(_mixer_prompt_kernel, tile=tile, tiles_per_seq=tiles_per_seq, sched=sched)
    consts = [w_in, w_dt, _head_expand(), _group_reduce(), conv_w, conv_nw, sconv_w, sconv_b, dtb, alog, dexp, snw,
              w_out, ln_g, ln_b]
    first = lambda s: jnp.minimum(s, n_tiles - 1)
    second = lambda s: jnp.maximum(s - 1, 0)
    return pl.pallas_call(
        kern,
        grid=(n_tiles + 1,),
        in_specs=[pl.BlockSpec((tile, D_MODEL), lambda s: (first(s), 0)),
                  pl.BlockSpec((None, 1, 6 * D_MODEL), lambda s: (first(s) // tiles_per_seq, 0, 0)),
                  pl.BlockSpec((None, 1, 6 * D_MODEL), lambda s: (second(s) // tiles_per_seq, 0, 0))]
                 + [_const_spec(a.shape) for a in consts],
        out_specs=[pl.BlockSpec((tile, D_MODEL), lambda s: (second(s), 0)),
                   pl.BlockSpec((None, 2, CONV_CH), lambda s: (second(s) // tiles_per_seq, 0, 0)),
                   pl.BlockSpec((None, 3, XBC_CH), lambda s: (second(s) // tiles_per_seq, 0, 0)),
                   pl.BlockSpec((None, SSM_CH, SSM_STATE), lambda s: (second(s) // tiles_per_seq, 0, 0))],
        out_shape=[jax.ShapeDtypeStruct((nb * seq, D_MODEL), F32),
                   jax.ShapeDtypeStruct((nb, 2, CONV_CH), F32),
                   jax.ShapeDtypeStruct((nb, 3, XBC_CH), F32),
                   jax.ShapeDtypeStruct((nb, SSM_CH, SSM_STATE), F32)],
        scratch_shapes=[pltpu.VMEM((tile, IN_PAD), F32),
                        pltpu.VMEM((tile, IN_PAD), F32),
                        pltpu.VMEM((tile, D_MODEL), F32),
                        pltpu.VMEM((tile, D_MODEL), F32),
                        pltpu.VMEM((SUBLANES, CONV_CH), F32),
                        pltpu.VMEM((SUBLANES, XBC_CH), F32),
                        pltpu.VMEM((SSM_STATE, SSM_CH), F32),
                        pltpu.VMEM((tile, SSM_CH), F32),
                        pltpu.VMEM((tile, 2 * SSM_GROUPS * SSM_STATE), F32),
                        pltpu.VMEM((tile, SSM_CH), F32),
                        pltpu.VMEM((tile, SSM_CH), F32),
                        pltpu.VMEM((tile // SSM_CHUNK * SUBLANES, SSM_CH), F32),
                        pltpu.VMEM((tile // SSM_CHUNK * SSM_HEADS * SUBLANES, SSM_CHUNK), F32),
                        pltpu.VMEM((tile, SSM_GROUPS * SSM_STATE), F32),
                        pltpu.VMEM((tile, SSM_GROUPS * SSM_STATE), BF16),
                        pltpu.VMEM((tile, SSM_CH), F32),
                        pltpu.VMEM((tile, CONV_CH), BF16)],
        compiler_params=pltpu.CompilerParams(dimension_semantics=("arbitrary",),
                                             vmem_limit_bytes=VMEM_LIMIT),
        name="mixer_prompt",
    )(x.reshape(nb * seq, D_MODEL), mod, mod, *consts)


def _ffn_kernel(x_ref, mod_ref, w_up_ref, w_down_ref, ln_g_ref, ln_b_ref, o_ref, *, ff_tile):
    x = x_ref[...]
    sh2 = mod_ref[:, 3 * D_MODEL:4 * D_MODEL]
    sc2 = mod_ref[:, 4 * D_MODEL:5 * D_MODEL]
    g2 = mod_ref[:, 5 * D_MODEL:6 * D_MODEL]
    v = (x * (1.0 + sc2) + sh2).astype(BF16)
    acc = jnp.zeros(x.shape, F32)
    for k in range(D_FF // ff_tile):
        h = jnp.maximum(_dot(v, w_up_ref[:, k * ff_tile:(k + 1) * ff_tile]), 0.0)
        acc = acc + _dot((h * h).astype(BF16), w_down_ref[k * ff_tile:(k + 1) * ff_tile, :])
    o_ref[...] = _layer_norm(ALPHA * x + (1.0 + g2) * acc, ln_g_ref[...], ln_b_ref[...])


def _ffn(x, mod, rows_per_mod, w_up, w_down, ln_g, ln_b, tile, ff_tile=1024):
    rows = x.shape[0]
    mod_rows = mod.shape[1]
    tiles_per_mod = rows_per_mod // tile
    kern = functools.partial(_ffn_kernel, ff_tile=ff_tile)
    return pl.pallas_call(
        kern,
        grid=(rows // tile,),
        in_specs=[pl.BlockSpec((tile, D_MODEL), lambda i: (i, 0)),
                  pl.BlockSpec((None, mod_rows, 6 * D_MODEL), lambda i: (i // tiles_per_mod, 0, 0)),
                  _const_spec(w_up.shape), _const_spec(w_down.shape),
                  _const_spec(ln_g.shape), _const_spec(ln_b.shape)],
        out_specs=pl.BlockSpec((tile, D_MODEL), lambda i: (i, 0)),
        out_shape=jax.ShapeDtypeStruct((rows, D_MODEL), F32),
        compiler_params=pltpu.CompilerParams(dimension_semantics=("arbitrary",),
                                             vmem_limit_bytes=VMEM_LIMIT),
        name="ffn",
    )(x, mod, w_up, w_down, ln_g, ln_b)


_PRODUCT_TERMS = ((0, 0), (0, 1), (1, 0), (0, 2), (2, 0), (1, 1))
UPDATE_TERMS = 16


def _sample_pre_kernel(x_ref, mod_ref, w_in_ref, w_dt_ref, conv_w_ref, conv_nw_ref, sconv_w_ref, sconv_b_ref,
                       dtb_ref, alog_ref, cb0_ref, cb1_ref, sb0_ref, sb1_ref, sb2_ref,
                       yconv_ref, ch_ref, xbc_ref, z_ref, xs_ref, ydiag_ref, decx_ref, cm_ref, lhs_ref, rhs_ref):
    expand = _head_expand()
    reduce = _group_reduce()
    x = x_ref[...]
    sh1 = mod_ref[:, 0:D_MODEL]
    sc1 = mod_ref[:, D_MODEL:2 * D_MODEL]
    u = (x * (1.0 + sc1) + sh1).astype(BF16)

    def proj(lo, width):
        return _dot(u, w_in_ref[:, lo:lo + width])

    ch = proj(COL_GC, CONV_CH) * proj(COL_HV, CONV_CH)
    ch_ref[...] = ch
    cw = conv_w_ref[...]
    cv = cw[0:1, :] * cb0_ref[...] + cw[1:2, :] * cb1_ref[...] + cw[2:3, :] * ch
    yconv_ref[...] = _conv_group_norm(proj(COL_GB, CONV_CH) * cv, conv_nw_ref[...], expand, reduce)

    xbc = proj(COL_XBC, XBC_CH)
    xbc_ref[...] = xbc
    sw = sconv_w_ref[...]
    xc = _silu(sw[0:1, :] * sb0_ref[...] + sw[1:2, :] * sb1_ref[...] + sw[2:3, :] * sb2_ref[...]
               + sw[3:4, :] * xbc + sconv_b_ref[...])
    xs = xc[:, 0:SSM_CH]
    xs_ref[...] = xs
    cm_ref[...] = xc[:, SSM_CH + SSM_GROUPS * SSM_STATE:XBC_CH]
    z_ref[...] = proj(COL_Z, SSM_CH)

    dt = _softplus(_dot(u, w_dt_ref[...]) + dtb_ref[...])
    dta = dt * (-jnp.exp(alog_ref[...]))
    xdt = xs * _dot_f32_lhs(dt, expand)
    decx = jnp.exp(_dot_f32_lhs(dta, expand))
    decx_ref[...] = decx
    xdt_t, dec_t = _split(xdt, 3), _split(decx, 3)
    bm_t = _split(xc[:, SSM_CH:SSM_CH + SSM_GROUPS * SSM_STATE], 3)
    group_of = lax.broadcasted_iota(jnp.int32, xdt.shape, 1) // SSM_GROUP_CH
    zeros = jnp.zeros((x.shape[0], SSM_STATE), F32)
    r = 0
    for g in range(SSM_GROUPS):
        for tx, tb in _PRODUCT_TERMS:
            lhs_ref[r] = jnp.where(group_of == g, xdt_t[tx].astype(F32), 0.0)
            rhs_ref[r] = jnp.concatenate([bm_t[tb][:, g * SSM_STATE:(g + 1) * SSM_STATE].astype(F32), zeros], axis=1)
            r += 1
    for t in range(3):
        lhs_ref[r] = dec_t[t].astype(F32)
        rhs_ref[r] = jnp.concatenate([zeros, zeros + 1.0], axis=1)
        r += 1
    for r in range(r, UPDATE_TERMS):
        lhs_ref[r] = jnp.zeros_like(xdt)
        rhs_ref[r] = jnp.concatenate([zeros, zeros], axis=1)
    for g in range(SSM_GROUPS):
        bm = xc[:, SSM_CH + g * SSM_STATE:SSM_CH + (g + 1) * SSM_STATE]
        cm = xc[:, SSM_CH + (SSM_GROUPS + g) * SSM_STATE:SSM_CH + (SSM_GROUPS + g + 1) * SSM_STATE]
        cb = jnp.sum(cm * bm, axis=-1, keepdims=True)
        gl = g * SSM_GROUP_CH
        ydiag_ref[:, gl:gl + SSM_GROUP_CH] = cb * xdt[:, gl:gl + SSM_GROUP_CH]


def _sample_pre(x, mod, w_in, w_dt, conv_w, conv_nw, sconv_w, sconv_b, dtb, alog, cb0, cb1, sb0, sb1, sb2):
    n = x.shape[0]
    args = (x, mod, w_in, w_dt, conv_w, conv_nw, sconv_w, sconv_b, dtb, alog, cb0, cb1, sb0, sb1, sb2)
    f32_shapes = [(n, CONV_CH), (n, CONV_CH), (n, XBC_CH), (n, SSM_CH), (n, SSM_CH), (n, SSM_CH), (n, SSM_CH),
                  (n, SSM_GROUPS * SSM_STATE)]
    f32_shapes += [(UPDATE_TERMS, n, SSM_CH), (UPDATE_TERMS, n, 2 * SSM_STATE)]
    return pl.pallas_call(
        _sample_pre_kernel,
        out_shape=[jax.ShapeDtypeStruct(s, F32) for s in f32_shapes],
        compiler_params=pltpu.CompilerParams(vmem_limit_bytes=VMEM_LIMIT),
        name="sample_pre",
    )(*args)


def _sample_state_kernel(s_ref, lhs_ref, rhs_ref, cm_ref, decx_ref, o_ref, yoff_ref, *, block):
    i = pl.program_id(0)
    rows = UPDATE_TERMS * block
    lhs_t = lhs_ref[...].reshape(rows, SSM_CH).T.astype(BF16)
    rhs_all = rhs_ref[...].reshape(rows, 2 * SSM_STATE)
    token_of = lax.broadcasted_iota(jnp.int32, rhs_all.shape, 0) % block

    def body(k, carry):
        b = i * block + k
        s = s_ref[k]
        upd = _dot(lhs_t, jnp.where(token_of == k, rhs_all, 0.0).astype(BF16))
        o_ref[k] = s * upd[:, SSM_STATE:2 * SSM_STATE] + upd[:, 0:SSM_STATE]
        cm = cm_ref[pl.ds(b, 1), :]
        sums = []
        for c0 in range(0, SSM_CH, LANES):
            g = c0 // SSM_GROUP_CH
            prod = s[c0:c0 + LANES, :] * cm[:, g * SSM_STATE:(g + 1) * SSM_STATE]
            sums.append(jnp.sum(prod.T, axis=0, keepdims=True))
        yoff_ref[pl.ds(b, 1), :] = jnp.concatenate(sums, axis=1) * decx_ref[pl.ds(b, 1), :]
        return carry

    lax.fori_loop(0, block, body, 0, unroll=True)


def _sample_state(state, lhs, rhs, cm, decx, block=8):
    n = state.shape[0]
    kern = functools.partial(_sample_state_kernel, block=block)
    return pl.pallas_call(
        kern,
        grid=(n // block,),
        in_specs=[pl.BlockSpec((block, SSM_CH, SSM_STATE), lambda i: (i, 0, 0)),
                  pl.BlockSpec((UPDATE_TERMS, block, SSM_CH), lambda i: (0, i, 0)),
                  pl.BlockSpec((UPDATE_TERMS, block, 2 * SSM_STATE), lambda i: (0, i, 0)),
                  _const_spec(cm.shape), _const_spec(decx.shape)],
        out_specs=[pl.BlockSpec((block, SSM_CH, SSM_STATE), lambda i: (i, 0, 0)),
                   pl.BlockSpec((n, SSM_CH), lambda i: (0, 0))],
        out_shape=[jax.ShapeDtypeStruct(state.shape, F32), jax.ShapeDtypeStruct((n, SSM_CH), F32)],
        compiler_params=pltpu.CompilerParams(dimension_semantics=("arbitrary",),
                                             vmem_limit_bytes=VMEM_LIMIT),
        name="sample_state",
    )(state, lhs, rhs, cm, decx)


def _sample_post_kernel(x_ref, mod_ref, yconv_ref, ydiag_ref, yoff_ref, xs_ref, z_ref, dexp_ref, snw_ref,
                        w_out_ref, ln_g_ref, ln_b_ref, x1_ref):
    g1 = mod_ref[:, 2 * D_MODEL:3 * D_MODEL]
    y = ydiag_ref[...] + yoff_ref[...] + xs_ref[...] * dexp_ref[...]
    y = y * _silu(z_ref[...])
    m = _mix_out(yconv_ref[...], _ssm_group_norm(y, snw_ref[...]), w_out_ref)
    x1_ref[...] = _layer_norm(ALPHA * x_ref[...] + (1.0 + g1) * m, ln_g_ref[...], ln_b_ref[...])


def _sample_post(x, mod, yconv, ydiag, yoff, xs, z, dexp, snw, w_out, ln_g, ln_b):
    return pl.pallas_call(
        _sample_post_kernel,
        out_shape=jax.ShapeDtypeStruct(x.shape, F32),
        compiler_params=pltpu.CompilerParams(vmem_limit_bytes=VMEM_LIMIT),
        name="sample_post",
    )(x, mod, yconv, ydiag, yoff, xs, z, dexp, snw, w_out, ln_g, ln_b)


def kernel(x_prompt, x_sample, state_conv, state_ssm_conv, state_ssm, c_prompt, c_sample, w_ada, b_ada, w_in, conv_w, conv_norm_w, ssm_conv_w, ssm_conv_b, dt_bias, a_log, d_skip, ssm_norm_w, w_out, ln1_g, ln1_b, w_up, w_down, ln2_g, ln2_b):
    assert w_ada.shape[0] == 1, "single-layer trunk"
    nb, seq, _ = x_prompt.shape
    ns = x_sample.shape[0]
    row = lambda a: a.reshape(1, -1)
    pad_heads = lambda a: jnp.pad(a.reshape(1, -1), ((0, 0), (0, LANES - SSM_HEADS)))

    w_in_b = _cast_cols(w_in[0], COL_DT)
    w_dt_b = jnp.pad(w_in[0, :, COL_DT:], ((0, 0), (0, IN_PAD - IN_COLS))).astype(BF16)
    w_out_b = w_out[0].astype(BF16)
    w_up_b = w_up[0].astype(BF16)
    w_down_b = w_down[0].astype(BF16)
    conv_nw, sconv_b, snw = row(conv_norm_w[0]), row(ssm_conv_b[0]), row(ssm_norm_w[0])
    dtb, alog = pad_heads(dt_bias[0]), pad_heads(a_log[0])
    dexp = row(jnp.repeat(d_skip[0], SSM_HEAD_DIM))
    g1, b1, g2, b2 = row(ln1_g[0]), row(ln1_b[0]), row(ln2_g[0]), row(ln2_b[0])

    mod_p, mod_s = _ada(c_sample, c_prompt, w_ada[0], row(b_ada[0]))
    mod_p = mod_p.reshape(nb, 1, 6 * D_MODEL)

    x1_p, cst_p, scst_p, sst_p = _mixer_prompt(x_prompt, mod_p, w_in_b, w_dt_b, conv_w[0], conv_nw, ssm_conv_w[0],
                                               sconv_b, dtb, alog, dexp, snw, w_out_b, g1, b1)
    y_p = _ffn(x1_p, mod_p, seq, w_up_b, w_down_b, g2, b2, tile=512)

    xs2 = x_sample.reshape(ns, D_MODEL)
    (yconv_s, ch_s, xbc_s, z_s, xs_s, ydiag_s, decx_s, cm_s, lhs_s, rhs_s) = _sample_pre(
        xs2, mod_s, w_in_b, w_dt_b, conv_w[0], conv_nw, ssm_conv_w[0], sconv_b, dtb, alog,
        state_conv[0, :, 0], state_conv[0, :, 1],
        state_ssm_conv[0, :, 0], state_ssm_conv[0, :, 1], state_ssm_conv[0, :, 2])
    new_state_s, yoff_s = _sample_state(state_ssm[0].reshape(ns, SSM_CH, SSM_STATE), lhs_s, rhs_s, cm_s, decx_s)
    x1_s = _sample_post(xs2, mod_s, yconv_s, ydiag_s, yoff_s, xs_s, z_s, dexp, snw, w_out_b, g1, b1)
    y_s = _ffn(x1_s, mod_s.reshape(1, ns, 6 * D_MODEL), ns, w_up_b, w_down_b, g2, b2, tile=ns)

    return (y_p.reshape(nb, seq, D_MODEL),
            y_s.reshape(ns, 1, D_MODEL),
            cst_p[None],
            scst_p[None],
            sst_p.reshape(1, nb, SSM_HEADS, SSM_HEAD_DIM, SSM_STATE),
            jnp.stack([state_conv[0, :, 1], ch_s], axis=1)[None],
            jnp.stack([state_ssm_conv[0, :, 1], state_ssm_conv[0, :, 2], xbc_s], axis=1)[None],
            new_state_s.reshape(1, ns, SSM_HEADS, SSM_HEAD_DIM, SSM_STATE))
```

```python
import functools

import jax
import jax.numpy as jnp
from jax import lax
from jax.experimental import pallas as pl
from jax.experimental.pallas import tpu as pltpu

F32 = jnp.float32
BF16 = jnp.bfloat16

D_MODEL = 1024
CONV_CH = 1024
CONV_GROUP = 64
SSM_CH = 1024
SSM_HEADS = 16
SSM_HEAD_DIM = 64
SSM_GROUPS = 2
SSM_GROUP_CH = SSM_CH // SSM_GROUPS
SSM_STATE = 128
SSM_CHUNK = 128
XBC_CH = SSM_CH + 2 * SSM_GROUPS * SSM_STATE
D_FF = 4 * D_MODEL
LANES = 128
SUBLANES = 8
MXU_COLS = 256
COL_GB, COL_GC, COL_HV, COL_Z, COL_XBC = 0, 1024, 2048, 3072, 4096
COL_DT = COL_XBC + XBC_CH
IN_COLS = COL_DT + SSM_HEADS
IN_PAD = COL_DT + LANES
ALPHA = 2.0 ** 0.25
LN_EPS = 1e-5
RMS_EPS = 1e-5
VMEM_LIMIT = 56 * 1024 * 1024


def _dot(a, b):
    return jnp.dot(a, b, preferred_element_type=F32)


def _split(a, terms):
    parts = []
    r = a
    for t in range(terms):
        p = r.astype(BF16)
        parts.append(p)
        if t + 1 < terms:
            r = r - p.astype(F32)
    return parts


def _dot_f32_lhs(a, b_exact, terms=3):
    parts = _split(a, terms)
    out = _dot(parts[0], b_exact)
    for p in parts[1:]:
        out = out + _dot(p, b_exact)
    return out


def _dot_f32_rhs(a_exact, b, terms=3):
    parts = _split(b, terms)
    out = _dot(a_exact, parts[0])
    for p in parts[1:]:
        out = out + _dot(a_exact, p)
    return out


def _head_expand():
    h = lax.broadcasted_iota(jnp.int32, (LANES, SSM_CH), 0)
    c = lax.broadcasted_iota(jnp.int32, (LANES, SSM_CH), 1)
    return (c // SSM_HEAD_DIM == h).astype(BF16)


def _group_reduce():
    c = lax.broadcasted_iota(jnp.int32, (CONV_CH, LANES), 0)
    k = lax.broadcasted_iota(jnp.int32, (CONV_CH, LANES), 1)
    return (c // CONV_GROUP == k).astype(BF16)


def _sigmoid(x):
    return 1.0 / (1.0 + jnp.exp(-x))


def _silu(x):
    return x * _sigmoid(x)


def _softplus(x):
    return jnp.maximum(x, 0.0) + jnp.log1p(jnp.exp(-jnp.abs(x)))


def _layer_norm(r, g, b):
    mu = jnp.mean(r, axis=-1, keepdims=True)
    d = r - mu
    var = jnp.mean(d * d, axis=-1, keepdims=True)
    return d * lax.rsqrt(var + LN_EPS) * g + b


def _conv_group_norm(prod, w, expand, reduce):
    ssum = _dot_f32_lhs(prod * prod, reduce, terms=2)
    rstd = lax.rsqrt(ssum * (1.0 / CONV_GROUP) + RMS_EPS)
    return prod * _dot_f32_lhs(rstd, expand, terms=2) * w


def _ssm_group_norm(y, w):
    outs = []
    for g in range(SSM_GROUPS):
        yg = y[:, g * SSM_GROUP_CH:(g + 1) * SSM_GROUP_CH]
        ms = jnp.mean(yg * yg, axis=-1, keepdims=True)
        outs.append((yg * lax.rsqrt(ms + RMS_EPS) * w[:, g * SSM_GROUP_CH:(g + 1) * SSM_GROUP_CH]).astype(BF16))
    return outs


def _mix_out(y_conv, y_ssm_groups, w_out_ref):
    m = _dot(y_conv.astype(BF16), w_out_ref[0:CONV_CH, :])
    for g, yg in enumerate(y_ssm_groups):
        lo = CONV_CH + g * SSM_GROUP_CH
        m = m + _dot(yg, w_out_ref[lo:lo + SSM_GROUP_CH, :])
    return m


def _ada_kernel(c_ref, w_ref, b_ref, op_ref, os_ref):
    c = c_ref[...]
    w = w_ref[...]
    c_hi = c.astype(BF16)
    c_lo = (c - c_hi.astype(F32)).astype(BF16)
    w_hi = w.astype(BF16)
    w_lo = (w - w_hi.astype(F32)).astype(BF16)
    mod = _dot(c_hi, w_hi) + _dot(c_hi, w_lo) + _dot(c_lo, w_hi) + b_ref[...]
    n_sample = os_ref.shape[0]
    os_ref[...] = mod[0:n_sample, :]
    op_ref[...] = mod[n_sample:, :]


def _ada(c_sample, c_prompt, w_ada, b_ada, tile_n=1024):
    ns, nb = c_sample.shape[0], c_prompt.shape[0]
    n = w_ada.shape[1]
    return pl.pallas_call(
        _ada_kernel,
        grid=(n // tile_n,),
        in_specs=[pl.BlockSpec((ns + nb, D_MODEL), lambda i: (0, 0)),
                  pl.BlockSpec((D_MODEL, tile_n), lambda i: (0, i)),
                  pl.BlockSpec((1, tile_n), lambda i: (0, i))],
        out_specs=[pl.BlockSpec((nb, tile_n), lambda i: (0, i)),
                   pl.BlockSpec((ns, tile_n), lambda i: (0, i))],
        out_shape=[jax.ShapeDtypeStruct((nb, n), F32), jax.ShapeDtypeStruct((ns, n), F32)],
        name="ada_mod",
    )(jnp.concatenate([c_sample, c_prompt], axis=0), w_ada, b_ada)


def _cast_in_proj_kernel(w_ref, o_ref, odt_ref):
    width = w_ref.shape[1]
    o_ref[...] = w_ref[:, 0:COL_DT].astype(BF16)
    tail = pltpu.roll(w_ref[:, width - LANES:width], width - COL_DT, axis=1)
    lane = lax.broadcasted_iota(jnp.int32, tail.shape, 1)
    odt_ref[...] = jnp.where(lane < width - COL_DT, tail, 0.0).astype(BF16)


def _cast_in_proj(w, tile_rows=128):
    rows, width = w.shape
    return pl.pallas_call(
        _cast_in_proj_kernel,
        grid=(rows // tile_rows,),
        in_specs=[pl.BlockSpec((tile_rows, width), lambda i: (i, 0))],
        out_specs=[pl.BlockSpec((tile_rows, COL_DT), lambda i: (i, 0)),
                   pl.BlockSpec((tile_rows, LANES), lambda i: (i, 0))],
        out_shape=[jax.ShapeDtypeStruct((rows, COL_DT), BF16), jax.ShapeDtypeStruct((rows, LANES), BF16)],
        name="cast_in_proj",
    )(w)


def _mixer_prompt_kernel(xa_ref, moda_ref, modb_ref, w_in_ref, w_dt_ref, expand_ref, reduce_ref,
                         conv_w_ref, conv_nw_ref, sconv_w_ref, sconv_b_ref,
                         dtb_ref, alog_ref, dexp_ref, snw_ref, w_out_ref, ln_g_ref, ln_b_ref,
                         x1_ref, cst_ref, scst_ref, sst_ref,
                         p0, p1, xk0, xk1, cbuf, xbuf, st_ref, xs_ref, bc_ref, dtx_ref, acsx_ref, endx_ref,
                         acst_ref, cb_ref, bmt_ref, y_ref, yc_ref,
                         *, tile, tiles_per_seq, sched):
    s = pl.program_id(0)
    jb = lax.rem(s + (tiles_per_seq - 1), tiles_per_seq)

    @pl.when(s == 0)
    def _():
        p1[...] = jnp.zeros_like(p1)
        xk1[...] = jnp.zeros_like(xk1)

    @pl.when((jb == 0) | (s == 0))
    def _():
        cbuf[...] = jnp.zeros_like(cbuf)
        xbuf[...] = jnp.zeros_like(xbuf)
        st_ref[...] = jnp.zeros_like(st_ref)

    def stages(pa, xka, pb, xkb):
        xa = xa_ref[...]
        xka[...] = xa
        u = (xa * (1.0 + moda_ref[:, D_MODEL:2 * D_MODEL]) + moda_ref[:, 0:D_MODEL]).astype(BF16)
        pieces = iter(list(range(0, COL_DT, MXU_COLS)) + [COL_DT])

        def first_stage(n):
            for _ in range(n):
                lo = next(pieces, None)
                if lo == COL_DT:
                    pa[:, COL_DT:IN_PAD] = _dot(u, w_dt_ref[...])
                elif lo is not None:
                    pa[:, lo:lo + MXU_COLS] = _dot(u, w_in_ref[:, lo:lo + MXU_COLS])

        expand = expand_ref[...]
        x = xkb[...]
        g1 = modb_ref[:, 2 * D_MODEL:3 * D_MODEL]

        def proj(lo, width):
            return pb[:, lo:lo + width]

        def delayed(tail_ref, cs, cur, taps):
            seq = jnp.concatenate([tail_ref[:, cs], cur], axis=0)
            tail_ref[:, cs] = cur[tile - SUBLANES:, :]
            return [pltpu.roll(seq, k, axis=0)[SUBLANES:, :] for k in range(1, taps + 1)]

        for k in range(CONV_CH // MXU_COLS):
            first_stage(sched[0])
            c0 = k * MXU_COLS
            cs = slice(c0, c0 + MXU_COLS)
            ch = proj(COL_GC + c0, MXU_COLS) * proj(COL_HV + c0, MXU_COLS)
            ch1, ch2 = delayed(cbuf, cs, ch, 2)
            cv = conv_w_ref[0:1, cs] * ch2 + conv_w_ref[1:2, cs] * ch1 + conv_w_ref[2:3, cs] * ch
            prod = proj(COL_GB + c0, MXU_COLS) * cv
            ssum = _dot_f32_lhs(prod * prod, reduce_ref[cs, :], terms=1)
            rstd = lax.rsqrt(ssum * (1.0 / CONV_GROUP) + RMS_EPS)
            yc_ref[:, cs] = (prod * _dot_f32_lhs(rstd, expand_ref[:, cs], terms=2)
                             * conv_nw_ref[:, cs]).astype(BF16)

        def pre_conv(c0):
            cs = slice(c0, c0 + MXU_COLS)
            xbc = proj(COL_XBC + c0, MXU_COLS)
            x1, x2, x3 = delayed(xbuf, cs, xbc, 3)
            return _silu(sconv_w_ref[0:1, cs] * x3 + sconv_w_ref[1:2, cs] * x2 + sconv_w_ref[2:3, cs] * x1
                         + sconv_w_ref[3:4, cs] * xbc + sconv_b_ref[:, cs])

        row = lax.broadcasted_iota(jnp.int32, (SSM_CHUNK, SSM_CHUNK), 0)
        col = lax.broadcasted_iota(jnp.int32, (SSM_CHUNK, SSM_CHUNK), 1)
        causal = row >= col
        tri = causal.astype(BF16)
        groups = SSM_CHUNK // SUBLANES
        causal_bias = jnp.where(causal, 0.0, -jnp.inf).reshape(groups, SUBLANES, SSM_CHUNK)
        first_half = (col < SSM_HEAD_DIM).reshape(groups, SUBLANES, SSM_CHUNK)
        half_rows = col < SSM_HEAD_DIM
        chunks = [slice(c * SSM_CHUNK, (c + 1) * SSM_CHUNK) for c in range(tile // SSM_CHUNK)]

        first_stage(sched[1])
        dt = _softplus(proj(COL_DT, LANES) + dtb_ref[...])
        dta = dt * (-jnp.exp(alog_ref[...]))
        dtx_ref[...] = _dot_f32_lhs(dt, expand, terms=1)
        for c, rows in enumerate(chunks):
            acs = _dot_f32_rhs(tri, dta[rows, :])
            acs_t = acs.T
            for h in range(SSM_HEADS):
                r8 = (c * SSM_HEADS + h) * SUBLANES
                acst_ref[r8:r8 + SUBLANES, :] = jnp.broadcast_to(acs_t[h:h + 1, :], (SUBLANES, SSM_CHUNK))
            acs_x = _dot_f32_lhs(acs, expand, terms=2)
            acsx_ref[rows, :] = acs_x
            endx_ref[c * SUBLANES:(c + 1) * SUBLANES, :] = jnp.broadcast_to(acs_x[SSM_CHUNK - 1:SSM_CHUNK, :],
                                                                             (SUBLANES, SSM_CH))
        for c0 in range(SSM_CH, XBC_CH, MXU_COLS):
            first_stage(sched[2])
            bc_ref[:, c0 - SSM_CH:c0 - SSM_CH + MXU_COLS] = pre_conv(c0)
        for rows in chunks:
            for g in range(SSM_GROUPS):
                gs = slice(g * SSM_STATE, (g + 1) * SSM_STATE)
                bm = bc_ref[rows, gs]
                cm = bc_ref[rows, (SSM_GROUPS + g) * SSM_STATE:(SSM_GROUPS + g + 1) * SSM_STATE]
                cb_ref[rows, gs] = lax.dot_general(cm.astype(BF16), bm.astype(BF16), (((1,), (1,)), ((), ())),
                                                   preferred_element_type=F32)
                bmt_ref[rows, gs] = bm.T.astype(BF16)

        for c0 in range(0, SSM_CH, MXU_COLS):
            first_stage(sched[3])
            cs = slice(c0, c0 + MXU_COLS)
            g = c0 // SSM_GROUP_CH
            gs = slice(g * SSM_STATE, (g + 1) * SSM_STATE)
            xs = pre_conv(c0)
            xs_ref[:, cs] = xs
            xdt = xs * dtx_ref[:, cs]
            for c, rows in enumerate(chunks):
                first_stage(sched[4])
                acs_x = acsx_ref[rows, cs].reshape(groups, SUBLANES, MXU_COLS)
                end_x = endx_ref[c * SUBLANES:(c + 1) * SUBLANES, cs]
                xdt_c = xdt[rows, :]
                xdec = (xdt_c * jnp.exp(end_x[None] - acs_x).reshape(SSM_CHUNK, MXU_COLS)).astype(BF16)
                cm = bc_ref[rows, (SSM_GROUPS + g) * SSM_STATE:(SSM_GROUPS + g + 1) * SSM_STATE].astype(BF16)
                cb = cb_ref[rows, gs]
                st = st_ref[:, cs]
                y_off = _dot(cm, st.astype(BF16)) * jnp.exp(acs_x).reshape(SSM_CHUNK, MXU_COLS)
                st_ref[:, cs] = ((st.reshape(groups, SUBLANES, MXU_COLS) * jnp.exp(end_x)[None])
                                 .reshape(SSM_STATE, MXU_COLS) + _dot(bmt_ref[rows, gs], xdec))
                for lo in range(0, MXU_COLS, LANES):
                    h0 = (c * SSM_HEADS + (c0 + lo) // SSM_HEAD_DIM) * SUBLANES
                    slab = acs_x[:, :, lo:lo + LANES]
                    rolled = pltpu.roll(slab, SSM_HEAD_DIM, axis=2)
                    a0 = jnp.where(first_half, slab, rolled) - acst_ref[h0:h0 + SUBLANES, :][None]
                    a1 = jnp.where(first_half, rolled, slab) - acst_ref[h0 + SUBLANES:h0 + 2 * SUBLANES, :][None]
                    l0 = jnp.exp(a0 + causal_bias).reshape(SSM_CHUNK, SSM_CHUNK)
                    l1 = jnp.exp(a1 + causal_bias).reshape(SSM_CHUNK, SSM_CHUNK)
                    m = jnp.concatenate([(cb * l0).astype(BF16), (cb * l1).astype(BF16)], axis=1)
                    xp = xdt_c[:, lo:lo + LANES]
                    rhs = jnp.concatenate([jnp.where(half_rows, xp, 0.0), jnp.where(half_rows, 0.0, xp)],
                                          axis=0).astype(BF16)
                    y_ref[rows, c0 + lo:c0 + lo + LANES] = _dot(m, rhs) + y_off[:, lo:lo + LANES]

        for k in range(SSM_CH // MXU_COLS):
            first_stage(sched[5])
            c0 = k * MXU_COLS
            cs = slice(c0, c0 + MXU_COLS)
            y_ref[:, cs] = (y_ref[:, cs] + xs_ref[:, cs] * dexp_ref[:, cs]) * _silu(proj(COL_Z + c0, MXU_COLS))
        first_stage(sched[6])
        m = _mix_out(yc_ref[...], _ssm_group_norm(y_ref[...], snw_ref[...]), w_out_ref)
        x1_ref[...] = _layer_norm(ALPHA * x + (1.0 + g1) * m, ln_g_ref[...], ln_b_ref[...])
        first_stage(IN_PAD // MXU_COLS)

    @pl.when(lax.rem(s, 2) == 0)
    def _():
        stages(p0, xk0, p1, xk1)

    @pl.when(lax.rem(s, 2) == 1)
    def _():
        stages(p1, xk1, p0, xk0)

    @pl.when((jb == tiles_per_seq - 1) & (s > 0))
    def _():
        cst_ref[...] = cbuf[SUBLANES - 2:SUBLANES, :]
        scst_ref[...] = xbuf[SUBLANES - 3:SUBLANES, :]
        sst_ref[...] = st_ref[...].T


def _const_spec(shape):
    return pl.BlockSpec(shape, lambda *_: (0,) * len(shape), pipeline_mode=pl.Buffered(1))


def _mixer_prompt(x, mod, w_in, w_dt, conv_w, conv_nw, sconv_w, sconv_b, dtb, alog, dexp, snw, w_out, ln_g, ln_b,
                  tile=256, sched=(1, 0, 0, 2, 1, 0, 0)):
    assert CONV_GROUP == SSM_HEAD_DIM and CONV_CH == SSM_CH
    nb, seq, _ = x.shape
    tiles_per_seq = seq // tile
    n_tiles = nb * tiles_per_seq
    kern = functools.partial(_mixer_prompt_kernel, tile=tile, tiles_per_seq=tiles_per_seq, sched=sched)
    consts = [w_in, w_dt, _head_expand(), _group_reduce(), conv_w, conv_nw, sconv_w, sconv_b, dtb, alog, dexp, snw,
              w_out, ln_g, ln_b]
    first = lambda s: jnp.minimum(s, n_tiles - 1)
    second = lambda s: jnp.maximum(s - 1, 0)
    return pl.pallas_call(
        kern,
        grid=(n_tiles + 1,),
        in_specs=[pl.BlockSpec((tile, D_MODEL), lambda s: (first(s), 0)),
                  pl.BlockSpec((None, 1, 6 * D_MODEL), lambda s: (first(s) // tiles_per_seq, 0, 0)),
                  pl.BlockSpec((None, 1, 6 * D_MODEL), lambda s: (second(s) // tiles_per_seq, 0, 0))]
                 + [_const_spec(a.shape) for a in consts],
        out_specs=[pl.BlockSpec((tile, D_MODEL), lambda s: (second(s), 0)),
                   pl.BlockSpec((None, 2, CONV_CH), lambda s: (second(s) // tiles_per_seq, 0, 0)),
                   pl.BlockSpec((None, 3, XBC_CH), lambda s: (second(s) // tiles_per_seq, 0, 0)),
                   pl.BlockSpec((None, SSM_CH, SSM_STATE), lambda s: (second(s) // tiles_per_seq, 0, 0))],
        out_shape=[jax.ShapeDtypeStruct((nb * seq, D_MODEL), F32),
                   jax.ShapeDtypeStruct((nb, 2, CONV_CH), F32),
                   jax.ShapeDtypeStruct((nb, 3, XBC_CH), F32),
                   jax.ShapeDtypeStruct((nb, SSM_CH, SSM_STATE), F32)],
        scratch_shapes=[pltpu.VMEM((tile, IN_PAD), F32),
                        pltpu.VMEM((tile, IN_PAD), F32),
                        pltpu.VMEM((tile, D_MODEL), F32),
                        pltpu.VMEM((tile, D_MODEL), F32),
                        pltpu.VMEM((SUBLANES, CONV_CH), F32),
                        pltpu.VMEM((SUBLANES, XBC_CH), F32),
                        pltpu.VMEM((SSM_STATE, SSM_CH), F32),
                        pltpu.VMEM((tile, SSM_CH), F32),
                        pltpu.VMEM((tile, 2 * SSM_GROUPS * SSM_STATE), F32),
                        pltpu.VMEM((tile, SSM_CH), F32),
                        pltpu.VMEM((tile, SSM_CH), F32),
                        pltpu.VMEM((tile // SSM_CHUNK * SUBLANES, SSM_CH), F32),
                        pltpu.VMEM((tile // SSM_CHUNK * SSM_HEADS * SUBLANES, SSM_CHUNK), F32),
                        pltpu.VMEM((tile, SSM_GROUPS * SSM_STATE), F32),
                        pltpu.VMEM((tile, SSM_GROUPS * SSM_STATE), BF16),
                        pltpu.VMEM((tile, SSM_CH), F32),
                        pltpu.VMEM((tile, CONV_CH), BF16)],
        compiler_params=pltpu.CompilerParams(dimension_semantics=("arbitrary",),
                                             vmem_limit_bytes=VMEM_LIMIT),
        name="mixer_prompt",
    )(x.reshape(nb * seq, D_MODEL), mod, mod, *consts)


def _ffn_kernel(x_ref, mod_ref, w_up_ref, w_down_ref, ln_g_ref, ln_b_ref, o_ref, *, ff_tile):
    x = x_ref[...]
    sh2 = mod_ref[:, 3 * D_MODEL:4 * D_MODEL]
    sc2 = mod_ref[:, 4 * D_MODEL:5 * D_MODEL]
    g2 = mod_ref[:, 5 * D_MODEL:6 * D_MODEL]
    v = (x * (1.0 + sc2) + sh2).astype(BF16)
    acc = jnp.zeros(x.shape, F32)
    for k in range(D_FF // ff_tile):
        h = jnp.maximum(_dot(v, w_up_ref[:, k * ff_tile:(k + 1) * ff_tile]), 0.0)
        acc = acc + _dot((h * h).astype(BF16), w_down_ref[k * ff_tile:(k + 1) * ff_tile, :])
    o_ref[...] = _layer_norm(ALPHA * x + (1.0 + g2) * acc, ln_g_ref[...], ln_b_ref[...])


def _ffn(x, mod, rows_per_mod, w_up, w_down, ln_g, ln_b, tile, ff_tile=1024):
    rows = x.shape[0]
    mod_rows = mod.shape[1]
    tiles_per_mod = rows_per_mod // tile
    kern = functools.partial(_ffn_kernel, ff_tile=ff_tile)
    return pl.pallas_call(
        kern,
        grid=(rows // tile,),
        in_specs=[pl.BlockSpec((tile, D_MODEL), lambda i: (i, 0)),
                  pl.BlockSpec((None, mod_rows, 6 * D_MODEL), lambda i: (i // tiles_per_mod, 0, 0)),
                  _const_spec(w_up.shape), _const_spec(w_down.shape),
                  _const_spec(ln_g.shape), _const_spec(ln_b.shape)],
        out_specs=pl.BlockSpec((tile, D_MODEL), lambda i: (i, 0)),
        out_shape=jax.ShapeDtypeStruct((rows, D_MODEL), F32),
        compiler_params=pltpu.CompilerParams(dimension_semantics=("arbitrary",),
                                             vmem_limit_bytes=VMEM_LIMIT),
        name="ffn",
    )(x, mod, w_up, w_down, ln_g, ln_b)


_PRODUCT_TERMS = ((0, 0), (0, 1), (1, 0), (0, 2), (2, 0), (1, 1))
UPDATE_TERMS = 16


def _sample_pre_kernel(x_ref, mod_ref, w_in_ref, w_dt_ref, conv_w_ref, conv_nw_ref, sconv_w_ref, sconv_b_ref,
                       dtb_ref, alog_ref, cb0_ref, cb1_ref, sb0_ref, sb1_ref, sb2_ref,
                       yconv_ref, ch_ref, xbc_ref, z_ref, xs_ref, ydiag_ref, decx_ref, cm_ref, lhs_ref, rhs_ref):
    expand = _head_expand()
    reduce = _group_reduce()
    x = x_ref[...]
    sh1 = mod_ref[:, 0:D_MODEL]
    sc1 = mod_ref[:, D_MODEL:2 * D_MODEL]
    u = (x * (1.0 + sc1) + sh1).astype(BF16)

    def proj(lo, width):
        return _dot(u, w_in_ref[:, lo:lo + width])

    ch = proj(COL_GC, CONV_CH) * proj(COL_HV, CONV_CH)
    ch_ref[...] = ch
    cw = conv_w_ref[...]
    cv = cw[0:1, :] * cb0_ref[...] + cw[1:2, :] * cb1_ref[...] + cw[2:3, :] * ch
    yconv_ref[...] = _conv_group_norm(proj(COL_GB, CONV_CH) * cv, conv_nw_ref[...], expand, reduce)

    xbc = proj(COL_XBC, XBC_CH)
    xbc_ref[...] = xbc
    sw = sconv_w_ref[...]
    xc = _silu(sw[0:1, :] * sb0_ref[...] + sw[1:2, :] * sb1_ref[...] + sw[2:3, :] * sb2_ref[...]
               + sw[3:4, :] * xbc + sconv_b_ref[...])
    xs = xc[:, 0:SSM_CH]
    xs_ref[...] = xs
    cm_ref[...] = xc[:, SSM_CH + SSM_GROUPS * SSM_STATE:XBC_CH]
    z_ref[...] = proj(COL_Z, SSM_CH)

    dt = _softplus(_dot(u, w_dt_ref[...]) + dtb_ref[...])
    dta = dt * (-jnp.exp(alog_ref[...]))
    xdt = xs * _dot_f32_lhs(dt, expand)
    decx = jnp.exp(_dot_f32_lhs(dta, expand))
    decx_ref[...] = decx
    xdt_t, dec_t = _split(xdt, 3), _split(decx, 3)
    bm_t = _split(xc[:, SSM_CH:SSM_CH + SSM_GROUPS * SSM_STATE], 3)
    group_of = lax.broadcasted_iota(jnp.int32, xdt.shape, 1) // SSM_GROUP_CH
    zeros = jnp.zeros((x.shape[0], SSM_STATE), F32)
    r = 0
    for g in range(SSM_GROUPS):
        for tx, tb in _PRODUCT_TERMS:
            lhs_ref[r] = jnp.where(group_of == g, xdt_t[tx].astype(F32), 0.0)
            rhs_ref[r] = jnp.concatenate([bm_t[tb][:, g * SSM_STATE:(g + 1) * SSM_STATE].astype(F32), zeros], axis=1)
            r += 1
    for t in range(3):
        lhs_ref[r] = dec_t[t].astype(F32)
        rhs_ref[r] = jnp.concatenate([zeros, zeros + 1.0], axis=1)
        r += 1
    for r in range(r, UPDATE_TERMS):
        lhs_ref[r] = jnp.zeros_like(xdt)
        rhs_ref[r] = jnp.concatenate([zeros, zeros], axis=1)
    for g in range(SSM_GROUPS):
        bm = xc[:, SSM_CH + g * SSM_STATE:SSM_CH + (g + 1) * SSM_STATE]
        cm = xc[:, SSM_CH + (SSM_GROUPS + g) * SSM_STATE:SSM_CH + (SSM_GROUPS + g + 1) * SSM_STATE]
        cb = jnp.sum(cm * bm, axis=-1, keepdims=True)
        gl = g * SSM_GROUP_CH
        ydiag_ref[:, gl:gl + SSM_GROUP_CH] = cb * xdt[:, gl:gl + SSM_GROUP_CH]


def _sample_pre(x, mod, w_in, w_dt, conv_w, conv_nw, sconv_w, sconv_b, dtb, alog, cb0, cb1, sb0, sb1, sb2):
    n = x.shape[0]
    args = (x, mod, w_in, w_dt, conv_w, conv_nw, sconv_w, sconv_b, dtb, alog, cb0, cb1, sb0, sb1, sb2)
    f32_shapes = [(n, CONV_CH), (n, CONV_CH), (n, XBC_CH), (n, SSM_CH), (n, SSM_CH), (n, SSM_CH), (n, SSM_CH),
                  (n, SSM_GROUPS * SSM_STATE)]
    f32_shapes += [(UPDATE_TERMS, n, SSM_CH), (UPDATE_TERMS, n, 2 * SSM_STATE)]
    return pl.pallas_call(
        _sample_pre_kernel,
        out_shape=[jax.ShapeDtypeStruct(s, F32) for s in f32_shapes],
        compiler_params=pltpu.CompilerParams(vmem_limit_bytes=VMEM_LIMIT),
        name="sample_pre",
    )(*args)


def _sample_state_kernel(s_ref, lhs_ref, rhs_ref, cm_ref, decx_ref, o_ref, yoff_ref, *, block):
    i = pl.program_id(0)
    rows = UPDATE_TERMS * block
    lhs_t = lhs_ref[...].reshape(rows, SSM_CH).T.astype(BF16)
    rhs_all = rhs_ref[...].reshape(rows, 2 * SSM_STATE)
    token_of = lax.broadcasted_iota(jnp.int32, rhs_all.shape, 0) % block

    def body(k, carry):
        b = i * block + k
        s = s_ref[k]
        upd = _dot(lhs_t, jnp.where(token_of == k, rhs_all, 0.0).astype(BF16))
        o_ref[k] = s * upd[:, SSM_STATE:2 * SSM_STATE] + upd[:, 0:SSM_STATE]
        cm = cm_ref[pl.ds(b, 1), :]
        sums = []
        for c0 in range(0, SSM_CH, LANES):
            g = c0 // SSM_GROUP_CH
            prod = s[c0:c0 + LANES, :] * cm[:, g * SSM_STATE:(g + 1) * SSM_STATE]
            sums.append(jnp.sum(prod.T, axis=0, keepdims=True))
        yoff_ref[pl.ds(b, 1), :] = jnp.concatenate(sums, axis=1) * decx_ref[pl.ds(b, 1), :]
        return carry

    lax.fori_loop(0, block, body, 0, unroll=True)


def _sample_state(state, lhs, rhs, cm, decx, block=8):
    n = state.shape[0]
    kern = functools.partial(_sample_state_kernel, block=block)
    return pl.pallas_call(
        kern,
        grid=(n // block,),
        in_specs=[pl.BlockSpec((block, SSM_CH, SSM_STATE), lambda i: (i, 0, 0)),
                  pl.BlockSpec((UPDATE_TERMS, block, SSM_CH), lambda i: (0, i, 0)),
                  pl.BlockSpec((UPDATE_TERMS, block, 2 * SSM_STATE), lambda i: (0, i, 0)),
                  _const_spec(cm.shape), _const_spec(decx.shape)],
        out_specs=[pl.BlockSpec((block, SSM_CH, SSM_STATE), lambda i: (i, 0, 0)),
                   pl.BlockSpec((n, SSM_CH), lambda i: (0, 0))],
        out_shape=[jax.ShapeDtypeStruct(state.shape, F32), jax.ShapeDtypeStruct((n, SSM_CH), F32)],
        compiler_params=pltpu.CompilerParams(dimension_semantics=("arbitrary",),
                                             vmem_limit_bytes=VMEM_LIMIT),
        name="sample_state",
    )(state, lhs, rhs, cm, decx)


def _sample_post_kernel(x_ref, mod_ref, yconv_ref, ydiag_ref, yoff_ref, xs_ref, z_ref, dexp_ref, snw_ref,
                        w_out_ref, ln_g_ref, ln_b_ref, x1_ref):
    g1 = mod_ref[:, 2 * D_MODEL:3 * D_MODEL]
    y = ydiag_ref[...] + yoff_ref[...] + xs_ref[...] * dexp_ref[...]
    y = y * _silu(z_ref[...])
    m = _mix_out(yconv_ref[...], _ssm_group_norm(y, snw_ref[...]), w_out_ref)
    x1_ref[...] = _layer_norm(ALPHA * x_ref[...] + (1.0 + g1) * m, ln_g_ref[...], ln_b_ref[...])


def _sample_post(x, mod, yconv, ydiag, yoff, xs, z, dexp, snw, w_out, ln_g, ln_b):
    return pl.pallas_call(
        _sample_post_kernel,
        out_shape=jax.ShapeDtypeStruct(x.shape, F32),
        compiler_params=pltpu.CompilerParams(vmem_limit_bytes=VMEM_LIMIT),
        name="sample_post",
    )(x, mod, yconv, ydiag, yoff, xs, z, dexp, snw, w_out, ln_g, ln_b)


def kernel(x_prompt, x_sample, state_conv, state_ssm_conv, state_ssm, c_prompt, c_sample, w_ada, b_ada, w_in, conv_w, conv_norm_w, ssm_conv_w, ssm_conv_b, dt_bias, a_log, d_skip, ssm_norm_w, w_out, ln1_g, ln1_b, w_up, w_down, ln2_g, ln2_b):
    assert w_ada.shape[0] == 1, "single-layer trunk"
    nb, seq, _ = x_prompt.shape
    ns = x_sample.shape[0]
    row = lambda a: a.reshape(1, -1)
    pad_heads = lambda a: jnp.pad(a.reshape(1, -1), ((0, 0), (0, LANES - SSM_HEADS)))

    w_in_b, w_dt_b = _cast_in_proj(w_in[0])
    w_out_b = w_out[0].astype(BF16)
    w_up_b = w_up[0].astype(BF16)
    w_down_b = w_down[0].astype(BF16)
    conv_nw, sconv_b, snw = row(conv_norm_w[0]), row(ssm_conv_b[0]), row(ssm_norm_w[0])
    dtb, alog = pad_heads(dt_bias[0]), pad_heads(a_log[0])
    dexp = row(jnp.repeat(d_skip[0], SSM_HEAD_DIM))
    g1, b1, g2, b2 = row(ln1_g[0]), row(ln1_b[0]), row(ln2_g[0]), row(ln2_b[0])

    mod_p, mod_s = _ada(c_sample, c_prompt, w_ada[0], row(b_ada[0]))
    mod_p = mod_p.reshape(nb, 1, 6 * D_MODEL)

    x1_p, cst_p, scst_p, sst_p = _mixer_prompt(x_prompt, mod_p, w_in_b, w_dt_b, conv_w[0], conv_nw, ssm_conv_w[0],
                                               sconv_b, dtb, alog, dexp, snw, w_out_b, g1, b1)
    y_p = _ffn(x1_p, mod_p, seq, w_up_b, w_down_b, g2, b2, tile=512)

    xs2 = x_sample.reshape(ns, D_MODEL)
    (yconv_s, ch_s, xbc_s, z_s, xs_s, ydiag_s, decx_s, cm_s, lhs_s, rhs_s) = _sample_pre(
        xs2, mod_s, w_in_b, w_dt_b, conv_w[0], conv_nw, ssm_conv_w[0], sconv_b, dtb, alog,
        state_conv[0, :, 0], state_conv[0, :, 1],
        state_ssm_conv[0, :, 0], state_ssm_conv[0, :, 1], state_ssm_conv[0, :, 2])
    new_state_s, yoff_s = _sample_state(state_ssm[0].reshape(ns, SSM_CH, SSM_STATE), lhs_s, rhs_s, cm_s, decx_s)
    x1_s = _sample_post(xs2, mod_s, yconv_s, ydiag_s, yoff_s, xs_s, z_s, dexp, snw, w_out_b, g1, b1)
    y_s = _ffn(x1_s, mod_s.reshape(1, ns, 6 * D_MODEL), ns, w_up_b, w_down_b, g2, b2, tile=ns)

    return (y_p.reshape(nb, seq, D_MODEL),
            y_s.reshape(ns, 1, D_MODEL),
            cst_p[None],
            scst_p[None],
            sst_p.reshape(1, nb, SSM_HEADS, SSM_HEAD_DIM, SSM_STATE),
            jnp.stack([state_conv[0, :, 1], ch_s], axis=1)[None],
            jnp.stack([state_ssm_conv[0, :, 1], state_ssm_conv[0, :, 2], xbc_s], axis=1)[None],
            new_state_s.reshape(1, ns, SSM_HEADS, SSM_HEAD_DIM, SSM_STATE))
```

```python
import functools

import jax
import jax.numpy as jnp
from jax import lax
from jax.experimental import pallas as pl
from jax.experimental.pallas import tpu as pltpu

F32 = jnp.float32
BF16 = jnp.bfloat16

D_MODEL = 1024
CONV_CH = 1024
CONV_GROUP = 64
SSM_CH = 1024
SSM_HEADS = 16
SSM_HEAD_DIM = 64
SSM_GROUPS = 2
SSM_GROUP_CH = SSM_CH // SSM_GROUPS
SSM_STATE = 128
SSM_CHUNK = 128
XBC_CH = SSM_CH + 2 * SSM_GROUPS * SSM_STATE
D_FF = 4 * D_MODEL
LANES = 128
SUBLANES = 8
MXU_COLS = 256
COL_GB, COL_GC, COL_HV, COL_Z, COL_XBC = 0, 1024, 2048, 3072, 4096
COL_DT = COL_XBC + XBC_CH
IN_COLS = COL_DT + SSM_HEADS
IN_PAD = COL_DT + LANES
ALPHA = 2.0 ** 0.25
LN_EPS = 1e-5
RMS_EPS = 1e-5
VMEM_LIMIT = 56 * 1024 * 1024


def _dot(a, b):
    return jnp.dot(a, b, preferred_element_type=F32)


def _split(a, terms):
    parts = []
    r = a
    for t in range(terms):
        p = r.astype(BF16)
        parts.append(p)
        if t + 1 < terms:
            r = r - p.astype(F32)
    return parts


def _dot_f32_lhs(a, b_exact, terms=3):
    parts = _split(a, terms)
    out = _dot(parts[0], b_exact)
    for p in parts[1:]:
        out = out + _dot(p, b_exact)
    return out


def _dot_f32_rhs(a_exact, b, terms=3):
    parts = _split(b, terms)
    out = _dot(a_exact, parts[0])
    for p in parts[1:]:
        out = out + _dot(a_exact, p)
    return out


def _head_expand():
    h = lax.broadcasted_iota(jnp.int32, (LANES, SSM_CH), 0)
    c = lax.broadcasted_iota(jnp.int32, (LANES, SSM_CH), 1)
    return (c // SSM_HEAD_DIM == h).astype(BF16)


def _group_reduce():
    c = lax.broadcasted_iota(jnp.int32, (CONV_CH, LANES), 0)
    k = lax.broadcasted_iota(jnp.int32, (CONV_CH, LANES), 1)
    return (c // CONV_GROUP == k).astype(BF16)


def _sigmoid(x):
    return 1.0 / (1.0 + jnp.exp(-x))


def _silu(x):
    return x * _sigmoid(x)


def _softplus(x):
    return jnp.maximum(x, 0.0) + jnp.log1p(jnp.exp(-jnp.abs(x)))


def _layer_norm(r, g, b):
    mu = jnp.mean(r, axis=-1, keepdims=True)
    d = r - mu
    var = jnp.mean(d * d, axis=-1, keepdims=True)
    return d * lax.rsqrt(var + LN_EPS) * g + b


def _conv_group_norm(prod, w, expand, reduce):
    ssum = _dot_f32_lhs(prod * prod, reduce, terms=2)
    rstd = lax.rsqrt(ssum * (1.0 / CONV_GROUP) + RMS_EPS)
    return prod * _dot_f32_lhs(rstd, expand, terms=2) * w


def _ssm_group_norm(y, w):
    outs = []
    for g in range(SSM_GROUPS):
        yg = y[:, g * SSM_GROUP_CH:(g + 1) * SSM_GROUP_CH]
        ms = jnp.mean(yg * yg, axis=-1, keepdims=True)
        outs.append((yg * lax.rsqrt(ms + RMS_EPS) * w[:, g * SSM_GROUP_CH:(g + 1) * SSM_GROUP_CH]).astype(BF16))
    return outs


def _mix_out(y_conv, y_ssm_groups, w_out_ref):
    m = _dot(y_conv.astype(BF16), w_out_ref[0:CONV_CH, :])
    for g, yg in enumerate(y_ssm_groups):
        lo = CONV_CH + g * SSM_GROUP_CH
        m = m + _dot(yg, w_out_ref[lo:lo + SSM_GROUP_CH, :])
    return m


def _ada_kernel(c_ref, w_ref, b_ref, op_ref, os_ref):
    c = c_ref[...]
    w = w_ref[...]
    c_hi = c.astype(BF16)
    c_lo = (c - c_hi.astype(F32)).astype(BF16)
    w_hi = w.astype(BF16)
    w_lo = (w - w_hi.astype(F32)).astype(BF16)
    mod = _dot(c_hi, w_hi) + _dot(c_hi, w_lo) + _dot(c_lo, w_hi) + b_ref[...]
    n_sample = os_ref.shape[0]
    os_ref[...] = mod[0:n_sample, :]
    op_ref[...] = mod[n_sample:, :]


def _ada(c_sample, c_prompt, w_ada, b_ada, tile_n=1024):
    ns, nb = c_sample.shape[0], c_prompt.shape[0]
    n = w_ada.shape[1]
    return pl.pallas_call(
        _ada_kernel,
        grid=(n // tile_n,),
        in_specs=[pl.BlockSpec((ns + nb, D_MODEL), lambda i: (0, 0)),
                  pl.BlockSpec((D_MODEL, tile_n), lambda i: (0, i)),
                  pl.BlockSpec((1, tile_n), lambda i: (0, i))],
        out_specs=[pl.BlockSpec((nb, tile_n), lambda i: (0, i)),
                   pl.BlockSpec((ns, tile_n), lambda i: (0, i))],
        out_shape=[jax.ShapeDtypeStruct((nb, n), F32), jax.ShapeDtypeStruct((ns, n), F32)],
        name="ada_mod",
    )(jnp.concatenate([c_sample, c_prompt], axis=0), w_ada, b_ada)


def _cast_transposed_kernel(wt_ref, o_ref):
    rows = wt_ref.shape[0]
    wt = wt_ref[...]
    if rows < o_ref.shape[1]:
        wt = jnp.concatenate([wt, jnp.zeros((o_ref.shape[1] - rows, wt.shape[1]), wt.dtype)], axis=0)
    o_ref[...] = wt.T.astype(o_ref.dtype)


def _cast_in_proj(w):
    wt = w.T
    n_dt = w.shape[1] - COL_DT
    main = pl.pallas_call(
        _cast_transposed_kernel,
        grid=(COL_DT // MXU_COLS,),
        in_specs=[pl.BlockSpec((MXU_COLS, D_MODEL), lambda j: (j, 0))],
        out_specs=pl.BlockSpec((D_MODEL, MXU_COLS), lambda j: (0, j)),
        out_shape=jax.ShapeDtypeStruct((D_MODEL, COL_DT), BF16),
        name="cast_in_proj",
    )(wt)
    dt = pl.pallas_call(
        _cast_transposed_kernel,
        grid=(1,),
        in_specs=[pl.BlockSpec((n_dt, D_MODEL), lambda j: (COL_DT // n_dt, 0))],
        out_specs=pl.BlockSpec((D_MODEL, LANES), lambda j: (0, 0)),
        out_shape=jax.ShapeDtypeStruct((D_MODEL, LANES), BF16),
        name="cast_in_proj_dt",
    )(wt)
    return main, dt


def _mixer_prompt_kernel(xa_ref, moda_ref, modb_ref, w_in_ref, w_dt_ref, expand_ref, reduce_ref,
                         conv_w_ref, conv_nw_ref, sconv_w_ref, sconv_b_ref,
                         dtb_ref, alog_ref, dexp_ref, snw_ref, w_out_ref, ln_g_ref, ln_b_ref,
                         x1_ref, cst_ref, scst_ref, sst_ref,
                         p0, p1, xk0, xk1, cbuf, xbuf, st_ref, xs_ref, bc_ref, dtx_ref, acsx_ref, endx_ref,
                         acst_ref, cb_ref, bmt_ref, y_ref, yc_ref,
                         *, tile, tiles_per_seq, sched):
    s = pl.program_id(0)
    jb = lax.rem(s + (tiles_per_seq - 1), tiles_per_seq)

    @pl.when(s == 0)
    def _():
        p1[...] = jnp.zeros_like(p1)
        xk1[...] = jnp.zeros_like(xk1)

    @pl.when((jb == 0) | (s == 0))
    def _():
        cbuf[...] = jnp.zeros_like(cbuf)
        xbuf[...] = jnp.zeros_like(xbuf)
        st_ref[...] = jnp.zeros_like(st_ref)

    def stages(pa, xka, pb, xkb):
        xa = xa_ref[...]
        xka[...] = xa
        u = (xa * (1.0 + moda_ref[:, D_MODEL:2 * D_MODEL]) + moda_ref[:, 0:D_MODEL]).astype(BF16)
        pieces = iter(list(range(0, COL_DT, MXU_COLS)) + [COL_DT])

        def first_stage(n):
            for _ in range(n):
                lo = next(pieces, None)
                if lo == COL_DT:
                    pa[:, COL_DT:IN_PAD] = _dot(u, w_dt_ref[...])
                elif lo is not None:
                    pa[:, lo:lo + MXU_COLS] = _dot(u, w_in_ref[:, lo:lo + MXU_COLS])

        expand = expand_ref[...]
        x = xkb[...]
        g1 = modb_ref[:, 2 * D_MODEL:3 * D_MODEL]

        def proj(lo, width):
            return pb[:, lo:lo + width]

        def delayed(tail_ref, cs, cur, taps):
            seq = jnp.concatenate([tail_ref[:, cs], cur], axis=0)
            tail_ref[:, cs] = cur[tile - SUBLANES:, :]
            return [pltpu.roll(seq, k, axis=0)[SUBLANES:, :] for k in range(1, taps + 1)]

        for k in range(CONV_CH // MXU_COLS):
            first_stage(sched[0])
            c0 = k * MXU_COLS
            cs = slice(c0, c0 + MXU_COLS)
            ch = proj(COL_GC + c0, MXU_COLS) * proj(COL_HV + c0, MXU_COLS)
            ch1, ch2 = delayed(cbuf, cs, ch, 2)
            cv = conv_w_ref[0:1, cs] * ch2 + conv_w_ref[1:2, cs] * ch1 + conv_w_ref[2:3, cs] * ch
            prod = proj(COL_GB + c0, MXU_COLS) * cv
            ssum = _dot_f32_lhs(prod * prod, reduce_ref[cs, :], terms=1)
            rstd = lax.rsqrt(ssum * (1.0 / CONV_GROUP) + RMS_EPS)
            yc_ref[:, cs] = (prod * _dot_f32_lhs(rstd, expand_ref[:, cs], terms=2)
                             * conv_nw_ref[:, cs]).astype(BF16)

        def pre_conv(c0):
            cs = slice(c0, c0 + MXU_COLS)
            xbc = proj(COL_XBC + c0, MXU_COLS)
            x1, x2, x3 = delayed(xbuf, cs, xbc, 3)
            return _silu(sconv_w_ref[0:1, cs] * x3 + sconv_w_ref[1:2, cs] * x2 + sconv_w_ref[2:3, cs] * x1
                         + sconv_w_ref[3:4, cs] * xbc + sconv_b_ref[:, cs])

        row = lax.broadcasted_iota(jnp.int32, (SSM_CHUNK, SSM_CHUNK), 0)
        col = lax.broadcasted_iota(jnp.int32, (SSM_CHUNK, SSM_CHUNK), 1)
        causal = row >= col
        tri = causal.astype(BF16)
        groups = SSM_CHUNK // SUBLANES
        causal_bias = jnp.where(causal, 0.0, -jnp.inf).reshape(groups, SUBLANES, SSM_CHUNK)
        first_half = (col < SSM_HEAD_DIM).reshape(groups, SUBLANES, SSM_CHUNK)
        half_rows = col < SSM_HEAD_DIM
        chunks = [slice(c * SSM_CHUNK, (c + 1) * SSM_CHUNK) for c in range(tile // SSM_CHUNK)]

        first_stage(sched[1])
        dt = _softplus(proj(COL_DT, LANES) + dtb_ref[...])
        dta = dt * (-jnp.exp(alog_ref[...]))
        dtx_ref[...] = _dot_f32_lhs(dt, expand, terms=1)
        for c, rows in enumerate(chunks):
            acs = _dot_f32_rhs(tri, dta[rows, :])
            acs_t = acs.T
            for h in range(SSM_HEADS):
                r8 = (c * SSM_HEADS + h) * SUBLANES
                acst_ref[r8:r8 + SUBLANES, :] = jnp.broadcast_to(acs_t[h:h + 1, :], (SUBLANES, SSM_CHUNK))
            acs_x = _dot_f32_lhs(acs, expand, terms=2)
            acsx_ref[rows, :] = acs_x
            endx_ref[c * SUBLANES:(c + 1) * SUBLANES, :] = jnp.broadcast_to(acs_x[SSM_CHUNK - 1:SSM_CHUNK, :],
                                                                             (SUBLANES, SSM_CH))
        for c0 in range(SSM_CH, XBC_CH, MXU_COLS):
            first_stage(sched[2])
            bc_ref[:, c0 - SSM_CH:c0 - SSM_CH + MXU_COLS] = pre_conv(c0)
        for rows in chunks:
            for g in range(SSM_GROUPS):
                gs = slice(g * SSM_STATE, (g + 1) * SSM_STATE)
                bm = bc_ref[rows, gs]
                cm = bc_ref[rows, (SSM_GROUPS + g) * SSM_STATE:(SSM_GROUPS + g + 1) * SSM_STATE]
                cb_ref[rows, gs] = lax.dot_general(cm.astype(BF16), bm.astype(BF16), (((1,), (1,)), ((), ())),
                                                   preferred_element_type=F32)
                bmt_ref[rows, gs] = bm.T.astype(BF16)

        for c0 in range(0, SSM_CH, MXU_COLS):
            first_stage(sched[3])
            cs = slice(c0, c0 + MXU_COLS)
            g = c0 // SSM_GROUP_CH
            gs = slice(g * SSM_STATE, (g + 1) * SSM_STATE)
            xs = pre_conv(c0)
            xs_ref[:, cs] = xs
            xdt = xs * dtx_ref[:, cs]
            for c, rows in enumerate(chunks):
                first_stage(sched[4])
                acs_x = acsx_ref[rows, cs].reshape(groups, SUBLANES, MXU_COLS)
                end_x = endx_ref[c * SUBLANES:(c + 1) * SUBLANES, cs]
                xdt_c = xdt[rows, :]
                xdec = (xdt_c * jnp.exp(end_x[None] - acs_x).reshape(SSM_CHUNK, MXU_COLS)).astype(BF16)
                cm = bc_ref[rows, (SSM_GROUPS + g) * SSM_STATE:(SSM_GROUPS + g + 1) * SSM_STATE].astype(BF16)
                cb = cb_ref[rows, gs]
                st = st_ref[:, cs]
                y_off = _dot(cm, st.astype(BF16)) * jnp.exp(acs_x).reshape(SSM_CHUNK, MXU_COLS)
                st_ref[:, cs] = ((st.reshape(groups, SUBLANES, MXU_COLS) * jnp.exp(end_x)[None])
                                 .reshape(SSM_STATE, MXU_COLS) + _dot(bmt_ref[rows, gs], xdec))
                for lo in range(0, MXU_COLS, LANES):
                    h0 = (c * SSM_HEADS + (c0 + lo) // SSM_HEAD_DIM) * SUBLANES
                    slab = acs_x[:, :, lo:lo + LANES]
                    rolled = pltpu.roll(slab, SSM_HEAD_DIM, axis=2)
                    a0 = jnp.where(first_half, slab, rolled) - acst_ref[h0:h0 + SUBLANES, :][None]
                    a1 = jnp.where(first_half, rolled, slab) - acst_ref[h0 + SUBLANES:h0 + 2 * SUBLANES, :][None]
                    l0 = jnp.exp(a0 + causal_bias).reshape(SSM_CHUNK, SSM_CHUNK)
                    l1 = jnp.exp(a1 + causal_bias).reshape(SSM_CHUNK, SSM_CHUNK)
                    m = jnp.concatenate([(cb * l0).astype(BF16), (cb * l1).astype(BF16)], axis=1)
                    xp = xdt_c[:, lo:lo + LANES]
                    rhs = jnp.concatenate([jnp.where(half_rows, xp, 0.0), jnp.where(half_rows, 0.0, xp)],
                                          axis=0).astype(BF16)
                    y_ref[rows, c0 + lo:c0 + lo + LANES] = _dot(m, rhs) + y_off[:, lo:lo + LANES]

        for k in range(SSM_CH // MXU_COLS):
            first_stage(sched[5])
            c0 = k * MXU_COLS
            cs = slice(c0, c0 + MXU_COLS)
            y_ref[:, cs] = (y_ref[:, cs] + xs_ref[:, cs] * dexp_ref[:, cs]) * _silu(proj(COL_Z + c0, MXU_COLS))
        first_stage(sched[6])
        m = _mix_out(yc_ref[...], _ssm_group_norm(y_ref[...], snw_ref[...]), w_out_ref)
        x1_ref[...] = _layer_norm(ALPHA * x + (1.0 + g1) * m, ln_g_ref[...], ln_b_ref[...])
        first_stage(IN_PAD // MXU_COLS)

    @pl.when(lax.rem(s, 2) == 0)
    def _():
        stages(p0, xk0, p1, xk1)

    @pl.when(lax.rem(s, 2) == 1)
    def _():
        stages(p1, xk1, p0, xk0)

    @pl.when((jb == tiles_per_seq - 1) & (s > 0))
    def _():
        cst_ref[...] = cbuf[SUBLANES - 2:SUBLANES, :]
        scst_ref[...] = xbuf[SUBLANES - 3:SUBLANES, :]
        sst_ref[...] = st_ref[...].T


def _const_spec(shape):
    return pl.BlockSpec(shape, lambda *_: (0,) * len(shape), pipeline_mode=pl.Buffered(1))


def _mixer_prompt(x, mod, w_in, w_dt, conv_w, conv_nw, sconv_w, sconv_b, dtb, alog, dexp, snw, w_out, ln_g, ln_b,
                  tile=256, sched=(1, 0, 0, 2, 1, 0, 0)):
    assert CONV_GROUP == SSM_HEAD_DIM and CONV_CH == SSM_CH
    nb, seq, _ = x.shape
    tiles_per_seq = seq // tile
    n_tiles = nb * tiles_per_seq
    kern = functools.partial(_mixer_prompt_kernel, tile=tile, tiles_per_seq=tiles_per_seq, sched=sched)
    consts = [w_in, w_dt, _head_expand(), _group_reduce(), conv_w, conv_nw, sconv_w, sconv_b, dtb, alog, dexp, snw,
              w_out, ln_g, ln_b]
    first = lambda s: jnp.minimum(s, n_tiles - 1)
    second = lambda s: jnp.maximum(s - 1, 0)
    return pl.pallas_call(
        kern,
        grid=(n_tiles + 1,),
        in_specs=[pl.BlockSpec((tile, D_MODEL), lambda s: (first(s), 0)),
                  pl.BlockSpec((None, 1, 6 * D_MODEL), lambda s: (first(s) // tiles_per_seq, 0, 0)),
                  pl.BlockSpec((None, 1, 6 * D_MODEL), lambda s: (second(s) // tiles_per_seq, 0, 0))]
                 + [_const_spec(a.shape) for a in consts],
        out_specs=[pl.BlockSpec((tile, D_MODEL), lambda s: (second(s), 0)),
                   pl.BlockSpec((None, 2, CONV_CH), lambda s: (second(s) // tiles_per_seq, 0, 0)),
                   pl.BlockSpec((None, 3, XBC_CH), lambda s: (second(s) // tiles_per_seq, 0, 0)),
                   pl.BlockSpec((None, SSM_CH, SSM_STATE), lambda s: (second(s) // tiles_per_seq, 0, 0))],
        out_shape=[jax.ShapeDtypeStruct((nb * seq, D_MODEL), F32),
                   jax.ShapeDtypeStruct((nb, 2, CONV_CH), F32),
                   jax.ShapeDtypeStruct((nb, 3, XBC_CH), F32),
                   jax.ShapeDtypeStruct((nb, SSM_CH, SSM_STATE), F32)],
        scratch_shapes=[pltpu.VMEM((tile, IN_PAD), F32),
                        pltpu.VMEM((tile, IN_PAD), F32),
                        pltpu.VMEM((tile, D_MODEL), F32),
                        pltpu.VMEM((tile, D_MODEL), F32),
                        pltpu.VMEM((SUBLANES, CONV_CH), F32),
                        pltpu.VMEM((SUBLANES, XBC_CH), F32),
                        pltpu.VMEM((SSM_STATE, SSM_CH), F32),
                        pltpu.VMEM((tile, SSM_CH), F32),
                        pltpu.VMEM((tile, 2 * SSM_GROUPS * SSM_STATE), F32),
                        pltpu.VMEM((tile, SSM_CH), F32),
                        pltpu.VMEM((tile, SSM_CH), F32),
                        pltpu.VMEM((tile // SSM_CHUNK * SUBLANES, SSM_CH), F32),
                        pltpu.VMEM((tile // SSM_CHUNK * SSM_HEADS * SUBLANES, SSM_CHUNK), F32),
                        pltpu.VMEM((tile, SSM_GROUPS * SSM_STATE), F32),
                        pltpu.VMEM((tile, SSM_GROUPS * SSM_STATE), BF16),
                        pltpu.VMEM((tile, SSM_CH), F32),
                        pltpu.VMEM((tile, CONV_CH), BF16)],
        compiler_params=pltpu.CompilerParams(dimension_semantics=("arbitrary",),
                                             vmem_limit_bytes=VMEM_LIMIT),
        name="mixer_prompt",
    )(x.reshape(nb * seq, D_MODEL), mod, mod, *consts)


def _ffn_kernel(x_ref, mod_ref, w_up_ref, w_down_ref, ln_g_ref, ln_b_ref, o_ref, *, ff_tile):
    x = x_ref[...]
    sh2 = mod_ref[:, 3 * D_MODEL:4 * D_MODEL]
    sc2 = mod_ref[:, 4 * D_MODEL:5 * D_MODEL]
    g2 = mod_ref[:, 5 * D_MODEL:6 * D_MODEL]
    v = (x * (1.0 + sc2) + sh2).astype(BF16)
    acc = jnp.zeros(x.shape, F32)
    for k in range(D_FF // ff_tile):
        h = jnp.maximum(_dot(v, w_up_ref[:, k * ff_tile:(k + 1) * ff_tile]), 0.0)
        acc = acc + _dot((h * h).astype(BF16), w_down_ref[k * ff_tile:(k + 1) * ff_tile, :])
    o_ref[...] = _layer_norm(ALPHA * x + (1.0 + g2) * acc, ln_g_ref[...], ln_b_ref[...])


def _ffn(x, mod, rows_per_mod, w_up, w_down, ln_g, ln_b, tile, ff_tile=1024):
    rows = x.shape[0]
    mod_rows = mod.shape[1]
    tiles_per_mod = rows_per_mod // tile
    kern = functools.partial(_ffn_kernel, ff_tile=ff_tile)
    return pl.pallas_call(
        kern,
        grid=(rows // tile,),
        in_specs=[pl.BlockSpec((tile, D_MODEL), lambda i: (i, 0)),
                  pl.BlockSpec((None, mod_rows, 6 * D_MODEL), lambda i: (i // tiles_per_mod, 0, 0)),
                  _const_spec(w_up.shape), _const_spec(w_down.shape),
                  _const_spec(ln_g.shape), _const_spec(ln_b.shape)],
        out_specs=pl.BlockSpec((tile, D_MODEL), lambda i: (i, 0)),
        out_shape=jax.ShapeDtypeStruct((rows, D_MODEL), F32),
        compiler_params=pltpu.CompilerParams(dimension_semantics=("arbitrary",),
                                             vmem_limit_bytes=VMEM_LIMIT),
        name="ffn",
    )(x, mod, w_up, w_down, ln_g, ln_b)


_PRODUCT_TERMS = ((0, 0), (0, 1), (1, 0), (0, 2), (2, 0), (1, 1))
UPDATE_TERMS = 16


def _sample_pre_kernel(x_ref, mod_ref, w_in_ref, w_dt_ref, conv_w_ref, conv_nw_ref, sconv_w_ref, sconv_b_ref,
                       dtb_ref, alog_ref, cb0_ref, cb1_ref, sb0_ref, sb1_ref, sb2_ref,
                       yconv_ref, ch_ref, xbc_ref, z_ref, xs_ref, ydiag_ref, decx_ref, cm_ref, lhs_ref, rhs_ref):
    expand = _head_expand()
    reduce = _group_reduce()
    x = x_ref[...]
    sh1 = mod_ref[:, 0:D_MODEL]
    sc1 = mod_ref[:, D_MODEL:2 * D_MODEL]
    u = (x * (1.0 + sc1) + sh1).astype(BF16)

    def proj(lo, width):
        return _dot(u, w_in_ref[:, lo:lo + width])

    ch = proj(COL_GC, CONV_CH) * proj(COL_HV, CONV_CH)
    ch_ref[...] = ch
    cw = conv_w_ref[...]
    cv = cw[0:1, :] * cb0_ref[...] + cw[1:2, :] * cb1_ref[...] + cw[2:3, :] * ch
    yconv_ref[...] = _conv_group_norm(proj(COL_GB, CONV_CH) * cv, conv_nw_ref[...], expand, reduce)

    xbc = proj(COL_XBC, XBC_CH)
    xbc_ref[...] = xbc
    sw = sconv_w_ref[...]
    xc = _silu(sw[0:1, :] * sb0_ref[...] + sw[1:2, :] * sb1_ref[...] + sw[2:3, :] * sb2_ref[...]
               + sw[3:4, :] * xbc + sconv_b_ref[...])
    xs = xc[:, 0:SSM_CH]
    xs_ref[...] = xs
    cm_ref[...] = xc[:, SSM_CH + SSM_GROUPS * SSM_STATE:XBC_CH]
    z_ref[...] = proj(COL_Z, SSM_CH)

    dt = _softplus(_dot(u, w_dt_ref[...]) + dtb_ref[...])
    dta = dt * (-jnp.exp(alog_ref[...]))
    xdt = xs * _dot_f32_lhs(dt, expand)
    decx = jnp.exp(_dot_f32_lhs(dta, expand))
    decx_ref[...] = decx
    xdt_t, dec_t = _split(xdt, 3), _split(decx, 3)
    bm_t = _split(xc[:, SSM_CH:SSM_CH + SSM_GROUPS * SSM_STATE], 3)
    group_of = lax.broadcasted_iota(jnp.int32, xdt.shape, 1) // SSM_GROUP_CH
    zeros = jnp.zeros((x.shape[0], SSM_STATE), F32)
    r = 0
    for g in range(SSM_GROUPS):
        for tx, tb in _PRODUCT_TERMS:
            lhs_ref[r] = jnp.where(group_of == g, xdt_t[tx].astype(F32), 0.0)
            rhs_ref[r] = jnp.concatenate([bm_t[tb][:, g * SSM_STATE:(g + 1) * SSM_STATE].astype(F32), zeros], axis=1)
            r += 1
    for t in range(3):
        lhs_ref[r] = dec_t[t].astype(F32)
        rhs_ref[r] = jnp.concatenate([zeros, zeros + 1.0], axis=1)
        r += 1
    for r in range(r, UPDATE_TERMS):
        lhs_ref[r] = jnp.zeros_like(xdt)
        rhs_ref[r] = jnp.concatenate([zeros, zeros], axis=1)
    for g in range(SSM_GROUPS):
        bm = xc[:, SSM_CH + g * SSM_STATE:SSM_CH + (g + 1) * SSM_STATE]
        cm = xc[:, SSM_CH + (SSM_GROUPS + g) * SSM_STATE:SSM_CH + (SSM_GROUPS + g + 1) * SSM_STATE]
        cb = jnp.sum(cm * bm, axis=-1, keepdims=True)
        gl = g * SSM_GROUP_CH
        ydiag_ref[:, gl:gl + SSM_GROUP_CH] = cb * xdt[:, gl:gl + SSM_GROUP_CH]


def _sample_pre(x, mod, w_in, w_dt, conv_w, conv_nw, sconv_w, sconv_b, dtb, alog, cb0, cb1, sb0, sb1, sb2):
    n = x.shape[0]
    args = (x, mod, w_in, w_dt, conv_w, conv_nw, sconv_w, sconv_b, dtb, alog, cb0, cb1, sb0, sb1, sb2)
    f32_shapes = [(n, CONV_CH), (n, CONV_CH), (n, XBC_CH), (n, SSM_CH), (n, SSM_CH), (n, SSM_CH), (n, SSM_CH),
                  (n, SSM_GROUPS * SSM_STATE)]
    f32_shapes += [(UPDATE_TERMS, n, SSM_CH), (UPDATE_TERMS, n, 2 * SSM_STATE)]
    return pl.pallas_call(
        _sample_pre_kernel,
        out_shape=[jax.ShapeDtypeStruct(s, F32) for s in f32_shapes],
        compiler_params=pltpu.CompilerParams(vmem_limit_bytes=VMEM_LIMIT),
        name="sample_pre",
    )(*args)


def _sample_state_kernel(s_ref, lhs_ref, rhs_ref, cm_ref, decx_ref, o_ref, yoff_ref, *, block):
    i = pl.program_id(0)
    rows = UPDATE_TERMS * block
    lhs_t = lhs_ref[...].reshape(rows, SSM_CH).T.astype(BF16)
    rhs_all = rhs_ref[...].reshape(rows, 2 * SSM_STATE)
    token_of = lax.broadcasted_iota(jnp.int32, rhs_all.shape, 0) % block

    def body(k, carry):
        b = i * block + k
        s = s_ref[k]
        upd = _dot(lhs_t, jnp.where(token_of == k, rhs_all, 0.0).astype(BF16))
        o_ref[k] = s * upd[:, SSM_STATE:2 * SSM_STATE] + upd[:, 0:SSM_STATE]
        cm = cm_ref[pl.ds(b, 1), :]
        sums = []
        for c0 in range(0, SSM_CH, LANES):
            g = c0 // SSM_GROUP_CH
            prod = s[c0:c0 + LANES, :] * cm[:, g * SSM_STATE:(g + 1) * SSM_STATE]
            sums.append(jnp.sum(prod.T, axis=0, keepdims=True))
        yoff_ref[pl.ds(b, 1), :] = jnp.concatenate(sums, axis=1) * decx_ref[pl.ds(b, 1), :]
        return carry

    lax.fori_loop(0, block, body, 0, unroll=True)


def _sample_state(state, lhs, rhs, cm, decx, block=8):
    n = state.shape[0]
    kern = functools.partial(_sample_state_kernel, block=block)
    return pl.pallas_call(
        kern,
        grid=(n // block,),
        in_specs=[pl.BlockSpec((block, SSM_CH, SSM_STATE), lambda i: (i, 0, 0)),
                  pl.BlockSpec((UPDATE_TERMS, block, SSM_CH), lambda i: (0, i, 0)),
                  pl.BlockSpec((UPDATE_TERMS, block, 2 * SSM_STATE), lambda i: (0, i, 0)),
                  _const_spec(cm.shape), _const_spec(decx.shape)],
        out_specs=[pl.BlockSpec((block, SSM_CH, SSM_STATE), lambda i: (i, 0, 0)),
                   pl.BlockSpec((n, SSM_CH), lambda i: (0, 0))],
        out_shape=[jax.ShapeDtypeStruct(state.shape, F32), jax.ShapeDtypeStruct((n, SSM_CH), F32)],
        compiler_params=pltpu.CompilerParams(dimension_semantics=("arbitrary",),
                                             vmem_limit_bytes=VMEM_LIMIT),
        name="sample_state",
    )(state, lhs, rhs, cm, decx)


def _sample_post_kernel(x_ref, mod_ref, yconv_ref, ydiag_ref, yoff_ref, xs_ref, z_ref, dexp_ref, snw_ref,
                        w_out_ref, ln_g_ref, ln_b_ref, x1_ref):
    g1 = mod_ref[:, 2 * D_MODEL:3 * D_MODEL]
    y = ydiag_ref[...] + yoff_ref[...] + xs_ref[...] * dexp_ref[...]
    y = y * _silu(z_ref[...])
    m = _mix_out(yconv_ref[...], _ssm_group_norm(y, snw_ref[...]), w_out_ref)
    x1_ref[...] = _layer_norm(ALPHA * x_ref[...] + (1.0 + g1) * m, ln_g_ref[...], ln_b_ref[...])


def _sample_post(x, mod, yconv, ydiag, yoff, xs, z, dexp, snw, w_out, ln_g, ln_b):
    return pl.pallas_call(
        _sample_post_kernel,
        out_shape=jax.ShapeDtypeStruct(x.shape, F32),
        compiler_params=pltpu.CompilerParams(vmem_limit_bytes=VMEM_LIMIT),
        name="sample_post",
    )(x, mod, yconv, ydiag, yoff, xs, z, dexp, snw, w_out, ln_g, ln_b)


def kernel(x_prompt, x_sample, state_conv, state_ssm_conv, state_ssm, c_prompt, c_sample, w_ada, b_ada, w_in, conv_w, conv_norm_w, ssm_conv_w, ssm_conv_b, dt_bias, a_log, d_skip, ssm_norm_w, w_out, ln1_g, ln1_b, w_up, w_down, ln2_g, ln2_b):
    assert w_ada.shape[0] == 1, "single-layer trunk"
    nb, seq, _ = x_prompt.shape
    ns = x_sample.shape[0]
    row = lambda a: a.reshape(1, -1)
    pad_heads = lambda a: jnp.pad(a.reshape(1, -1), ((0, 0), (0, LANES - SSM_HEADS)))

    w_in_b, w_dt_b = _cast_in_proj(w_in[0])
    w_out_b = w_out[0].astype(BF16)
    w_up_b = w_up[0].astype(BF16)
    w_down_b = w_down[0].astype(BF16)
    conv_nw, sconv_b, snw = row(conv_norm_w[0]), row(ssm_conv_b[0]), row(ssm_norm_w[0])
    dtb, alog = pad_heads(dt_bias[0]), pad_heads(a_log[0])
    dexp = row(jnp.repeat(d_skip[0], SSM_HEAD_DIM))
    g1, b1, g2, b2 = row(ln1_g[0]), row(ln1_b[0]), row(ln2_g[0]), row(ln2_b[0])

    mod_p, mod_s = _ada(c_sample, c_prompt, w_ada[0], row(b_ada[0]))
    mod_p = mod_p.reshape(nb, 1, 6 * D_MODEL)

    x1_p, cst_p, scst_p, sst_p = _mixer_prompt(x_prompt, mod_p, w_in_b, w_dt_b, conv_w[0], conv_nw, ssm_conv_w[0],
                                               sconv_b, dtb, alog, dexp, snw, w_out_b, g1, b1)
    y_p = _ffn(x1_p, mod_p, seq, w_up_b, w_down_b, g2, b2, tile=512)

    xs2 = x_sample.reshape(ns, D_MODEL)
    (yconv_s, ch_s, xbc_s, z_s, xs_s, ydiag_s, decx_s, cm_s, lhs_s, rhs_s) = _sample_pre(
        xs2, mod_s, w_in_b, w_dt_b, conv_w[0], conv_nw, ssm_conv_w[0], sconv_b, dtb, alog,
        state_conv[0, :, 0], state_conv[0, :, 1],
        state_ssm_conv[0, :, 0], state_ssm_conv[0, :, 1], state_ssm_conv[0, :, 2])
    new_state_s, yoff_s = _sample_state(state_ssm[0].reshape(ns, SSM_CH, SSM_STATE), lhs_s, rhs_s, cm_s, decx_s)
    x1_s = _sample_post(xs2, mod_s, yconv_s, ydiag_s, yoff_s, xs_s, z_s, dexp, snw, w_out_b, g1, b1)
    y_s = _ffn(x1_s, mod_s.reshape(1, ns, 6 * D_MODEL), ns, w_up_b, w_down_b, g2, b2, tile=ns)

    return (y_p.reshape(nb, seq, D_MODEL),
            y_s.reshape(ns, 1, D_MODEL),
            cst_p[None],
            scst_p[None],
            sst_p.reshape(1, nb, SSM_HEADS, SSM_HEAD_DIM, SSM_STATE),
            jnp.stack([state_conv[0, :, 1], ch_s], axis=1)[None],
            jnp.stack([state_ssm_conv[0, :, 1], state_ssm_conv[0, :, 2], xbc_s], axis=1)[None],
            new_state_s.reshape(1, ns, SSM_HEADS, SSM_HEAD_DIM, SSM_STATE))
```

```python
import functools

import jax
import jax.numpy as jnp
from jax import lax
from jax.experimental import pallas as pl
from jax.experimental.pallas import tpu as pltpu

F32 = jnp.float32
BF16 = jnp.bfloat16

D_MODEL = 1024
CONV_CH = 1024
CONV_GROUP = 64
SSM_CH = 1024
SSM_HEADS = 16
SSM_HEAD_DIM = 64
SSM_GROUPS = 2
SSM_GROUP_CH = SSM_CH // SSM_GROUPS
SSM_STATE = 128
SSM_CHUNK = 128
XBC_CH = SSM_CH + 2 * SSM_GROUPS * SSM_STATE
D_FF = 4 * D_MODEL
LANES = 128
SUBLANES = 8
MXU_COLS = 256
COL_GB, COL_GC, COL_HV, COL_Z, COL_XBC = 0, 1024, 2048, 3072, 4096
COL_DT = COL_XBC + XBC_CH
IN_COLS = COL_DT + SSM_HEADS
IN_PAD = COL_DT + LANES
ALPHA = 2.0 ** 0.25
LN_EPS = 1e-5
RMS_EPS = 1e-5
VMEM_LIMIT = 56 * 1024 * 1024


def _dot(a, b):
    return jnp.dot(a, b, preferred_element_type=F32)


def _split(a, terms):
    parts = []
    r = a
    for t in range(terms):
        p = r.astype(BF16)
        parts.append(p)
        if t + 1 < terms:
            r = r - p.astype(F32)
    return parts


def _dot_f32_lhs(a, b_exact, terms=3):
    parts = _split(a, terms)
    out = _dot(parts[0], b_exact)
    for p in parts[1:]:
        out = out + _dot(p, b_exact)
    return out


def _dot_f32_rhs(a_exact, b, terms=3):
    parts = _split(b, terms)
    out = _dot(a_exact, parts[0])
    for p in parts[1:]:
        out = out + _dot(a_exact, p)
    return out


def _head_expand():
    h = lax.broadcasted_iota(jnp.int32, (LANES, SSM_CH), 0)
    c = lax.broadcasted_iota(jnp.int32, (LANES, SSM_CH), 1)
    return (c // SSM_HEAD_DIM == h).astype(BF16)


def _group_reduce():
    c = lax.broadcasted_iota(jnp.int32, (CONV_CH, LANES), 0)
    k = lax.broadcasted_iota(jnp.int32, (CONV_CH, LANES), 1)
    return (c // CONV_GROUP == k).astype(BF16)


def _sigmoid(x):
    return 1.0 / (1.0 + jnp.exp(-x))


def _silu(x):
    return x * _sigmoid(x)


def _softplus(x):
    return jnp.maximum(x, 0.0) + jnp.log1p(jnp.exp(-jnp.abs(x)))


def _layer_norm(r, g, b):
    mu = jnp.mean(r, axis=-1, keepdims=True)
    d = r - mu
    var = jnp.mean(d * d, axis=-1, keepdims=True)
    return d * lax.rsqrt(var + LN_EPS) * g + b


def _conv_group_norm(prod, w, expand, reduce):
    ssum = _dot_f32_lhs(prod * prod, reduce, terms=2)
    rstd = lax.rsqrt(ssum * (1.0 / CONV_GROUP) + RMS_EPS)
    return prod * _dot_f32_lhs(rstd, expand, terms=2) * w


def _ssm_group_norm(y, w):
    outs = []
    for g in range(SSM_GROUPS):
        yg = y[:, g * SSM_GROUP_CH:(g + 1) * SSM_GROUP_CH]
        ms = jnp.mean(yg * yg, axis=-1, keepdims=True)
        outs.append((yg * lax.rsqrt(ms + RMS_EPS) * w[:, g * SSM_GROUP_CH:(g + 1) * SSM_GROUP_CH]).astype(BF16))
    return outs


def _mix_out(y_conv, y_ssm_groups, w_out_ref):
    m = _dot(y_conv.astype(BF16), w_out_ref[0:CONV_CH, :])
    for g, yg in enumerate(y_ssm_groups):
        lo = CONV_CH + g * SSM_GROUP_CH
        m = m + _dot(yg, w_out_ref[lo:lo + SSM_GROUP_CH, :])
    return m


def _ada_kernel(c_ref, w_ref, b_ref, op_ref, os_ref):
    c = c_ref[...]
    w = w_ref[...]
    c_hi = c.astype(BF16)
    c_lo = (c - c_hi.astype(F32)).astype(BF16)
    w_hi = w.astype(BF16)
    mod = _dot(c_hi, w_hi) + _dot(c_lo, w_hi) + b_ref[...]
    n_sample = os_ref.shape[0]
    os_ref[...] = mod[0:n_sample, :]
    op_ref[...] = mod[n_sample:, :]


def _ada(c_sample, c_prompt, w_ada, b_ada, tile_n=1024):
    ns, nb = c_sample.shape[0], c_prompt.shape[0]
    n = w_ada.shape[1]
    return pl.pallas_call(
        _ada_kernel,
        grid=(n // tile_n,),
        in_specs=[pl.BlockSpec((ns + nb, D_MODEL), lambda i: (0, 0)),
                  pl.BlockSpec((D_MODEL, tile_n), lambda i: (0, i)),
                  pl.BlockSpec((1, tile_n), lambda i: (0, i))],
        out_specs=[pl.BlockSpec((nb, tile_n), lambda i: (0, i)),
                   pl.BlockSpec((ns, tile_n), lambda i: (0, i))],
        out_shape=[jax.ShapeDtypeStruct((nb, n), F32), jax.ShapeDtypeStruct((ns, n), F32)],
        name="ada_mod",
    )(jnp.concatenate([c_sample, c_prompt], axis=0), w_ada, b_ada)


def _cast_transposed_kernel(wt_ref, o_ref):
    rows = wt_ref.shape[0]
    wt = wt_ref[...]
    if rows < o_ref.shape[1]:
        wt = jnp.concatenate([wt, jnp.zeros((o_ref.shape[1] - rows, wt.shape[1]), wt.dtype)], axis=0)
    o_ref[...] = wt.T.astype(o_ref.dtype)


def _cast_in_proj(w):
    wt = w.T
    n_dt = w.shape[1] - COL_DT
    cols = COL_DT // 4
    assert cols % LANES == 0
    main = pl.pallas_call(
        _cast_transposed_kernel,
        grid=(COL_DT // cols,),
        in_specs=[pl.BlockSpec((cols, D_MODEL), lambda j: (j, 0))],
        out_specs=pl.BlockSpec((D_MODEL, cols), lambda j: (0, j)),
        out_shape=jax.ShapeDtypeStruct((D_MODEL, COL_DT), BF16),
        compiler_params=pltpu.CompilerParams(vmem_limit_bytes=VMEM_LIMIT),
        name="cast_in_proj",
    )(wt)
    dt = pl.pallas_call(
        _cast_transposed_kernel,
        grid=(1,),
        in_specs=[pl.BlockSpec((n_dt, D_MODEL), lambda j: (COL_DT // n_dt, 0))],
        out_specs=pl.BlockSpec((D_MODEL, LANES), lambda j: (0, 0)),
        out_shape=jax.ShapeDtypeStruct((D_MODEL, LANES), BF16),
        name="cast_in_proj_dt",
    )(wt)
    return main, dt


def _mixer_prompt_kernel(xa_ref, moda_ref, modb_ref, w_in_ref, w_dt_ref, expand_ref, reduce_ref,
                         conv_w_ref, conv_nw_ref, sconv_w_ref, sconv_b_ref,
                         dtb_ref, alog_ref, dexp_ref, snw_ref, w_out_ref, ln_g_ref, ln_b_ref,
                         x1_ref, cst_ref, scst_ref, sst_ref,
                         p0, p1, xk0, xk1, cbuf, xbuf, st_ref, xs_ref, bc_ref, dtx_ref, acsx_ref, endx_ref,
                         acst_ref, cb_ref, bmt_ref, y_ref, yc_ref,
                         *, tile, tiles_per_seq, sched):
    s = pl.program_id(0)
    jb = lax.rem(s + (tiles_per_seq - 1), tiles_per_seq)

    @pl.when(s == 0)
    def _():
        p1[...] = jnp.zeros_like(p1)
        xk1[...] = jnp.zeros_like(xk1)

    @pl.when((jb == 0) | (s == 0))
    def _():
        cbuf[...] = jnp.zeros_like(cbuf)
        xbuf[...] = jnp.zeros_like(xbuf)
        st_ref[...] = jnp.zeros_like(st_ref)

    def stages(pa, xka, pb, xkb):
        xa = xa_ref[...]
        xka[...] = xa
        u = (xa * (1.0 + moda_ref[:, D_MODEL:2 * D_MODEL]) + moda_ref[:, 0:D_MODEL]).astype(BF16)
        pieces = iter(list(range(0, COL_DT, MXU_COLS)) + [COL_DT])

        def first_stage(n):
            for _ in range(n):
                lo = next(pieces, None)
                if lo == COL_DT:
                    pa[:, COL_DT:IN_PAD] = _dot(u, w_dt_ref[...])
                elif lo is not None:
                    pa[:, lo:lo + MXU_COLS] = _dot(u, w_in_ref[:, lo:lo + MXU_COLS])

        expand = expand_ref[...]
        x = xkb[...]
        g1 = modb_ref[:, 2 * D_MODEL:3 * D_MODEL]

        def proj(lo, width):
            return pb[:, lo:lo + width]

        def delayed(tail_ref, cs, cur, taps):
            seq = jnp.concatenate([tail_ref[:, cs], cur], axis=0)
            tail_ref[:, cs] = cur[tile - SUBLANES:, :]
            return [pltpu.roll(seq, k, axis=0)[SUBLANES:, :] for k in range(1, taps + 1)]

        for k in range(CONV_CH // MXU_COLS):
            first_stage(sched[0])
            c0 = k * MXU_COLS
            cs = slice(c0, c0 + MXU_COLS)
            ch = proj(COL_GC + c0, MXU_COLS) * proj(COL_HV + c0, MXU_COLS)
            ch1, ch2 = delayed(cbuf, cs, ch, 2)
            cv = conv_w_ref[0:1, cs] * ch2 + conv_w_ref[1:2, cs] * ch1 + conv_w_ref[2:3, cs] * ch
            prod = proj(COL_GB + c0, MXU_COLS) * cv
            ssum = _dot_f32_lhs(prod * prod, reduce_ref[cs, :], terms=1)
            rstd = lax.rsqrt(ssum * (1.0 / CONV_GROUP) + RMS_EPS)
            yc_ref[:, cs] = (prod * _dot_f32_lhs(rstd, expand_ref[:, cs], terms=2)
                             * conv_nw_ref[:, cs]).astype(BF16)

        def pre_conv(c0):
            cs = slice(c0, c0 + MXU_COLS)
            xbc = proj(COL_XBC + c0, MXU_COLS)
            x1, x2, x3 = delayed(xbuf, cs, xbc, 3)
            return _silu(sconv_w_ref[0:1, cs] * x3 + sconv_w_ref[1:2, cs] * x2 + sconv_w_ref[2:3, cs] * x1
                         + sconv_w_ref[3:4, cs] * xbc + sconv_b_ref[:, cs])

        row = lax.broadcasted_iota(jnp.int32, (SSM_CHUNK, SSM_CHUNK), 0)
        col = lax.broadcasted_iota(jnp.int32, (SSM_CHUNK, SSM_CHUNK), 1)
        causal = row >= col
        tri = causal.astype(BF16)
        groups = SSM_CHUNK // SUBLANES
        causal_bias = jnp.where(causal, 0.0, -jnp.inf).reshape(groups, SUBLANES, SSM_CHUNK)
        first_half = (col < SSM_HEAD_DIM).reshape(groups, SUBLANES, SSM_CHUNK)
        half_rows = col < SSM_HEAD_DIM
        chunks = [slice(c * SSM_CHUNK, (c + 1) * SSM_CHUNK) for c in range(tile // SSM_CHUNK)]

        first_stage(sched[1])
        dt = _softplus(proj(COL_DT, LANES) + dtb_ref[...])
        dta = dt * (-jnp.exp(alog_ref[...]))
        dtx_ref[...] = _dot_f32_lhs(dt, expand, terms=1)
        for c, rows in enumerate(chunks):
            acs = _dot_f32_rhs(tri, dta[rows, :])
            acs_t = acs.T
            for h in range(SSM_HEADS):
                r8 = (c * SSM_HEADS + h) * SUBLANES
                acst_ref[r8:r8 + SUBLANES, :] = jnp.broadcast_to(acs_t[h:h + 1, :], (SUBLANES, SSM_CHUNK))
            acs_x = _dot_f32_lhs(acs, expand, terms=2)
            acsx_ref[rows, :] = acs_x
            endx_ref[c * SUBLANES:(c + 1) * SUBLANES, :] = jnp.broadcast_to(acs_x[SSM_CHUNK - 1:SSM_CHUNK, :],
                                                                             (SUBLANES, SSM_CH))
        for c0 in range(SSM_CH, XBC_CH, MXU_COLS):
            first_stage(sched[2])
            bc_ref[:, c0 - SSM_CH:c0 - SSM_CH + MXU_COLS] = pre_conv(c0)
        for rows in chunks:
            for g in range(SSM_GROUPS):
                gs = slice(g * SSM_STATE, (g + 1) * SSM_STATE)
                bm = bc_ref[rows, gs]
                cm = bc_ref[rows, (SSM_GROUPS + g) * SSM_STATE:(SSM_GROUPS + g + 1) * SSM_STATE]
                cb_ref[rows, gs] = lax.dot_general(cm.astype(BF16), bm.astype(BF16), (((1,), (1,)), ((), ())),
                                                   preferred_element_type=F32)
                bmt_ref[rows, gs] = bm.T.astype(BF16)

        for c0 in range(0, SSM_CH, MXU_COLS):
            first_stage(sched[3])
            cs = slice(c0, c0 + MXU_COLS)
            g = c0 // SSM_GROUP_CH
            gs = slice(g * SSM_STATE, (g + 1) * SSM_STATE)
            xs = pre_conv(c0)
            xs_ref[:, cs] = xs
            xdt = xs * dtx_ref[:, cs]
            for c, rows in enumerate(chunks):
                first_stage(sched[4])
                acs_x = acsx_ref[rows, cs].reshape(groups, SUBLANES, MXU_COLS)
                end_x = endx_ref[c * SUBLANES:(c + 1) * SUBLANES, cs]
                xdt_c = xdt[rows, :]
                xdec = (xdt_c * jnp.exp(end_x[None] - acs_x).reshape(SSM_CHUNK, MXU_COLS)).astype(BF16)
                cm = bc_ref[rows, (SSM_GROUPS + g) * SSM_STATE:(SSM_GROUPS + g + 1) * SSM_STATE].astype(BF16)
                cb = cb_ref[rows, gs]
                st = st_ref[:, cs]
                y_off = _dot(cm, st.astype(BF16)) * jnp.exp(acs_x).reshape(SSM_CHUNK, MXU_COLS)
                st_ref[:, cs] = ((st.reshape(groups, SUBLANES, MXU_COLS) * jnp.exp(end_x)[None])
                                 .reshape(SSM_STATE, MXU_COLS) + _dot(bmt_ref[rows, gs], xdec))
                for lo in range(0, MXU_COLS, LANES):
                    h0 = (c * SSM_HEADS + (c0 + lo) // SSM_HEAD_DIM) * SUBLANES
                    slab = acs_x[:, :, lo:lo + LANES]
                    rolled = pltpu.roll(slab, SSM_HEAD_DIM, axis=2)
                    a0 = jnp.where(first_half, slab, rolled) - acst_ref[h0:h0 + SUBLANES, :][None]
                    a1 = jnp.where(first_half, rolled, slab) - acst_ref[h0 + SUBLANES:h0 + 2 * SUBLANES, :][None]
                    l0 = jnp.exp(a0 + causal_bias).reshape(SSM_CHUNK, SSM_CHUNK)
                    l1 = jnp.exp(a1 + causal_bias).reshape(SSM_CHUNK, SSM_CHUNK)
                    m = jnp.concatenate([(cb * l0).astype(BF16), (cb * l1).astype(BF16)], axis=1)
                    xp = xdt_c[:, lo:lo + LANES]
                    rhs = jnp.concatenate([jnp.where(half_rows, xp, 0.0), jnp.where(half_rows, 0.0, xp)],
                                          axis=0).astype(BF16)
                    y_ref[rows, c0 + lo:c0 + lo + LANES] = _dot(m, rhs) + y_off[:, lo:lo + LANES]

        for k in range(SSM_CH // MXU_COLS):
            first_stage(sched[5])
            c0 = k * MXU_COLS
            cs = slice(c0, c0 + MXU_COLS)
            y_ref[:, cs] = (y_ref[:, cs] + xs_ref[:, cs] * dexp_ref[:, cs]) * _silu(proj(COL_Z + c0, MXU_COLS))
        first_stage(sched[6])
        m = _mix_out(yc_ref[...], _ssm_group_norm(y_ref[...], snw_ref[...]), w_out_ref)
        x1_ref[...] = _layer_norm(ALPHA * x + (1.0 + g1) * m, ln_g_ref[...], ln_b_ref[...])
        first_stage(IN_PAD // MXU_COLS)

    @pl.when(lax.rem(s, 2) == 0)
    def _():
        stages(p0, xk0, p1, xk1)

    @pl.when(lax.rem(s, 2) == 1)
    def _():
        stages(p1, xk1, p0, xk0)

    @pl.when((jb == tiles_per_seq - 1) & (s > 0))
    def _():
        cst_ref[...] = cbuf[SUBLANES - 2:SUBLANES, :]
        scst_ref[...] = xbuf[SUBLANES - 3:SUBLANES, :]
        sst_ref[...] = st_ref[...].T


def _const_spec(shape):
    return pl.BlockSpec(shape, lambda *_: (0,) * len(shape), pipeline_mode=pl.Buffered(1))


def _mixer_prompt(x, mod, w_in, w_dt, conv_w, conv_nw, sconv_w, sconv_b, dtb, alog, dexp, snw, w_out, ln_g, ln_b,
                  tile=256, sched=(1, 0, 0, 2, 1, 0, 0)):
    assert CONV_GROUP == SSM_HEAD_DIM and CONV_CH == SSM_CH
    nb, seq, _ = x.shape
    tiles_per_seq = seq // tile
    n_tiles = nb * tiles_per_seq
    kern = functools.partial(_mixer_prompt_kernel, tile=tile, tiles_per_seq=tiles_per_seq, sched=sched)
    consts = [w_in, w_dt, _head_expand(), _group_reduce(), conv_w, conv_nw, sconv_w, sconv_b, dtb, alog, dexp, snw,
              w_out, ln_g, ln_b]
    first = lambda s: jnp.minimum(s, n_tiles - 1)
    second = lambda s: jnp.maximum(s - 1, 0)
    return pl.pallas_call(
        kern,
        grid=(n_tiles + 1,),
        in_specs=[pl.BlockSpec((tile, D_MODEL), lambda s: (first(s), 0)),
                  pl.BlockSpec((None, 1, 6 * D_MODEL), lambda s: (first(s) // tiles_per_seq, 0, 0)),
                  pl.BlockSpec((None, 1, 6 * D_MODEL), lambda s: (second(s) // tiles_per_seq, 0, 0))]
                 + [_const_spec(a.shape) for a in consts],
        out_specs=[pl.BlockSpec((tile, D_MODEL), lambda s: (second(s), 0)),
                   pl.BlockSpec((None, 2, CONV_CH), lambda s: (second(s) // tiles_per_seq, 0, 0)),
                   pl.BlockSpec((None, 3, XBC_CH), lambda s: (second(s) // tiles_per_seq, 0, 0)),
                   pl.BlockSpec((None, SSM_CH, SSM_STATE), lambda s: (second(s) // tiles_per_seq, 0, 0))],
        out_shape=[jax.ShapeDtypeStruct((nb * seq, D_MODEL), F32),
                   jax.ShapeDtypeStruct((nb, 2, CONV_CH), F32),
                   jax.ShapeDtypeStruct((nb, 3, XBC_CH), F32),
                   jax.ShapeDtypeStruct((nb, SSM_CH, SSM_STATE), F32)],
        scratch_shapes=[pltpu.VMEM((tile, IN_PAD), F32),
                        pltpu.VMEM((tile, IN_PAD), F32),
                        pltpu.VMEM((tile, D_MODEL), F32),
                        pltpu.VMEM((tile, D_MODEL), F32),
                        pltpu.VMEM((SUBLANES, CONV_CH), F32),
                        pltpu.VMEM((SUBLANES, XBC_CH), F32),
                        pltpu.VMEM((SSM_STATE, SSM_CH), F32),
                        pltpu.VMEM((tile, SSM_CH), F32),
                        pltpu.VMEM((tile, 2 * SSM_GROUPS * SSM_STATE), F32),
                        pltpu.VMEM((tile, SSM_CH), F32),
                        pltpu.VMEM((tile, SSM_CH), F32),
                        pltpu.VMEM((tile // SSM_CHUNK * SUBLANES, SSM_CH), F32),
                        pltpu.VMEM((tile // SSM_CHUNK * SSM_HEADS * SUBLANES, SSM_CHUNK), F32),
                        pltpu.VMEM((tile, SSM_GROUPS * SSM_STATE), F32),
                        pltpu.VMEM((tile, SSM_GROUPS * SSM_STATE), BF16),
                        pltpu.VMEM((tile, SSM_CH), F32),
                        pltpu.VMEM((tile, CONV_CH), BF16)],
        compiler_params=pltpu.CompilerParams(dimension_semantics=("arbitrary",),
                                             vmem_limit_bytes=VMEM_LIMIT),
        name="mixer_prompt",
    )(x.reshape(nb * seq, D_MODEL), mod, mod, *consts)


def _ffn_kernel(x_ref, mod_ref, w_up_ref, w_down_ref, ln_g_ref, ln_b_ref, o_ref, *, ff_tile):
    x = x_ref[...]
    sh2 = mod_ref[:, 3 * D_MODEL:4 * D_MODEL]
    sc2 = mod_ref[:, 4 * D_MODEL:5 * D_MODEL]
    g2 = mod_ref[:, 5 * D_MODEL:6 * D_MODEL]
    v = (x * (1.0 + sc2) + sh2).astype(BF16)
    acc = jnp.zeros(x.shape, F32)
    for k in range(D_FF // ff_tile):
        h = jnp.maximum(_dot(v, w_up_ref[:, k * ff_tile:(k + 1) * ff_tile]), 0.0)
        acc = acc + _dot((h * h).astype(BF16), w_down_ref[k * ff_tile:(k + 1) * ff_tile, :])
    o_ref[...] = _layer_norm(ALPHA * x + (1.0 + g2) * acc, ln_g_ref[...], ln_b_ref[...])


def _ffn(x, mod, rows_per_mod, w_up, w_down, ln_g, ln_b, tile, ff_tile=1024):
    rows = x.shape[0]
    mod_rows = mod.shape[1]
    tiles_per_mod = rows_per_mod // tile
    kern = functools.partial(_ffn_kernel, ff_tile=ff_tile)
    return pl.pallas_call(
        kern,
        grid=(rows // tile,),
        in_specs=[pl.BlockSpec((tile, D_MODEL), lambda i: (i, 0)),
                  pl.BlockSpec((None, mod_rows, 6 * D_MODEL), lambda i: (i // tiles_per_mod, 0, 0)),
                  _const_spec(w_up.shape), _const_spec(w_down.shape),
                  _const_spec(ln_g.shape), _const_spec(ln_b.shape)],
        out_specs=pl.BlockSpec((tile, D_MODEL), lambda i: (i, 0)),
        out_shape=jax.ShapeDtypeStruct((rows, D_MODEL), F32),
        compiler_params=pltpu.CompilerParams(dimension_semantics=("arbitrary",),
                                             vmem_limit_bytes=VMEM_LIMIT),
        name="ffn",
    )(x, mod, w_up, w_down, ln_g, ln_b)


def _ffn_stream_kernel(x_ref, mod_ref, w_up_ref, w_down_ref, ln_g_ref, ln_b_ref, o_ref, acc_ref):
    k = pl.program_id(0)

    @pl.when(k == 0)
    def _():
        acc_ref[...] = jnp.zeros_like(acc_ref)

    x = x_ref[...]
    sh2 = mod_ref[:, 3 * D_MODEL:4 * D_MODEL]
    sc2 = mod_ref[:, 4 * D_MODEL:5 * D_MODEL]
    v = (x * (1.0 + sc2) + sh2).astype(BF16)
    h = jnp.maximum(_dot(v, w_up_ref[...]), 0.0)
    acc_ref[...] += _dot((h * h).astype(BF16), w_down_ref[...])

    @pl.when(k == pl.num_programs(0) - 1)
    def _():
        g2 = mod_ref[:, 5 * D_MODEL:6 * D_MODEL]
        o_ref[...] = _layer_norm(ALPHA * x + (1.0 + g2) * acc_ref[...], ln_g_ref[...], ln_b_ref[...])


def _ffn_stream(x, mod, w_up, w_down, ln_g, ln_b, ff_tile=1024):
    rows = x.shape[0]
    return pl.pallas_call(
        _ffn_stream_kernel,
        grid=(D_FF // ff_tile,),
        in_specs=[_const_spec(x.shape), _const_spec(mod.shape),
                  pl.BlockSpec((D_MODEL, ff_tile), lambda k: (0, k)),
                  pl.BlockSpec((ff_tile, D_MODEL), lambda k: (k, 0)),
                  _const_spec(ln_g.shape), _const_spec(ln_b.shape)],
        out_specs=pl.BlockSpec((rows, D_MODEL), lambda k: (0, 0)),
        out_shape=jax.ShapeDtypeStruct((rows, D_MODEL), F32),
        scratch_shapes=[pltpu.VMEM((rows, D_MODEL), F32)],
        compiler_params=pltpu.CompilerParams(dimension_semantics=("arbitrary",),
                                             vmem_limit_bytes=VMEM_LIMIT),
        name="ffn_stream",
    )(x, mod, w_up, w_down, ln_g, ln_b)


_PRODUCT_TERMS = ((0, 0), (0, 1), (1, 0), (0, 2), (2, 0), (1, 1))
UPDATE_TERMS = 16


def _sample_pre_kernel(x_ref, mod_ref, w_in_ref, w_dt_ref, conv_w_ref, conv_nw_ref, sconv_w_ref, sconv_b_ref,
                       dtb_ref, alog_ref, cb0_ref, cb1_ref, sb0_ref, sb1_ref, sb2_ref,
                       yconv_ref, ch_ref, xbc_ref, z_ref, xs_ref, ydiag_ref, decx_ref, cm_ref, lhs_ref, rhs_ref):
    expand = _head_expand()
    reduce = _group_reduce()
    x = x_ref[...]
    sh1 = mod_ref[:, 0:D_MODEL]
    sc1 = mod_ref[:, D_MODEL:2 * D_MODEL]
    u = (x * (1.0 + sc1) + sh1).astype(BF16)

    def proj(lo, width):
        return _dot(u, w_in_ref[:, lo:lo + width])

    ch = proj(COL_GC, CONV_CH) * proj(COL_HV, CONV_CH)
    ch_ref[...] = ch
    cw = conv_w_ref[...]
    cv = cw[0:1, :] * cb0_ref[...] + cw[1:2, :] * cb1_ref[...] + cw[2:3, :] * ch
    yconv_ref[...] = _conv_group_norm(proj(COL_GB, CONV_CH) * cv, conv_nw_ref[...], expand, reduce)

    xbc = proj(COL_XBC, XBC_CH)
    xbc_ref[...] = xbc
    sw = sconv_w_ref[...]
    xc = _silu(sw[0:1, :] * sb0_ref[...] + sw[1:2, :] * sb1_ref[...] + sw[2:3, :] * sb2_ref[...]
               + sw[3:4, :] * xbc + sconv_b_ref[...])
    xs = xc[:, 0:SSM_CH]
    xs_ref[...] = xs
    cm_ref[...] = xc[:, SSM_CH + SSM_GROUPS * SSM_STATE:XBC_CH]
    z_ref[...] = proj(COL_Z, SSM_CH)

    dt = _softplus(_dot(u, w_dt_ref[...]) + dtb_ref[...])
    dta = dt * (-jnp.exp(alog_ref[...]))
    xdt = xs * _dot_f32_lhs(dt, expand)
    decx = jnp.exp(_dot_f32_lhs(dta, expand))
    decx_ref[...] = decx
    xdt_t, dec_t = _split(xdt, 3), _split(decx, 3)
    bm_t = _split(xc[:, SSM_CH:SSM_CH + SSM_GROUPS * SSM_STATE], 3)
    group_of = lax.broadcasted_iota(jnp.int32, xdt.shape, 1) // SSM_GROUP_CH
    zeros = jnp.zeros((x.shape[0], SSM_STATE), F32)
    r = 0
    for g in range(SSM_GROUPS):
        for tx, tb in _PRODUCT_TERMS:
            lhs_ref[r] = jnp.where(group_of == g, xdt_t[tx].astype(F32), 0.0)
            rhs_ref[r] = jnp.concatenate([bm_t[tb][:, g * SSM_STATE:(g + 1) * SSM_STATE].astype(F32), zeros], axis=1)
            r += 1
    for t in range(3):
        lhs_ref[r] = dec_t[t].astype(F32)
        rhs_ref[r] = jnp.concatenate([zeros, zeros + 1.0], axis=1)
        r += 1
    for r in range(r, UPDATE_TERMS):
        lhs_ref[r] = jnp.zeros_like(xdt)
        rhs_ref[r] = jnp.concatenate([zeros, zeros], axis=1)
    for g in range(SSM_GROUPS):
        bm = xc[:, SSM_CH + g * SSM_STATE:SSM_CH + (g + 1) * SSM_STATE]
        cm = xc[:, SSM_CH + (SSM_GROUPS + g) * SSM_STATE:SSM_CH + (SSM_GROUPS + g + 1) * SSM_STATE]
        cb = jnp.sum(cm * bm, axis=-1, keepdims=True)
        gl = g * SSM_GROUP_CH
        ydiag_ref[:, gl:gl + SSM_GROUP_CH] = cb * xdt[:, gl:gl + SSM_GROUP_CH]


def _sample_pre(x, mod, w_in, w_dt, conv_w, conv_nw, sconv_w, sconv_b, dtb, alog, cb0, cb1, sb0, sb1, sb2):
    n = x.shape[0]
    args = (x, mod, w_in, w_dt, conv_w, conv_nw, sconv_w, sconv_b, dtb, alog, cb0, cb1, sb0, sb1, sb2)
    f32_shapes = [(n, CONV_CH), (n, CONV_CH), (n, XBC_CH), (n, SSM_CH), (n, SSM_CH), (n, SSM_CH), (n, SSM_CH),
                  (n, SSM_GROUPS * SSM_STATE)]
    f32_shapes += [(UPDATE_TERMS, n, SSM_CH), (UPDATE_TERMS, n, 2 * SSM_STATE)]
    return pl.pallas_call(
        _sample_pre_kernel,
        out_shape=[jax.ShapeDtypeStruct(s, F32) for s in f32_shapes],
        compiler_params=pltpu.CompilerParams(vmem_limit_bytes=VMEM_LIMIT),
        name="sample_pre",
    )(*args)


def _sample_state_kernel(s_ref, lhs_ref, rhs_ref, cm_ref, decx_ref, o_ref, yoff_ref, *, block):
    i = pl.program_id(0)
    rows = UPDATE_TERMS * block
    lhs_t = lhs_ref[...].reshape(rows, SSM_CH).T.astype(BF16)
    rhs_all = rhs_ref[...].reshape(rows, 2 * SSM_STATE)
    token_of = lax.broadcasted_iota(jnp.int32, rhs_all.shape, 0) % block

    def body(k, carry):
        b = i * block + k
        s = s_ref[k]
        upd = _dot(lhs_t, jnp.where(token_of == k, rhs_all, 0.0).astype(BF16))
        o_ref[k] = s * upd[:, SSM_STATE:2 * SSM_STATE] + upd[:, 0:SSM_STATE]
        cm = cm_ref[pl.ds(b, 1), :]
        sums = []
        for c0 in range(0, SSM_CH, LANES):
            g = c0 // SSM_GROUP_CH
            prod = s[c0:c0 + LANES, :] * cm[:, g * SSM_STATE:(g + 1) * SSM_STATE]
            sums.append(jnp.sum(prod.T, axis=0, keepdims=True))
        yoff_ref[pl.ds(b, 1), :] = jnp.concatenate(sums, axis=1) * decx_ref[pl.ds(b, 1), :]
        return carry

    lax.fori_loop(0, block, body, 0, unroll=True)


def _sample_state(state, lhs, rhs, cm, decx, block=8):
    n = state.shape[0]
    kern = functools.partial(_sample_state_kernel, block=block)
    return pl.pallas_call(
        kern,
        grid=(n // block,),
        in_specs=[pl.BlockSpec((block, SSM_CH, SSM_STATE), lambda i: (i, 0, 0)),
                  pl.BlockSpec((UPDATE_TERMS, block, SSM_CH), lambda i: (0, i, 0)),
                  pl.BlockSpec((UPDATE_TERMS, block, 2 * SSM_STATE), lambda i: (0, i, 0)),
                  _const_spec(cm.shape), _const_spec(decx.shape)],
        out_specs=[pl.BlockSpec((block, SSM_CH, SSM_STATE), lambda i: (i, 0, 0)),
                   pl.BlockSpec((n, SSM_CH), lambda i: (0, 0))],
        out_shape=[jax.ShapeDtypeStruct(state.shape, F32), jax.ShapeDtypeStruct((n, SSM_CH), F32)],
        compiler_params=pltpu.CompilerParams(dimension_semantics=("arbitrary",),
                                             vmem_limit_bytes=VMEM_LIMIT),
        name="sample_state",
    )(state, lhs, rhs, cm, decx)


def _sample_post_kernel(x_ref, mod_ref, yconv_ref, ydiag_ref, yoff_ref, xs_ref, z_ref, dexp_ref, snw_ref,
                        w_out_ref, ln_g_ref, ln_b_ref, x1_ref):
    g1 = mod_ref[:, 2 * D_MODEL:3 * D_MODEL]
    y = ydiag_ref[...] + yoff_ref[...] + xs_ref[...] * dexp_ref[...]
    y = y * _silu(z_ref[...])
    m = _mix_out(yconv_ref[...], _ssm_group_norm(y, snw_ref[...]), w_out_ref)
    x1_ref[...] = _layer_norm(ALPHA * x_ref[...] + (1.0 + g1) * m, ln_g_ref[...], ln_b_ref[...])


def _sample_post(x, mod, yconv, ydiag, yoff, xs, z, dexp, snw, w_out, ln_g, ln_b):
    return pl.pallas_call(
        _sample_post_kernel,
        out_shape=jax.ShapeDtypeStruct(x.shape, F32),
        compiler_params=pltpu.CompilerParams(vmem_limit_bytes=VMEM_LIMIT),
        name="sample_post",
    )(x, mod, yconv, ydiag, yoff, xs, z, dexp, snw, w_out, ln_g, ln_b)


def kernel(x_prompt, x_sample, state_conv, state_ssm_conv, state_ssm, c_prompt, c_sample, w_ada, b_ada, w_in, conv_w, conv_norm_w, ssm_conv_w, ssm_conv_b, dt_bias, a_log, d_skip, ssm_norm_w, w_out, ln1_g, ln1_b, w_up, w_down, ln2_g, ln2_b):
    assert w_ada.shape[0] == 1, "single-layer trunk"
    nb, seq, _ = x_prompt.shape
    ns = x_sample.shape[0]
    row = lambda a: a.reshape(1, -1)
    pad_heads = lambda a: jnp.pad(a.reshape(1, -1), ((0, 0), (0, LANES - SSM_HEADS)))

    w_in_b, w_dt_b = _cast_in_proj(w_in[0])
    w_out_b = w_out[0].astype(BF16)
    w_up_b = w_up[0].astype(BF16)
    w_down_b = w_down[0].astype(BF16)
    conv_nw, sconv_b, snw = row(conv_norm_w[0]), row(ssm_conv_b[0]), row(ssm_norm_w[0])
    dtb, alog = pad_heads(dt_bias[0]), pad_heads(a_log[0])
    dexp = row(jnp.repeat(d_skip[0], SSM_HEAD_DIM))
    g1, b1, g2, b2 = row(ln1_g[0]), row(ln1_b[0]), row(ln2_g[0]), row(ln2_b[0])

    mod_p, mod_s = _ada(c_sample, c_prompt, w_ada[0], row(b_ada[0]))
    mod_p = mod_p.reshape(nb, 1, 6 * D_MODEL)

    x1_p, cst_p, scst_p, sst_p = _mixer_prompt(x_prompt, mod_p, w_in_b, w_dt_b, conv_w[0], conv_nw, ssm_conv_w[0],
                                               sconv_b, dtb, alog, dexp, snw, w_out_b, g1, b1)
    y_p = _ffn(x1_p, mod_p, seq, w_up_b, w_down_b, g2, b2, tile=512)

    xs2 = x_sample.reshape(ns, D_MODEL)
    (yconv_s, ch_s, xbc_s, z_s, xs_s, ydiag_s, decx_s, cm_s, lhs_s, rhs_s) = _sample_pre(
        xs2, mod_s, w_in_b, w_dt_b, conv_w[0], conv_nw, ssm_conv_w[0], sconv_b, dtb, alog,
        state_conv[0, :, 0], state_conv[0, :, 1],
        state_ssm_conv[0, :, 0], state_ssm_conv[0, :, 1], state_ssm_conv[0, :, 2])
    new_state_s, yoff_s = _sample_state(state_ssm[0].reshape(ns, SSM_CH, SSM_STATE), lhs_s, rhs_s, cm_s, decx_s)
    x1_s = _sample_post(xs2, mod_s, yconv_s, ydiag_s, yoff_s, xs_s, z_s, dexp, snw, w_out_b, g1, b1)
    y_s = _ffn_stream(x1_s, mod_s, w_up_b, w_down_b, g2, b2)

    return (y_p.reshape(nb, seq, D_MODEL),
            y_s.reshape(ns, 1, D_MODEL),
            cst_p[None],
            scst_p[None],
            sst_p.reshape(1, nb, SSM_HEADS, SSM_HEAD_DIM, SSM_STATE),
            jnp.stack([state_conv[0, :, 1], ch_s], axis=1)[None],
            jnp.stack([state_ssm_conv[0, :, 1], state_ssm_conv[0, :, 2], xbc_s], axis=1)[None],
            new_state_s.reshape(1, ns, SSM_HEADS, SSM_HEAD_DIM, SSM_STATE))
```

```python
import functools

import jax
import jax.numpy as jnp
from jax import lax
from jax.experimental import pallas as pl
from jax.experimental.pallas import tpu as pltpu

F32 = jnp.float32
BF16 = jnp.bfloat16

D_MODEL = 1024
CONV_CH = 1024
CONV_GROUP = 64
SSM_CH = 1024
SSM_HEADS = 16
SSM_HEAD_DIM = 64
SSM_GROUPS = 2
SSM_GROUP_CH = SSM_CH // SSM_GROUPS
SSM_STATE = 128
SSM_CHUNK = 128
XBC_CH = SSM_CH + 2 * SSM_GROUPS * SSM_STATE
D_FF = 4 * D_MODEL
LANES = 128
SUBLANES = 8
MXU_COLS = 256
COL_GB, COL_GC, COL_HV, COL_Z, COL_XBC = 0, 1024, 2048, 3072, 4096
COL_DT = COL_XBC + XBC_CH
IN_COLS = COL_DT + SSM_HEADS
IN_PAD = COL_DT + LANES
ALPHA = 2.0 ** 0.25
LN_EPS = 1e-5
RMS_EPS = 1e-5
VMEM_LIMIT = 56 * 1024 * 1024


def _dot(a, b):
    return jnp.dot(a, b, preferred_element_type=F32)


def _split(a, terms):
    parts = []
    r = a
    for t in range(terms):
        p = r.astype(BF16)
        parts.append(p)
        if t + 1 < terms:
            r = r - p.astype(F32)
    return parts


def _dot_f32_lhs(a, b_exact, terms=3):
    parts = _split(a, terms)
    out = _dot(parts[0], b_exact)
    for p in parts[1:]:
        out = out + _dot(p, b_exact)
    return out


def _dot_f32_rhs(a_exact, b, terms=3):
    parts = _split(b, terms)
    out = _dot(a_exact, parts[0])
    for p in parts[1:]:
        out = out + _dot(a_exact, p)
    return out


def _head_expand():
    h = lax.broadcasted_iota(jnp.int32, (LANES, SSM_CH), 0)
    c = lax.broadcasted_iota(jnp.int32, (LANES, SSM_CH), 1)
    return (c // SSM_HEAD_DIM == h).astype(BF16)


def _group_reduce():
    c = lax.broadcasted_iota(jnp.int32, (CONV_CH, LANES), 0)
    k = lax.broadcasted_iota(jnp.int32, (CONV_CH, LANES), 1)
    return (c // CONV_GROUP == k).astype(BF16)


def _sigmoid(x):
    return 1.0 / (1.0 + jnp.exp(-x))


def _silu(x):
    return x * _sigmoid(x)


def _softplus(x):
    return jnp.maximum(x, 0.0) + jnp.log1p(jnp.exp(-jnp.abs(x)))


def _layer_norm(r, g, b):
    mu = jnp.mean(r, axis=-1, keepdims=True)
    d = r - mu
    var = jnp.mean(d * d, axis=-1, keepdims=True)
    return d * lax.rsqrt(var + LN_EPS) * g + b


def _conv_group_norm(prod, w, expand, reduce):
    ssum = _dot_f32_lhs(prod * prod, reduce, terms=2)
    rstd = lax.rsqrt(ssum * (1.0 / CONV_GROUP) + RMS_EPS)
    return prod * _dot_f32_lhs(rstd, expand, terms=2) * w


def _ssm_group_norm(y, w):
    outs = []
    for g in range(SSM_GROUPS):
        yg = y[:, g * SSM_GROUP_CH:(g + 1) * SSM_GROUP_CH]
        ms = jnp.mean(yg * yg, axis=-1, keepdims=True)
        outs.append((yg * lax.rsqrt(ms + RMS_EPS) * w[:, g * SSM_GROUP_CH:(g + 1) * SSM_GROUP_CH]).astype(BF16))
    return outs


def _mix_out(y_conv, y_ssm_groups, w_out_ref):
    m = _dot(y_conv.astype(BF16), w_out_ref[0:CONV_CH, :])
    for g, yg in enumerate(y_ssm_groups):
        lo = CONV_CH + g * SSM_GROUP_CH
        m = m + _dot(yg, w_out_ref[lo:lo + SSM_GROUP_CH, :])
    return m


def _ada_kernel(c_ref, w_ref, b_ref, op_ref, os_ref):
    c = c_ref[...]
    w = w_ref[...]
    c_hi = c.astype(BF16)
    c_lo = (c - c_hi.astype(F32)).astype(BF16)
    w_hi = w.astype(BF16)
    mod = _dot(c_hi, w_hi) + _dot(c_lo, w_hi) + b_ref[...]
    n_sample = os_ref.shape[0]
    os_ref[...] = mod[0:n_sample, :]
    op_ref[...] = mod[n_sample:, :]


def _ada(c_sample, c_prompt, w_ada, b_ada, tile_n=1024):
    ns, nb = c_sample.shape[0], c_prompt.shape[0]
    n = w_ada.shape[1]
    return pl.pallas_call(
        _ada_kernel,
        grid=(n // tile_n,),
        in_specs=[pl.BlockSpec((ns + nb, D_MODEL), lambda i: (0, 0)),
                  pl.BlockSpec((D_MODEL, tile_n), lambda i: (0, i)),
                  pl.BlockSpec((1, tile_n), lambda i: (0, i))],
        out_specs=[pl.BlockSpec((nb, tile_n), lambda i: (0, i)),
                   pl.BlockSpec((ns, tile_n), lambda i: (0, i))],
        out_shape=[jax.ShapeDtypeStruct((nb, n), F32), jax.ShapeDtypeStruct((ns, n), F32)],
        name="ada_mod",
    )(jnp.concatenate([c_sample, c_prompt], axis=0), w_ada, b_ada)


def _cast_transposed_kernel(wt_ref, o_ref):
    rows = wt_ref.shape[0]
    wt = wt_ref[...]
    if rows < o_ref.shape[1]:
        wt = jnp.concatenate([wt, jnp.zeros((o_ref.shape[1] - rows, wt.shape[1]), wt.dtype)], axis=0)
    o_ref[...] = wt.T.astype(o_ref.dtype)


def _cast_in_proj(w):
    wt = w.T
    n_dt = w.shape[1] - COL_DT
    cols = COL_DT // 4
    assert cols % LANES == 0
    main = pl.pallas_call(
        _cast_transposed_kernel,
        grid=(COL_DT // cols,),
        in_specs=[pl.BlockSpec((cols, D_MODEL), lambda j: (j, 0))],
        out_specs=pl.BlockSpec((D_MODEL, cols), lambda j: (0, j)),
        out_shape=jax.ShapeDtypeStruct((D_MODEL, COL_DT), BF16),
        compiler_params=pltpu.CompilerParams(vmem_limit_bytes=VMEM_LIMIT),
        name="cast_in_proj",
    )(wt)
    dt = pl.pallas_call(
        _cast_transposed_kernel,
        grid=(1,),
        in_specs=[pl.BlockSpec((n_dt, D_MODEL), lambda j: (COL_DT // n_dt, 0))],
        out_specs=pl.BlockSpec((D_MODEL, LANES), lambda j: (0, 0)),
        out_shape=jax.ShapeDtypeStruct((D_MODEL, LANES), BF16),
        name="cast_in_proj_dt",
    )(wt)
    return main, dt


def _mixer_prompt_kernel(xa_ref, moda_ref, modb_ref, w_in_ref, w_dt_ref, expand_ref, reduce_ref,
                         conv_w_ref, conv_nw_ref, sconv_w_ref, sconv_b_ref,
                         dtb_ref, alog_ref, dexp_ref, snw_ref, w_out_ref, ln_g_ref, ln_b_ref,
                         x1_ref, cst_ref, scst_ref, sst_ref,
                         p0, p1, xk0, xk1, cbuf, xbuf, st_ref, xs_ref, bc_ref, dtx_ref, acsx_ref, endx_ref,
                         acst_ref, cb_ref, bmt_ref, y_ref, yc_ref,
                         *, tile, tiles_per_seq, sched):
    s = pl.program_id(0)
    jb = lax.rem(s + (tiles_per_seq - 1), tiles_per_seq)

    @pl.when(s == 0)
    def _():
        p1[...] = jnp.zeros_like(p1)
        xk1[...] = jnp.zeros_like(xk1)

    @pl.when((jb == 0) | (s == 0))
    def _():
        cbuf[...] = jnp.zeros_like(cbuf)
        xbuf[...] = jnp.zeros_like(xbuf)
        st_ref[...] = jnp.zeros_like(st_ref)

    def stages(pa, xka, pb, xkb):
        xa = xa_ref[...]
        xka[...] = xa
        u = (xa * (1.0 + moda_ref[:, D_MODEL:2 * D_MODEL]) + moda_ref[:, 0:D_MODEL]).astype(BF16)
        pieces = iter(list(range(0, COL_DT, MXU_COLS)) + [COL_DT])

        def first_stage(n):
            for _ in range(n):
                lo = next(pieces, None)
                if lo == COL_DT:
                    pa[:, COL_DT:IN_PAD] = _dot(u, w_dt_ref[...])
                elif lo is not None:
                    pa[:, lo:lo + MXU_COLS] = _dot(u, w_in_ref[:, lo:lo + MXU_COLS])

        expand = expand_ref[...]
        x = xkb[...]
        g1 = modb_ref[:, 2 * D_MODEL:3 * D_MODEL]

        def proj(lo, width):
            return pb[:, lo:lo + width]

        def delayed(tail_ref, cs, cur, taps):
            seq = jnp.concatenate([tail_ref[:, cs], cur], axis=0)
            tail_ref[:, cs] = cur[tile - SUBLANES:, :]
            return [pltpu.roll(seq, k, axis=0)[SUBLANES:, :] for k in range(1, taps + 1)]

        for k in range(CONV_CH // MXU_COLS):
            first_stage(sched[0])
            c0 = k * MXU_COLS
            cs = slice(c0, c0 + MXU_COLS)
            ch = proj(COL_GC + c0, MXU_COLS) * proj(COL_HV + c0, MXU_COLS)
            ch1, ch2 = delayed(cbuf, cs, ch, 2)
            cv = conv_w_ref[0:1, cs] * ch2 + conv_w_ref[1:2, cs] * ch1 + conv_w_ref[2:3, cs] * ch
            prod = proj(COL_GB + c0, MXU_COLS) * cv
            ssum = _dot_f32_lhs(prod * prod, reduce_ref[cs, :], terms=1)
            rstd = lax.rsqrt(ssum * (1.0 / CONV_GROUP) + RMS_EPS)
            yc_ref[:, cs] = (prod * _dot_f32_lhs(rstd, expand_ref[:, cs], terms=2)
                             * conv_nw_ref[:, cs]).astype(BF16)

        def pre_conv(c0):
            cs = slice(c0, c0 + MXU_COLS)
            xbc = proj(COL_XBC + c0, MXU_COLS)
            x1, x2, x3 = delayed(xbuf, cs, xbc, 3)
            return _silu(sconv_w_ref[0:1, cs] * x3 + sconv_w_ref[1:2, cs] * x2 + sconv_w_ref[2:3, cs] * x1
                         + sconv_w_ref[3:4, cs] * xbc + sconv_b_ref[:, cs])

        row = lax.broadcasted_iota(jnp.int32, (SSM_CHUNK, SSM_CHUNK), 0)
        col = lax.broadcasted_iota(jnp.int32, (SSM_CHUNK, SSM_CHUNK), 1)
        causal = row >= col
        tri = causal.astype(BF16)
        groups = SSM_CHUNK // SUBLANES
        causal_bias = jnp.where(causal, 0.0, -jnp.inf).reshape(groups, SUBLANES, SSM_CHUNK)
        first_half = (col < SSM_HEAD_DIM).reshape(groups, SUBLANES, SSM_CHUNK)
        half_rows = col < SSM_HEAD_DIM
        chunks = [slice(c * SSM_CHUNK, (c + 1) * SSM_CHUNK) for c in range(tile // SSM_CHUNK)]

        first_stage(sched[1])
        dt = _softplus(proj(COL_DT, LANES) + dtb_ref[...])
        dta = dt * (-jnp.exp(alog_ref[...]))
        dtx_ref[...] = _dot_f32_lhs(dt, expand, terms=1)
        for c, rows in enumerate(chunks):
            acs = _dot_f32_rhs(tri, dta[rows, :])
            acs_t = acs.T
            for h in range(SSM_HEADS):
                r8 = (c * SSM_HEADS + h) * SUBLANES
                acst_ref[r8:r8 + SUBLANES, :] = jnp.broadcast_to(acs_t[h:h + 1, :], (SUBLANES, SSM_CHUNK))
            acs_x = _dot_f32_lhs(acs, expand, terms=2)
            acsx_ref[rows, :] = acs_x
            endx_ref[c * SUBLANES:(c + 1) * SUBLANES, :] = jnp.broadcast_to(acs_x[SSM_CHUNK - 1:SSM_CHUNK, :],
                                                                             (SUBLANES, SSM_CH))
        for c0 in range(SSM_CH, XBC_CH, MXU_COLS):
            first_stage(sched[2])
            bc_ref[:, c0 - SSM_CH:c0 - SSM_CH + MXU_COLS] = pre_conv(c0)
        for rows in chunks:
            for g in range(SSM_GROUPS):
                gs = slice(g * SSM_STATE, (g + 1) * SSM_STATE)
                bm = bc_ref[rows, gs]
                cm = bc_ref[rows, (SSM_GROUPS + g) * SSM_STATE:(SSM_GROUPS + g + 1) * SSM_STATE]
                cb_ref[rows, gs] = lax.dot_general(cm.astype(BF16), bm.astype(BF16), (((1,), (1,)), ((), ())),
                                                   preferred_element_type=F32)
                bmt_ref[rows, gs] = bm.T.astype(BF16)

        for c0 in range(0, SSM_CH, MXU_COLS):
            first_stage(sched[3])
            cs = slice(c0, c0 + MXU_COLS)
            g = c0 // SSM_GROUP_CH
            gs = slice(g * SSM_STATE, (g + 1) * SSM_STATE)
            xs = pre_conv(c0)
            xs_ref[:, cs] = xs
            xdt = xs * dtx_ref[:, cs]
            for c, rows in enumerate(chunks):
                first_stage(sched[4])
                acs_x = acsx_ref[rows, cs].reshape(groups, SUBLANES, MXU_COLS)
                end_x = endx_ref[c * SUBLANES:(c + 1) * SUBLANES, cs]
                xdt_c = xdt[rows, :]
                xdec = (xdt_c * jnp.exp(end_x[None] - acs_x).reshape(SSM_CHUNK, MXU_COLS)).astype(BF16)
                cm = bc_ref[rows, (SSM_GROUPS + g) * SSM_STATE:(SSM_GROUPS + g + 1) * SSM_STATE].astype(BF16)
                cb = cb_ref[rows, gs]
                st = st_ref[:, cs]
                y_off = _dot(cm, st.astype(BF16)) * jnp.exp(acs_x).reshape(SSM_CHUNK, MXU_COLS)
                st_ref[:, cs] = ((st.reshape(groups, SUBLANES, MXU_COLS) * jnp.exp(end_x)[None])
                                 .reshape(SSM_STATE, MXU_COLS) + _dot(bmt_ref[rows, gs], xdec))
                for lo in range(0, MXU_COLS, LANES):
                    h0 = (c * SSM_HEADS + (c0 + lo) // SSM_HEAD_DIM) * SUBLANES
                    slab = acs_x[:, :, lo:lo + LANES]
                    rolled = pltpu.roll(slab, SSM_HEAD_DIM, axis=2)
                    a0 = jnp.where(first_half, slab, rolled) - acst_ref[h0:h0 + SUBLANES, :][None]
                    a1 = jnp.where(first_half, rolled, slab) - acst_ref[h0 + SUBLANES:h0 + 2 * SUBLANES, :][None]
                    l0 = jnp.exp(a0 + causal_bias).reshape(SSM_CHUNK, SSM_CHUNK)
                    l1 = jnp.exp(a1 + causal_bias).reshape(SSM_CHUNK, SSM_CHUNK)
                    m = jnp.concatenate([(cb * l0).astype(BF16), (cb * l1).astype(BF16)], axis=1)
                    xp = xdt_c[:, lo:lo + LANES]
                    rhs = jnp.concatenate([jnp.where(half_rows, xp, 0.0), jnp.where(half_rows, 0.0, xp)],
                                          axis=0).astype(BF16)
                    y_ref[rows, c0 + lo:c0 + lo + LANES] = _dot(m, rhs) + y_off[:, lo:lo + LANES]

        for k in range(SSM_CH // MXU_COLS):
            first_stage(sched[5])
            c0 = k * MXU_COLS
            cs = slice(c0, c0 + MXU_COLS)
            y_ref[:, cs] = (y_ref[:, cs] + xs_ref[:, cs] * dexp_ref[:, cs]) * _silu(proj(COL_Z + c0, MXU_COLS))
        first_stage(sched[6])
        m = _mix_out(yc_ref[...], _ssm_group_norm(y_ref[...], snw_ref[...]), w_out_ref)
        x1_ref[...] = _layer_norm(ALPHA * x + (1.0 + g1) * m, ln_g_ref[...], ln_b_ref[...])
        first_stage(IN_PAD // MXU_COLS)

    @pl.when(lax.rem(s, 2) == 0)
    def _():
        stages(p0, xk0, p1, xk1)

    @pl.when(lax.rem(s, 2) == 1)
    def _():
        stages(p1, xk1, p0, xk0)

    @pl.when((jb == tiles_per_seq - 1) & (s > 0))
    def _():
        cst_ref[...] = cbuf[SUBLANES - 2:SUBLANES, :]
        scst_ref[...] = xbuf[SUBLANES - 3:SUBLANES, :]
        sst_ref[...] = st_ref[...].T


def _const_spec(shape):
    return pl.BlockSpec(shape, lambda *_: (0,) * len(shape), pipeline_mode=pl.Buffered(1))


def _mixer_prompt(x, mod, w_in, w_dt, conv_w, conv_nw, sconv_w, sconv_b, dtb, alog, dexp, snw, w_out, ln_g, ln_b,
                  tile=256, sched=(1, 0, 0, 2, 1, 0, 0)):
    assert CONV_GROUP == SSM_HEAD_DIM and CONV_CH == SSM_CH
    nb, seq, _ = x.shape
    tiles_per_seq = seq // tile
    n_tiles = nb * tiles_per_seq
    kern = functools.partial(_mixer_prompt_kernel, tile=tile, tiles_per_seq=tiles_per_seq, sched=sched)
    consts = [w_in, w_dt, _head_expand(), _group_reduce(), conv_w, conv_nw, sconv_w, sconv_b, dtb, alog, dexp, snw,
              w_out, ln_g, ln_b]
    first = lambda s: jnp.minimum(s, n_tiles - 1)
    second = lambda s: jnp.maximum(s - 1, 0)
    return pl.pallas_call(
        kern,
        grid=(n_tiles + 1,),
        in_specs=[pl.BlockSpec((tile, D_MODEL), lambda s: (first(s), 0)),
                  pl.BlockSpec((None, 1, 6 * D_MODEL), lambda s: (first(s) // tiles_per_seq, 0, 0)),
                  pl.BlockSpec((None, 1, 6 * D_MODEL), lambda s: (second(s) // tiles_per_seq, 0, 0))]
                 + [_const_spec(a.shape) for a in consts],
        out_specs=[pl.BlockSpec((tile, D_MODEL), lambda s: (second(s), 0)),
                   pl.BlockSpec((None, 2, CONV_CH), lambda s: (second(s) // tiles_per_seq, 0, 0)),
                   pl.BlockSpec((None, 3, XBC_CH), lambda s: (second(s) // tiles_per_seq, 0, 0)),
                   pl.BlockSpec((None, SSM_CH, SSM_STATE), lambda s: (second(s) // tiles_per_seq, 0, 0))],
        out_shape=[jax.ShapeDtypeStruct((nb * seq, D_MODEL), F32),
                   jax.ShapeDtypeStruct((nb, 2, CONV_CH), F32),
                   jax.ShapeDtypeStruct((nb, 3, XBC_CH), F32),
                   jax.ShapeDtypeStruct((nb, SSM_CH, SSM_STATE), F32)],
        scratch_shapes=[pltpu.VMEM((tile, IN_PAD), F32),
                        pltpu.VMEM((tile, IN_PAD), F32),
                        pltpu.VMEM((tile, D_MODEL), F32),
                        pltpu.VMEM((tile, D_MODEL), F32),
                        pltpu.VMEM((SUBLANES, CONV_CH), F32),
                        pltpu.VMEM((SUBLANES, XBC_CH), F32),
                        pltpu.VMEM((SSM_STATE, SSM_CH), F32),
                        pltpu.VMEM((tile, SSM_CH), F32),
                        pltpu.VMEM((tile, 2 * SSM_GROUPS * SSM_STATE), F32),
                        pltpu.VMEM((tile, SSM_CH), F32),
                        pltpu.VMEM((tile, SSM_CH), F32),
                        pltpu.VMEM((tile // SSM_CHUNK * SUBLANES, SSM_CH), F32),
                        pltpu.VMEM((tile // SSM_CHUNK * SSM_HEADS * SUBLANES, SSM_CHUNK), F32),
                        pltpu.VMEM((tile, SSM_GROUPS * SSM_STATE), F32),
                        pltpu.VMEM((tile, SSM_GROUPS * SSM_STATE), BF16),
                        pltpu.VMEM((tile, SSM_CH), F32),
                        pltpu.VMEM((tile, CONV_CH), BF16)],
        compiler_params=pltpu.CompilerParams(dimension_semantics=("arbitrary",),
                                             vmem_limit_bytes=VMEM_LIMIT),
        name="mixer_prompt",
    )(x.reshape(nb * seq, D_MODEL), mod, mod, *consts)


def _ffn_kernel(x_ref, mod_ref, w_up_ref, w_down_ref, ln_g_ref, ln_b_ref, o_ref, r_ref, *, ff_tile):
    s = pl.program_id(0)
    n_tiles = pl.num_programs(0) - 1

    @pl.when(s == 0)
    def _():
        r_ref[...] = jnp.zeros_like(r_ref)

    def norm_previous():
        o_ref[...] = _layer_norm(r_ref[...], ln_g_ref[...], ln_b_ref[...])

    @pl.when(s < n_tiles)
    def _():
        norm_previous()
        x = x_ref[...]
        sh2 = mod_ref[:, 3 * D_MODEL:4 * D_MODEL]
        sc2 = mod_ref[:, 4 * D_MODEL:5 * D_MODEL]
        g2 = mod_ref[:, 5 * D_MODEL:6 * D_MODEL]
        v = (x * (1.0 + sc2) + sh2).astype(BF16)
        acc = jnp.zeros(x.shape, F32)
        for k in range(D_FF // ff_tile):
            h = jnp.maximum(_dot(v, w_up_ref[:, k * ff_tile:(k + 1) * ff_tile]), 0.0)
            acc = acc + _dot((h * h).astype(BF16), w_down_ref[k * ff_tile:(k + 1) * ff_tile, :])
        r_ref[...] = ALPHA * x + (1.0 + g2) * acc

    @pl.when(s == n_tiles)
    def _():
        norm_previous()


def _ffn(x, mod, rows_per_mod, w_up, w_down, ln_g, ln_b, tile, ff_tile=1024):
    rows = x.shape[0]
    mod_rows = mod.shape[1]
    tiles_per_mod = rows_per_mod // tile
    n_tiles = rows // tile
    kern = functools.partial(_ffn_kernel, ff_tile=ff_tile)
    first = lambda s: jnp.minimum(s, n_tiles - 1)
    second = lambda s: jnp.maximum(s - 1, 0)
    return pl.pallas_call(
        kern,
        grid=(n_tiles + 1,),
        in_specs=[pl.BlockSpec((tile, D_MODEL), lambda s: (first(s), 0)),
                  pl.BlockSpec((None, mod_rows, 6 * D_MODEL), lambda s: (first(s) // tiles_per_mod, 0, 0)),
                  _const_spec(w_up.shape), _const_spec(w_down.shape),
                  _const_spec(ln_g.shape), _const_spec(ln_b.shape)],
        out_specs=pl.BlockSpec((tile, D_MODEL), lambda s: (second(s), 0)),
        out_shape=jax.ShapeDtypeStruct((rows, D_MODEL), F32),
        scratch_shapes=[pltpu.VMEM((tile, D_MODEL), F32)],
        compiler_params=pltpu.CompilerParams(dimension_semantics=("arbitrary",),
                                             vmem_limit_bytes=VMEM_LIMIT),
        name="ffn",
    )(x, mod, w_up, w_down, ln_g, ln_b)


def _ffn_stream_kernel(x_ref, mod_ref, w_up_ref, w_down_ref, ln_g_ref, ln_b_ref, o_ref, acc_ref):
    k = pl.program_id(0)

    @pl.when(k == 0)
    def _():
        acc_ref[...] = jnp.zeros_like(acc_ref)

    x = x_ref[...]
    sh2 = mod_ref[:, 3 * D_MODEL:4 * D_MODEL]
    sc2 = mod_ref[:, 4 * D_MODEL:5 * D_MODEL]
    v = (x * (1.0 + sc2) + sh2).astype(BF16)
    h = jnp.maximum(_dot(v, w_up_ref[...]), 0.0)
    acc_ref[...] += _dot((h * h).astype(BF16), w_down_ref[...])

    @pl.when(k == pl.num_programs(0) - 1)
    def _():
        g2 = mod_ref[:, 5 * D_MODEL:6 * D_MODEL]
        o_ref[...] = _layer_norm(ALPHA * x + (1.0 + g2) * acc_ref[...], ln_g_ref[...], ln_b_ref[...])


def _ffn_stream(x, mod, w_up, w_down, ln_g, ln_b, ff_tile=1024):
    rows = x.shape[0]
    return pl.pallas_call(
        _ffn_stream_kernel,
        grid=(D_FF // ff_tile,),
        in_specs=[_const_spec(x.shape), _const_spec(mod.shape),
                  pl.BlockSpec((D_MODEL, ff_tile), lambda k: (0, k)),
                  pl.BlockSpec((ff_tile, D_MODEL), lambda k: (k, 0)),
                  _const_spec(ln_g.shape), _const_spec(ln_b.shape)],
        out_specs=pl.BlockSpec((rows, D_MODEL), lambda k: (0, 0)),
        out_shape=jax.ShapeDtypeStruct((rows, D_MODEL), F32),
        scratch_shapes=[pltpu.VMEM((rows, D_MODEL), F32)],
        compiler_params=pltpu.CompilerParams(dimension_semantics=("arbitrary",),
                                             vmem_limit_bytes=VMEM_LIMIT),
        name="ffn_stream",
    )(x, mod, w_up, w_down, ln_g, ln_b)


_PRODUCT_TERMS = ((0, 0), (0, 1), (1, 0), (0, 2), (2, 0), (1, 1))
UPDATE_TERMS = 16


def _sample_pre_kernel(x_ref, mod_ref, w_in_ref, w_dt_ref, conv_w_ref, conv_nw_ref, sconv_w_ref, sconv_b_ref,
                       dtb_ref, alog_ref, cb0_ref, cb1_ref, sb0_ref, sb1_ref, sb2_ref,
                       yconv_ref, ch_ref, xbc_ref, z_ref, xs_ref, ydiag_ref, decx_ref, cm_ref, lhs_ref, rhs_ref):
    expand = _head_expand()
    reduce = _group_reduce()
    x = x_ref[...]
    sh1 = mod_ref[:, 0:D_MODEL]
    sc1 = mod_ref[:, D_MODEL:2 * D_MODEL]
    u = (x * (1.0 + sc1) + sh1).astype(BF16)

    def proj(lo, width):
        return _dot(u, w_in_ref[:, lo:lo + width])

    ch = proj(COL_GC, CONV_CH) * proj(COL_HV, CONV_CH)
    ch_ref[...] = ch
    cw = conv_w_ref[...]
    cv = cw[0:1, :] * cb0_ref[...] + cw[1:2, :] * cb1_ref[...] + cw[2:3, :] * ch
    yconv_ref[...] = _conv_group_norm(proj(COL_GB, CONV_CH) * cv, conv_nw_ref[...], expand, reduce)

    xbc = proj(COL_XBC, XBC_CH)
    xbc_ref[...] = xbc
    sw = sconv_w_ref[...]
    xc = _silu(sw[0:1, :] * sb0_ref[...] + sw[1:2, :] * sb1_ref[...] + sw[2:3, :] * sb2_ref[...]
               + sw[3:4, :] * xbc + sconv_b_ref[...])
    xs = xc[:, 0:SSM_CH]
    xs_ref[...] = xs
    cm_ref[...] = xc[:, SSM_CH + SSM_GROUPS * SSM_STATE:XBC_CH]
    z_ref[...] = proj(COL_Z, SSM_CH)

    dt = _softplus(_dot(u, w_dt_ref[...]) + dtb_ref[...])
    dta = dt * (-jnp.exp(alog_ref[...]))
    xdt = xs * _dot_f32_lhs(dt, expand)
    decx = jnp.exp(_dot_f32_lhs(dta, expand))
    decx_ref[...] = decx
    xdt_t, dec_t = _split(xdt, 3), _split(decx, 3)
    bm_t = _split(xc[:, SSM_CH:SSM_CH + SSM_GROUPS * SSM_STATE], 3)
    group_of = lax.broadcasted_iota(jnp.int32, xdt.shape, 1) // SSM_GROUP_CH
    zeros = jnp.zeros((x.shape[0], SSM_STATE), BF16)
    r = 0
    for g in range(SSM_GROUPS):
        for tx, tb in _PRODUCT_TERMS:
            lhs_ref[r] = jnp.where(group_of == g, xdt_t[tx], jnp.zeros_like(xdt_t[tx]))
            rhs_ref[r] = jnp.concatenate([bm_t[tb][:, g * SSM_STATE:(g + 1) * SSM_STATE], zeros], axis=1)
            r += 1
    for t in range(3):
        lhs_ref[r] = dec_t[t]
        rhs_ref[r] = jnp.concatenate([zeros, jnp.ones_like(zeros)], axis=1)
        r += 1
    for r in range(r, UPDATE_TERMS):
        lhs_ref[r] = jnp.zeros_like(dec_t[0])
        rhs_ref[r] = jnp.concatenate([zeros, zeros], axis=1)
    for g in range(SSM_GROUPS):
        bm = xc[:, SSM_CH + g * SSM_STATE:SSM_CH + (g + 1) * SSM_STATE]
        cm = xc[:, SSM_CH + (SSM_GROUPS + g) * SSM_STATE:SSM_CH + (SSM_GROUPS + g + 1) * SSM_STATE]
        cb = jnp.sum(cm * bm, axis=-1, keepdims=True)
        gl = g * SSM_GROUP_CH
        ydiag_ref[:, gl:gl + SSM_GROUP_CH] = cb * xdt[:, gl:gl + SSM_GROUP_CH]


def _sample_pre(x, mod, w_in, w_dt, conv_w, conv_nw, sconv_w, sconv_b, dtb, alog, cb0, cb1, sb0, sb1, sb2):
    n = x.shape[0]
    args = (x, mod, w_in, w_dt, conv_w, conv_nw, sconv_w, sconv_b, dtb, alog, cb0, cb1, sb0, sb1, sb2)
    f32_shapes = [(n, CONV_CH), (n, CONV_CH), (n, XBC_CH), (n, SSM_CH), (n, SSM_CH), (n, SSM_CH), (n, SSM_CH),
                  (n, SSM_GROUPS * SSM_STATE)]
    bf16_shapes = [(UPDATE_TERMS, n, SSM_CH), (UPDATE_TERMS, n, 2 * SSM_STATE)]
    return pl.pallas_call(
        _sample_pre_kernel,
        out_shape=[jax.ShapeDtypeStruct(s, F32) for s in f32_shapes]
                  + [jax.ShapeDtypeStruct(s, BF16) for s in bf16_shapes],
        compiler_params=pltpu.CompilerParams(vmem_limit_bytes=VMEM_LIMIT),
        name="sample_pre",
    )(*args)


def _sample_state_kernel(s_ref, lhs_ref, rhs_ref, cm_ref, decx_ref, o_ref, yoff_ref, *, block):
    i = pl.program_id(0)
    rows = UPDATE_TERMS * block
    lhs_t = lhs_ref[...].reshape(rows, SSM_CH).astype(F32).T.astype(BF16)
    rhs_all = rhs_ref[...].reshape(rows, 2 * SSM_STATE)
    token_of = lax.broadcasted_iota(jnp.int32, rhs_all.shape, 0) % block

    def body(k, carry):
        b = i * block + k
        s = s_ref[k]
        upd = _dot(lhs_t, jnp.where(token_of == k, rhs_all, jnp.zeros_like(rhs_all)))
        o_ref[k] = s * upd[:, SSM_STATE:2 * SSM_STATE] + upd[:, 0:SSM_STATE]
        cm = cm_ref[pl.ds(b, 1), :]
        sums = []
        for c0 in range(0, SSM_CH, LANES):
            g = c0 // SSM_GROUP_CH
            prod = s[c0:c0 + LANES, :] * cm[:, g * SSM_STATE:(g + 1) * SSM_STATE]
            sums.append(jnp.sum(prod.T, axis=0, keepdims=True))
        yoff_ref[pl.ds(b, 1), :] = jnp.concatenate(sums, axis=1) * decx_ref[pl.ds(b, 1), :]
        return carry

    lax.fori_loop(0, block, body, 0, unroll=True)


def _sample_state(state, lhs, rhs, cm, decx, block=16):
    n = state.shape[0]
    kern = functools.partial(_sample_state_kernel, block=block)
    return pl.pallas_call(
        kern,
        grid=(n // block,),
        in_specs=[pl.BlockSpec((block, SSM_CH, SSM_STATE), lambda i: (i, 0, 0)),
                  pl.BlockSpec((UPDATE_TERMS, block, SSM_CH), lambda i: (0, i, 0)),
                  pl.BlockSpec((UPDATE_TERMS, block, 2 * SSM_STATE), lambda i: (0, i, 0)),
                  _const_spec(cm.shape), _const_spec(decx.shape)],
        out_specs=[pl.BlockSpec((block, SSM_CH, SSM_STATE), lambda i: (i, 0, 0)),
                   pl.BlockSpec((n, SSM_CH), lambda i: (0, 0))],
        out_shape=[jax.ShapeDtypeStruct(state.shape, F32), jax.ShapeDtypeStruct((n, SSM_CH), F32)],
        compiler_params=pltpu.CompilerParams(dimension_semantics=("arbitrary",),
                                             vmem_limit_bytes=VMEM_LIMIT),
        name="sample_state",
    )(state, lhs, rhs, cm, decx)


def _sample_post_kernel(x_ref, mod_ref, yconv_ref, ydiag_ref, yoff_ref, xs_ref, z_ref, dexp_ref, snw_ref,
                        w_out_ref, ln_g_ref, ln_b_ref, x1_ref):
    g1 = mod_ref[:, 2 * D_MODEL:3 * D_MODEL]
    y = ydiag_ref[...] + yoff_ref[...] + xs_ref[...] * dexp_ref[...]
    y = y * _silu(z_ref[...])
    m = _mix_out(yconv_ref[...], _ssm_group_norm(y, snw_ref[...]), w_out_ref)
    x1_ref[...] = _layer_norm(ALPHA * x_ref[...] + (1.0 + g1) * m, ln_g_ref[...], ln_b_ref[...])


def _sample_post(x, mod, yconv, ydiag, yoff, xs, z, dexp, snw, w_out, ln_g, ln_b):
    return pl.pallas_call(
        _sample_post_kernel,
        out_shape=jax.ShapeDtypeStruct(x.shape, F32),
        compiler_params=pltpu.CompilerParams(vmem_limit_bytes=VMEM_LIMIT),
        name="sample_post",
    )(x, mod, yconv, ydiag, yoff, xs, z, dexp, snw, w_out, ln_g, ln_b)


def kernel(x_prompt, x_sample, state_conv, state_ssm_conv, state_ssm, c_prompt, c_sample, w_ada, b_ada, w_in, conv_w, conv_norm_w, ssm_conv_w, ssm_conv_b, dt_bias, a_log, d_skip, ssm_norm_w, w_out, ln1_g, ln1_b, w_up, w_down, ln2_g, ln2_b):
    assert w_ada.shape[0] == 1, "single-layer trunk"
    nb, seq, _ = x_prompt.shape
    ns = x_sample.shape[0]
    row = lambda a: a.reshape(1, -1)
    pad_heads = lambda a: jnp.pad(a.reshape(1, -1), ((0, 0), (0, LANES - SSM_HEADS)))

    w_in_b, w_dt_b = _cast_in_proj(w_in[0])
    w_out_b = w_out[0].astype(BF16)
    w_up_b = w_up[0].astype(BF16)
    w_down_b = w_down[0].astype(BF16)
    conv_nw, sconv_b, snw = row(conv_norm_w[0]), row(ssm_conv_b[0]), row(ssm_norm_w[0])
    dtb, alog = pad_heads(dt_bias[0]), pad_heads(a_log[0])
    dexp = row(jnp.repeat(d_skip[0], SSM_HEAD_DIM))
    g1, b1, g2, b2 = row(ln1_g[0]), row(ln1_b[0]), row(ln2_g[0]), row(ln2_b[0])

    mod_p, mod_s = _ada(c_sample, c_prompt, w_ada[0], row(b_ada[0]))
    mod_p = mod_p.reshape(nb, 1, 6 * D_MODEL)

    x1_p, cst_p, scst_p, sst_p = _mixer_prompt(x_prompt, mod_p, w_in_b, w_dt_b, conv_w[0], conv_nw, ssm_conv_w[0],
                                               sconv_b, dtb, alog, dexp, snw, w_out_b, g1, b1)
    y_p = _ffn(x1_p, mod_p, seq, w_up_b, w_down_b, g2, b2, tile=512)

    xs2 = x_sample.reshape(ns, D_MODEL)
    (yconv_s, ch_s, xbc_s, z_s, xs_s, ydiag_s, decx_s, cm_s, lhs_s, rhs_s) = _sample_pre(
        xs2, mod_s, w_in_b, w_dt_b, conv_w[0], conv_nw, ssm_conv_w[0], sconv_b, dtb, alog,
        state_conv[0, :, 0], state_conv[0, :, 1],
        state_ssm_conv[0, :, 0], state_ssm_conv[0, :, 1], state_ssm_conv[0, :, 2])
    new_state_s, yoff_s = _sample_state(state_ssm[0].reshape(ns, SSM_CH, SSM_STATE), lhs_s, rhs_s, cm_s, decx_s)
    x1_s = _sample_post(xs2, mod_s, yconv_s, ydiag_s, yoff_s, xs_s, z_s, dexp, snw, w_out_b, g1, b1)
    y_s = _ffn_stream(x1_s, mod_s, w_up_b, w_down_b, g2, b2)

    return (y_p.reshape(nb, seq, D_MODEL),
            y_s.reshape(ns, 1, D_MODEL),
            cst_p[None],
            scst_p[None],
            sst_p.reshape(1, nb, SSM_HEADS, SSM_HEAD_DIM, SSM_STATE),
            jnp.stack([state_conv[0, :, 1], ch_s], axis=1)[None],
            jnp.stack([state_ssm_conv[0, :, 1], state_ssm_conv[0, :, 2], xbc_s], axis=1)[None],
            new_state_s.reshape(1, ns, SSM_HEADS, SSM_HEAD_DIM, SSM_STATE))
```

```python
import functools

import jax
import jax.numpy as jnp
from jax import lax
from jax.experimental import pallas as pl
from jax.experimental.pallas import tpu as pltpu

F32 = jnp.float32
BF16 = jnp.bfloat16

D_MODEL = 1024
CONV_CH = 1024
CONV_GROUP = 64
SSM_CH = 1024
SSM_HEADS = 16
SSM_HEAD_DIM = 64
SSM_GROUPS = 2
SSM_GROUP_CH = SSM_CH // SSM_GROUPS
SSM_STATE = 128
SSM_CHUNK = 128
XBC_CH = SSM_CH + 2 * SSM_GROUPS * SSM_STATE
D_FF = 4 * D_MODEL
LANES = 128
SUBLANES = 8
MXU_COLS = 256
COL_GB, COL_GC, COL_HV, COL_Z, COL_XBC = 0, 1024, 2048, 3072, 4096
COL_DT = COL_XBC + XBC_CH
IN_COLS = COL_DT + SSM_HEADS
IN_PAD = COL_DT + LANES
N_PIECES = COL_DT // MXU_COLS + 1
ALPHA = 2.0 ** 0.25
LN_EPS = 1e-5
RMS_EPS = 1e-5
VMEM_LIMIT = 56 * 1024 * 1024


def _dot(a, b):
    return jnp.dot(a, b, preferred_element_type=F32)


def _split(a, terms):
    parts = []
    r = a
    for t in range(terms):
        p = r.astype(BF16)
        parts.append(p)
        if t + 1 < terms:
            r = r - p.astype(F32)
    return parts


def _dot_f32_lhs(a, b_exact, terms=3):
    parts = _split(a, terms)
    out = _dot(parts[0], b_exact)
    for p in parts[1:]:
        out = out + _dot(p, b_exact)
    return out


def _dot_f32_rhs(a_exact, b, terms=3):
    parts = _split(b, terms)
    out = _dot(a_exact, parts[0])
    for p in parts[1:]:
        out = out + _dot(a_exact, p)
    return out


def _head_expand():
    h = lax.broadcasted_iota(jnp.int32, (LANES, SSM_CH), 0)
    c = lax.broadcasted_iota(jnp.int32, (LANES, SSM_CH), 1)
    return (c // SSM_HEAD_DIM == h).astype(BF16)


def _group_reduce():
    c = lax.broadcasted_iota(jnp.int32, (CONV_CH, LANES), 0)
    k = lax.broadcasted_iota(jnp.int32, (CONV_CH, LANES), 1)
    return (c // CONV_GROUP == k).astype(BF16)


def _sigmoid(x):
    return 1.0 / (1.0 + jnp.exp(-x))


def _silu(x):
    return x * _sigmoid(x)


def _softplus(x):
    return jnp.maximum(x, 0.0) + jnp.log1p(jnp.exp(-jnp.abs(x)))


def _layer_norm(r, g, b):
    mu = jnp.mean(r, axis=-1, keepdims=True)
    d = r - mu
    var = jnp.mean(d * d, axis=-1, keepdims=True)
    return d * lax.rsqrt(var + LN_EPS) * g + b


def _conv_group_norm(prod, w, expand, reduce):
    ssum = _dot_f32_lhs(prod * prod, reduce, terms=2)
    rstd = lax.rsqrt(ssum * (1.0 / CONV_GROUP) + RMS_EPS)
    return prod * _dot_f32_lhs(rstd, expand, terms=2) * w


def _ssm_group_norm(y, w):
    outs = []
    for g in range(SSM_GROUPS):
        yg = y[:, g * SSM_GROUP_CH:(g + 1) * SSM_GROUP_CH]
        ms = jnp.mean(yg * yg, axis=-1, keepdims=True)
        outs.append((yg * lax.rsqrt(ms + RMS_EPS) * w[:, g * SSM_GROUP_CH:(g + 1) * SSM_GROUP_CH]).astype(BF16))
    return outs


def _mix_out(y_conv, y_ssm_groups, w_out_ref):
    m = _dot(y_conv.astype(BF16), w_out_ref[0:CONV_CH, :])
    for g, yg in enumerate(y_ssm_groups):
        lo = CONV_CH + g * SSM_GROUP_CH
        m = m + _dot(yg, w_out_ref[lo:lo + SSM_GROUP_CH, :])
    return m


def _ada_kernel(c_ref, w_ref, b_ref, op_ref, os_ref):
    c = c_ref[...]
    w = w_ref[...]
    c_hi = c.astype(BF16)
    c_lo = (c - c_hi.astype(F32)).astype(BF16)
    w_hi = w.astype(BF16)
    mod = _dot(c_hi, w_hi) + _dot(c_lo, w_hi) + b_ref[...]
    n_sample = os_ref.shape[0]
    os_ref[...] = mod[0:n_sample, :]
    op_ref[...] = mod[n_sample:, :]


def _ada(c_sample, c_prompt, w_ada, b_ada, tile_n=1024):
    ns, nb = c_sample.shape[0], c_prompt.shape[0]
    n = w_ada.shape[1]
    return pl.pallas_call(
        _ada_kernel,
        grid=(n // tile_n,),
        in_specs=[pl.BlockSpec((ns + nb, D_MODEL), lambda i: (0, 0)),
                  pl.BlockSpec((D_MODEL, tile_n), lambda i: (0, i)),
                  pl.BlockSpec((1, tile_n), lambda i: (0, i))],
        out_specs=[pl.BlockSpec((nb, tile_n), lambda i: (0, i)),
                   pl.BlockSpec((ns, tile_n), lambda i: (0, i))],
        out_shape=[jax.ShapeDtypeStruct((nb, n), F32), jax.ShapeDtypeStruct((ns, n), F32)],
        name="ada_mod",
    )(jnp.concatenate([c_sample, c_prompt], axis=0), w_ada, b_ada)


def _cast_transposed_kernel(wt_ref, o_ref):
    rows = wt_ref.shape[0]
    wt = wt_ref[...]
    if rows < o_ref.shape[1]:
        wt = jnp.concatenate([wt, jnp.zeros((o_ref.shape[1] - rows, wt.shape[1]), wt.dtype)], axis=0)
    o_ref[...] = wt.T.astype(o_ref.dtype)


def _cast_in_proj(w):
    wt = w.T
    n_dt = w.shape[1] - COL_DT
    cols = COL_DT // 4
    assert cols % LANES == 0
    main = pl.pallas_call(
        _cast_transposed_kernel,
        grid=(COL_DT // cols,),
        in_specs=[pl.BlockSpec((cols, D_MODEL), lambda j: (j, 0))],
        out_specs=pl.BlockSpec((D_MODEL, cols), lambda j: (0, j)),
        out_shape=jax.ShapeDtypeStruct((D_MODEL, COL_DT), BF16),
        compiler_params=pltpu.CompilerParams(vmem_limit_bytes=VMEM_LIMIT),
        name="cast_in_proj",
    )(wt)
    dt = pl.pallas_call(
        _cast_transposed_kernel,
        grid=(1,),
        in_specs=[pl.BlockSpec((n_dt, D_MODEL), lambda j: (COL_DT // n_dt, 0))],
        out_specs=pl.BlockSpec((D_MODEL, LANES), lambda j: (0, 0)),
        out_shape=jax.ShapeDtypeStruct((D_MODEL, LANES), BF16),
        name="cast_in_proj_dt",
    )(wt)
    return main, dt


def _mixer_prompt_kernel(xa_ref, moda_ref, modb_ref, w_in_ref, w_dt_ref, expand_ref, reduce_ref,
                         conv_w_ref, conv_nw_ref, sconv_w_ref, sconv_b_ref,
                         dtb_ref, alog_ref, dexp_ref, snw_ref, w_out_ref, ln_g_ref, ln_b_ref,
                         x1_ref, cst_ref, scst_ref, sst_ref,
                         p0, p1, xk0, xk1, cbuf, xbuf, st_ref, xs_ref, bc_ref, dtx_ref, acsx_ref, endx_ref,
                         acst_ref, cb_ref, bmt_ref, y_ref, yc_ref,
                         *, tile, tiles_per_seq, n_tiles, sched):
    s = pl.program_id(0)
    jb = lax.rem(s + (tiles_per_seq - 1), tiles_per_seq)

    @pl.when(jb == 0)
    def _():
        cbuf[...] = jnp.zeros_like(cbuf)
        xbuf[...] = jnp.zeros_like(xbuf)
        st_ref[...] = jnp.zeros_like(st_ref)

    def stages(pa, xka, pb, xkb):
        if pa is not None:
            xa = xa_ref[...]
            xka[...] = xa
            u = (xa * (1.0 + moda_ref[:, D_MODEL:2 * D_MODEL]) + moda_ref[:, 0:D_MODEL]).astype(BF16)
            pieces = iter(list(range(0, COL_DT, MXU_COLS)) + [COL_DT])
        else:
            pieces = iter(())

        def first_stage(n):
            for _ in range(n):
                lo = next(pieces, None)
                if lo == COL_DT:
                    pa[:, COL_DT:IN_PAD] = _dot(u, w_dt_ref[...])
                elif lo is not None:
                    pa[:, lo:lo + MXU_COLS] = _dot(u, w_in_ref[:, lo:lo + MXU_COLS])

        if pb is None:
            first_stage(N_PIECES)
            return

        expand = expand_ref[...]
        x = xkb[...]
        g1 = modb_ref[:, 2 * D_MODEL:3 * D_MODEL]

        def proj(lo, width):
            return pb[:, lo:lo + width]

        def delayed(tail_ref, cs, cur, taps):
            seq = jnp.concatenate([tail_ref[:, cs], cur], axis=0)
            tail_ref[:, cs] = cur[tile - SUBLANES:, :]
            return [pltpu.roll(seq, k, axis=0)[SUBLANES:, :] for k in range(1, taps + 1)]

        for k in range(CONV_CH // MXU_COLS):
            first_stage(sched[0])
            c0 = k * MXU_COLS
            cs = slice(c0, c0 + MXU_COLS)
            ch = proj(COL_GC + c0, MXU_COLS) * proj(COL_HV + c0, MXU_COLS)
            ch1, ch2 = delayed(cbuf, cs, ch, 2)
            cv = conv_w_ref[0:1, cs] * ch2 + conv_w_ref[1:2, cs] * ch1 + conv_w_ref[2:3, cs] * ch
            prod = proj(COL_GB + c0, MXU_COLS) * cv
            ssum = _dot_f32_lhs(prod * prod, reduce_ref[cs, :], terms=1)
            rstd = lax.rsqrt(ssum * (1.0 / CONV_GROUP) + RMS_EPS)
            yc_ref[:, cs] = (prod * _dot_f32_lhs(rstd, expand_ref[:, cs], terms=2)
                             * conv_nw_ref[:, cs]).astype(BF16)

        def pre_conv(c0):
            cs = slice(c0, c0 + MXU_COLS)
            xbc = proj(COL_XBC + c0, MXU_COLS)
            x1, x2, x3 = delayed(xbuf, cs, xbc, 3)
            return _silu(sconv_w_ref[0:1, cs] * x3 + sconv_w_ref[1:2, cs] * x2 + sconv_w_ref[2:3, cs] * x1
                         + sconv_w_ref[3:4, cs] * xbc + sconv_b_ref[:, cs])

        row = lax.broadcasted_iota(jnp.int32, (SSM_CHUNK, SSM_CHUNK), 0)
        col = lax.broadcasted_iota(jnp.int32, (SSM_CHUNK, SSM_CHUNK), 1)
        causal = row >= col
        tri = causal.astype(BF16)
        groups = SSM_CHUNK // SUBLANES
        causal_bias = jnp.where(causal, 0.0, -jnp.inf).reshape(groups, SUBLANES, SSM_CHUNK)
        first_half = (col < SSM_HEAD_DIM).reshape(groups, SUBLANES, SSM_CHUNK)
        half_rows = col < SSM_HEAD_DIM
        chunks = [slice(c * SSM_CHUNK, (c + 1) * SSM_CHUNK) for c in range(tile // SSM_CHUNK)]

        first_stage(sched[1])
        dt = _softplus(proj(COL_DT, LANES) + dtb_ref[...])
        dta = dt * (-jnp.exp(alog_ref[...]))
        dtx_ref[...] = _dot_f32_lhs(dt, expand, terms=1)
        for c, rows in enumerate(chunks):
            acs = _dot_f32_rhs(tri, dta[rows, :])
            acs_t = acs.T
            for h in range(SSM_HEADS):
                r8 = (c * SSM_HEADS + h) * SUBLANES
                acst_ref[r8:r8 + SUBLANES, :] = jnp.broadcast_to(acs_t[h:h + 1, :], (SUBLANES, SSM_CHUNK))
            acs_x = _dot_f32_lhs(acs, expand, terms=2)
            acsx_ref[rows, :] = acs_x
            endx_ref[c * SUBLANES:(c + 1) * SUBLANES, :] = jnp.broadcast_to(acs_x[SSM_CHUNK - 1:SSM_CHUNK, :],
                                                                             (SUBLANES, SSM_CH))
        for c0 in range(SSM_CH, XBC_CH, MXU_COLS):
            first_stage(sched[2])
            bc_ref[:, c0 - SSM_CH:c0 - SSM_CH + MXU_COLS] = pre_conv(c0)
        for rows in chunks:
            for g in range(SSM_GROUPS):
                gs = slice(g * SSM_STATE, (g + 1) * SSM_STATE)
                bm = bc_ref[rows, gs]
                cm = bc_ref[rows, (SSM_GROUPS + g) * SSM_STATE:(SSM_GROUPS + g + 1) * SSM_STATE]
                cb_ref[rows, gs] = lax.dot_general(cm.astype(BF16), bm.astype(BF16), (((1,), (1,)), ((), ())),
                                                   preferred_element_type=F32)
                bmt_ref[rows, gs] = bm.T.astype(BF16)

        for c0 in range(0, SSM_CH, MXU_COLS):
            first_stage(sched[3])
            cs = slice(c0, c0 + MXU_COLS)
            g = c0 // SSM_GROUP_CH
            gs = slice(g * SSM_STATE, (g + 1) * SSM_STATE)
            xs = pre_conv(c0)
            xs_ref[:, cs] = xs
            xdt = xs * dtx_ref[:, cs]
            for c, rows in enumerate(chunks):
                first_stage(sched[4])
                acs_x = acsx_ref[rows, cs].reshape(groups, SUBLANES, MXU_COLS)
                end_x = endx_ref[c * SUBLANES:(c + 1) * SUBLANES, cs]
                xdt_c = xdt[rows, :]
                xdec = (xdt_c * jnp.exp(end_x[None] - acs_x).reshape(SSM_CHUNK, MXU_COLS)).astype(BF16)
                cm = bc_ref[rows, (SSM_GROUPS + g) * SSM_STATE:(SSM_GROUPS + g + 1) * SSM_STATE].astype(BF16)
                cb = cb_ref[rows, gs]
                st = st_ref[:, cs]
                y_off = _dot(cm, st.astype(BF16)) * jnp.exp(acs_x).reshape(SSM_CHUNK, MXU_COLS)
                st_ref[:, cs] = ((st.reshape(groups, SUBLANES, MXU_COLS) * jnp.exp(end_x)[None])
                                 .reshape(SSM_STATE, MXU_COLS) + _dot(bmt_ref[rows, gs], xdec))
                for lo in range(0, MXU_COLS, LANES):
                    h0 = (c * SSM_HEADS + (c0 + lo) // SSM_HEAD_DIM) * SUBLANES
                    slab = acs_x[:, :, lo:lo + LANES]
                    rolled = pltpu.roll(slab, SSM_HEAD_DIM, axis=2)
                    a0 = jnp.where(first_half, slab, rolled) - acst_ref[h0:h0 + SUBLANES, :][None]
                    a1 = jnp.where(first_half, rolled, slab) - acst_ref[h0 + SUBLANES:h0 + 2 * SUBLANES, :][None]
                    l0 = jnp.exp(a0 + causal_bias).reshape(SSM_CHUNK, SSM_CHUNK)
                    l1 = jnp.exp(a1 + causal_bias).reshape(SSM_CHUNK, SSM_CHUNK)
                    m = jnp.concatenate([(cb * l0).astype(BF16), (cb * l1).astype(BF16)], axis=1)
                    xp = xdt_c[:, lo:lo + LANES]
                    rhs = jnp.concatenate([jnp.where(half_rows, xp, 0.0), jnp.where(half_rows, 0.0, xp)],
                                          axis=0).astype(BF16)
                    y_ref[rows, c0 + lo:c0 + lo + LANES] = _dot(m, rhs) + y_off[:, lo:lo + LANES]

        for k in range(SSM_CH // MXU_COLS):
            first_stage(sched[5])
            c0 = k * MXU_COLS
            cs = slice(c0, c0 + MXU_COLS)
            y_ref[:, cs] = (y_ref[:, cs] + xs_ref[:, cs] * dexp_ref[:, cs]) * _silu(proj(COL_Z + c0, MXU_COLS))
        first_stage(sched[6])
        m = _mix_out(yc_ref[...], _ssm_group_norm(y_ref[...], snw_ref[...]), w_out_ref)
        x1_ref[...] = _layer_norm(ALPHA * x + (1.0 + g1) * m, ln_g_ref[...], ln_b_ref[...])
        first_stage(N_PIECES)

    both = (s > 0) & (s < n_tiles)

    @pl.when(s == 0)
    def _():
        stages(p0, xk0, None, None)

    @pl.when(both & (lax.rem(s, 2) == 0))
    def _():
        stages(p0, xk0, p1, xk1)

    @pl.when(both & (lax.rem(s, 2) == 1))
    def _():
        stages(p1, xk1, p0, xk0)

    @pl.when(s == n_tiles)
    def _():
        stages(None, None, *((p0, xk0) if (n_tiles - 1) % 2 == 0 else (p1, xk1)))

    @pl.when((jb == tiles_per_seq - 1) & (s > 0))
    def _():
        cst_ref[...] = cbuf[SUBLANES - 2:SUBLANES, :]
        scst_ref[...] = xbuf[SUBLANES - 3:SUBLANES, :]
        sst_ref[...] = st_ref[...].T


def _const_spec(shape):
    return pl.BlockSpec(shape, lambda *_: (0,) * len(shape), pipeline_mode=pl.Buffered(1))


def _mixer_prompt(x, mod, w_in, w_dt, conv_w, conv_nw, sconv_w, sconv_b, dtb, alog, dexp, snw, w_out, ln_g, ln_b,
                  tile=256, sched=(1, 0, 0, 2, 1, 0, 0)):
    assert CONV_GROUP == SSM_HEAD_DIM and CONV_CH == SSM_CH
    nb, seq, _ = x.shape
    tiles_per_seq = seq // tile
    n_tiles = nb * tiles_per_seq
    kern = functools.partial(_mixer_prompt_kernel, tile=tile, tiles_per_seq=tiles_per_seq, n_tiles=n_tiles,
                             sched=sched)
    consts = [w_in, w_dt, _head_expand(), _group_reduce(), conv_w, conv_nw, sconv_w, sconv_b, dtb, alog, dexp, snw,
              w_out, ln_g, ln_b]
    first = lambda s: jnp.minimum(s, n_tiles - 1)
    second = lambda s: jnp.maximum(s - 1, 0)
    return pl.pallas_call(
        kern,
        grid=(n_tiles + 1,),
        in_specs=[pl.BlockSpec((tile, D_MODEL), lambda s: (first(s), 0)),
                  pl.BlockSpec((None, 1, 6 * D_MODEL), lambda s: (first(s) // tiles_per_seq, 0, 0)),
                  pl.BlockSpec((None, 1, 6 * D_MODEL), lambda s: (second(s) // tiles_per_seq, 0, 0))]
                 + [_const_spec(a.shape) for a in consts],
        out_specs=[pl.BlockSpec((tile, D_MODEL), lambda s: (second(s), 0)),
                   pl.BlockSpec((None, 2, CONV_CH), lambda s: (second(s) // tiles_per_seq, 0, 0)),
                   pl.BlockSpec((None, 3, XBC_CH), lambda s: (second(s) // tiles_per_seq, 0, 0)),
                   pl.BlockSpec((None, SSM_CH, SSM_STATE), lambda s: (second(s) // tiles_per_seq, 0, 0))],
        out_shape=[jax.ShapeDtypeStruct((nb * seq, D_MODEL), F32),
                   jax.ShapeDtypeStruct((nb, 2, CONV_CH), F32),
                   jax.ShapeDtypeStruct((nb, 3, XBC_CH), F32),
                   jax.ShapeDtypeStruct((nb, SSM_CH, SSM_STATE), F32)],
        scratch_shapes=[pltpu.VMEM((tile, IN_PAD), F32),
                        pltpu.VMEM((tile, IN_PAD), F32),
                        pltpu.VMEM((tile, D_MODEL), F32),
                        pltpu.VMEM((tile, D_MODEL), F32),
                        pltpu.VMEM((SUBLANES, CONV_CH), F32),
                        pltpu.VMEM((SUBLANES, XBC_CH), F32),
                        pltpu.VMEM((SSM_STATE, SSM_CH), F32),
                        pltpu.VMEM((tile, SSM_CH), F32),
                        pltpu.VMEM((tile, 2 * SSM_GROUPS * SSM_STATE), F32),
                        pltpu.VMEM((tile, SSM_CH), F32),
                        pltpu.VMEM((tile, SSM_CH), F32),
                        pltpu.VMEM((tile // SSM_CHUNK * SUBLANES, SSM_CH), F32),
                        pltpu.VMEM((tile // SSM_CHUNK * SSM_HEADS * SUBLANES, SSM_CHUNK), F32),
                        pltpu.VMEM((tile, SSM_GROUPS * SSM_STATE), F32),
                        pltpu.VMEM((tile, SSM_GROUPS * SSM_STATE), BF16),
                        pltpu.VMEM((tile, SSM_CH), F32),
                        pltpu.VMEM((tile, CONV_CH), BF16)],
        compiler_params=pltpu.CompilerParams(dimension_semantics=("arbitrary",),
                                             vmem_limit_bytes=VMEM_LIMIT),
        name="mixer_prompt",
    )(x.reshape(nb * seq, D_MODEL), mod, mod, *consts)


def _ffn_kernel(x_ref, mod_ref, w_up_ref, w_down_ref, ln_g_ref, ln_b_ref, o_ref, r_ref, *, ff_tile):
    s = pl.program_id(0)
    n_tiles = pl.num_programs(0) - 1

    @pl.when(s == 0)
    def _():
        r_ref[...] = jnp.zeros_like(r_ref)

    def norm_previous():
        o_ref[...] = _layer_norm(r_ref[...], ln_g_ref[...], ln_b_ref[...])

    @pl.when(s < n_tiles)
    def _():
        norm_previous()
        x = x_ref[...]
        sh2 = mod_ref[:, 3 * D_MODEL:4 * D_MODEL]
        sc2 = mod_ref[:, 4 * D_MODEL:5 * D_MODEL]
        g2 = mod_ref[:, 5 * D_MODEL:6 * D_MODEL]
        v = (x * (1.0 + sc2) + sh2).astype(BF16)
        acc = jnp.zeros(x.shape, F32)
        for k in range(D_FF // ff_tile):
            h = jnp.maximum(_dot(v, w_up_ref[:, k * ff_tile:(k + 1) * ff_tile]), 0.0)
            acc = acc + _dot((h * h).astype(BF16), w_down_ref[k * ff_tile:(k + 1) * ff_tile, :])
        r_ref[...] = ALPHA * x + (1.0 + g2) * acc

    @pl.when(s == n_tiles)
    def _():
        norm_previous()


def _ffn(x, mod, rows_per_mod, w_up, w_down, ln_g, ln_b, tile, ff_tile=1024):
    rows = x.shape[0]
    mod_rows = mod.shape[1]
    tiles_per_mod = rows_per_mod // tile
    n_tiles = rows // tile
    kern = functools.partial(_ffn_kernel, ff_tile=ff_tile)
    first = lambda s: jnp.minimum(s, n_tiles - 1)
    second = lambda s: jnp.maximum(s - 1, 0)
    return pl.pallas_call(
        kern,
        grid=(n_tiles + 1,),
        in_specs=[pl.BlockSpec((tile, D_MODEL), lambda s: (first(s), 0)),
                  pl.BlockSpec((None, mod_rows, 6 * D_MODEL), lambda s: (first(s) // tiles_per_mod, 0, 0)),
                  _const_spec(w_up.shape), _const_spec(w_down.shape),
                  _const_spec(ln_g.shape), _const_spec(ln_b.shape)],
        out_specs=pl.BlockSpec((tile, D_MODEL), lambda s: (second(s), 0)),
        out_shape=jax.ShapeDtypeStruct((rows, D_MODEL), F32),
        scratch_shapes=[pltpu.VMEM((tile, D_MODEL), F32)],
        compiler_params=pltpu.CompilerParams(dimension_semantics=("arbitrary",),
                                             vmem_limit_bytes=VMEM_LIMIT),
        name="ffn",
    )(x, mod, w_up, w_down, ln_g, ln_b)


def _ffn_stream_kernel(x_ref, mod_ref, w_up_ref, w_down_ref, ln_g_ref, ln_b_ref, o_ref, acc_ref):
    k = pl.program_id(0)

    @pl.when(k == 0)
    def _():
        acc_ref[...] = jnp.zeros_like(acc_ref)

    x = x_ref[...]
    sh2 = mod_ref[:, 3 * D_MODEL:4 * D_MODEL]
    sc2 = mod_ref[:, 4 * D_MODEL:5 * D_MODEL]
    v = (x * (1.0 + sc2) + sh2).astype(BF16)
    h = jnp.maximum(_dot(v, w_up_ref[...]), 0.0)
    acc_ref[...] += _dot((h * h).astype(BF16), w_down_ref[...])

    @pl.when(k == pl.num_programs(0) - 1)
    def _():
        g2 = mod_ref[:, 5 * D_MODEL:6 * D_MODEL]
        o_ref[...] = _layer_norm(ALPHA * x + (1.0 + g2) * acc_ref[...], ln_g_ref[...], ln_b_ref[...])


def _ffn_stream(x, mod, w_up, w_down, ln_g, ln_b, ff_tile=1024):
    rows = x.shape[0]
    return pl.pallas_call(
        _ffn_stream_kernel,
        grid=(D_FF // ff_tile,),
        in_specs=[_const_spec(x.shape), _const_spec(mod.shape),
                  pl.BlockSpec((D_MODEL, ff_tile), lambda k: (0, k)),
                  pl.BlockSpec((ff_tile, D_MODEL), lambda k: (k, 0)),
                  _const_spec(ln_g.shape), _const_spec(ln_b.shape)],
        out_specs=pl.BlockSpec((rows, D_MODEL), lambda k: (0, 0)),
        out_shape=jax.ShapeDtypeStruct((rows, D_MODEL), F32),
        scratch_shapes=[pltpu.VMEM((rows, D_MODEL), F32)],
        compiler_params=pltpu.CompilerParams(dimension_semantics=("arbitrary",),
                                             vmem_limit_bytes=VMEM_LIMIT),
        name="ffn_stream",
    )(x, mod, w_up, w_down, ln_g, ln_b)


_PRODUCT_TERMS = ((0, 0), (0, 1), (1, 0), (0, 2), (2, 0), (1, 1))
UPDATE_TERMS = 16


def _sample_pre_kernel(x_ref, mod_ref, w_in_ref, w_dt_ref, conv_w_ref, conv_nw_ref, sconv_w_ref, sconv_b_ref,
                       dtb_ref, alog_ref, cb0_ref, cb1_ref, sb0_ref, sb1_ref, sb2_ref,
                       yconv_ref, ch_ref, xbc_ref, z_ref, xs_ref, ydiag_ref, decx_ref, cm_ref, lhs_ref, rhs_ref):
    expand = _head_expand()
    reduce = _group_reduce()
    x = x_ref[...]
    sh1 = mod_ref[:, 0:D_MODEL]
    sc1 = mod_ref[:, D_MODEL:2 * D_MODEL]
    u = (x * (1.0 + sc1) + sh1).astype(BF16)

    def proj(lo, width):
        return _dot(u, w_in_ref[:, lo:lo + width])

    ch = proj(COL_GC, CONV_CH) * proj(COL_HV, CONV_CH)
    ch_ref[...] = ch
    cw = conv_w_ref[...]
    cv = cw[0:1, :] * cb0_ref[...] + cw[1:2, :] * cb1_ref[...] + cw[2:3, :] * ch
    yconv_ref[...] = _conv_group_norm(proj(COL_GB, CONV_CH) * cv, conv_nw_ref[...], expand, reduce)

    xbc = proj(COL_XBC, XBC_CH)
    xbc_ref[...] = xbc
    sw = sconv_w_ref[...]
    xc = _silu(sw[0:1, :] * sb0_ref[...] + sw[1:2, :] * sb1_ref[...] + sw[2:3, :] * sb2_ref[...]
               + sw[3:4, :] * xbc + sconv_b_ref[...])
    xs = xc[:, 0:SSM_CH]
    xs_ref[...] = xs
    cm_ref[...] = xc[:, SSM_CH + SSM_GROUPS * SSM_STATE:XBC_CH]
    z_ref[...] = proj(COL_Z, SSM_CH)

    dt = _softplus(_dot(u, w_dt_ref[...]) + dtb_ref[...])
    dta = dt * (-jnp.exp(alog_ref[...]))
    xdt = xs * _dot_f32_lhs(dt, expand)
    decx = jnp.exp(_dot_f32_lhs(dta, expand))
    decx_ref[...] = decx
    xdt_t, dec_t = _split(xdt, 3), _split(decx, 3)
    bm_t = _split(xc[:, SSM_CH:SSM_CH + SSM_GROUPS * SSM_STATE], 3)
    group_of = lax.broadcasted_iota(jnp.int32, xdt.shape, 1) // SSM_GROUP_CH
    zeros = jnp.zeros((x.shape[0], SSM_STATE), BF16)
    r = 0
    for g in range(SSM_GROUPS):
        for tx, tb in _PRODUCT_TERMS:
            lhs_ref[r] = jnp.where(group_of == g, xdt_t[tx], jnp.zeros_like(xdt_t[tx]))
            rhs_ref[r] = jnp.concatenate([bm_t[tb][:, g * SSM_STATE:(g + 1) * SSM_STATE], zeros], axis=1)
            r += 1
    for t in range(3):
        lhs_ref[r] = dec_t[t]
        rhs_ref[r] = jnp.concatenate([zeros, jnp.ones_like(zeros)], axis=1)
        r += 1
    for r in range(r, UPDATE_TERMS):
        lhs_ref[r] = jnp.zeros_like(dec_t[0])
        rhs_ref[r] = jnp.concatenate([zeros, zeros], axis=1)
    for g in range(SSM_GROUPS):
        bm = xc[:, SSM_CH + g * SSM_STATE:SSM_CH + (g + 1) * SSM_STATE]
        cm = xc[:, SSM_CH + (SSM_GROUPS + g) * SSM_STATE:SSM_CH + (SSM_GROUPS + g + 1) * SSM_STATE]
        cb = jnp.sum(cm * bm, axis=-1, keepdims=True)
        gl = g * SSM_GROUP_CH
        ydiag_ref[:, gl:gl + SSM_GROUP_CH] = cb * xdt[:, gl:gl + SSM_GROUP_CH]


def _sample_pre(x, mod, w_in, w_dt, conv_w, conv_nw, sconv_w, sconv_b, dtb, alog, cb0, cb1, sb0, sb1, sb2):
    n = x.shape[0]
    args = (x, mod, w_in, w_dt, conv_w, conv_nw, sconv_w, sconv_b, dtb, alog, cb0, cb1, sb0, sb1, sb2)
    f32_shapes = [(n, CONV_CH), (n, CONV_CH), (n, XBC_CH), (n, SSM_CH), (n, SSM_CH), (n, SSM_CH), (n, SSM_CH),
                  (n, SSM_GROUPS * SSM_STATE)]
    bf16_shapes = [(UPDATE_TERMS, n, SSM_CH), (UPDATE_TERMS, n, 2 * SSM_STATE)]
    return pl.pallas_call(
        _sample_pre_kernel,
        out_shape=[jax.ShapeDtypeStruct(s, F32) for s in f32_shapes]
                  + [jax.ShapeDtypeStruct(s, BF16) for s in bf16_shapes],
        compiler_params=pltpu.CompilerParams(vmem_limit_bytes=VMEM_LIMIT),
        name="sample_pre",
    )(*args)


def _sample_state_kernel(s_ref, lhs_ref, rhs_ref, cm_ref, decx_ref, o_ref, yoff_ref, *, block):
    i = pl.program_id(0)
    rows = UPDATE_TERMS * block
    lhs_t = lhs_ref[...].reshape(rows, SSM_CH).astype(F32).T.astype(BF16)
    rhs_all = rhs_ref[...].reshape(rows, 2 * SSM_STATE)
    token_of = lax.broadcasted_iota(jnp.int32, rhs_all.shape, 0) % block

    def body(k, carry):
        b = i * block + k
        s = s_ref[k]
        upd = _dot(lhs_t, jnp.where(token_of == k, rhs_all, jnp.zeros_like(rhs_all)))
        o_ref[k] = s * upd[:, SSM_STATE:2 * SSM_STATE] + upd[:, 0:SSM_STATE]
        cm = cm_ref[pl.ds(b, 1), :]
        sums = []
        for c0 in range(0, SSM_CH, LANES):
            g = c0 // SSM_GROUP_CH
            prod = s[c0:c0 + LANES, :] * cm[:, g * SSM_STATE:(g + 1) * SSM_STATE]
            sums.append(jnp.sum(prod.T, axis=0, keepdims=True))
        yoff_ref[pl.ds(b, 1), :] = jnp.concatenate(sums, axis=1) * decx_ref[pl.ds(b, 1), :]
        return carry

    lax.fori_loop(0, block, body, 0, unroll=True)


def _sample_state(state, lhs, rhs, cm, decx, block=16):
    n = state.shape[0]
    kern = functools.partial(_sample_state_kernel, block=block)
    return pl.pallas_call(
        kern,
        grid=(n // block,),
        in_specs=[pl.BlockSpec((block, SSM_CH, SSM_STATE), lambda i: (i, 0, 0)),
                  pl.BlockSpec((UPDATE_TERMS, block, SSM_CH), lambda i: (0, i, 0)),
                  pl.BlockSpec((UPDATE_TERMS, block, 2 * SSM_STATE), lambda i: (0, i, 0)),
                  _const_spec(cm.shape), _const_spec(decx.shape)],
        out_specs=[pl.BlockSpec((block, SSM_CH, SSM_STATE), lambda i: (i, 0, 0)),
                   pl.BlockSpec((n, SSM_CH), lambda i: (0, 0))],
        out_shape=[jax.ShapeDtypeStruct(state.shape, F32), jax.ShapeDtypeStruct((n, SSM_CH), F32)],
        compiler_params=pltpu.CompilerParams(dimension_semantics=("arbitrary",),
                                             vmem_limit_bytes=VMEM_LIMIT),
        name="sample_state",
    )(state, lhs, rhs, cm, decx)


def _sample_post_kernel(x_ref, mod_ref, yconv_ref, ydiag_ref, yoff_ref, xs_ref, z_ref, dexp_ref, snw_ref,
                        w_out_ref, ln_g_ref, ln_b_ref, x1_ref):
    g1 = mod_ref[:, 2 * D_MODEL:3 * D_MODEL]
    y = ydiag_ref[...] + yoff_ref[...] + xs_ref[...] * dexp_ref[...]
    y = y * _silu(z_ref[...])
    m = _mix_out(yconv_ref[...], _ssm_group_norm(y, snw_ref[...]), w_out_ref)
    x1_ref[...] = _layer_norm(ALPHA * x_ref[...] + (1.0 + g1) * m, ln_g_ref[...], ln_b_ref[...])


def _sample_post(x, mod, yconv, ydiag, yoff, xs, z, dexp, snw, w_out, ln_g, ln_b):
    return pl.pallas_call(
        _sample_post_kernel,
        out_shape=jax.ShapeDtypeStruct(x.shape, F32),
        compiler_params=pltpu.CompilerParams(vmem_limit_bytes=VMEM_LIMIT),
        name="sample_post",
    )(x, mod, yconv, ydiag, yoff, xs, z, dexp, snw, w_out, ln_g, ln_b)


def kernel(x_prompt, x_sample, state_conv, state_ssm_conv, state_ssm, c_prompt, c_sample, w_ada, b_ada, w_in, conv_w, conv_norm_w, ssm_conv_w, ssm_conv_b, dt_bias, a_log, d_skip, ssm_norm_w, w_out, ln1_g, ln1_b, w_up, w_down, ln2_g, ln2_b):
    assert w_ada.shape[0] == 1, "single-layer trunk"
    nb, seq, _ = x_prompt.shape
    ns = x_sample.shape[0]
    row = lambda a: a.reshape(1, -1)
    pad_heads = lambda a: jnp.pad(a.reshape(1, -1), ((0, 0), (0, LANES - SSM_HEADS)))

    w_in_b, w_dt_b = _cast_in_proj(w_in[0])
    w_out_b = w_out[0].astype(BF16)
    w_up_b = w_up[0].astype(BF16)
    w_down_b = w_down[0].astype(BF16)
    conv_nw, sconv_b, snw = row(conv_norm_w[0]), row(ssm_conv_b[0]), row(ssm_norm_w[0])
    dtb, alog = pad_heads(dt_bias[0]), pad_heads(a_log[0])
    dexp = row(jnp.repeat(d_skip[0], SSM_HEAD_DIM))
    g1, b1, g2, b2 = row(ln1_g[0]), row(ln1_b[0]), row(ln2_g[0]), row(ln2_b[0])

    mod_p, mod_s = _ada(c_sample, c_prompt, w_ada[0], row(b_ada[0]))
    mod_p = mod_p.reshape(nb, 1, 6 * D_MODEL)

    x1_p, cst_p, scst_p, sst_p = _mixer_prompt(x_prompt, mod_p, w_in_b, w_dt_b, conv_w[0], conv_nw, ssm_conv_w[0],
                                               sconv_b, dtb, alog, dexp, snw, w_out_b, g1, b1)
    y_p = _ffn(x1_p, mod_p, seq, w_up_b, w_down_b, g2, b2, tile=512)

    xs2 = x_sample.reshape(ns, D_MODEL)
    (yconv_s, ch_s, xbc_s, z_s, xs_s, ydiag_s, decx_s, cm_s, lhs_s, rhs_s) = _sample_pre(
        xs2, mod_s, w_in_b, w_dt_b, conv_w[0], conv_nw, ssm_conv_w[0], sconv_b, dtb, alog,
        state_conv[0, :, 0], state_conv[0, :, 1],
        state_ssm_conv[0, :, 0], state_ssm_conv[0, :, 1], state_ssm_conv[0, :, 2])
    new_state_s, yoff_s = _sample_state(state_ssm[0].reshape(ns, SSM_CH, SSM_STATE), lhs_s, rhs_s, cm_s, decx_s)
    x1_s = _sample_post(xs2, mod_s, yconv_s, ydiag_s, yoff_s, xs_s, z_s, dexp, snw, w_out_b, g1, b1)
    y_s = _ffn_stream(x1_s, mod_s, w_up_b, w_down_b, g2, b2)

    return (y_p.reshape(nb, seq, D_MODEL),
            y_s.reshape(ns, 1, D_MODEL),
            cst_p[None],
            scst_p[None],
            sst_p.reshape(1, nb, SSM_HEADS, SSM_HEAD_DIM, SSM_STATE),
            jnp.stack([state_conv[0, :, 1], ch_s], axis=1)[None],
            jnp.stack([state_ssm_conv[0, :, 1], state_ssm_conv[0, :, 2], xbc_s], axis=1)[None],
            new_state_s.reshape(1, ns, SSM_HEADS, SSM_HEAD_DIM, SSM_STATE))
```

```python
import functools

import jax
import jax.numpy as jnp
from jax import lax
from jax.experimental import pallas as pl
from jax.experimental.pallas import tpu as pltpu

F32 = jnp.float32
BF16 = jnp.bfloat16

D_MODEL = 1024
CONV_CH = 1024
CONV_GROUP = 64
SSM_CH = 1024
SSM_HEADS = 16
SSM_HEAD_DIM = 64
SSM_GROUPS = 2
SSM_GROUP_CH = SSM_CH // SSM_GROUPS
SSM_STATE = 128
SSM_CHUNK = 128
XBC_CH = SSM_CH + 2 * SSM_GROUPS * SSM_STATE
D_FF = 4 * D_MODEL
LANES = 128
SUBLANES = 8
MXU_COLS = 256
COL_GB, COL_GC, COL_HV, COL_Z, COL_XBC = 0, 1024, 2048, 3072, 4096
COL_DT = COL_XBC + XBC_CH
IN_COLS = COL_DT + SSM_HEADS
IN_PAD = COL_DT + LANES
N_PIECES = COL_DT // MXU_COLS + 1
ALPHA = 2.0 ** 0.25
LN_EPS = 1e-5
RMS_EPS = 1e-5
VMEM_LIMIT = 56 * 1024 * 1024


def _dot(a, b):
    return jnp.dot(a, b, preferred_element_type=F32)


def _split(a, terms):
    parts = []
    r = a
    for t in range(terms):
        p = r.astype(BF16)
        parts.append(p)
        if t + 1 < terms:
            r = r - p.astype(F32)
    return parts


def _dot_f32_lhs(a, b_exact, terms=3):
    parts = _split(a, terms)
    out = _dot(parts[0], b_exact)
    for p in parts[1:]:
        out = out + _dot(p, b_exact)
    return out


def _dot_f32_rhs(a_exact, b, terms=3):
    parts = _split(b, terms)
    out = _dot(a_exact, parts[0])
    for p in parts[1:]:
        out = out + _dot(a_exact, p)
    return out


def _head_expand():
    h = lax.broadcasted_iota(jnp.int32, (LANES, SSM_CH), 0)
    c = lax.broadcasted_iota(jnp.int32, (LANES, SSM_CH), 1)
    return (c // SSM_HEAD_DIM == h).astype(BF16)


def _group_reduce():
    c = lax.broadcasted_iota(jnp.int32, (CONV_CH, LANES), 0)
    k = lax.broadcasted_iota(jnp.int32, (CONV_CH, LANES), 1)
    return (c // CONV_GROUP == k).astype(BF16)


def _sigmoid(x):
    return 1.0 / (1.0 + jnp.exp(-x))


def _silu(x):
    return x * _sigmoid(x)


def _softplus(x):
    return jnp.maximum(x, 0.0) + jnp.log1p(jnp.exp(-jnp.abs(x)))


def _layer_norm(r, g, b):
    mu = jnp.mean(r, axis=-1, keepdims=True)
    d = r - mu
    var = jnp.mean(d * d, axis=-1, keepdims=True)
    return d * lax.rsqrt(var + LN_EPS) * g + b


def _conv_group_norm(prod, w, expand, reduce):
    ssum = _dot_f32_lhs(prod * prod, reduce, terms=2)
    rstd = lax.rsqrt(ssum * (1.0 / CONV_GROUP) + RMS_EPS)
    return prod * _dot_f32_lhs(rstd, expand, terms=2) * w


def _ssm_group_norm(y, w):
    outs = []
    for g in range(SSM_GROUPS):
        yg = y[:, g * SSM_GROUP_CH:(g + 1) * SSM_GROUP_CH]
        ms = jnp.mean(yg * yg, axis=-1, keepdims=True)
        outs.append((yg * lax.rsqrt(ms + RMS_EPS) * w[:, g * SSM_GROUP_CH:(g + 1) * SSM_GROUP_CH]).astype(BF16))
    return outs


def _mix_out(y_conv, y_ssm_groups, w_out_ref):
    m = _dot(y_conv.astype(BF16), w_out_ref[0:CONV_CH, :])
    for g, yg in enumerate(y_ssm_groups):
        lo = CONV_CH + g * SSM_GROUP_CH
        m = m + _dot(yg, w_out_ref[lo:lo + SSM_GROUP_CH, :])
    return m


def _ada_kernel(c_ref, w_ref, b_ref, op_ref, os_ref):
    c = c_ref[...]
    w = w_ref[...]
    c_hi = c.astype(BF16)
    c_lo = (c - c_hi.astype(F32)).astype(BF16)
    w_hi = w.astype(BF16)
    mod = _dot(c_hi, w_hi) + _dot(c_lo, w_hi) + b_ref[...]
    n_sample = os_ref.shape[0]
    os_ref[...] = mod[0:n_sample, :]
    op_ref[...] = mod[n_sample:, :]


def _ada(c_sample, c_prompt, w_ada, b_ada, tile_n=1024):
    ns, nb = c_sample.shape[0], c_prompt.shape[0]
    n = w_ada.shape[1]
    return pl.pallas_call(
        _ada_kernel,
        grid=(n // tile_n,),
        in_specs=[pl.BlockSpec((ns + nb, D_MODEL), lambda i: (0, 0)),
                  pl.BlockSpec((D_MODEL, tile_n), lambda i: (0, i)),
                  pl.BlockSpec((1, tile_n), lambda i: (0, i))],
        out_specs=[pl.BlockSpec((nb, tile_n), lambda i: (0, i)),
                   pl.BlockSpec((ns, tile_n), lambda i: (0, i))],
        out_shape=[jax.ShapeDtypeStruct((nb, n), F32), jax.ShapeDtypeStruct((ns, n), F32)],
        name="ada_mod",
    )(jnp.concatenate([c_sample, c_prompt], axis=0), w_ada, b_ada)


def _cast_transposed_kernel(wt_ref, o_ref):
    rows = wt_ref.shape[0]
    wt = wt_ref[...]
    if rows < o_ref.shape[1]:
        wt = jnp.concatenate([wt, jnp.zeros((o_ref.shape[1] - rows, wt.shape[1]), wt.dtype)], axis=0)
    o_ref[...] = wt.T.astype(o_ref.dtype)


def _cast_in_proj(w):
    wt = w.T
    n_dt = w.shape[1] - COL_DT
    cols = COL_DT // 4
    assert cols % LANES == 0
    main = pl.pallas_call(
        _cast_transposed_kernel,
        grid=(COL_DT // cols,),
        in_specs=[pl.BlockSpec((cols, D_MODEL), lambda j: (j, 0))],
        out_specs=pl.BlockSpec((D_MODEL, cols), lambda j: (0, j)),
        out_shape=jax.ShapeDtypeStruct((D_MODEL, COL_DT), BF16),
        compiler_params=pltpu.CompilerParams(vmem_limit_bytes=VMEM_LIMIT),
        name="cast_in_proj",
    )(wt)
    dt = pl.pallas_call(
        _cast_transposed_kernel,
        grid=(1,),
        in_specs=[pl.BlockSpec((n_dt, D_MODEL), lambda j: (COL_DT // n_dt, 0))],
        out_specs=pl.BlockSpec((D_MODEL, LANES), lambda j: (0, 0)),
        out_shape=jax.ShapeDtypeStruct((D_MODEL, LANES), BF16),
        name="cast_in_proj_dt",
    )(wt)
    return main, dt


def _mixer_prompt_kernel(xa_ref, moda_ref, modb_ref, w_in_ref, w_dt_ref, expand_ref, reduce_ref,
                         conv_w_ref, conv_nw_ref, sconv_w_ref, sconv_b_ref,
                         dtb_ref, alog_ref, dexp_ref, snw_ref, w_out_ref, ln_g_ref, ln_b_ref,
                         x1_ref, cst_ref, scst_ref, sst_ref,
                         p, xk, cbuf, xbuf, st_ref, xs_ref, bc_ref, dtx_ref, acsx_ref, endx_ref,
                         acst_ref, cb_ref, bmt_ref, y_ref, yc_ref,
                         *, tile, tiles_per_seq, sched):
    s = pl.program_id(0)
    jb = lax.rem(s + (tiles_per_seq - 1), tiles_per_seq)

    @pl.when(s == 0)
    def _():
        p[...] = jnp.zeros_like(p)
        xk[...] = jnp.zeros_like(xk)

    @pl.when((jb == 0) | (s == 0))
    def _():
        cbuf[...] = jnp.zeros_like(cbuf)
        xbuf[...] = jnp.zeros_like(xbuf)
        st_ref[...] = jnp.zeros_like(st_ref)

    def stages():
        xa = xa_ref[...]
        u = (xa * (1.0 + moda_ref[:, D_MODEL:2 * D_MODEL]) + moda_ref[:, 0:D_MODEL]).astype(BF16)
        free = []

        def first_stage(n):
            for _ in range(min(n, len(free))):
                lo = free.pop(0)
                if lo == COL_DT:
                    p[:, COL_DT:IN_PAD] = _dot(u, w_dt_ref[...])
                else:
                    p[:, lo:lo + MXU_COLS] = _dot(u, w_in_ref[:, lo:lo + MXU_COLS])

        expand = expand_ref[...]
        x = xk[...]
        g1 = modb_ref[:, 2 * D_MODEL:3 * D_MODEL]

        def proj(lo, width):
            return p[:, lo:lo + width]

        def delayed(tail_ref, cs, cur, taps):
            seq = jnp.concatenate([tail_ref[:, cs], cur], axis=0)
            tail_ref[:, cs] = cur[tile - SUBLANES:, :]
            return [pltpu.roll(seq, k, axis=0)[SUBLANES:, :] for k in range(1, taps + 1)]

        for k in range(CONV_CH // MXU_COLS):
            c0 = k * MXU_COLS
            cs = slice(c0, c0 + MXU_COLS)
            ch = proj(COL_GC + c0, MXU_COLS) * proj(COL_HV + c0, MXU_COLS)
            ch1, ch2 = delayed(cbuf, cs, ch, 2)
            cv = conv_w_ref[0:1, cs] * ch2 + conv_w_ref[1:2, cs] * ch1 + conv_w_ref[2:3, cs] * ch
            prod = proj(COL_GB + c0, MXU_COLS) * cv
            free.extend((COL_GC + c0, COL_HV + c0, COL_GB + c0))
            first_stage(sched[0])
            ssum = _dot_f32_lhs(prod * prod, reduce_ref[cs, :], terms=1)
            rstd = lax.rsqrt(ssum * (1.0 / CONV_GROUP) + RMS_EPS)
            yc_ref[:, cs] = (prod * _dot_f32_lhs(rstd, expand_ref[:, cs], terms=2)
                             * conv_nw_ref[:, cs]).astype(BF16)

        def pre_conv(c0):
            cs = slice(c0, c0 + MXU_COLS)
            xbc = proj(COL_XBC + c0, MXU_COLS)
            free.append(COL_XBC + c0)
            x1, x2, x3 = delayed(xbuf, cs, xbc, 3)
            return _silu(sconv_w_ref[0:1, cs] * x3 + sconv_w_ref[1:2, cs] * x2 + sconv_w_ref[2:3, cs] * x1
                         + sconv_w_ref[3:4, cs] * xbc + sconv_b_ref[:, cs])

        row = lax.broadcasted_iota(jnp.int32, (SSM_CHUNK, SSM_CHUNK), 0)
        col = lax.broadcasted_iota(jnp.int32, (SSM_CHUNK, SSM_CHUNK), 1)
        causal = row >= col
        tri = causal.astype(BF16)
        groups = SSM_CHUNK // SUBLANES
        causal_bias = jnp.where(causal, 0.0, -jnp.inf).reshape(groups, SUBLANES, SSM_CHUNK)
        first_half = (col < SSM_HEAD_DIM).reshape(groups, SUBLANES, SSM_CHUNK)
        half_rows = col < SSM_HEAD_DIM
        chunks = [slice(c * SSM_CHUNK, (c + 1) * SSM_CHUNK) for c in range(tile // SSM_CHUNK)]

        dt = _softplus(proj(COL_DT, LANES) + dtb_ref[...])
        free.append(COL_DT)
        first_stage(sched[1])
        dta = dt * (-jnp.exp(alog_ref[...]))
        dtx_ref[...] = _dot_f32_lhs(dt, expand, terms=1)
        for c, rows in enumerate(chunks):
            acs = _dot_f32_rhs(tri, dta[rows, :])
            acs_t = acs.T
            for h in range(SSM_HEADS):
                r8 = (c * SSM_HEADS + h) * SUBLANES
                acst_ref[r8:r8 + SUBLANES, :] = jnp.broadcast_to(acs_t[h:h + 1, :], (SUBLANES, SSM_CHUNK))
            acs_x = _dot_f32_lhs(acs, expand, terms=2)
            acsx_ref[rows, :] = acs_x
            endx_ref[c * SUBLANES:(c + 1) * SUBLANES, :] = jnp.broadcast_to(acs_x[SSM_CHUNK - 1:SSM_CHUNK, :],
                                                                             (SUBLANES, SSM_CH))
        for c0 in range(SSM_CH, XBC_CH, MXU_COLS):
            first_stage(sched[2])
            bc_ref[:, c0 - SSM_CH:c0 - SSM_CH + MXU_COLS] = pre_conv(c0)
        for rows in chunks:
            for g in range(SSM_GROUPS):
                gs = slice(g * SSM_STATE, (g + 1) * SSM_STATE)
                bm = bc_ref[rows, gs]
                cm = bc_ref[rows, (SSM_GROUPS + g) * SSM_STATE:(SSM_GROUPS + g + 1) * SSM_STATE]
                cb_ref[rows, gs] = lax.dot_general(cm.astype(BF16), bm.astype(BF16), (((1,), (1,)), ((), ())),
                                                   preferred_element_type=F32)
                bmt_ref[rows, gs] = bm.T.astype(BF16)

        for c0 in range(0, SSM_CH, MXU_COLS):
            first_stage(sched[3])
            cs = slice(c0, c0 + MXU_COLS)
            g = c0 // SSM_GROUP_CH
            gs = slice(g * SSM_STATE, (g + 1) * SSM_STATE)
            xs = pre_conv(c0)
            xs_ref[:, cs] = xs
            xdt = xs * dtx_ref[:, cs]
            for c, rows in enumerate(chunks):
                first_stage(sched[4])
                acs_x = acsx_ref[rows, cs].reshape(groups, SUBLANES, MXU_COLS)
                end_x = endx_ref[c * SUBLANES:(c + 1) * SUBLANES, cs]
                xdt_c = xdt[rows, :]
                xdec = (xdt_c * jnp.exp(end_x[None] - acs_x).reshape(SSM_CHUNK, MXU_COLS)).astype(BF16)
                cm = bc_ref[rows, (SSM_GROUPS + g) * SSM_STATE:(SSM_GROUPS + g + 1) * SSM_STATE].astype(BF16)
                cb = cb_ref[rows, gs]
                st = st_ref[:, cs]
                y_off = _dot(cm, st.astype(BF16)) * jnp.exp(acs_x).reshape(SSM_CHUNK, MXU_COLS)
                st_ref[:, cs] = ((st.reshape(groups, SUBLANES, MXU_COLS) * jnp.exp(end_x)[None])
                                 .reshape(SSM_STATE, MXU_COLS) + _dot(bmt_ref[rows, gs], xdec))
                for lo in range(0, MXU_COLS, LANES):
                    h0 = (c * SSM_HEADS + (c0 + lo) // SSM_HEAD_DIM) * SUBLANES
                    slab = acs_x[:, :, lo:lo + LANES]
                    rolled = pltpu.roll(slab, SSM_HEAD_DIM, axis=2)
                    a0 = jnp.where(first_half, slab, rolled) - acst_ref[h0:h0 + SUBLANES, :][None]
                    a1 = jnp.where(first_half, rolled, slab) - acst_ref[h0 + SUBLANES:h0 + 2 * SUBLANES, :][None]
                    l0 = jnp.exp(a0 + causal_bias).reshape(SSM_CHUNK, SSM_CHUNK)
                    l1 = jnp.exp(a1 + causal_bias).reshape(SSM_CHUNK, SSM_CHUNK)
                    m = jnp.concatenate([(cb * l0).astype(BF16), (cb * l1).astype(BF16)], axis=1)
                    xp = xdt_c[:, lo:lo + LANES]
                    rhs = jnp.concatenate([jnp.where(half_rows, xp, 0.0), jnp.where(half_rows, 0.0, xp)],
                                          axis=0).astype(BF16)
                    y_ref[rows, c0 + lo:c0 + lo + LANES] = _dot(m, rhs) + y_off[:, lo:lo + LANES]

        for k in range(SSM_CH // MXU_COLS):
            first_stage(sched[5])
            c0 = k * MXU_COLS
            cs = slice(c0, c0 + MXU_COLS)
            y_ref[:, cs] = (y_ref[:, cs] + xs_ref[:, cs] * dexp_ref[:, cs]) * _silu(proj(COL_Z + c0, MXU_COLS))
            free.append(COL_Z + c0)
        first_stage(sched[6])
        m = _mix_out(yc_ref[...], _ssm_group_norm(y_ref[...], snw_ref[...]), w_out_ref)
        x1_ref[...] = _layer_norm(ALPHA * x + (1.0 + g1) * m, ln_g_ref[...], ln_b_ref[...])
        first_stage(N_PIECES)
        assert not free
        xk[...] = xa

    stages()

    @pl.when((jb == tiles_per_seq - 1) & (s > 0))
    def _():
        cst_ref[...] = cbuf[SUBLANES - 2:SUBLANES, :]
        scst_ref[...] = xbuf[SUBLANES - 3:SUBLANES, :]
        sst_ref[...] = st_ref[...].T


def _const_spec(shape):
    return pl.BlockSpec(shape, lambda *_: (0,) * len(shape), pipeline_mode=pl.Buffered(1))


def _mixer_prompt(x, mod, w_in, w_dt, conv_w, conv_nw, sconv_w, sconv_b, dtb, alog, dexp, snw, w_out, ln_g, ln_b,
                  tile=256, sched=(1, 0, 0, 2, 1, 0, 0)):
    assert CONV_GROUP == SSM_HEAD_DIM and CONV_CH == SSM_CH
    nb, seq, _ = x.shape
    tiles_per_seq = seq // tile
    n_tiles = nb * tiles_per_seq
    kern = functools.partial(_mixer_prompt_kernel, tile=tile, tiles_per_seq=tiles_per_seq, sched=sched)
    consts = [w_in, w_dt, _head_expand(), _group_reduce(), conv_w, conv_nw, sconv_w, sconv_b, dtb, alog, dexp, snw,
              w_out, ln_g, ln_b]
    first = lambda s: jnp.minimum(s, n_tiles - 1)
    second = lambda s: jnp.maximum(s - 1, 0)
    return pl.pallas_call(
        kern,
        grid=(n_tiles + 1,),
        in_specs=[pl.BlockSpec((tile, D_MODEL), lambda s: (first(s), 0)),
                  pl.BlockSpec((None, 1, 6 * D_MODEL), lambda s: (first(s) // tiles_per_seq, 0, 0)),
                  pl.BlockSpec((None, 1, 6 * D_MODEL), lambda s: (second(s) // tiles_per_seq, 0, 0))]
                 + [_const_spec(a.shape) for a in consts],
        out_specs=[pl.BlockSpec((tile, D_MODEL), lambda s: (second(s), 0)),
                   pl.BlockSpec((None, 2, CONV_CH), lambda s: (second(s) // tiles_per_seq, 0, 0)),
                   pl.BlockSpec((None, 3, XBC_CH), lambda s: (second(s) // tiles_per_seq, 0, 0)),
                   pl.BlockSpec((None, SSM_CH, SSM_STATE), lambda s: (second(s) // tiles_per_seq, 0, 0))],
        out_shape=[jax.ShapeDtypeStruct((nb * seq, D_MODEL), F32),
                   jax.ShapeDtypeStruct((nb, 2, CONV_CH), F32),
                   jax.ShapeDtypeStruct((nb, 3, XBC_CH), F32),
                   jax.ShapeDtypeStruct((nb, SSM_CH, SSM_STATE), F32)],
        scratch_shapes=[pltpu.VMEM((tile, IN_PAD), F32),
                        pltpu.VMEM((tile, D_MODEL), F32),
                        pltpu.VMEM((SUBLANES, CONV_CH), F32),
                        pltpu.VMEM((SUBLANES, XBC_CH), F32),
                        pltpu.VMEM((SSM_STATE, SSM_CH), F32),
                        pltpu.VMEM((tile, SSM_CH), F32),
                        pltpu.VMEM((tile, 2 * SSM_GROUPS * SSM_STATE), F32),
                        pltpu.VMEM((tile, SSM_CH), F32),
                        pltpu.VMEM((tile, SSM_CH), F32),
                        pltpu.VMEM((tile // SSM_CHUNK * SUBLANES, SSM_CH), F32),
                        pltpu.VMEM((tile // SSM_CHUNK * SSM_HEADS * SUBLANES, SSM_CHUNK), F32),
                        pltpu.VMEM((tile, SSM_GROUPS * SSM_STATE), F32),
                        pltpu.VMEM((tile, SSM_GROUPS * SSM_STATE), BF16),
                        pltpu.VMEM((tile, SSM_CH), F32),
                        pltpu.VMEM((tile, CONV_CH), BF16)],
        compiler_params=pltpu.CompilerParams(dimension_semantics=("arbitrary",),
                                             vmem_limit_bytes=VMEM_LIMIT),
        name="mixer_prompt",
    )(x.reshape(nb * seq, D_MODEL), mod, mod, *consts)


def _ffn_kernel(x_ref, mod_ref, w_up_ref, w_down_ref, ln_g_ref, ln_b_ref, o_ref, r_ref, *, ff_tile):
    s = pl.program_id(0)
    n_tiles = pl.num_programs(0) - 1

    @pl.when(s == 0)
    def _():
        r_ref[...] = jnp.zeros_like(r_ref)

    def norm_previous():
        o_ref[...] = _layer_norm(r_ref[...], ln_g_ref[...], ln_b_ref[...])

    @pl.when(s < n_tiles)
    def _():
        norm_previous()
        x = x_ref[...]
        sh2 = mod_ref[:, 3 * D_MODEL:4 * D_MODEL]
        sc2 = mod_ref[:, 4 * D_MODEL:5 * D_MODEL]
        g2 = mod_ref[:, 5 * D_MODEL:6 * D_MODEL]
        v = (x * (1.0 + sc2) + sh2).astype(BF16)
        acc = jnp.zeros(x.shape, F32)
        for k in range(D_FF // ff_tile):
            h = jnp.maximum(_dot(v, w_up_ref[:, k * ff_tile:(k + 1) * ff_tile]), 0.0)
            acc = acc + _dot((h * h).astype(BF16), w_down_ref[k * ff_tile:(k + 1) * ff_tile, :])
        r_ref[...] = ALPHA * x + (1.0 + g2) * acc

    @pl.when(s == n_tiles)
    def _():
        norm_previous()


def _ffn(x, mod, rows_per_mod, w_up, w_down, ln_g, ln_b, tile, ff_tile=1024):
    rows = x.shape[0]
    mod_rows = mod.shape[1]
    tiles_per_mod = rows_per_mod // tile
    n_tiles = rows // tile
    kern = functools.partial(_ffn_kernel, ff_tile=ff_tile)
    first = lambda s: jnp.minimum(s, n_tiles - 1)
    second = lambda s: jnp.maximum(s - 1, 0)
    return pl.pallas_call(
        kern,
        grid=(n_tiles + 1,),
        in_specs=[pl.BlockSpec((tile, D_MODEL), lambda s: (first(s), 0)),
                  pl.BlockSpec((None, mod_rows, 6 * D_MODEL), lambda s: (first(s) // tiles_per_mod, 0, 0)),
                  _const_spec(w_up.shape), _const_spec(w_down.shape),
                  _const_spec(ln_g.shape), _const_spec(ln_b.shape)],
        out_specs=pl.BlockSpec((tile, D_MODEL), lambda s: (second(s), 0)),
        out_shape=jax.ShapeDtypeStruct((rows, D_MODEL), F32),
        scratch_shapes=[pltpu.VMEM((tile, D_MODEL), F32)],
        compiler_params=pltpu.CompilerParams(dimension_semantics=("arbitrary",),
                                             vmem_limit_bytes=VMEM_LIMIT),
        name="ffn",
    )(x, mod, w_up, w_down, ln_g, ln_b)


def _ffn_stream_kernel(x_ref, mod_ref, w_up_ref, w_down_ref, ln_g_ref, ln_b_ref, o_ref, acc_ref):
    k = pl.program_id(0)

    @pl.when(k == 0)
    def _():
        acc_ref[...] = jnp.zeros_like(acc_ref)

    x = x_ref[...]
    sh2 = mod_ref[:, 3 * D_MODEL:4 * D_MODEL]
    sc2 = mod_ref[:, 4 * D_MODEL:5 * D_MODEL]
    v = (x * (1.0 + sc2) + sh2).astype(BF16)
    h = jnp.maximum(_dot(v, w_up_ref[...]), 0.0)
    acc_ref[...] += _dot((h * h).astype(BF16), w_down_ref[...])

    @pl.when(k == pl.num_programs(0) - 1)
    def _():
        g2 = mod_ref[:, 5 * D_MODEL:6 * D_MODEL]
        o_ref[...] = _layer_norm(ALPHA * x + (1.0 + g2) * acc_ref[...], ln_g_ref[...], ln_b_ref[...])


def _ffn_stream(x, mod, w_up, w_down, ln_g, ln_b, ff_tile=1024):
    rows = x.shape[0]
    return pl.pallas_call(
        _ffn_stream_kernel,
        grid=(D_FF // ff_tile,),
        in_specs=[_const_spec(x.shape), _const_spec(mod.shape),
                  pl.BlockSpec((D_MODEL, ff_tile), lambda k: (0, k)),
                  pl.BlockSpec((ff_tile, D_MODEL), lambda k: (k, 0)),
                  _const_spec(ln_g.shape), _const_spec(ln_b.shape)],
        out_specs=pl.BlockSpec((rows, D_MODEL), lambda k: (0, 0)),
        out_shape=jax.ShapeDtypeStruct((rows, D_MODEL), F32),
        scratch_shapes=[pltpu.VMEM((rows, D_MODEL), F32)],
        compiler_params=pltpu.CompilerParams(dimension_semantics=("arbitrary",),
                                             vmem_limit_bytes=VMEM_LIMIT),
        name="ffn_stream",
    )(x, mod, w_up, w_down, ln_g, ln_b)


_PRODUCT_TERMS = ((0, 0), (0, 1), (1, 0), (0, 2), (2, 0), (1, 1))
UPDATE_TERMS = 16


def _sample_pre_kernel(x_ref, mod_ref, w_in_ref, w_dt_ref, conv_w_ref, conv_nw_ref, sconv_w_ref, sconv_b_ref,
                       dtb_ref, alog_ref, cb0_ref, cb1_ref, sb0_ref, sb1_ref, sb2_ref,
                       yconv_ref, ch_ref, xbc_ref, z_ref, xs_ref, ydiag_ref, decx_ref, cm_ref, lhs_ref, rhs_ref):
    expand = _head_expand()
    reduce = _group_reduce()
    x = x_ref[...]
    sh1 = mod_ref[:, 0:D_MODEL]
    sc1 = mod_ref[:, D_MODEL:2 * D_MODEL]
    u = (x * (1.0 + sc1) + sh1).astype(BF16)

    def proj(lo, width):
        return _dot(u, w_in_ref[:, lo:lo + width])

    ch = proj(COL_GC, CONV_CH) * proj(COL_HV, CONV_CH)
    ch_ref[...] = ch
    cw = conv_w_ref[...]
    cv = cw[0:1, :] * cb0_ref[...] + cw[1:2, :] * cb1_ref[...] + cw[2:3, :] * ch
    yconv_ref[...] = _conv_group_norm(proj(COL_GB, CONV_CH) * cv, conv_nw_ref[...], expand, reduce)

    xbc = proj(COL_XBC, XBC_CH)
    xbc_ref[...] = xbc
    sw = sconv_w_ref[...]
    xc = _silu(sw[0:1, :] * sb0_ref[...] + sw[1:2, :] * sb1_ref[...] + sw[2:3, :] * sb2_ref[...]
               + sw[3:4, :] * xbc + sconv_b_ref[...])
    xs = xc[:, 0:SSM_CH]
    xs_ref[...] = xs
    cm_ref[...] = xc[:, SSM_CH + SSM_GROUPS * SSM_STATE:XBC_CH]
    z_ref[...] = proj(COL_Z, SSM_CH)

    dt = _softplus(_dot(u, w_dt_ref[...]) + dtb_ref[...])
    dta = dt * (-jnp.exp(alog_ref[...]))
    xdt = xs * _dot_f32_lhs(dt, expand)
    decx = jnp.exp(_dot_f32_lhs(dta, expand))
    decx_ref[...] = decx
    xdt_t, dec_t = _split(xdt, 3), _split(decx, 3)
    bm_t = _split(xc[:, SSM_CH:SSM_CH + SSM_GROUPS * SSM_STATE], 3)
    group_of = lax.broadcasted_iota(jnp.int32, xdt.shape, 1) // SSM_GROUP_CH
    zeros = jnp.zeros((x.shape[0], SSM_STATE), BF16)
    r = 0
    for g in range(SSM_GROUPS):
        for tx, tb in _PRODUCT_TERMS:
            lhs_ref[r] = jnp.where(group_of == g, xdt_t[tx], jnp.zeros_like(xdt_t[tx]))
            rhs_ref[r] = jnp.concatenate([bm_t[tb][:, g * SSM_STATE:(g + 1) * SSM_STATE], zeros], axis=1)
            r += 1
    for t in range(3):
        lhs_ref[r] = dec_t[t]
        rhs_ref[r] = jnp.concatenate([zeros, jnp.ones_like(zeros)], axis=1)
        r += 1
    for r in range(r, UPDATE_TERMS):
        lhs_ref[r] = jnp.zeros_like(dec_t[0])
        rhs_ref[r] = jnp.concatenate([zeros, zeros], axis=1)
    for g in range(SSM_GROUPS):
        bm = xc[:, SSM_CH + g * SSM_STATE:SSM_CH + (g + 1) * SSM_STATE]
        cm = xc[:, SSM_CH + (SSM_GROUPS + g) * SSM_STATE:SSM_CH + (SSM_GROUPS + g + 1) * SSM_STATE]
        cb = jnp.sum(cm * bm, axis=-1, keepdims=True)
        gl = g * SSM_GROUP_CH
        ydiag_ref[:, gl:gl + SSM_GROUP_CH] = cb * xdt[:, gl:gl + SSM_GROUP_CH]


def _sample_pre(x, mod, w_in, w_dt, conv_w, conv_nw, sconv_w, sconv_b, dtb, alog, cb0, cb1, sb0, sb1, sb2):
    n = x.shape[0]
    args = (x, mod, w_in, w_dt, conv_w, conv_nw, sconv_w, sconv_b, dtb, alog, cb0, cb1, sb0, sb1, sb2)
    f32_shapes = [(n, CONV_CH), (n, CONV_CH), (n, XBC_CH), (n, SSM_CH), (n, SSM_CH), (n, SSM_CH), (n, SSM_CH),
                  (n, SSM_GROUPS * SSM_STATE)]
    bf16_shapes = [(UPDATE_TERMS, n, SSM_CH), (UPDATE_TERMS, n, 2 * SSM_STATE)]
    return pl.pallas_call(
        _sample_pre_kernel,
        out_shape=[jax.ShapeDtypeStruct(s, F32) for s in f32_shapes]
                  + [jax.ShapeDtypeStruct(s, BF16) for s in bf16_shapes],
        compiler_params=pltpu.CompilerParams(vmem_limit_bytes=VMEM_LIMIT),
        name="sample_pre",
    )(*args)


def _sample_state_kernel(s_ref, lhs_ref, rhs_ref, cm_ref, decx_ref, o_ref, yoff_ref, *, block):
    i = pl.program_id(0)
    rows = UPDATE_TERMS * block
    lhs_t = lhs_ref[...].reshape(rows, SSM_CH).astype(F32).T.astype(BF16)
    rhs_all = rhs_ref[...].reshape(rows, 2 * SSM_STATE)
    token_of = lax.broadcasted_iota(jnp.int32, rhs_all.shape, 0) % block

    def body(k, carry):
        b = i * block + k
        s = s_ref[k]
        upd = _dot(lhs_t, jnp.where(token_of == k, rhs_all, jnp.zeros_like(rhs_all)))
        o_ref[k] = s * upd[:, SSM_STATE:2 * SSM_STATE] + upd[:, 0:SSM_STATE]
        cm = cm_ref[pl.ds(b, 1), :]
        sums = []
        for c0 in range(0, SSM_CH, LANES):
            g = c0 // SSM_GROUP_CH
            prod = s[c0:c0 + LANES, :] * cm[:, g * SSM_STATE:(g + 1) * SSM_STATE]
            sums.append(jnp.sum(prod.T, axis=0, keepdims=True))
        yoff_ref[pl.ds(b, 1), :] = jnp.concatenate(sums, axis=1) * decx_ref[pl.ds(b, 1), :]
        return carry

    lax.fori_loop(0, block, body, 0, unroll=True)


def _sample_state(state, lhs, rhs, cm, decx, block=16):
    n = state.shape[0]
    kern = functools.partial(_sample_state_kernel, block=block)
    return pl.pallas_call(
        kern,
        grid=(n // block,),
        in_specs=[pl.BlockSpec((block, SSM_CH, SSM_STATE), lambda i: (i, 0, 0)),
                  pl.BlockSpec((UPDATE_TERMS, block, SSM_CH), lambda i: (0, i, 0)),
                  pl.BlockSpec((UPDATE_TERMS, block, 2 * SSM_STATE), lambda i: (0, i, 0)),
                  _const_spec(cm.shape), _const_spec(decx.shape)],
        out_specs=[pl.BlockSpec((block, SSM_CH, SSM_STATE), lambda i: (i, 0, 0)),
                   pl.BlockSpec((n, SSM_CH), lambda i: (0, 0))],
        out_shape=[jax.ShapeDtypeStruct(state.shape, F32), jax.ShapeDtypeStruct((n, SSM_CH), F32)],
        compiler_params=pltpu.CompilerParams(dimension_semantics=("arbitrary",),
                                             vmem_limit_bytes=VMEM_LIMIT),
        name="sample_state",
    )(state, lhs, rhs, cm, decx)


def _sample_post_kernel(x_ref, mod_ref, yconv_ref, ydiag_ref, yoff_ref, xs_ref, z_ref, dexp_ref, snw_ref,
                        w_out_ref, ln_g_ref, ln_b_ref, x1_ref):
    g1 = mod_ref[:, 2 * D_MODEL:3 * D_MODEL]
    y = ydiag_ref[...] + yoff_ref[...] + xs_ref[...] * dexp_ref[...]
    y = y * _silu(z_ref[...])
    m = _mix_out(yconv_ref[...], _ssm_group_norm(y, snw_ref[...]), w_out_ref)
    x1_ref[...] = _layer_norm(ALPHA * x_ref[...] + (1.0 + g1) * m, ln_g_ref[...], ln_b_ref[...])


def _sample_post(x, mod, yconv, ydiag, yoff, xs, z, dexp, snw, w_out, ln_g, ln_b):
    return pl.pallas_call(
        _sample_post_kernel,
        out_shape=jax.ShapeDtypeStruct(x.shape, F32),
        compiler_params=pltpu.CompilerParams(vmem_limit_bytes=VMEM_LIMIT),
        name="sample_post",
    )(x, mod, yconv, ydiag, yoff, xs, z, dexp, snw, w_out, ln_g, ln_b)


def kernel(x_prompt, x_sample, state_conv, state_ssm_conv, state_ssm, c_prompt, c_sample, w_ada, b_ada, w_in, conv_w, conv_norm_w, ssm_conv_w, ssm_conv_b, dt_bias, a_log, d_skip, ssm_norm_w, w_out, ln1_g, ln1_b, w_up, w_down, ln2_g, ln2_b):
    assert w_ada.shape[0] == 1, "single-layer trunk"
    nb, seq, _ = x_prompt.shape
    ns = x_sample.shape[0]
    row = lambda a: a.reshape(1, -1)
    pad_heads = lambda a: jnp.pad(a.reshape(1, -1), ((0, 0), (0, LANES - SSM_HEADS)))

    w_in_b, w_dt_b = _cast_in_proj(w_in[0])
    w_out_b = w_out[0].astype(BF16)
    w_up_b = w_up[0].astype(BF16)
    w_down_b = w_down[0].astype(BF16)
    conv_nw, sconv_b, snw = row(conv_norm_w[0]), row(ssm_conv_b[0]), row(ssm_norm_w[0])
    dtb, alog = pad_heads(dt_bias[0]), pad_heads(a_log[0])
    dexp = row(jnp.repeat(d_skip[0], SSM_HEAD_DIM))
    g1, b1, g2, b2 = row(ln1_g[0]), row(ln1_b[0]), row(ln2_g[0]), row(ln2_b[0])

    mod_p, mod_s = _ada(c_sample, c_prompt, w_ada[0], row(b_ada[0]))
    mod_p = mod_p.reshape(nb, 1, 6 * D_MODEL)

    x1_p, cst_p, scst_p, sst_p = _mixer_prompt(x_prompt, mod_p, w_in_b, w_dt_b, conv_w[0], conv_nw, ssm_conv_w[0],
                                               sconv_b, dtb, alog, dexp, snw, w_out_b, g1, b1)
    y_p = _ffn(x1_p, mod_p, seq, w_up_b, w_down_b, g2, b2, tile=512)

    xs2 = x_sample.reshape(ns, D_MODEL)
    (yconv_s, ch_s, xbc_s, z_s, xs_s, ydiag_s, decx_s, cm_s, lhs_s, rhs_s) = _sample_pre(
        xs2, mod_s, w_in_b, w_dt_b, conv_w[0], conv_nw, ssm_conv_w[0], sconv_b, dtb, alog,
        state_conv[0, :, 0], state_conv[0, :, 1],
        state_ssm_conv[0, :, 0], state_ssm_conv[0, :, 1], state_ssm_conv[0, :, 2])
    new_state_s, yoff_s = _sample_state(state_ssm[0].reshape(ns, SSM_CH, SSM_STATE), lhs_s, rhs_s, cm_s, decx_s)
    x1_s = _sample_post(xs2, mod_s, yconv_s, ydiag_s, yoff_s, xs_s, z_s, dexp, snw, w_out_b, g1, b1)
    y_s = _ffn_stream(x1_s, mod_s, w_up_b, w_down_b, g2, b2)

    return (y_p.reshape(nb, seq, D_MODEL),
            y_s.reshape(ns, 1, D_MODEL),
            cst_p[None],
            scst_p[None],
            sst_p.reshape(1, nb, SSM_HEADS, SSM_HEAD_DIM, SSM_STATE),
            jnp.stack([state_conv[0, :, 1], ch_s], axis=1)[None],
            jnp.stack([state_ssm_conv[0, :, 1], state_ssm_conv[0, :, 2], xbc_s], axis=1)[None],
            new_state_s.reshape(1, ns, SSM_HEADS, SSM_HEAD_DIM, SSM_STATE))
```

```python
import functools

import jax
import jax.numpy as jnp
import numpy as np
from jax import lax
from jax.experimental import pallas as pl
from jax.experimental.pallas import tpu as pltpu

F32 = jnp.float32
BF16 = jnp.bfloat16

D_MODEL = 1024
CONV_CH = 1024
CONV_GROUP = 64
SSM_CH = 1024
SSM_HEADS = 16
SSM_HEAD_DIM = 64
SSM_GROUPS = 2
SSM_GROUP_CH = SSM_CH // SSM_GROUPS
SSM_STATE = 128
SSM_CHUNK = 128
XBC_CH = SSM_CH + 2 * SSM_GROUPS * SSM_STATE
D_FF = 4 * D_MODEL
LANES = 128
SUBLANES = 8
MXU_COLS = 256
COL_GB, COL_GC, COL_HV, COL_Z, COL_XBC = 0, 1024, 2048, 3072, 4096
COL_DT = COL_XBC + XBC_CH
IN_COLS = COL_DT + SSM_HEADS
IN_PAD = COL_DT + LANES
N_PIECES = COL_DT // MXU_COLS + 1
ALPHA = 2.0 ** 0.25
LN_EPS = 1e-5
RMS_EPS = 1e-5
VMEM_LIMIT = 56 * 1024 * 1024


def _dot(a, b):
    return jnp.dot(a, b, preferred_element_type=F32)


def _split(a, terms):
    parts = []
    r = a
    for t in range(terms):
        p = r.astype(BF16)
        parts.append(p)
        if t + 1 < terms:
            r = r - p.astype(F32)
    return parts


def _dot_f32_lhs(a, b_exact, terms=3):
    parts = _split(a, terms)
    out = _dot(parts[0], b_exact)
    for p in parts[1:]:
        out = out + _dot(p, b_exact)
    return out


def _dot_f32_rhs(a_exact, b, terms=3):
    parts = _split(b, terms)
    out = _dot(a_exact, parts[0])
    for p in parts[1:]:
        out = out + _dot(a_exact, p)
    return out


def _head_expand(xp=jnp):
    if xp is np:
        return jnp.asarray(np.arange(SSM_CH)[None, :] // SSM_HEAD_DIM == np.arange(LANES)[:, None], BF16)
    h = lax.broadcasted_iota(jnp.int32, (LANES, SSM_CH), 0)
    c = lax.broadcasted_iota(jnp.int32, (LANES, SSM_CH), 1)
    return (c // SSM_HEAD_DIM == h).astype(BF16)


def _group_reduce(xp=jnp):
    if xp is np:
        return jnp.asarray(np.arange(CONV_CH)[:, None] // CONV_GROUP == np.arange(LANES)[None, :], BF16)
    c = lax.broadcasted_iota(jnp.int32, (CONV_CH, LANES), 0)
    k = lax.broadcasted_iota(jnp.int32, (CONV_CH, LANES), 1)
    return (c // CONV_GROUP == k).astype(BF16)


def _sigmoid(x):
    return 1.0 / (1.0 + jnp.exp(-x))


def _silu(x):
    return x * _sigmoid(x)


def _softplus(x):
    return jnp.maximum(x, 0.0) + jnp.log1p(jnp.exp(-jnp.abs(x)))


def _layer_norm(r, g, b):
    mu = jnp.mean(r, axis=-1, keepdims=True)
    d = r - mu
    var = jnp.mean(d * d, axis=-1, keepdims=True)
    return d * lax.rsqrt(var + LN_EPS) * g + b


def _conv_group_norm(prod, w, expand, reduce):
    ssum = _dot_f32_lhs(prod * prod, reduce, terms=2)
    rstd = lax.rsqrt(ssum * (1.0 / CONV_GROUP) + RMS_EPS)
    return prod * _dot_f32_lhs(rstd, expand, terms=2) * w


def _ssm_group_norm(y, w):
    outs = []
    for g in range(SSM_GROUPS):
        yg = y[:, g * SSM_GROUP_CH:(g + 1) * SSM_GROUP_CH]
        ms = jnp.mean(yg * yg, axis=-1, keepdims=True)
        outs.append((yg * lax.rsqrt(ms + RMS_EPS) * w[:, g * SSM_GROUP_CH:(g + 1) * SSM_GROUP_CH]).astype(BF16))
    return outs


def _mix_out(y_conv, y_ssm_groups, w_out_ref):
    m = _dot(y_conv.astype(BF16), w_out_ref[0:CONV_CH, :])
    for g, yg in enumerate(y_ssm_groups):
        lo = CONV_CH + g * SSM_GROUP_CH
        m = m + _dot(yg, w_out_ref[lo:lo + SSM_GROUP_CH, :])
    return m


def _ada_kernel(cs_ref, cp_ref, w_ref, b_ref, op_ref, os_ref):
    c = jnp.concatenate([cs_ref[...], cp_ref[...]], axis=0)
    w = w_ref[...]
    c_hi = c.astype(BF16)
    c_lo = (c - c_hi.astype(F32)).astype(BF16)
    w_hi = w.astype(BF16)
    mod = _dot(c_hi, w_hi) + _dot(c_lo, w_hi) + b_ref[...]
    n_sample = os_ref.shape[0]
    os_ref[...] = mod[0:n_sample, :]
    op_ref[...] = mod[n_sample:, :]


def _ada(c_sample, c_prompt, w_ada, b_ada, tile_n=2048):
    ns, nb = c_sample.shape[0], c_prompt.shape[0]
    n = w_ada.shape[1]
    return pl.pallas_call(
        _ada_kernel,
        grid=(n // tile_n,),
        in_specs=[pl.BlockSpec((ns, D_MODEL), lambda i: (0, 0)),
                  pl.BlockSpec((nb, D_MODEL), lambda i: (0, 0)),
                  pl.BlockSpec((D_MODEL, tile_n), lambda i: (0, i)),
                  pl.BlockSpec((1, tile_n), lambda i: (0, i))],
        out_specs=[pl.BlockSpec((nb, tile_n), lambda i: (0, i)),
                   pl.BlockSpec((ns, tile_n), lambda i: (0, i))],
        out_shape=[jax.ShapeDtypeStruct((nb, n), F32), jax.ShapeDtypeStruct((ns, n), F32)],
        compiler_params=pltpu.CompilerParams(vmem_limit_bytes=VMEM_LIMIT),
        name="ada_mod",
    )(c_sample, c_prompt, w_ada, b_ada)


def _cast_transposed_kernel(wt_ref, o_ref):
    rows = wt_ref.shape[0]
    wt = wt_ref[...]
    if rows < o_ref.shape[1]:
        wt = jnp.concatenate([wt, jnp.zeros((o_ref.shape[1] - rows, wt.shape[1]), wt.dtype)], axis=0)
    o_ref[...] = wt.T.astype(o_ref.dtype)


def _cast_in_proj(w):
    wt = w.T
    n_dt = w.shape[1] - COL_DT
    cols = COL_DT // 4
    assert cols % LANES == 0
    main = pl.pallas_call(
        _cast_transposed_kernel,
        grid=(COL_DT // cols,),
        in_specs=[pl.BlockSpec((cols, D_MODEL), lambda j: (j, 0))],
        out_specs=pl.BlockSpec((D_MODEL, cols), lambda j: (0, j)),
        out_shape=jax.ShapeDtypeStruct((D_MODEL, COL_DT), BF16),
        compiler_params=pltpu.CompilerParams(vmem_limit_bytes=VMEM_LIMIT),
        name="cast_in_proj",
    )(wt)
    dt = pl.pallas_call(
        _cast_transposed_kernel,
        grid=(1,),
        in_specs=[pl.BlockSpec((n_dt, D_MODEL), lambda j: (COL_DT // n_dt, 0))],
        out_specs=pl.BlockSpec((D_MODEL, LANES), lambda j: (0, 0)),
        out_shape=jax.ShapeDtypeStruct((D_MODEL, LANES), BF16),
        name="cast_in_proj_dt",
    )(wt)
    return main, dt


def _mixer_prompt_kernel(xa_ref, moda_ref, modb_ref, w_in_ref, w_dt_ref, expand_ref, reduce_ref,
                         conv_w_ref, conv_nw_ref, sconv_w_ref, sconv_b_ref,
                         dtb_ref, alog_ref, dexp_ref, snw_ref, w_out_ref, ln_g_ref, ln_b_ref,
                         x1_ref, cst_ref, scst_ref, sst_ref,
                         p, xk, cbuf, xbuf, st_ref, xs_ref, bc_ref, dtx_ref, acsx_ref, endx_ref,
                         acst_ref, cb_ref, bmt_ref, y_ref, yc_ref,
                         *, tile, tiles_per_seq, sched):
    s = pl.program_id(0)
    jb = lax.rem(s + (tiles_per_seq - 1), tiles_per_seq)

    @pl.when(s == 0)
    def _():
        p[...] = jnp.zeros_like(p)
        xk[...] = jnp.zeros_like(xk)

    @pl.when((jb == 0) | (s == 0))
    def _():
        cbuf[...] = jnp.zeros_like(cbuf)
        xbuf[...] = jnp.zeros_like(xbuf)
        st_ref[...] = jnp.zeros_like(st_ref)

    def stages():
        xa = xa_ref[...]
        u = (xa * (1.0 + moda_ref[:, D_MODEL:2 * D_MODEL]) + moda_ref[:, 0:D_MODEL]).astype(BF16)
        free = []

        def first_stage(n):
            for _ in range(min(n, len(free))):
                lo = free.pop(0)
                if lo == COL_DT:
                    p[:, COL_DT:IN_PAD] = _dot(u, w_dt_ref[...])
                else:
                    p[:, lo:lo + MXU_COLS] = _dot(u, w_in_ref[:, lo:lo + MXU_COLS])

        expand = expand_ref[...]
        x = xk[...]
        g1 = modb_ref[:, 2 * D_MODEL:3 * D_MODEL]

        def proj(lo, width):
            return p[:, lo:lo + width]

        def delayed(tail_ref, cs, cur, taps):
            seq = jnp.concatenate([tail_ref[:, cs], cur], axis=0)
            tail_ref[:, cs] = cur[tile - SUBLANES:, :]
            return [pltpu.roll(seq, k, axis=0)[SUBLANES:, :] for k in range(1, taps + 1)]

        for k in range(CONV_CH // MXU_COLS):
            c0 = k * MXU_COLS
            cs = slice(c0, c0 + MXU_COLS)
            ch = proj(COL_GC + c0, MXU_COLS) * proj(COL_HV + c0, MXU_COLS)
            ch1, ch2 = delayed(cbuf, cs, ch, 2)
            cv = conv_w_ref[0:1, cs] * ch2 + conv_w_ref[1:2, cs] * ch1 + conv_w_ref[2:3, cs] * ch
            prod = proj(COL_GB + c0, MXU_COLS) * cv
            free.extend((COL_GC + c0, COL_HV + c0, COL_GB + c0))
            first_stage(sched[0])
            ssum = _dot_f32_lhs(prod * prod, reduce_ref[cs, :], terms=1)
            rstd = lax.rsqrt(ssum * (1.0 / CONV_GROUP) + RMS_EPS)
            yc_ref[:, cs] = (prod * _dot_f32_lhs(rstd, expand_ref[:, cs], terms=2)
                             * conv_nw_ref[:, cs]).astype(BF16)

        def pre_conv(c0):
            cs = slice(c0, c0 + MXU_COLS)
            xbc = proj(COL_XBC + c0, MXU_COLS)
            free.append(COL_XBC + c0)
            x1, x2, x3 = delayed(xbuf, cs, xbc, 3)
            return _silu(sconv_w_ref[0:1, cs] * x3 + sconv_w_ref[1:2, cs] * x2 + sconv_w_ref[2:3, cs] * x1
                         + sconv_w_ref[3:4, cs] * xbc + sconv_b_ref[:, cs])

        row = lax.broadcasted_iota(jnp.int32, (SSM_CHUNK, SSM_CHUNK), 0)
        col = lax.broadcasted_iota(jnp.int32, (SSM_CHUNK, SSM_CHUNK), 1)
        causal = row >= col
        tri = causal.astype(BF16)
        groups = SSM_CHUNK // SUBLANES
        causal_bias = jnp.where(causal, 0.0, -jnp.inf).reshape(groups, SUBLANES, SSM_CHUNK)
        first_half = (col < SSM_HEAD_DIM).reshape(groups, SUBLANES, SSM_CHUNK)
        half_rows = col < SSM_HEAD_DIM
        chunks = [slice(c * SSM_CHUNK, (c + 1) * SSM_CHUNK) for c in range(tile // SSM_CHUNK)]

        dt = _softplus(proj(COL_DT, LANES) + dtb_ref[...])
        free.append(COL_DT)
        first_stage(sched[1])
        dta = dt * (-jnp.exp(alog_ref[...]))
        dtx_ref[...] = _dot_f32_lhs(dt, expand, terms=1)
        for c, rows in enumerate(chunks):
            acs = _dot_f32_rhs(tri, dta[rows, :])
            acs_t = acs.T
            for h in range(SSM_HEADS):
                r8 = (c * SSM_HEADS + h) * SUBLANES
                acst_ref[r8:r8 + SUBLANES, :] = jnp.broadcast_to(acs_t[h:h + 1, :], (SUBLANES, SSM_CHUNK))
            acs_x = _dot_f32_lhs(acs, expand, terms=2)
            acsx_ref[rows, :] = acs_x
            endx_ref[c * SUBLANES:(c + 1) * SUBLANES, :] = jnp.broadcast_to(acs_x[SSM_CHUNK - 1:SSM_CHUNK, :],
                                                                             (SUBLANES, SSM_CH))
        for c0 in range(SSM_CH, XBC_CH, MXU_COLS):
            first_stage(sched[2])
            bc_ref[:, c0 - SSM_CH:c0 - SSM_CH + MXU_COLS] = pre_conv(c0)
        for rows in chunks:
            for g in range(SSM_GROUPS):
                gs = slice(g * SSM_STATE, (g + 1) * SSM_STATE)
                bm = bc_ref[rows, gs]
                cm = bc_ref[rows, (SSM_GROUPS + g) * SSM_STATE:(SSM_GROUPS + g + 1) * SSM_STATE]
                cb_ref[rows, gs] = lax.dot_general(cm.astype(BF16), bm.astype(BF16), (((1,), (1,)), ((), ())),
                                                   preferred_element_type=F32)
                bmt_ref[rows, gs] = bm.T.astype(BF16)

        for c0 in range(0, SSM_CH, MXU_COLS):
            first_stage(sched[3])
            cs = slice(c0, c0 + MXU_COLS)
            g = c0 // SSM_GROUP_CH
            gs = slice(g * SSM_STATE, (g + 1) * SSM_STATE)
            xs = pre_conv(c0)
            xs_ref[:, cs] = xs
            xdt = xs * dtx_ref[:, cs]
            for c, rows in enumerate(chunks):
                first_stage(sched[4])
                acs_x = acsx_ref[rows, cs].reshape(groups, SUBLANES, MXU_COLS)
                end_x = endx_ref[c * SUBLANES:(c + 1) * SUBLANES, cs]
                xdt_c = xdt[rows, :]
                xdec = (xdt_c * jnp.exp(end_x[None] - acs_x).reshape(SSM_CHUNK, MXU_COLS)).astype(BF16)
                cm = bc_ref[rows, (SSM_GROUPS + g) * SSM_STATE:(SSM_GROUPS + g + 1) * SSM_STATE].astype(BF16)
                cb = cb_ref[rows, gs]
                st = st_ref[:, cs]
                y_off = _dot(cm, st.astype(BF16)) * jnp.exp(acs_x).reshape(SSM_CHUNK, MXU_COLS)
                st_ref[:, cs] = ((st.reshape(groups, SUBLANES, MXU_COLS) * jnp.exp(end_x)[None])
                                 .reshape(SSM_STATE, MXU_COLS) + _dot(bmt_ref[rows, gs], xdec))
                for lo in range(0, MXU_COLS, LANES):
                    h0 = (c * SSM_HEADS + (c0 + lo) // SSM_HEAD_DIM) * SUBLANES
                    slab = acs_x[:, :, lo:lo + LANES]
                    rolled = pltpu.roll(slab, SSM_HEAD_DIM, axis=2)
                    a0 = jnp.where(first_half, slab, rolled) - acst_ref[h0:h0 + SUBLANES, :][None]
                    a1 = jnp.where(first_half, rolled, slab) - acst_ref[h0 + SUBLANES:h0 + 2 * SUBLANES, :][None]
                    l0 = jnp.exp(a0 + causal_bias).reshape(SSM_CHUNK, SSM_CHUNK)
                    l1 = jnp.exp(a1 + causal_bias).reshape(SSM_CHUNK, SSM_CHUNK)
                    m = jnp.concatenate([(cb * l0).astype(BF16), (cb * l1).astype(BF16)], axis=1)
                    xp = xdt_c[:, lo:lo + LANES]
                    rhs = jnp.concatenate([jnp.where(half_rows, xp, 0.0), jnp.where(half_rows, 0.0, xp)],
                                          axis=0).astype(BF16)
                    y_ref[rows, c0 + lo:c0 + lo + LANES] = _dot(m, rhs) + y_off[:, lo:lo + LANES]

        for k in range(SSM_CH // MXU_COLS):
            first_stage(sched[5])
            c0 = k * MXU_COLS
            cs = slice(c0, c0 + MXU_COLS)
            y_ref[:, cs] = (y_ref[:, cs] + xs_ref[:, cs] * dexp_ref[:, cs]) * _silu(proj(COL_Z + c0, MXU_COLS))
            free.append(COL_Z + c0)
        first_stage(sched[6])
        m = _mix_out(yc_ref[...], _ssm_group_norm(y_ref[...], snw_ref[...]), w_out_ref)
        x1_ref[...] = _layer_norm(ALPHA * x + (1.0 + g1) * m, ln_g_ref[...], ln_b_ref[...])
        first_stage(N_PIECES)
        assert not free
        xk[...] = xa

    stages()

    @pl.when((jb == tiles_per_seq - 1) & (s > 0))
    def _():
        cst_ref[...] = cbuf[SUBLANES - 2:SUBLANES, :]
        scst_ref[...] = xbuf[SUBLANES - 3:SUBLANES, :]
        sst_ref[...] = st_ref[...].T


def _const_spec(shape):
    return pl.BlockSpec(shape, lambda *_: (0,) * len(shape), pipeline_mode=pl.Buffered(1))


def _mixer_prompt(x, mod, w_in, w_dt, conv_w, conv_nw, sconv_w, sconv_b, dtb, alog, dexp, snw, w_out, ln_g, ln_b,
                  tile=256, sched=(1, 0, 0, 2, 1, 0, 0)):
    assert CONV_GROUP == SSM_HEAD_DIM and CONV_CH == SSM_CH
    nb, seq, _ = x.shape
    tiles_per_seq = seq // tile
    n_tiles = nb * tiles_per_seq
    kern = functools.partial(_mixer_prompt_kernel, tile=tile, tiles_per_seq=tiles_per_seq, sched=sched)
    consts = [w_in, w_dt, _head_expand(np), _group_reduce(np), conv_w, conv_nw, sconv_w, sconv_b, dtb, alog, dexp, snw,
              w_out, ln_g, ln_b]
    first = lambda s: jnp.minimum(s, n_tiles - 1)
    second = lambda s: jnp.maximum(s - 1, 0)
    return pl.pallas_call(
        kern,
        grid=(n_tiles + 1,),
        in_specs=[pl.BlockSpec((tile, D_MODEL), lambda s: (first(s), 0)),
                  pl.BlockSpec((None, 1, 6 * D_MODEL), lambda s: (first(s) // tiles_per_seq, 0, 0)),
                  pl.BlockSpec((None, 1, 6 * D_MODEL), lambda s: (second(s) // tiles_per_seq, 0, 0))]
                 + [_const_spec(a.shape) for a in consts],
        out_specs=[pl.BlockSpec((tile, D_MODEL), lambda s: (second(s), 0)),
                   pl.BlockSpec((None, 2, CONV_CH), lambda s: (second(s) // tiles_per_seq, 0, 0)),
                   pl.BlockSpec((None, 3, XBC_CH), lambda s: (second(s) // tiles_per_seq, 0, 0)),
                   pl.BlockSpec((None, SSM_CH, SSM_STATE), lambda s: (second(s) // tiles_per_seq, 0, 0))],
        out_shape=[jax.ShapeDtypeStruct((nb * seq, D_MODEL), F32),
                   jax.ShapeDtypeStruct((nb, 2, CONV_CH), F32),
                   jax.ShapeDtypeStruct((nb, 3, XBC_CH), F32),
                   jax.ShapeDtypeStruct((nb, SSM_CH, SSM_STATE), F32)],
        scratch_shapes=[pltpu.VMEM((tile, IN_PAD), F32),
                        pltpu.VMEM((tile, D_MODEL), F32),
                        pltpu.VMEM((SUBLANES, CONV_CH), F32),
                        pltpu.VMEM((SUBLANES, XBC_CH), F32),
                        pltpu.VMEM((SSM_STATE, SSM_CH), F32),
                        pltpu.VMEM((tile, SSM_CH), F32),
                        pltpu.VMEM((tile, 2 * SSM_GROUPS * SSM_STATE), F32),
                        pltpu.VMEM((tile, SSM_CH), F32),
                        pltpu.VMEM((tile, SSM_CH), F32),
                        pltpu.VMEM((tile // SSM_CHUNK * SUBLANES, SSM_CH), F32),
                        pltpu.VMEM((tile // SSM_CHUNK * SSM_HEADS * SUBLANES, SSM_CHUNK), F32),
                        pltpu.VMEM((tile, SSM_GROUPS * SSM_STATE), F32),
                        pltpu.VMEM((tile, SSM_GROUPS * SSM_STATE), BF16),
                        pltpu.VMEM((tile, SSM_CH), F32),
                        pltpu.VMEM((tile, CONV_CH), BF16)],
        compiler_params=pltpu.CompilerParams(dimension_semantics=("arbitrary",),
                                             vmem_limit_bytes=VMEM_LIMIT),
        name="mixer_prompt",
    )(x.reshape(nb * seq, D_MODEL), mod, mod, *consts)


def _ffn_kernel(x_ref, mod_ref, w_up_ref, w_down_ref, ln_g_ref, ln_b_ref, o_ref, r_ref, *, ff_tile):
    s = pl.program_id(0)
    n_tiles = pl.num_programs(0) - 1

    @pl.when(s == 0)
    def _():
        r_ref[...] = jnp.zeros_like(r_ref)

    def norm_previous():
        o_ref[...] = _layer_norm(r_ref[...], ln_g_ref[...], ln_b_ref[...])

    @pl.when(s < n_tiles)
    def _():
        norm_previous()
        x = x_ref[...]
        sh2 = mod_ref[:, 3 * D_MODEL:4 * D_MODEL]
        sc2 = mod_ref[:, 4 * D_MODEL:5 * D_MODEL]
        g2 = mod_ref[:, 5 * D_MODEL:6 * D_MODEL]
        v = (x * (1.0 + sc2) + sh2).astype(BF16)
        acc = jnp.zeros(x.shape, F32)
        for k in range(D_FF // ff_tile):
            h = jnp.maximum(_dot(v, w_up_ref[:, k * ff_tile:(k + 1) * ff_tile]), 0.0)
            acc = acc + _dot((h * h).astype(BF16), w_down_ref[k * ff_tile:(k + 1) * ff_tile, :])
        r_ref[...] = ALPHA * x + (1.0 + g2) * acc

    @pl.when(s == n_tiles)
    def _():
        norm_previous()


def _ffn(x, mod, rows_per_mod, w_up, w_down, ln_g, ln_b, tile, ff_tile=1024):
    rows = x.shape[0]
    mod_rows = mod.shape[1]
    tiles_per_mod = rows_per_mod // tile
    n_tiles = rows // tile
    kern = functools.partial(_ffn_kernel, ff_tile=ff_tile)
    first = lambda s: jnp.minimum(s, n_tiles - 1)
    second = lambda s: jnp.maximum(s - 1, 0)
    return pl.pallas_call(
        kern,
        grid=(n_tiles + 1,),
        in_specs=[pl.BlockSpec((tile, D_MODEL), lambda s: (first(s), 0)),
                  pl.BlockSpec((None, mod_rows, 6 * D_MODEL), lambda s: (first(s) // tiles_per_mod, 0, 0)),
                  _const_spec(w_up.shape), _const_spec(w_down.shape),
                  _const_spec(ln_g.shape), _const_spec(ln_b.shape)],
        out_specs=pl.BlockSpec((tile, D_MODEL), lambda s: (second(s), 0)),
        out_shape=jax.ShapeDtypeStruct((rows, D_MODEL), F32),
        scratch_shapes=[pltpu.VMEM((tile, D_MODEL), F32)],
        compiler_params=pltpu.CompilerParams(dimension_semantics=("arbitrary",),
                                             vmem_limit_bytes=VMEM_LIMIT),
        name="ffn",
    )(x, mod, w_up, w_down, ln_g, ln_b)


def _ffn_stream_kernel(x_ref, mod_ref, w_up_ref, w_down_ref, ln_g_ref, ln_b_ref, o_ref, acc_ref):
    k = pl.program_id(0)

    @pl.when(k == 0)
    def _():
        acc_ref[...] = jnp.zeros_like(acc_ref)

    x = x_ref[...]
    sh2 = mod_ref[:, 3 * D_MODEL:4 * D_MODEL]
    sc2 = mod_ref[:, 4 * D_MODEL:5 * D_MODEL]
    v = (x * (1.0 + sc2) + sh2).astype(BF16)
    h = jnp.maximum(_dot(v, w_up_ref[...]), 0.0)
    acc_ref[...] += _dot((h * h).astype(BF16), w_down_ref[...])

    @pl.when(k == pl.num_programs(0) - 1)
    def _():
        g2 = mod_ref[:, 5 * D_MODEL:6 * D_MODEL]
        o_ref[...] = _layer_norm(ALPHA * x + (1.0 + g2) * acc_ref[...], ln_g_ref[...], ln_b_ref[...])


def _ffn_stream(x, mod, w_up, w_down, ln_g, ln_b, ff_tile=1024):
    rows = x.shape[0]
    return pl.pallas_call(
        _ffn_stream_kernel,
        grid=(D_FF // ff_tile,),
        in_specs=[_const_spec(x.shape), _const_spec(mod.shape),
                  pl.BlockSpec((D_MODEL, ff_tile), lambda k: (0, k)),
                  pl.BlockSpec((ff_tile, D_MODEL), lambda k: (k, 0)),
                  _const_spec(ln_g.shape), _const_spec(ln_b.shape)],
        out_specs=pl.BlockSpec((rows, D_MODEL), lambda k: (0, 0)),
        out_shape=jax.ShapeDtypeStruct((rows, D_MODEL), F32),
        scratch_shapes=[pltpu.VMEM((rows, D_MODEL), F32)],
        compiler_params=pltpu.CompilerParams(dimension_semantics=("arbitrary",),
                                             vmem_limit_bytes=VMEM_LIMIT),
        name="ffn_stream",
    )(x, mod, w_up, w_down, ln_g, ln_b)


_PRODUCT_TERMS = ((0, 0), (0, 1), (1, 0), (0, 2), (2, 0), (1, 1))
UPDATE_TERMS = 16


def _sample_pre_kernel(x_ref, mod_ref, w_in_ref, w_dt_ref, conv_w_ref, conv_nw_ref, sconv_w_ref, sconv_b_ref,
                       dtb_ref, alog_ref, cb0_ref, cb1_ref, sb0_ref, sb1_ref, sb2_ref,
                       yconv_ref, ch_ref, xbc_ref, z_ref, xs_ref, ydiag_ref, decx_ref, cm_ref, lhs_ref, rhs_ref):
    expand = _head_expand()
    reduce = _group_reduce()
    x = x_ref[...]
    sh1 = mod_ref[:, 0:D_MODEL]
    sc1 = mod_ref[:, D_MODEL:2 * D_MODEL]
    u = (x * (1.0 + sc1) + sh1).astype(BF16)

    def proj(lo, width):
        return _dot(u, w_in_ref[:, lo:lo + width])

    ch = proj(COL_GC, CONV_CH) * proj(COL_HV, CONV_CH)
    ch_ref[...] = ch
    cw = conv_w_ref[...]
    cv = cw[0:1, :] * cb0_ref[...] + cw[1:2, :] * cb1_ref[...] + cw[2:3, :] * ch
    yconv_ref[...] = _conv_group_norm(proj(COL_GB, CONV_CH) * cv, conv_nw_ref[...], expand, reduce)

    xbc = proj(COL_XBC, XBC_CH)
    xbc_ref[...] = xbc
    sw = sconv_w_ref[...]
    xc = _silu(sw[0:1, :] * sb0_ref[...] + sw[1:2, :] * sb1_ref[...] + sw[2:3, :] * sb2_ref[...]
               + sw[3:4, :] * xbc + sconv_b_ref[...])
    xs = xc[:, 0:SSM_CH]
    xs_ref[...] = xs
    cm_ref[...] = xc[:, SSM_CH + SSM_GROUPS * SSM_STATE:XBC_CH]
    z_ref[...] = proj(COL_Z, SSM_CH)

    dt = _softplus(_dot(u, w_dt_ref[...]) + dtb_ref[...])
    dta = dt * (-jnp.exp(alog_ref[...]))
    xdt = xs * _dot_f32_lhs(dt, expand)
    decx = jnp.exp(_dot_f32_lhs(dta, expand))
    decx_ref[...] = decx
    xdt_t, dec_t = _split(xdt, 3), _split(decx, 3)
    bm_t = _split(xc[:, SSM_CH:SSM_CH + SSM_GROUPS * SSM_STATE], 3)
    group_of = lax.broadcasted_iota(jnp.int32, xdt.shape, 1) // SSM_GROUP_CH
    zeros = jnp.zeros((x.shape[0], SSM_STATE), BF16)
    r = 0
    for g in range(SSM_GROUPS):
        for tx, tb in _PRODUCT_TERMS:
            lhs_ref[r] = jnp.where(group_of == g, xdt_t[tx], jnp.zeros_like(xdt_t[tx]))
            rhs_ref[r] = jnp.concatenate([bm_t[tb][:, g * SSM_STATE:(g + 1) * SSM_STATE], zeros], axis=1)
            r += 1
    for t in range(3):
        lhs_ref[r] = dec_t[t]
        rhs_ref[r] = jnp.concatenate([zeros, jnp.ones_like(zeros)], axis=1)
        r += 1
    for r in range(r, UPDATE_TERMS):
        lhs_ref[r] = jnp.zeros_like(dec_t[0])
        rhs_ref[r] = jnp.concatenate([zeros, zeros], axis=1)
    for g in range(SSM_GROUPS):
        bm = xc[:, SSM_CH + g * SSM_STATE:SSM_CH + (g + 1) * SSM_STATE]
        cm = xc[:, SSM_CH + (SSM_GROUPS + g) * SSM_STATE:SSM_CH + (SSM_GROUPS + g + 1) * SSM_STATE]
        cb = jnp.sum(cm * bm, axis=-1, keepdims=True)
        gl = g * SSM_GROUP_CH
        ydiag_ref[:, gl:gl + SSM_GROUP_CH] = cb * xdt[:, gl:gl + SSM_GROUP_CH]


def _sample_pre(x, mod, w_in, w_dt, conv_w, conv_nw, sconv_w, sconv_b, dtb, alog, cb0, cb1, sb0, sb1, sb2):
    n = x.shape[0]
    args = (x, mod, w_in, w_dt, conv_w, conv_nw, sconv_w, sconv_b, dtb, alog, cb0, cb1, sb0, sb1, sb2)
    f32_shapes = [(n, CONV_CH), (n, CONV_CH), (n, XBC_CH), (n, SSM_CH), (n, SSM_CH), (n, SSM_CH), (n, SSM_CH),
                  (n, SSM_GROUPS * SSM_STATE)]
    bf16_shapes = [(UPDATE_TERMS, n, SSM_CH), (UPDATE_TERMS, n, 2 * SSM_STATE)]
    return pl.pallas_call(
        _sample_pre_kernel,
        out_shape=[jax.ShapeDtypeStruct(s, F32) for s in f32_shapes]
                  + [jax.ShapeDtypeStruct(s, BF16) for s in bf16_shapes],
        compiler_params=pltpu.CompilerParams(vmem_limit_bytes=VMEM_LIMIT),
        name="sample_pre",
    )(*args)


def _sample_state_kernel(s_ref, lhs_ref, rhs_ref, cm_ref, decx_ref, o_ref, yoff_ref, *, block):
    i = pl.program_id(0)
    rows = UPDATE_TERMS * block
    lhs_t = lhs_ref[...].reshape(rows, SSM_CH).astype(F32).T.astype(BF16)
    rhs_all = rhs_ref[...].reshape(rows, 2 * SSM_STATE)
    token_of = lax.broadcasted_iota(jnp.int32, rhs_all.shape, 0) % block

    def body(k, carry):
        b = i * block + k
        s = s_ref[k]
        upd = _dot(lhs_t, jnp.where(token_of == k, rhs_all, jnp.zeros_like(rhs_all)))
        o_ref[k] = s * upd[:, SSM_STATE:2 * SSM_STATE] + upd[:, 0:SSM_STATE]
        cm = cm_ref[pl.ds(b, 1), :]
        sums = []
        for c0 in range(0, SSM_CH, LANES):
            g = c0 // SSM_GROUP_CH
            prod = s[c0:c0 + LANES, :] * cm[:, g * SSM_STATE:(g + 1) * SSM_STATE]
            sums.append(jnp.sum(prod.T, axis=0, keepdims=True))
        yoff_ref[pl.ds(b, 1), :] = jnp.concatenate(sums, axis=1) * decx_ref[pl.ds(b, 1), :]
        return carry

    lax.fori_loop(0, block, body, 0, unroll=True)


def _sample_state(state, lhs, rhs, cm, decx, block=16):
    n = state.shape[0]
    kern = functools.partial(_sample_state_kernel, block=block)
    return pl.pallas_call(
        kern,
        grid=(n // block,),
        in_specs=[pl.BlockSpec((block, SSM_CH, SSM_STATE), lambda i: (i, 0, 0)),
                  pl.BlockSpec((UPDATE_TERMS, block, SSM_CH), lambda i: (0, i, 0)),
                  pl.BlockSpec((UPDATE_TERMS, block, 2 * SSM_STATE), lambda i: (0, i, 0)),
                  _const_spec(cm.shape), _const_spec(decx.shape)],
        out_specs=[pl.BlockSpec((block, SSM_CH, SSM_STATE), lambda i: (i, 0, 0)),
                   pl.BlockSpec((n, SSM_CH), lambda i: (0, 0))],
        out_shape=[jax.ShapeDtypeStruct(state.shape, F32), jax.ShapeDtypeStruct((n, SSM_CH), F32)],
        compiler_params=pltpu.CompilerParams(dimension_semantics=("arbitrary",),
                                             vmem_limit_bytes=VMEM_LIMIT),
        name="sample_state",
    )(state, lhs, rhs, cm, decx)


def _sample_post_kernel(x_ref, mod_ref, yconv_ref, ydiag_ref, yoff_ref, xs_ref, z_ref, dexp_ref, snw_ref,
                        w_out_ref, ln_g_ref, ln_b_ref, x1_ref):
    g1 = mod_ref[:, 2 * D_MODEL:3 * D_MODEL]
    y = ydiag_ref[...] + yoff_ref[...] + xs_ref[...] * dexp_ref[...]
    y = y * _silu(z_ref[...])
    m = _mix_out(yconv_ref[...], _ssm_group_norm(y, snw_ref[...]), w_out_ref)
    x1_ref[...] = _layer_norm(ALPHA * x_ref[...] + (1.0 + g1) * m, ln_g_ref[...], ln_b_ref[...])


def _sample_post(x, mod, yconv, ydiag, yoff, xs, z, dexp, snw, w_out, ln_g, ln_b):
    return pl.pallas_call(
        _sample_post_kernel,
        out_shape=jax.ShapeDtypeStruct(x.shape, F32),
        compiler_params=pltpu.CompilerParams(vmem_limit_bytes=VMEM_LIMIT),
        name="sample_post",
    )(x, mod, yconv, ydiag, yoff, xs, z, dexp, snw, w_out, ln_g, ln_b)


def kernel(x_prompt, x_sample, state_conv, state_ssm_conv, state_ssm, c_prompt, c_sample, w_ada, b_ada, w_in, conv_w, conv_norm_w, ssm_conv_w, ssm_conv_b, dt_bias, a_log, d_skip, ssm_norm_w, w_out, ln1_g, ln1_b, w_up, w_down, ln2_g, ln2_b):
    assert w_ada.shape[0] == 1, "single-layer trunk"
    nb, seq, _ = x_prompt.shape
    ns = x_sample.shape[0]
    row = lambda a: a.reshape(1, -1)
    pad_heads = lambda a: jnp.pad(a.reshape(1, -1), ((0, 0), (0, LANES - SSM_HEADS)))

    w_in_b, w_dt_b = _cast_in_proj(w_in[0])
    w_out_b = w_out[0].astype(BF16)
    w_up_b = w_up[0].astype(BF16)
    w_down_b = w_down[0].astype(BF16)
    conv_nw, sconv_b, snw = row(conv_norm_w[0]), row(ssm_conv_b[0]), row(ssm_norm_w[0])
    dtb, alog = pad_heads(dt_bias[0]), pad_heads(a_log[0])
    dexp = row(jnp.repeat(d_skip[0], SSM_HEAD_DIM))
    g1, b1, g2, b2 = row(ln1_g[0]), row(ln1_b[0]), row(ln2_g[0]), row(ln2_b[0])

    mod_p, mod_s = _ada(c_sample, c_prompt, w_ada[0], row(b_ada[0]))
    mod_p = mod_p.reshape(nb, 1, 6 * D_MODEL)

    x1_p, cst_p, scst_p, sst_p = _mixer_prompt(x_prompt, mod_p, w_in_b, w_dt_b, conv_w[0], conv_nw, ssm_conv_w[0],
                                               sconv_b, dtb, alog, dexp, snw, w_out_b, g1, b1)
    y_p = _ffn(x1_p, mod_p, seq, w_up_b, w_down_b, g2, b2, tile=512)

    xs2 = x_sample.reshape(ns, D_MODEL)
    (yconv_s, ch_s, xbc_s, z_s, xs_s, ydiag_s, decx_s, cm_s, lhs_s, rhs_s) = _sample_pre(
        xs2, mod_s, w_in_b, w_dt_b, conv_w[0], conv_nw, ssm_conv_w[0], sconv_b, dtb, alog,
        state_conv[0, :, 0], state_conv[0, :, 1],
        state_ssm_conv[0, :, 0], state_ssm_conv[0, :, 1], state_ssm_conv[0, :, 2])
    new_state_s, yoff_s = _sample_state(state_ssm[0].reshape(ns, SSM_CH, SSM_STATE), lhs_s, rhs_s, cm_s, decx_s)
    x1_s = _sample_post(xs2, mod_s, yconv_s, ydiag_s, yoff_s, xs_s, z_s, dexp, snw, w_out_b, g1, b1)
    y_s = _ffn_stream(x1_s, mod_s, w_up_b, w_down_b, g2, b2)

    return (y_p.reshape(nb, seq, D_MODEL),
            y_s.reshape(ns, 1, D_MODEL),
            cst_p[None],
            scst_p[None],
            sst_p.reshape(1, nb, SSM_HEADS, SSM_HEAD_DIM, SSM_STATE),
            jnp.stack([state_conv[0, :, 1], ch_s], axis=1)[None],
            jnp.stack([state_ssm_conv[0, :, 1], state_ssm_conv[0, :, 2], xbc_s], axis=1)[None],
            new_state_s.reshape(1, ns, SSM_HEADS, SSM_HEAD_DIM, SSM_STATE))
```

```python
import functools

import jax
import jax.numpy as jnp
import numpy as np
from jax import lax
from jax.experimental import pallas as pl
from jax.experimental.pallas import tpu as pltpu

F32 = jnp.float32
BF16 = jnp.bfloat16

D_MODEL = 1024
CONV_CH = 1024
CONV_GROUP = 64
SSM_CH = 1024
SSM_HEADS = 16
SSM_HEAD_DIM = 64
SSM_GROUPS = 2
SSM_GROUP_CH = SSM_CH // SSM_GROUPS
SSM_STATE = 128
SSM_CHUNK = 128
XBC_CH = SSM_CH + 2 * SSM_GROUPS * SSM_STATE
D_FF = 4 * D_MODEL
LANES = 128
SUBLANES = 8
MXU_COLS = 256
COL_GB, COL_GC, COL_HV, COL_Z, COL_XBC = 0, 1024, 2048, 3072, 4096
COL_DT = COL_XBC + XBC_CH
IN_COLS = COL_DT + SSM_HEADS
IN_PAD = COL_DT + LANES
N_PIECES = COL_DT // MXU_COLS + 1
ALPHA = 2.0 ** 0.25
LN_EPS = 1e-5
RMS_EPS = 1e-5
VMEM_LIMIT = 56 * 1024 * 1024


def _dot(a, b):
    return jnp.dot(a, b, preferred_element_type=F32)


def _split(a, terms):
    parts = []
    r = a
    for t in range(terms):
        p = r.astype(BF16)
        parts.append(p)
        if t + 1 < terms:
            r = r - p.astype(F32)
    return parts


def _dot_f32_lhs(a, b_exact, terms=3):
    parts = _split(a, terms)
    out = _dot(parts[0], b_exact)
    for p in parts[1:]:
        out = out + _dot(p, b_exact)
    return out


def _dot_f32_rhs(a_exact, b, terms=3):
    parts = _split(b, terms)
    out = _dot(a_exact, parts[0])
    for p in parts[1:]:
        out = out + _dot(a_exact, p)
    return out


def _head_expand(xp=jnp):
    if xp is np:
        return jnp.asarray(np.arange(SSM_CH)[None, :] // SSM_HEAD_DIM == np.arange(LANES)[:, None], BF16)
    h = lax.broadcasted_iota(jnp.int32, (LANES, SSM_CH), 0)
    c = lax.broadcasted_iota(jnp.int32, (LANES, SSM_CH), 1)
    return (c // SSM_HEAD_DIM == h).astype(BF16)


def _group_reduce(xp=jnp):
    if xp is np:
        return jnp.asarray(np.arange(CONV_CH)[:, None] // CONV_GROUP == np.arange(LANES)[None, :], BF16)
    c = lax.broadcasted_iota(jnp.int32, (CONV_CH, LANES), 0)
    k = lax.broadcasted_iota(jnp.int32, (CONV_CH, LANES), 1)
    return (c // CONV_GROUP == k).astype(BF16)


def _sigmoid(x):
    return 1.0 / (1.0 + jnp.exp(-x))


def _silu(x):
    return x * _sigmoid(x)


def _softplus(x):
    return jnp.maximum(x, 0.0) + jnp.log1p(jnp.exp(-jnp.abs(x)))


def _layer_norm(r, g, b):
    mu = jnp.mean(r, axis=-1, keepdims=True)
    d = r - mu
    var = jnp.mean(d * d, axis=-1, keepdims=True)
    return d * lax.rsqrt(var + LN_EPS) * g + b


def _conv_group_norm(prod, w, expand, reduce):
    ssum = _dot_f32_lhs(prod * prod, reduce, terms=2)
    rstd = lax.rsqrt(ssum * (1.0 / CONV_GROUP) + RMS_EPS)
    return prod * _dot_f32_lhs(rstd, expand, terms=2) * w


def _ssm_group_norm(y, w):
    outs = []
    for g in range(SSM_GROUPS):
        yg = y[:, g * SSM_GROUP_CH:(g + 1) * SSM_GROUP_CH]
        ms = jnp.mean(yg * yg, axis=-1, keepdims=True)
        outs.append((yg * lax.rsqrt(ms + RMS_EPS) * w[:, g * SSM_GROUP_CH:(g + 1) * SSM_GROUP_CH]).astype(BF16))
    return outs


def _mix_out(y_conv, y_ssm_groups, w_out_ref):
    m = _dot(y_conv.astype(BF16), w_out_ref[0:CONV_CH, :])
    for g, yg in enumerate(y_ssm_groups):
        lo = CONV_CH + g * SSM_GROUP_CH
        m = m + _dot(yg, w_out_ref[lo:lo + SSM_GROUP_CH, :])
    return m


def _ada_kernel(cs_ref, cp_ref, w_ref, b_ref, op_ref, os_ref):
    c = jnp.concatenate([cs_ref[...], cp_ref[...]], axis=0)
    w = w_ref[...]
    c_hi = c.astype(BF16)
    c_lo = (c - c_hi.astype(F32)).astype(BF16)
    w_hi = w.astype(BF16)
    mod = _dot(c_hi, w_hi) + _dot(c_lo, w_hi) + b_ref[...]
    n_sample = os_ref.shape[0]
    os_ref[...] = mod[0:n_sample, :]
    op_ref[...] = mod[n_sample:, :]


def _ada(c_sample, c_prompt, w_ada, b_ada, tile_n=2048):
    ns, nb = c_sample.shape[0], c_prompt.shape[0]
    n = w_ada.shape[1]
    return pl.pallas_call(
        _ada_kernel,
        grid=(n // tile_n,),
        in_specs=[pl.BlockSpec((ns, D_MODEL), lambda i: (0, 0)),
                  pl.BlockSpec((nb, D_MODEL), lambda i: (0, 0)),
                  pl.BlockSpec((D_MODEL, tile_n), lambda i: (0, i)),
                  pl.BlockSpec((1, tile_n), lambda i: (0, i))],
        out_specs=[pl.BlockSpec((nb, tile_n), lambda i: (0, i)),
                   pl.BlockSpec((ns, tile_n), lambda i: (0, i))],
        out_shape=[jax.ShapeDtypeStruct((nb, n), F32), jax.ShapeDtypeStruct((ns, n), F32)],
        compiler_params=pltpu.CompilerParams(vmem_limit_bytes=VMEM_LIMIT),
        name="ada_mod",
    )(c_sample, c_prompt, w_ada, b_ada)


def _cast_transposed_kernel(wt_ref, o_ref):
    rows = wt_ref.shape[0]
    wt = wt_ref[...]
    if rows < o_ref.shape[1]:
        wt = jnp.concatenate([wt, jnp.zeros((o_ref.shape[1] - rows, wt.shape[1]), wt.dtype)], axis=0)
    o_ref[...] = wt.T.astype(o_ref.dtype)


def _cast_in_proj(w):
    wt = w.T
    n_dt = w.shape[1] - COL_DT
    cols = COL_DT // 4
    assert cols % LANES == 0
    main = pl.pallas_call(
        _cast_transposed_kernel,
        grid=(COL_DT // cols,),
        in_specs=[pl.BlockSpec((cols, D_MODEL), lambda j: (j, 0))],
        out_specs=pl.BlockSpec((D_MODEL, cols), lambda j: (0, j)),
        out_shape=jax.ShapeDtypeStruct((D_MODEL, COL_DT), BF16),
        compiler_params=pltpu.CompilerParams(vmem_limit_bytes=VMEM_LIMIT),
        name="cast_in_proj",
    )(wt)
    dt = pl.pallas_call(
        _cast_transposed_kernel,
        grid=(1,),
        in_specs=[pl.BlockSpec((n_dt, D_MODEL), lambda j: (COL_DT // n_dt, 0))],
        out_specs=pl.BlockSpec((D_MODEL, LANES), lambda j: (0, 0)),
        out_shape=jax.ShapeDtypeStruct((D_MODEL, LANES), BF16),
        name="cast_in_proj_dt",
    )(wt)
    return main, dt


def _mixer_prompt_kernel(xa_ref, moda_ref, modb_ref, w_in_ref, w_dt_ref, expand_ref, reduce_ref,
                         conv_w_ref, conv_nw_ref, sconv_w_ref, sconv_b_ref,
                         dtb_ref, alog_ref, dexp_ref, snw_ref, w_out_ref, ln_g_ref, ln_b_ref,
                         x1_ref, cst_ref, scst_ref, sst_ref,
                         p, xk, cbuf, xbuf, st_ref, xs_ref, bc_ref, dtx_ref, acsx_ref, endx_ref,
                         acst_ref, cb_ref, bmt_ref, y_ref, yc_ref,
                         *, tile, tiles_per_seq, sched):
    s = pl.program_id(0)
    jb = lax.rem(s + (tiles_per_seq - 1), tiles_per_seq)

    @pl.when(s == 0)
    def _():
        p[...] = jnp.zeros_like(p)
        xk[...] = jnp.zeros_like(xk)

    @pl.when((jb == 0) | (s == 0))
    def _():
        cbuf[...] = jnp.zeros_like(cbuf)
        xbuf[...] = jnp.zeros_like(xbuf)
        st_ref[...] = jnp.zeros_like(st_ref)

    def stages():
        xa = xa_ref[...]
        u = (xa * (1.0 + moda_ref[:, D_MODEL:2 * D_MODEL]) + moda_ref[:, 0:D_MODEL]).astype(BF16)
        free = []

        def first_stage(n):
            for _ in range(min(n, len(free))):
                lo = free.pop(0)
                if lo == COL_DT:
                    p[:, COL_DT:IN_PAD] = _dot(u, w_dt_ref[...])
                else:
                    p[:, lo:lo + MXU_COLS] = _dot(u, w_in_ref[:, lo:lo + MXU_COLS])

        expand = expand_ref[...]
        x = xk[...]
        g1 = modb_ref[:, 2 * D_MODEL:3 * D_MODEL]

        def proj(lo, width):
            return p[:, lo:lo + width]

        def delayed(tail_ref, cs, cur, taps):
            seq = jnp.concatenate([tail_ref[:, cs], cur], axis=0)
            tail_ref[:, cs] = cur[tile - SUBLANES:, :]
            return [pltpu.roll(seq, k, axis=0)[SUBLANES:, :] for k in range(1, taps + 1)]

        for k in range(CONV_CH // MXU_COLS):
            c0 = k * MXU_COLS
            cs = slice(c0, c0 + MXU_COLS)
            ch = proj(COL_GC + c0, MXU_COLS) * proj(COL_HV + c0, MXU_COLS)
            ch1, ch2 = delayed(cbuf, cs, ch, 2)
            cv = conv_w_ref[0:1, cs] * ch2 + conv_w_ref[1:2, cs] * ch1 + conv_w_ref[2:3, cs] * ch
            prod = proj(COL_GB + c0, MXU_COLS) * cv
            free.extend((COL_GC + c0, COL_HV + c0, COL_GB + c0))
            first_stage(sched[0])
            ssum = _dot_f32_lhs(prod * prod, reduce_ref[cs, :], terms=1)
            rstd = lax.rsqrt(ssum * (1.0 / CONV_GROUP) + RMS_EPS)
            yc_ref[:, cs] = (prod * _dot_f32_lhs(rstd, expand_ref[:, cs], terms=2)
                             * conv_nw_ref[:, cs]).astype(BF16)

        def pre_conv(c0):
            cs = slice(c0, c0 + MXU_COLS)
            xbc = proj(COL_XBC + c0, MXU_COLS)
            free.append(COL_XBC + c0)
            x1, x2, x3 = delayed(xbuf, cs, xbc, 3)
            return _silu(sconv_w_ref[0:1, cs] * x3 + sconv_w_ref[1:2, cs] * x2 + sconv_w_ref[2:3, cs] * x1
                         + sconv_w_ref[3:4, cs] * xbc + sconv_b_ref[:, cs])

        row = lax.broadcasted_iota(jnp.int32, (SSM_CHUNK, SSM_CHUNK), 0)
        col = lax.broadcasted_iota(jnp.int32, (SSM_CHUNK, SSM_CHUNK), 1)
        causal = row >= col
        tri = causal.astype(BF16)
        groups = SSM_CHUNK // SUBLANES
        causal_bias = jnp.where(causal, 0.0, -jnp.inf).reshape(groups, SUBLANES, SSM_CHUNK)
        first_half = (col < SSM_HEAD_DIM).reshape(groups, SUBLANES, SSM_CHUNK)
        half_rows = col < SSM_HEAD_DIM
        chunks = [slice(c * SSM_CHUNK, (c + 1) * SSM_CHUNK) for c in range(tile // SSM_CHUNK)]

        dt = _softplus(proj(COL_DT, LANES) + dtb_ref[...])
        free.append(COL_DT)
        first_stage(sched[1])
        dta = dt * (-jnp.exp(alog_ref[...]))
        dtx_ref[...] = _dot_f32_lhs(dt, expand, terms=1)
        for c, rows in enumerate(chunks):
            acs = _dot_f32_rhs(tri, dta[rows, :])
            acs_t = acs.T
            for h in range(SSM_HEADS):
                r8 = (c * SSM_HEADS + h) * SUBLANES
                acst_ref[r8:r8 + SUBLANES, :] = jnp.broadcast_to(acs_t[h:h + 1, :], (SUBLANES, SSM_CHUNK))
            acs_x = _dot_f32_lhs(acs, expand, terms=2)
            acsx_ref[rows, :] = acs_x
            endx_ref[c * SUBLANES:(c + 1) * SUBLANES, :] = jnp.broadcast_to(acs_x[SSM_CHUNK - 1:SSM_CHUNK, :],
                                                                             (SUBLANES, SSM_CH))
        for c0 in range(SSM_CH, XBC_CH, MXU_COLS):
            first_stage(sched[2])
            bc_ref[:, c0 - SSM_CH:c0 - SSM_CH + MXU_COLS] = pre_conv(c0)
        for rows in chunks:
            for g in range(SSM_GROUPS):
                gs = slice(g * SSM_STATE, (g + 1) * SSM_STATE)
                bm = bc_ref[rows, gs]
                cm = bc_ref[rows, (SSM_GROUPS + g) * SSM_STATE:(SSM_GROUPS + g + 1) * SSM_STATE]
                cb_ref[rows, gs] = lax.dot_general(cm.astype(BF16), bm.astype(BF16), (((1,), (1,)), ((), ())),
                                                   preferred_element_type=F32)
                bmt_ref[rows, gs] = bm.T.astype(BF16)

        for c0 in range(0, SSM_CH, MXU_COLS):
            first_stage(sched[3])
            cs = slice(c0, c0 + MXU_COLS)
            g = c0 // SSM_GROUP_CH
            gs = slice(g * SSM_STATE, (g + 1) * SSM_STATE)
            xs = pre_conv(c0)
            xs_ref[:, cs] = xs
            xdt = xs * dtx_ref[:, cs]
            for c, rows in enumerate(chunks):
                first_stage(sched[4])
                acs_x = acsx_ref[rows, cs].reshape(groups, SUBLANES, MXU_COLS)
                end_x = endx_ref[c * SUBLANES:(c + 1) * SUBLANES, cs]
                xdt_c = xdt[rows, :]
                xdec = (xdt_c * jnp.exp(end_x[None] - acs_x).reshape(SSM_CHUNK, MXU_COLS)).astype(BF16)
                cm = bc_ref[rows, (SSM_GROUPS + g) * SSM_STATE:(SSM_GROUPS + g + 1) * SSM_STATE].astype(BF16)
                cb = cb_ref[rows, gs]
                st = st_ref[:, cs]
                y_off = _dot(cm, st.astype(BF16)) * jnp.exp(acs_x).reshape(SSM_CHUNK, MXU_COLS)
                st_ref[:, cs] = ((st.reshape(groups, SUBLANES, MXU_COLS) * jnp.exp(end_x)[None])
                                 .reshape(SSM_STATE, MXU_COLS) + _dot(bmt_ref[rows, gs], xdec))
                for lo in range(0, MXU_COLS, LANES):
                    h0 = (c * SSM_HEADS + (c0 + lo) // SSM_HEAD_DIM) * SUBLANES
                    slab = acs_x[:, :, lo:lo + LANES]
                    rolled = pltpu.roll(slab, SSM_HEAD_DIM, axis=2)
                    a0 = jnp.where(first_half, slab, rolled) - acst_ref[h0:h0 + SUBLANES, :][None]
                    a1 = jnp.where(first_half, rolled, slab) - acst_ref[h0 + SUBLANES:h0 + 2 * SUBLANES, :][None]
                    l0 = jnp.exp(a0 + causal_bias).reshape(SSM_CHUNK, SSM_CHUNK)
                    l1 = jnp.exp(a1 + causal_bias).reshape(SSM_CHUNK, SSM_CHUNK)
                    m = jnp.concatenate([(cb * l0).astype(BF16), (cb * l1).astype(BF16)], axis=1)
                    xp = xdt_c[:, lo:lo + LANES]
                    rhs = jnp.concatenate([jnp.where(half_rows, xp, 0.0), jnp.where(half_rows, 0.0, xp)],
                                          axis=0).astype(BF16)
                    y_ref[rows, c0 + lo:c0 + lo + LANES] = _dot(m, rhs) + y_off[:, lo:lo + LANES]

        for k in range(SSM_CH // MXU_COLS):
            first_stage(sched[5])
            c0 = k * MXU_COLS
            cs = slice(c0, c0 + MXU_COLS)
            y_ref[:, cs] = (y_ref[:, cs] + xs_ref[:, cs] * dexp_ref[:, cs]) * _silu(proj(COL_Z + c0, MXU_COLS))
            free.append(COL_Z + c0)
        first_stage(sched[6])
        m = _mix_out(yc_ref[...], _ssm_group_norm(y_ref[...], snw_ref[...]), w_out_ref)
        x1_ref[...] = _layer_norm(ALPHA * x + (1.0 + g1) * m, ln_g_ref[...], ln_b_ref[...])
        first_stage(N_PIECES)
        assert not free
        xk[...] = xa

    stages()

    @pl.when((jb == tiles_per_seq - 1) & (s > 0))
    def _():
        cst_ref[...] = cbuf[SUBLANES - 2:SUBLANES, :]
        scst_ref[...] = xbuf[SUBLANES - 3:SUBLANES, :]
        sst_ref[...] = st_ref[...].T


def _const_spec(shape):
    return pl.BlockSpec(shape, lambda *_: (0,) * len(shape), pipeline_mode=pl.Buffered(1))


def _mixer_prompt(x, mod, w_in, w_dt, conv_w, conv_nw, sconv_w, sconv_b, dtb, alog, dexp, snw, w_out, ln_g, ln_b,
                  tile=256, sched=(1, 0, 0, 2, 1, 0, 0)):
    assert CONV_GROUP == SSM_HEAD_DIM and CONV_CH == SSM_CH
    nb, seq, _ = x.shape
    tiles_per_seq = seq // tile
    n_tiles = nb * tiles_per_seq
    kern = functools.partial(_mixer_prompt_kernel, tile=tile, tiles_per_seq=tiles_per_seq, sched=sched)
    consts = [w_in, w_dt, _head_expand(np), _group_reduce(np), conv_w, conv_nw, sconv_w, sconv_b, dtb, alog, dexp, snw,
              w_out, ln_g, ln_b]
    first = lambda s: jnp.minimum(s, n_tiles - 1)
    second = lambda s: jnp.maximum(s - 1, 0)
    return pl.pallas_call(
        kern,
        grid=(n_tiles + 1,),
        in_specs=[pl.BlockSpec((tile, D_MODEL), lambda s: (first(s), 0)),
                  pl.BlockSpec((None, 1, 6 * D_MODEL), lambda s: (first(s) // tiles_per_seq, 0, 0)),
                  pl.BlockSpec((None, 1, 6 * D_MODEL), lambda s: (second(s) // tiles_per_seq, 0, 0))]
                 + [_const_spec(a.shape) for a in consts],
        out_specs=[pl.BlockSpec((tile, D_MODEL), lambda s: (second(s), 0)),
                   pl.BlockSpec((None, 2, CONV_CH), lambda s: (second(s) // tiles_per_seq, 0, 0)),
                   pl.BlockSpec((None, 3, XBC_CH), lambda s: (second(s) // tiles_per_seq, 0, 0)),
                   pl.BlockSpec((None, SSM_CH, SSM_STATE), lambda s: (second(s) // tiles_per_seq, 0, 0))],
        out_shape=[jax.ShapeDtypeStruct((nb * seq, D_MODEL), F32),
                   jax.ShapeDtypeStruct((nb, 2, CONV_CH), F32),
                   jax.ShapeDtypeStruct((nb, 3, XBC_CH), F32),
                   jax.ShapeDtypeStruct((nb, SSM_CH, SSM_STATE), F32)],
        scratch_shapes=[pltpu.VMEM((tile, IN_PAD), F32),
                        pltpu.VMEM((tile, D_MODEL), F32),
                        pltpu.VMEM((SUBLANES, CONV_CH), F32),
                        pltpu.VMEM((SUBLANES, XBC_CH), F32),
                        pltpu.VMEM((SSM_STATE, SSM_CH), F32),
                        pltpu.VMEM((tile, SSM_CH), F32),
                        pltpu.VMEM((tile, 2 * SSM_GROUPS * SSM_STATE), F32),
                        pltpu.VMEM((tile, SSM_CH), F32),
                        pltpu.VMEM((tile, SSM_CH), F32),
                        pltpu.VMEM((tile // SSM_CHUNK * SUBLANES, SSM_CH), F32),
                        pltpu.VMEM((tile // SSM_CHUNK * SSM_HEADS * SUBLANES, SSM_CHUNK), F32),
                        pltpu.VMEM((tile, SSM_GROUPS * SSM_STATE), F32),
                        pltpu.VMEM((tile, SSM_GROUPS * SSM_STATE), BF16),
                        pltpu.VMEM((tile, SSM_CH), F32),
                        pltpu.VMEM((tile, CONV_CH), BF16)],
        compiler_params=pltpu.CompilerParams(dimension_semantics=("arbitrary",),
                                             vmem_limit_bytes=VMEM_LIMIT),
        name="mixer_prompt",
    )(x.reshape(nb * seq, D_MODEL), mod, mod, *consts)


def _ffn_state_kernel(x_ref, mod_ref, w_up_ref, w_down_ref, ln_g_ref, ln_b_ref,
                      s_ref, lhs_ref, rhs_ref, cm_ref, decx_ref,
                      o_ref, so_ref, yoff_ref, r_ref, *, ff_tile, state_block, state_steps):
    s = pl.program_id(0)
    n_tiles = pl.num_programs(0) - 1

    @pl.when(s == 0)
    def _():
        r_ref[...] = jnp.zeros_like(r_ref)

    def norm_previous():
        o_ref[...] = _layer_norm(r_ref[...], ln_g_ref[...], ln_b_ref[...])

    @pl.when(s < n_tiles)
    def _():
        norm_previous()
        x = x_ref[...]
        sh2 = mod_ref[:, 3 * D_MODEL:4 * D_MODEL]
        sc2 = mod_ref[:, 4 * D_MODEL:5 * D_MODEL]
        g2 = mod_ref[:, 5 * D_MODEL:6 * D_MODEL]
        v = (x * (1.0 + sc2) + sh2).astype(BF16)
        acc = jnp.zeros(x.shape, F32)
        for k in range(D_FF // ff_tile):
            h = jnp.maximum(_dot(v, w_up_ref[:, k * ff_tile:(k + 1) * ff_tile]), 0.0)
            acc = acc + _dot((h * h).astype(BF16), w_down_ref[k * ff_tile:(k + 1) * ff_tile, :])
        r_ref[...] = ALPHA * x + (1.0 + g2) * acc

    @pl.when(s == n_tiles)
    def _():
        norm_previous()

    @pl.when(s < state_steps)
    def _():
        _state_update(s, s_ref, lhs_ref, rhs_ref, cm_ref, decx_ref, so_ref, yoff_ref, state_block)


def _ffn_and_state(x, mod, rows_per_mod, w_up, w_down, ln_g, ln_b, state, lhs, rhs, cm, decx,
                   tile=512, ff_tile=1024, state_block=8):
    rows = x.shape[0]
    n_tok = state.shape[0]
    tiles_per_mod = rows_per_mod // tile
    n_tiles = rows // tile
    state_steps = n_tok // state_block
    assert state_steps <= n_tiles + 1
    kern = functools.partial(_ffn_state_kernel, ff_tile=ff_tile, state_block=state_block, state_steps=state_steps)
    first = lambda s: jnp.minimum(s, n_tiles - 1)
    second = lambda s: jnp.maximum(s - 1, 0)
    tokens = lambda s: jnp.minimum(s, state_steps - 1)
    return pl.pallas_call(
        kern,
        grid=(n_tiles + 1,),
        in_specs=[pl.BlockSpec((tile, D_MODEL), lambda s: (first(s), 0)),
                  pl.BlockSpec((None, 1, 6 * D_MODEL), lambda s: (first(s) // tiles_per_mod, 0, 0)),
                  _const_spec(w_up.shape), _const_spec(w_down.shape),
                  _const_spec(ln_g.shape), _const_spec(ln_b.shape),
                  pl.BlockSpec((state_block, SSM_CH, SSM_STATE), lambda s: (tokens(s), 0, 0)),
                  pl.BlockSpec((UPDATE_TERMS, state_block, SSM_CH), lambda s: (0, tokens(s), 0)),
                  pl.BlockSpec((UPDATE_TERMS, state_block, 2 * SSM_STATE), lambda s: (0, tokens(s), 0)),
                  _const_spec(cm.shape), _const_spec(decx.shape)],
        out_specs=[pl.BlockSpec((tile, D_MODEL), lambda s: (second(s), 0)),
                   pl.BlockSpec((state_block, SSM_CH, SSM_STATE), lambda s: (tokens(s), 0, 0)),
                   pl.BlockSpec((n_tok, SSM_CH), lambda s: (0, 0))],
        out_shape=[jax.ShapeDtypeStruct((rows, D_MODEL), F32), jax.ShapeDtypeStruct(state.shape, F32),
                   jax.ShapeDtypeStruct((n_tok, SSM_CH), F32)],
        scratch_shapes=[pltpu.VMEM((tile, D_MODEL), F32)],
        compiler_params=pltpu.CompilerParams(dimension_semantics=("arbitrary",),
                                             vmem_limit_bytes=VMEM_LIMIT),
        name="ffn_and_state",
    )(x, mod, w_up, w_down, ln_g, ln_b, state, lhs, rhs, cm, decx)


def _ffn_stream_kernel(x_ref, mod_ref, w_up_ref, w_down_ref, ln_g_ref, ln_b_ref, o_ref, acc_ref):
    k = pl.program_id(0)

    @pl.when(k == 0)
    def _():
        acc_ref[...] = jnp.zeros_like(acc_ref)

    x = x_ref[...]
    sh2 = mod_ref[:, 3 * D_MODEL:4 * D_MODEL]
    sc2 = mod_ref[:, 4 * D_MODEL:5 * D_MODEL]
    v = (x * (1.0 + sc2) + sh2).astype(BF16)
    h = jnp.maximum(_dot(v, w_up_ref[...]), 0.0)
    acc_ref[...] += _dot((h * h).astype(BF16), w_down_ref[...])

    @pl.when(k == pl.num_programs(0) - 1)
    def _():
        g2 = mod_ref[:, 5 * D_MODEL:6 * D_MODEL]
        o_ref[...] = _layer_norm(ALPHA * x + (1.0 + g2) * acc_ref[...], ln_g_ref[...], ln_b_ref[...])


def _ffn_stream(x, mod, w_up, w_down, ln_g, ln_b, ff_tile=1024):
    rows = x.shape[0]
    return pl.pallas_call(
        _ffn_stream_kernel,
        grid=(D_FF // ff_tile,),
        in_specs=[_const_spec(x.shape), _const_spec(mod.shape),
                  pl.BlockSpec((D_MODEL, ff_tile), lambda k: (0, k)),
                  pl.BlockSpec((ff_tile, D_MODEL), lambda k: (k, 0)),
                  _const_spec(ln_g.shape), _const_spec(ln_b.shape)],
        out_specs=pl.BlockSpec((rows, D_MODEL), lambda k: (0, 0)),
        out_shape=jax.ShapeDtypeStruct((rows, D_MODEL), F32),
        scratch_shapes=[pltpu.VMEM((rows, D_MODEL), F32)],
        compiler_params=pltpu.CompilerParams(dimension_semantics=("arbitrary",),
                                             vmem_limit_bytes=VMEM_LIMIT),
        name="ffn_stream",
    )(x, mod, w_up, w_down, ln_g, ln_b)


_PRODUCT_TERMS = ((0, 0), (0, 1), (1, 0), (0, 2), (2, 0), (1, 1))
UPDATE_TERMS = 16


def _sample_pre_kernel(x_ref, mod_ref, w_in_ref, w_dt_ref, conv_w_ref, conv_nw_ref, sconv_w_ref, sconv_b_ref,
                       dtb_ref, alog_ref, cb0_ref, cb1_ref, sb0_ref, sb1_ref, sb2_ref,
                       yconv_ref, ch_ref, xbc_ref, z_ref, xs_ref, ydiag_ref, decx_ref, cm_ref, lhs_ref, rhs_ref):
    expand = _head_expand()
    reduce = _group_reduce()
    x = x_ref[...]
    sh1 = mod_ref[:, 0:D_MODEL]
    sc1 = mod_ref[:, D_MODEL:2 * D_MODEL]
    u = (x * (1.0 + sc1) + sh1).astype(BF16)

    def proj(lo, width):
        return _dot(u, w_in_ref[:, lo:lo + width])

    ch = proj(COL_GC, CONV_CH) * proj(COL_HV, CONV_CH)
    ch_ref[...] = ch
    cw = conv_w_ref[...]
    cv = cw[0:1, :] * cb0_ref[...] + cw[1:2, :] * cb1_ref[...] + cw[2:3, :] * ch
    yconv_ref[...] = _conv_group_norm(proj(COL_GB, CONV_CH) * cv, conv_nw_ref[...], expand, reduce)

    xbc = proj(COL_XBC, XBC_CH)
    xbc_ref[...] = xbc
    sw = sconv_w_ref[...]
    xc = _silu(sw[0:1, :] * sb0_ref[...] + sw[1:2, :] * sb1_ref[...] + sw[2:3, :] * sb2_ref[...]
               + sw[3:4, :] * xbc + sconv_b_ref[...])
    xs = xc[:, 0:SSM_CH]
    xs_ref[...] = xs
    cm_ref[...] = xc[:, SSM_CH + SSM_GROUPS * SSM_STATE:XBC_CH]
    z_ref[...] = proj(COL_Z, SSM_CH)

    dt = _softplus(_dot(u, w_dt_ref[...]) + dtb_ref[...])
    dta = dt * (-jnp.exp(alog_ref[...]))
    xdt = xs * _dot_f32_lhs(dt, expand)
    decx = jnp.exp(_dot_f32_lhs(dta, expand))
    decx_ref[...] = decx
    xdt_t, dec_t = _split(xdt, 3), _split(decx, 3)
    bm_t = _split(xc[:, SSM_CH:SSM_CH + SSM_GROUPS * SSM_STATE], 3)
    group_of = lax.broadcasted_iota(jnp.int32, xdt.shape, 1) // SSM_GROUP_CH
    zeros = jnp.zeros((x.shape[0], SSM_STATE), F32)
    r = 0
    for g in range(SSM_GROUPS):
        for tx, tb in _PRODUCT_TERMS:
            lhs_ref[r] = jnp.where(group_of == g, xdt_t[tx].astype(F32), 0.0)
            rhs_ref[r] = jnp.concatenate([bm_t[tb][:, g * SSM_STATE:(g + 1) * SSM_STATE].astype(F32), zeros], axis=1)
            r += 1
    for t in range(3):
        lhs_ref[r] = dec_t[t].astype(F32)
        rhs_ref[r] = jnp.concatenate([zeros, jnp.ones_like(zeros)], axis=1)
        r += 1
    for r in range(r, UPDATE_TERMS):
        lhs_ref[r] = jnp.zeros_like(xdt)
        rhs_ref[r] = jnp.concatenate([zeros, zeros], axis=1)
    for g in range(SSM_GROUPS):
        bm = xc[:, SSM_CH + g * SSM_STATE:SSM_CH + (g + 1) * SSM_STATE]
        cm = xc[:, SSM_CH + (SSM_GROUPS + g) * SSM_STATE:SSM_CH + (SSM_GROUPS + g + 1) * SSM_STATE]
        cb = jnp.sum(cm * bm, axis=-1, keepdims=True)
        gl = g * SSM_GROUP_CH
        ydiag_ref[:, gl:gl + SSM_GROUP_CH] = cb * xdt[:, gl:gl + SSM_GROUP_CH]


def _sample_pre(x, mod, w_in, w_dt, conv_w, conv_nw, sconv_w, sconv_b, dtb, alog, cb0, cb1, sb0, sb1, sb2):
    n = x.shape[0]
    args = (x, mod, w_in, w_dt, conv_w, conv_nw, sconv_w, sconv_b, dtb, alog, cb0, cb1, sb0, sb1, sb2)
    f32_shapes = [(n, CONV_CH), (n, CONV_CH), (n, XBC_CH), (n, SSM_CH), (n, SSM_CH), (n, SSM_CH), (n, SSM_CH),
                  (n, SSM_GROUPS * SSM_STATE)]
    f32_shapes += [(UPDATE_TERMS, n, SSM_CH), (UPDATE_TERMS, n, 2 * SSM_STATE)]
    return pl.pallas_call(
        _sample_pre_kernel,
        out_shape=[jax.ShapeDtypeStruct(s, F32) for s in f32_shapes],
        compiler_params=pltpu.CompilerParams(vmem_limit_bytes=VMEM_LIMIT),
        name="sample_pre",
    )(*args)


def _state_update(i, s_ref, lhs_ref, rhs_ref, cm_ref, decx_ref, o_ref, yoff_ref, block):
    rows = UPDATE_TERMS * block
    lhs_t = lhs_ref[...].reshape(rows, SSM_CH).T.astype(BF16)
    rhs_all = rhs_ref[...].reshape(rows, 2 * SSM_STATE)
    token_of = lax.broadcasted_iota(jnp.int32, rhs_all.shape, 0) % block

    def body(k, carry):
        b = i * block + k
        s = s_ref[k]
        upd = _dot(lhs_t, jnp.where(token_of == k, rhs_all, 0.0).astype(BF16))
        o_ref[k] = s * upd[:, SSM_STATE:2 * SSM_STATE] + upd[:, 0:SSM_STATE]
        cm = cm_ref[pl.ds(b, 1), :]
        sums = []
        for c0 in range(0, SSM_CH, LANES):
            g = c0 // SSM_GROUP_CH
            prod = s[c0:c0 + LANES, :] * cm[:, g * SSM_STATE:(g + 1) * SSM_STATE]
            sums.append(jnp.sum(prod.T, axis=0, keepdims=True))
        yoff_ref[pl.ds(b, 1), :] = jnp.concatenate(sums, axis=1) * decx_ref[pl.ds(b, 1), :]
        return carry

    lax.fori_loop(0, block, body, 0, unroll=True)


def _sample_post_kernel(x_ref, mod_ref, yconv_ref, ydiag_ref, yoff_ref, xs_ref, z_ref, dexp_ref, snw_ref,
                        w_out_ref, ln_g_ref, ln_b_ref, x1_ref):
    g1 = mod_ref[:, 2 * D_MODEL:3 * D_MODEL]
    y = ydiag_ref[...] + yoff_ref[...] + xs_ref[...] * dexp_ref[...]
    y = y * _silu(z_ref[...])
    m = _mix_out(yconv_ref[...], _ssm_group_norm(y, snw_ref[...]), w_out_ref)
    x1_ref[...] = _layer_norm(ALPHA * x_ref[...] + (1.0 + g1) * m, ln_g_ref[...], ln_b_ref[...])


def _sample_post(x, mod, yconv, ydiag, yoff, xs, z, dexp, snw, w_out, ln_g, ln_b):
    return pl.pallas_call(
        _sample_post_kernel,
        out_shape=jax.ShapeDtypeStruct(x.shape, F32),
        compiler_params=pltpu.CompilerParams(vmem_limit_bytes=VMEM_LIMIT),
        name="sample_post",
    )(x, mod, yconv, ydiag, yoff, xs, z, dexp, snw, w_out, ln_g, ln_b)


def kernel(x_prompt, x_sample, state_conv, state_ssm_conv, state_ssm, c_prompt, c_sample, w_ada, b_ada, w_in, conv_w, conv_norm_w, ssm_conv_w, ssm_conv_b, dt_bias, a_log, d_skip, ssm_norm_w, w_out, ln1_g, ln1_b, w_up, w_down, ln2_g, ln2_b):
    assert w_ada.shape[0] == 1, "single-layer trunk"
    nb, seq, _ = x_prompt.shape
    ns = x_sample.shape[0]
    row = lambda a: a.reshape(1, -1)
    pad_heads = lambda a: jnp.pad(a.reshape(1, -1), ((0, 0), (0, LANES - SSM_HEADS)))

    w_in_b, w_dt_b = _cast_in_proj(w_in[0])
    w_out_b = w_out[0].astype(BF16)
    w_up_b = w_up[0].astype(BF16)
    w_down_b = w_down[0].astype(BF16)
    conv_nw, sconv_b, snw = row(conv_norm_w[0]), row(ssm_conv_b[0]), row(ssm_norm_w[0])
    dtb, alog = pad_heads(dt_bias[0]), pad_heads(a_log[0])
    dexp = row(jnp.repeat(d_skip[0], SSM_HEAD_DIM))
    g1, b1, g2, b2 = row(ln1_g[0]), row(ln1_b[0]), row(ln2_g[0]), row(ln2_b[0])

    mod_p, mod_s = _ada(c_sample, c_prompt, w_ada[0], row(b_ada[0]))
    mod_p = mod_p.reshape(nb, 1, 6 * D_MODEL)

    x1_p, cst_p, scst_p, sst_p = _mixer_prompt(x_prompt, mod_p, w_in_b, w_dt_b, conv_w[0], conv_nw, ssm_conv_w[0],
                                               sconv_b, dtb, alog, dexp, snw, w_out_b, g1, b1)

    xs2 = x_sample.reshape(ns, D_MODEL)
    (yconv_s, ch_s, xbc_s, z_s, xs_s, ydiag_s, decx_s, cm_s, lhs_s, rhs_s) = _sample_pre(
        xs2, mod_s, w_in_b, w_dt_b, conv_w[0], conv_nw, ssm_conv_w[0], sconv_b, dtb, alog,
        state_conv[0, :, 0], state_conv[0, :, 1],
        state_ssm_conv[0, :, 0], state_ssm_conv[0, :, 1], state_ssm_conv[0, :, 2])

    y_p, new_state_s, yoff_s = _ffn_and_state(x1_p, mod_p, seq, w_up_b, w_down_b, g2, b2,
                                              state_ssm[0].reshape(ns, SSM_CH, SSM_STATE), lhs_s, rhs_s, cm_s, decx_s)

    x1_s = _sample_post(xs2, mod_s, yconv_s, ydiag_s, yoff_s, xs_s, z_s, dexp, snw, w_out_b, g1, b1)
    y_s = _ffn_stream(x1_s, mod_s, w_up_b, w_down_b, g2, b2)

    return (y_p.reshape(nb, seq, D_MODEL),
            y_s.reshape(ns, 1, D_MODEL),
            cst_p[None],
            scst_p[None],
            sst_p.reshape(1, nb, SSM_HEADS, SSM_HEAD_DIM, SSM_STATE),
            jnp.stack([state_conv[0, :, 1], ch_s], axis=1)[None],
            jnp.stack([state_ssm_conv[0, :, 1], state_ssm_conv[0, :, 2], xbc_s], axis=1)[None],
            new_state_s.reshape(1, ns, SSM_HEADS, SSM_HEAD_DIM, SSM_STATE))
```

```python
import functools

import jax
import jax.numpy as jnp
import numpy as np
from jax import lax
from jax.experimental import pallas as pl
from jax.experimental.pallas import tpu as pltpu

F32 = jnp.float32
BF16 = jnp.bfloat16

D_MODEL = 1024
CONV_CH = 1024
CONV_GROUP = 64
SSM_CH = 1024
SSM_HEADS = 16
SSM_HEAD_DIM = 64
SSM_GROUPS = 2
SSM_GROUP_CH = SSM_CH // SSM_GROUPS
SSM_STATE = 128
SSM_CHUNK = 128
XBC_CH = SSM_CH + 2 * SSM_GROUPS * SSM_STATE
D_FF = 4 * D_MODEL
LANES = 128
SUBLANES = 8
MXU_COLS = 256
COL_GB, COL_GC, COL_HV, COL_Z, COL_XBC = 0, 1024, 2048, 3072, 4096
COL_DT = COL_XBC + XBC_CH
IN_COLS = COL_DT + SSM_HEADS
IN_PAD = COL_DT + LANES
N_PIECES = COL_DT // MXU_COLS + 1
ALPHA = 2.0 ** 0.25
LN_EPS = 1e-5
RMS_EPS = 1e-5
VMEM_LIMIT = 56 * 1024 * 1024


def _dot(a, b):
    return jnp.dot(a, b, preferred_element_type=F32)


def _split(a, terms):
    parts = []
    r = a
    for t in range(terms):
        p = r.astype(BF16)
        parts.append(p)
        if t + 1 < terms:
            r = r - p.astype(F32)
    return parts


def _dot_f32_lhs(a, b_exact, terms=3):
    parts = _split(a, terms)
    out = _dot(parts[0], b_exact)
    for p in parts[1:]:
        out = out + _dot(p, b_exact)
    return out


def _dot_f32_rhs(a_exact, b, terms=3):
    parts = _split(b, terms)
    out = _dot(a_exact, parts[0])
    for p in parts[1:]:
        out = out + _dot(a_exact, p)
    return out


def _head_expand(xp=jnp):
    if xp is np:
        return jnp.asarray(np.arange(SSM_CH)[None, :] // SSM_HEAD_DIM == np.arange(LANES)[:, None], BF16)
    h = lax.broadcasted_iota(jnp.int32, (LANES, SSM_CH), 0)
    c = lax.broadcasted_iota(jnp.int32, (LANES, SSM_CH), 1)
    return (c // SSM_HEAD_DIM == h).astype(BF16)


def _group_reduce(xp=jnp):
    if xp is np:
        return jnp.asarray(np.arange(CONV_CH)[:, None] // CONV_GROUP == np.arange(LANES)[None, :], BF16)
    c = lax.broadcasted_iota(jnp.int32, (CONV_CH, LANES), 0)
    k = lax.broadcasted_iota(jnp.int32, (CONV_CH, LANES), 1)
    return (c // CONV_GROUP == k).astype(BF16)


def _sigmoid(x):
    return 1.0 / (1.0 + jnp.exp(-x))


def _silu(x):
    return x * _sigmoid(x)


def _softplus(x):
    return jnp.maximum(x, 0.0) + jnp.log1p(jnp.exp(-jnp.abs(x)))


def _layer_norm(r, g, b):
    mu = jnp.mean(r, axis=-1, keepdims=True)
    d = r - mu
    var = jnp.mean(d * d, axis=-1, keepdims=True)
    return d * lax.rsqrt(var + LN_EPS) * g + b


def _conv_group_norm(prod, w, expand, reduce):
    ssum = _dot_f32_lhs(prod * prod, reduce, terms=2)
    rstd = lax.rsqrt(ssum * (1.0 / CONV_GROUP) + RMS_EPS)
    return prod * _dot_f32_lhs(rstd, expand, terms=2) * w


def _ssm_group_norm(y, w):
    outs = []
    for g in range(SSM_GROUPS):
        yg = y[:, g * SSM_GROUP_CH:(g + 1) * SSM_GROUP_CH]
        ms = jnp.mean(yg * yg, axis=-1, keepdims=True)
        outs.append((yg * lax.rsqrt(ms + RMS_EPS) * w[:, g * SSM_GROUP_CH:(g + 1) * SSM_GROUP_CH]).astype(BF16))
    return outs


def _mix_out(y_conv, y_ssm_groups, w_out_ref):
    m = _dot(y_conv.astype(BF16), w_out_ref[0:CONV_CH, :])
    for g, yg in enumerate(y_ssm_groups):
        lo = CONV_CH + g * SSM_GROUP_CH
        m = m + _dot(yg, w_out_ref[lo:lo + SSM_GROUP_CH, :])
    return m


def _ada_kernel(cs_ref, cp_ref, w_ref, b_ref, op_ref, os_ref):
    c = jnp.concatenate([cs_ref[...], cp_ref[...]], axis=0)
    w = w_ref[...]
    c_hi = c.astype(BF16)
    c_lo = (c - c_hi.astype(F32)).astype(BF16)
    w_hi = w.astype(BF16)
    mod = _dot(c_hi, w_hi) + _dot(c_lo, w_hi) + b_ref[...]
    n_sample = os_ref.shape[0]
    os_ref[...] = mod[0:n_sample, :]
    op_ref[...] = mod[n_sample:, :]


def _ada(c_sample, c_prompt, w_ada, b_ada, tile_n=2048):
    ns, nb = c_sample.shape[0], c_prompt.shape[0]
    n = w_ada.shape[1]
    return pl.pallas_call(
        _ada_kernel,
        grid=(n // tile_n,),
        in_specs=[pl.BlockSpec((ns, D_MODEL), lambda i: (0, 0)),
                  pl.BlockSpec((nb, D_MODEL), lambda i: (0, 0)),
                  pl.BlockSpec((D_MODEL, tile_n), lambda i: (0, i)),
                  pl.BlockSpec((1, tile_n), lambda i: (0, i))],
        out_specs=[pl.BlockSpec((nb, tile_n), lambda i: (0, i)),
                   pl.BlockSpec((ns, tile_n), lambda i: (0, i))],
        out_shape=[jax.ShapeDtypeStruct((nb, n), F32), jax.ShapeDtypeStruct((ns, n), F32)],
        compiler_params=pltpu.CompilerParams(vmem_limit_bytes=VMEM_LIMIT),
        name="ada_mod",
    )(c_sample, c_prompt, w_ada, b_ada)


def _cast_transposed_kernel(wt_ref, o_ref):
    rows = wt_ref.shape[0]
    wt = wt_ref[...]
    if rows < o_ref.shape[1]:
        wt = jnp.concatenate([wt, jnp.zeros((o_ref.shape[1] - rows, wt.shape[1]), wt.dtype)], axis=0)
    o_ref[...] = wt.T.astype(o_ref.dtype)


def _cast_in_proj(w):
    wt = w.T
    n_dt = w.shape[1] - COL_DT
    cols = COL_DT // 4
    assert cols % LANES == 0
    main = pl.pallas_call(
        _cast_transposed_kernel,
        grid=(COL_DT // cols,),
        in_specs=[pl.BlockSpec((cols, D_MODEL), lambda j: (j, 0))],
        out_specs=pl.BlockSpec((D_MODEL, cols), lambda j: (0, j)),
        out_shape=jax.ShapeDtypeStruct((D_MODEL, COL_DT), BF16),
        compiler_params=pltpu.CompilerParams(vmem_limit_bytes=VMEM_LIMIT),
        name="cast_in_proj",
    )(wt)
    dt = pl.pallas_call(
        _cast_transposed_kernel,
        grid=(1,),
        in_specs=[pl.BlockSpec((n_dt, D_MODEL), lambda j: (COL_DT // n_dt, 0))],
        out_specs=pl.BlockSpec((D_MODEL, LANES), lambda j: (0, 0)),
        out_shape=jax.ShapeDtypeStruct((D_MODEL, LANES), BF16),
        name="cast_in_proj_dt",
    )(wt)
    return main, dt


def _mixer_prompt_kernel(xa_ref, moda_ref, modb_ref, w_in_ref, w_dt_ref, expand_ref, reduce_ref,
                         conv_w_ref, conv_nw_ref, sconv_w_ref, sconv_b_ref,
                         dtb_ref, alog_ref, dexp_ref, snw_ref, w_out_ref, ln_g_ref, ln_b_ref, *rest,
                         tile, tiles_per_seq, sched, n_cast, cast_steps):
    cast_in, outs, cast_out, scratch = (rest[:n_cast], rest[n_cast:n_cast + 4],
                                        rest[n_cast + 4:2 * n_cast + 4], rest[2 * n_cast + 4:])

    @pl.when(pl.program_id(0) < cast_steps)
    def _():
        for src, dst in zip(cast_in, cast_out):
            dst[...] = src[...].astype(dst.dtype)

    _mixer_prompt_body(xa_ref, moda_ref, modb_ref, w_in_ref, w_dt_ref, expand_ref, reduce_ref,
                       conv_w_ref, conv_nw_ref, sconv_w_ref, sconv_b_ref,
                       dtb_ref, alog_ref, dexp_ref, snw_ref, w_out_ref, ln_g_ref, ln_b_ref, *outs, *scratch,
                       tile=tile, tiles_per_seq=tiles_per_seq, sched=sched)


def _mixer_prompt_body(xa_ref, moda_ref, modb_ref, w_in_ref, w_dt_ref, expand_ref, reduce_ref,
                       conv_w_ref, conv_nw_ref, sconv_w_ref, sconv_b_ref,
                       dtb_ref, alog_ref, dexp_ref, snw_ref, w_out_ref, ln_g_ref, ln_b_ref,
                       x1_ref, cst_ref, scst_ref, sst_ref,
                       p, xk, cbuf, xbuf, st_ref, xs_ref, bc_ref, dtx_ref, acsx_ref, endx_ref,
                       acst_ref, cb_ref, bmt_ref, y_ref, yc_ref,
                       *, tile, tiles_per_seq, sched):
    s = pl.program_id(0)
    jb = lax.rem(s + (tiles_per_seq - 1), tiles_per_seq)

    @pl.when(s == 0)
    def _():
        p[...] = jnp.zeros_like(p)
        xk[...] = jnp.zeros_like(xk)

    @pl.when((jb == 0) | (s == 0))
    def _():
        cbuf[...] = jnp.zeros_like(cbuf)
        xbuf[...] = jnp.zeros_like(xbuf)
        st_ref[...] = jnp.zeros_like(st_ref)

    def stages():
        xa = xa_ref[...]
        u = (xa * (1.0 + moda_ref[:, D_MODEL:2 * D_MODEL]) + moda_ref[:, 0:D_MODEL]).astype(BF16)
        free = []

        def first_stage(n):
            for _ in range(min(n, len(free))):
                lo = free.pop(0)
                if lo == COL_DT:
                    p[:, COL_DT:IN_PAD] = _dot(u, w_dt_ref[...])
                else:
                    p[:, lo:lo + MXU_COLS] = _dot(u, w_in_ref[:, lo:lo + MXU_COLS])

        expand = expand_ref[...]
        x = xk[...]
        g1 = modb_ref[:, 2 * D_MODEL:3 * D_MODEL]

        def proj(lo, width):
            return p[:, lo:lo + width]

        def delayed(tail_ref, cs, cur, taps):
            seq = jnp.concatenate([tail_ref[:, cs], cur], axis=0)
            tail_ref[:, cs] = cur[tile - SUBLANES:, :]
            return [pltpu.roll(seq, k, axis=0)[SUBLANES:, :] for k in range(1, taps + 1)]

        for k in range(CONV_CH // MXU_COLS):
            c0 = k * MXU_COLS
            cs = slice(c0, c0 + MXU_COLS)
            ch = proj(COL_GC + c0, MXU_COLS) * proj(COL_HV + c0, MXU_COLS)
            ch1, ch2 = delayed(cbuf, cs, ch, 2)
            cv = conv_w_ref[0:1, cs] * ch2 + conv_w_ref[1:2, cs] * ch1 + conv_w_ref[2:3, cs] * ch
            prod = proj(COL_GB + c0, MXU_COLS) * cv
            free.extend((COL_GC + c0, COL_HV + c0, COL_GB + c0))
            first_stage(sched[0])
            ssum = _dot_f32_lhs(prod * prod, reduce_ref[cs, :], terms=1)
            rstd = lax.rsqrt(ssum * (1.0 / CONV_GROUP) + RMS_EPS)
            yc_ref[:, cs] = (prod * _dot_f32_lhs(rstd, expand_ref[:, cs], terms=2)
                             * conv_nw_ref[:, cs]).astype(BF16)

        def pre_conv(c0):
            cs = slice(c0, c0 + MXU_COLS)
            xbc = proj(COL_XBC + c0, MXU_COLS)
            free.append(COL_XBC + c0)
            x1, x2, x3 = delayed(xbuf, cs, xbc, 3)
            return _silu(sconv_w_ref[0:1, cs] * x3 + sconv_w_ref[1:2, cs] * x2 + sconv_w_ref[2:3, cs] * x1
                         + sconv_w_ref[3:4, cs] * xbc + sconv_b_ref[:, cs])

        row = lax.broadcasted_iota(jnp.int32, (SSM_CHUNK, SSM_CHUNK), 0)
        col = lax.broadcasted_iota(jnp.int32, (SSM_CHUNK, SSM_CHUNK), 1)
        causal = row >= col
        tri = causal.astype(BF16)
        groups = SSM_CHUNK // SUBLANES
        causal_bias = jnp.where(causal, 0.0, -jnp.inf).reshape(groups, SUBLANES, SSM_CHUNK)
        first_half = (col < SSM_HEAD_DIM).reshape(groups, SUBLANES, SSM_CHUNK)
        half_rows = col < SSM_HEAD_DIM
        chunks = [slice(c * SSM_CHUNK, (c + 1) * SSM_CHUNK) for c in range(tile // SSM_CHUNK)]

        dt = _softplus(proj(COL_DT, LANES) + dtb_ref[...])
        free.append(COL_DT)
        first_stage(sched[1])
        dta = dt * (-jnp.exp(alog_ref[...]))
        dtx_ref[...] = _dot_f32_lhs(dt, expand, terms=1)
        for c, rows in enumerate(chunks):
            acs = _dot_f32_rhs(tri, dta[rows, :])
            acs_t = acs.T
            for h in range(SSM_HEADS):
                r8 = (c * SSM_HEADS + h) * SUBLANES
                acst_ref[r8:r8 + SUBLANES, :] = jnp.broadcast_to(acs_t[h:h + 1, :], (SUBLANES, SSM_CHUNK))
            acs_x = _dot_f32_lhs(acs, expand, terms=2)
            acsx_ref[rows, :] = acs_x
            endx_ref[c * SUBLANES:(c + 1) * SUBLANES, :] = jnp.broadcast_to(acs_x[SSM_CHUNK - 1:SSM_CHUNK, :],
                                                                             (SUBLANES, SSM_CH))
        for c0 in range(SSM_CH, XBC_CH, MXU_COLS):
            first_stage(sched[2])
            bc_ref[:, c0 - SSM_CH:c0 - SSM_CH + MXU_COLS] = pre_conv(c0)
        for rows in chunks:
            for g in range(SSM_GROUPS):
                gs = slice(g * SSM_STATE, (g + 1) * SSM_STATE)
                bm = bc_ref[rows, gs]
                cm = bc_ref[rows, (SSM_GROUPS + g) * SSM_STATE:(SSM_GROUPS + g + 1) * SSM_STATE]
                cb_ref[rows, gs] = lax.dot_general(cm.astype(BF16), bm.astype(BF16), (((1,), (1,)), ((), ())),
                                                   preferred_element_type=F32)
                bmt_ref[rows, gs] = bm.T.astype(BF16)

        for c0 in range(0, SSM_CH, MXU_COLS):
            first_stage(sched[3])
            cs = slice(c0, c0 + MXU_COLS)
            g = c0 // SSM_GROUP_CH
            gs = slice(g * SSM_STATE, (g + 1) * SSM_STATE)
            xs = pre_conv(c0)
            xs_ref[:, cs] = xs
            xdt = xs * dtx_ref[:, cs]
            for c, rows in enumerate(chunks):
                first_stage(sched[4])
                acs_x = acsx_ref[rows, cs].reshape(groups, SUBLANES, MXU_COLS)
                end_x = endx_ref[c * SUBLANES:(c + 1) * SUBLANES, cs]
                xdt_c = xdt[rows, :]
                xdec = (xdt_c * jnp.exp(end_x[None] - acs_x).reshape(SSM_CHUNK, MXU_COLS)).astype(BF16)
                cm = bc_ref[rows, (SSM_GROUPS + g) * SSM_STATE:(SSM_GROUPS + g + 1) * SSM_STATE].astype(BF16)
                cb = cb_ref[rows, gs]
                st = st_ref[:, cs]
                y_off = _dot(cm, st.astype(BF16)) * jnp.exp(acs_x).reshape(SSM_CHUNK, MXU_COLS)
                st_ref[:, cs] = ((st.reshape(groups, SUBLANES, MXU_COLS) * jnp.exp(end_x)[None])
                                 .reshape(SSM_STATE, MXU_COLS) + _dot(bmt_ref[rows, gs], xdec))
                for lo in range(0, MXU_COLS, LANES):
                    h0 = (c * SSM_HEADS + (c0 + lo) // SSM_HEAD_DIM) * SUBLANES
                    slab = acs_x[:, :, lo:lo + LANES]
                    rolled = pltpu.roll(slab, SSM_HEAD_DIM, axis=2)
                    a0 = jnp.where(first_half, slab, rolled) - acst_ref[h0:h0 + SUBLANES, :][None]
                    a1 = jnp.where(first_half, rolled, slab) - acst_ref[h0 + SUBLANES:h0 + 2 * SUBLANES, :][None]
                    l0 = jnp.exp(a0 + causal_bias).reshape(SSM_CHUNK, SSM_CHUNK)
                    l1 = jnp.exp(a1 + causal_bias).reshape(SSM_CHUNK, SSM_CHUNK)
                    m = jnp.concatenate([(cb * l0).astype(BF16), (cb * l1).astype(BF16)], axis=1)
                    xp = xdt_c[:, lo:lo + LANES]
                    rhs = jnp.concatenate([jnp.where(half_rows, xp, 0.0), jnp.where(half_rows, 0.0, xp)],
                                          axis=0).astype(BF16)
                    y_ref[rows, c0 + lo:c0 + lo + LANES] = _dot(m, rhs) + y_off[:, lo:lo + LANES]

        for k in range(SSM_CH // MXU_COLS):
            first_stage(sched[5])
            c0 = k * MXU_COLS
            cs = slice(c0, c0 + MXU_COLS)
            y_ref[:, cs] = (y_ref[:, cs] + xs_ref[:, cs] * dexp_ref[:, cs]) * _silu(proj(COL_Z + c0, MXU_COLS))
            free.append(COL_Z + c0)
        first_stage(sched[6])
        m = _mix_out(yc_ref[...], _ssm_group_norm(y_ref[...], snw_ref[...]), w_out_ref)
        x1_ref[...] = _layer_norm(ALPHA * x + (1.0 + g1) * m, ln_g_ref[...], ln_b_ref[...])
        first_stage(N_PIECES)
        assert not free
        xk[...] = xa

    stages()

    @pl.when((jb == tiles_per_seq - 1) & (s > 0))
    def _():
        cst_ref[...] = cbuf[SUBLANES - 2:SUBLANES, :]
        scst_ref[...] = xbuf[SUBLANES - 3:SUBLANES, :]
        sst_ref[...] = st_ref[...].T


def _const_spec(shape):
    return pl.BlockSpec(shape, lambda *_: (0,) * len(shape), pipeline_mode=pl.Buffered(1))


def _mixer_prompt(x, mod, w_in, w_dt, conv_w, conv_nw, sconv_w, sconv_b, dtb, alog, dexp, snw, w_out, ln_g, ln_b,
                  to_cast=(), tile=256, sched=(1, 0, 0, 2, 1, 0, 0), cast_steps=16):
    assert CONV_GROUP == SSM_HEAD_DIM and CONV_CH == SSM_CH
    nb, seq, _ = x.shape
    tiles_per_seq = seq // tile
    n_tiles = nb * tiles_per_seq
    kern = functools.partial(_mixer_prompt_kernel, tile=tile, tiles_per_seq=tiles_per_seq, sched=sched,
                             n_cast=len(to_cast), cast_steps=cast_steps)
    cast_block = lambda s: (jnp.minimum(s, cast_steps - 1), 0)
    cast_specs = [pl.BlockSpec((w.shape[0] // cast_steps, w.shape[1]), cast_block) for w in to_cast]
    consts = [w_in, w_dt, _head_expand(np), _group_reduce(np), conv_w, conv_nw, sconv_w, sconv_b, dtb, alog, dexp, snw,
              w_out, ln_g, ln_b]
    first = lambda s: jnp.minimum(s, n_tiles - 1)
    second = lambda s: jnp.maximum(s - 1, 0)
    return pl.pallas_call(
        kern,
        grid=(n_tiles + 1,),
        in_specs=[pl.BlockSpec((tile, D_MODEL), lambda s: (first(s), 0)),
                  pl.BlockSpec((None, 1, 6 * D_MODEL), lambda s: (first(s) // tiles_per_seq, 0, 0)),
                  pl.BlockSpec((None, 1, 6 * D_MODEL), lambda s: (second(s) // tiles_per_seq, 0, 0))]
                 + [_const_spec(a.shape) for a in consts] + cast_specs,
        out_specs=[pl.BlockSpec((tile, D_MODEL), lambda s: (second(s), 0)),
                   pl.BlockSpec((None, 2, CONV_CH), lambda s: (second(s) // tiles_per_seq, 0, 0)),
                   pl.BlockSpec((None, 3, XBC_CH), lambda s: (second(s) // tiles_per_seq, 0, 0)),
                   pl.BlockSpec((None, SSM_CH, SSM_STATE), lambda s: (second(s) // tiles_per_seq, 0, 0))]
                  + cast_specs,
        out_shape=[jax.ShapeDtypeStruct((nb * seq, D_MODEL), F32),
                   jax.ShapeDtypeStruct((nb, 2, CONV_CH), F32),
                   jax.ShapeDtypeStruct((nb, 3, XBC_CH), F32),
                   jax.ShapeDtypeStruct((nb, SSM_CH, SSM_STATE), F32)]
                  + [jax.ShapeDtypeStruct(w.shape, BF16) for w in to_cast],
        scratch_shapes=[pltpu.VMEM((tile, IN_PAD), F32),
                        pltpu.VMEM((tile, D_MODEL), F32),
                        pltpu.VMEM((SUBLANES, CONV_CH), F32),
                        pltpu.VMEM((SUBLANES, XBC_CH), F32),
                        pltpu.VMEM((SSM_STATE, SSM_CH), F32),
                        pltpu.VMEM((tile, SSM_CH), F32),
                        pltpu.VMEM((tile, 2 * SSM_GROUPS * SSM_STATE), F32),
                        pltpu.VMEM((tile, SSM_CH), F32),
                        pltpu.VMEM((tile, SSM_CH), F32),
                        pltpu.VMEM((tile // SSM_CHUNK * SUBLANES, SSM_CH), F32),
                        pltpu.VMEM((tile // SSM_CHUNK * SSM_HEADS * SUBLANES, SSM_CHUNK), F32),
                        pltpu.VMEM((tile, SSM_GROUPS * SSM_STATE), F32),
                        pltpu.VMEM((tile, SSM_GROUPS * SSM_STATE), BF16),
                        pltpu.VMEM((tile, SSM_CH), F32),
                        pltpu.VMEM((tile, CONV_CH), BF16)],
        compiler_params=pltpu.CompilerParams(dimension_semantics=("arbitrary",),
                                             vmem_limit_bytes=VMEM_LIMIT),
        name="mixer_prompt",
    )(x.reshape(nb * seq, D_MODEL), mod, mod, *consts, *to_cast)


def _ffn_state_kernel(x_ref, mod_ref, w_up_ref, w_down_ref, ln_g_ref, ln_b_ref,
                      s_ref, lhs_ref, rhs_ref, cm_ref, decx_ref,
                      o_ref, so_ref, yoff_ref, r_ref, *, ff_tile, state_block, state_steps):
    s = pl.program_id(0)
    n_tiles = pl.num_programs(0) - 1

    @pl.when(s == 0)
    def _():
        r_ref[...] = jnp.zeros_like(r_ref)

    def norm_previous():
        o_ref[...] = _layer_norm(r_ref[...], ln_g_ref[...], ln_b_ref[...])

    @pl.when(s < n_tiles)
    def _():
        norm_previous()
        x = x_ref[...]
        sh2 = mod_ref[:, 3 * D_MODEL:4 * D_MODEL]
        sc2 = mod_ref[:, 4 * D_MODEL:5 * D_MODEL]
        g2 = mod_ref[:, 5 * D_MODEL:6 * D_MODEL]
        v = (x * (1.0 + sc2) + sh2).astype(BF16)
        acc = jnp.zeros(x.shape, F32)
        for k in range(D_FF // ff_tile):
            h = jnp.maximum(_dot(v, w_up_ref[:, k * ff_tile:(k + 1) * ff_tile]), 0.0)
            acc = acc + _dot((h * h).astype(BF16), w_down_ref[k * ff_tile:(k + 1) * ff_tile, :])
        r_ref[...] = ALPHA * x + (1.0 + g2) * acc

    @pl.when(s == n_tiles)
    def _():
        norm_previous()

    @pl.when(s < state_steps)
    def _():
        _state_update(s, s_ref, lhs_ref, rhs_ref, cm_ref, decx_ref, so_ref, yoff_ref, state_block)


def _ffn_and_state(x, mod, rows_per_mod, w_up, w_down, ln_g, ln_b, state, lhs, rhs, cm, decx,
                   tile=512, ff_tile=1024, state_block=8):
    rows = x.shape[0]
    n_tok = state.shape[0]
    tiles_per_mod = rows_per_mod // tile
    n_tiles = rows // tile
    state_steps = n_tok // state_block
    assert state_steps <= n_tiles + 1
    kern = functools.partial(_ffn_state_kernel, ff_tile=ff_tile, state_block=state_block, state_steps=state_steps)
    first = lambda s: jnp.minimum(s, n_tiles - 1)
    second = lambda s: jnp.maximum(s - 1, 0)
    tokens = lambda s: jnp.minimum(s, state_steps - 1)
    return pl.pallas_call(
        kern,
        grid=(n_tiles + 1,),
        in_specs=[pl.BlockSpec((tile, D_MODEL), lambda s: (first(s), 0)),
                  pl.BlockSpec((None, 1, 6 * D_MODEL), lambda s: (first(s) // tiles_per_mod, 0, 0)),
                  _const_spec(w_up.shape), _const_spec(w_down.shape),
                  _const_spec(ln_g.shape), _const_spec(ln_b.shape),
                  pl.BlockSpec((state_block, SSM_CH, SSM_STATE), lambda s: (tokens(s), 0, 0)),
                  pl.BlockSpec((UPDATE_TERMS, state_block, SSM_CH), lambda s: (0, tokens(s), 0)),
                  pl.BlockSpec((UPDATE_TERMS, state_block, 2 * SSM_STATE), lambda s: (0, tokens(s), 0)),
                  _const_spec(cm.shape), _const_spec(decx.shape)],
        out_specs=[pl.BlockSpec((tile, D_MODEL), lambda s: (second(s), 0)),
                   pl.BlockSpec((state_block, SSM_CH, SSM_STATE), lambda s: (tokens(s), 0, 0)),
                   pl.BlockSpec((n_tok, SSM_CH), lambda s: (0, 0))],
        out_shape=[jax.ShapeDtypeStruct((rows, D_MODEL), F32), jax.ShapeDtypeStruct(state.shape, F32),
                   jax.ShapeDtypeStruct((n_tok, SSM_CH), F32)],
        scratch_shapes=[pltpu.VMEM((tile, D_MODEL), F32)],
        compiler_params=pltpu.CompilerParams(dimension_semantics=("arbitrary",),
                                             vmem_limit_bytes=VMEM_LIMIT),
        name="ffn_and_state",
    )(x, mod, w_up, w_down, ln_g, ln_b, state, lhs, rhs, cm, decx)


def _ffn_stream_kernel(x_ref, mod_ref, w_up_ref, w_down_ref, ln_g_ref, ln_b_ref, o_ref, acc_ref):
    k = pl.program_id(0)

    @pl.when(k == 0)
    def _():
        acc_ref[...] = jnp.zeros_like(acc_ref)

    x = x_ref[...]
    sh2 = mod_ref[:, 3 * D_MODEL:4 * D_MODEL]
    sc2 = mod_ref[:, 4 * D_MODEL:5 * D_MODEL]
    v = (x * (1.0 + sc2) + sh2).astype(BF16)
    h = jnp.maximum(_dot(v, w_up_ref[...]), 0.0)
    acc_ref[...] += _dot((h * h).astype(BF16), w_down_ref[...])

    @pl.when(k == pl.num_programs(0) - 1)
    def _():
        g2 = mod_ref[:, 5 * D_MODEL:6 * D_MODEL]
        o_ref[...] = _layer_norm(ALPHA * x + (1.0 + g2) * acc_ref[...], ln_g_ref[...], ln_b_ref[...])


def _ffn_stream(x, mod, w_up, w_down, ln_g, ln_b, ff_tile=1024):
    rows = x.shape[0]
    return pl.pallas_call(
        _ffn_stream_kernel,
        grid=(D_FF // ff_tile,),
        in_specs=[_const_spec(x.shape), _const_spec(mod.shape),
                  pl.BlockSpec((D_MODEL, ff_tile), lambda k: (0, k)),
                  pl.BlockSpec((ff_tile, D_MODEL), lambda k: (k, 0)),
                  _const_spec(ln_g.shape), _const_spec(ln_b.shape)],
        out_specs=pl.BlockSpec((rows, D_MODEL), lambda k: (0, 0)),
        out_shape=jax.ShapeDtypeStruct((rows, D_MODEL), F32),
        scratch_shapes=[pltpu.VMEM((rows, D_MODEL), F32)],
        compiler_params=pltpu.CompilerParams(dimension_semantics=("arbitrary",),
                                             vmem_limit_bytes=VMEM_LIMIT),
        name="ffn_stream",
    )(x, mod, w_up, w_down, ln_g, ln_b)


_PRODUCT_TERMS = ((0, 0), (0, 1), (1, 0), (0, 2), (2, 0), (1, 1))
UPDATE_TERMS = 16


def _sample_pre_kernel(x_ref, mod_ref, w_in_ref, w_dt_ref, conv_w_ref, conv_nw_ref, sconv_w_ref, sconv_b_ref,
                       dtb_ref, alog_ref, cb0_ref, cb1_ref, sb0_ref, sb1_ref, sb2_ref,
                       yconv_ref, ch_ref, xbc_ref, z_ref, xs_ref, ydiag_ref, decx_ref, cm_ref, lhs_ref, rhs_ref):
    expand = _head_expand()
    reduce = _group_reduce()
    x = x_ref[...]
    sh1 = mod_ref[:, 0:D_MODEL]
    sc1 = mod_ref[:, D_MODEL:2 * D_MODEL]
    u = (x * (1.0 + sc1) + sh1).astype(BF16)

    def proj(lo, width):
        return _dot(u, w_in_ref[:, lo:lo + width])

    ch = proj(COL_GC, CONV_CH) * proj(COL_HV, CONV_CH)
    ch_ref[...] = ch
    cw = conv_w_ref[...]
    cv = cw[0:1, :] * cb0_ref[...] + cw[1:2, :] * cb1_ref[...] + cw[2:3, :] * ch
    yconv_ref[...] = _conv_group_norm(proj(COL_GB, CONV_CH) * cv, conv_nw_ref[...], expand, reduce)

    xbc = proj(COL_XBC, XBC_CH)
    xbc_ref[...] = xbc
    sw = sconv_w_ref[...]
    xc = _silu(sw[0:1, :] * sb0_ref[...] + sw[1:2, :] * sb1_ref[...] + sw[2:3, :] * sb2_ref[...]
               + sw[3:4, :] * xbc + sconv_b_ref[...])
    xs = xc[:, 0:SSM_CH]
    xs_ref[...] = xs
    cm_ref[...] = xc[:, SSM_CH + SSM_GROUPS * SSM_STATE:XBC_CH]
    z_ref[...] = proj(COL_Z, SSM_CH)

    dt = _softplus(_dot(u, w_dt_ref[...]) + dtb_ref[...])
    dta = dt * (-jnp.exp(alog_ref[...]))
    xdt = xs * _dot_f32_lhs(dt, expand)
    decx = jnp.exp(_dot_f32_lhs(dta, expand))
    decx_ref[...] = decx
    xdt_t, dec_t = _split(xdt, 3), _split(decx, 3)
    bm_t = _split(xc[:, SSM_CH:SSM_CH + SSM_GROUPS * SSM_STATE], 3)
    group_of = lax.broadcasted_iota(jnp.int32, xdt.shape, 1) // SSM_GROUP_CH
    zeros = jnp.zeros((x.shape[0], SSM_STATE), F32)
    r = 0
    for g in range(SSM_GROUPS):
        for tx, tb in _PRODUCT_TERMS:
            lhs_ref[r] = jnp.where(group_of == g, xdt_t[tx].astype(F32), 0.0)
            rhs_ref[r] = jnp.concatenate([bm_t[tb][:, g * SSM_STATE:(g + 1) * SSM_STATE].astype(F32), zeros], axis=1)
            r += 1
    for t in range(3):
        lhs_ref[r] = dec_t[t].astype(F32)
        rhs_ref[r] = jnp.concatenate([zeros, jnp.ones_like(zeros)], axis=1)
        r += 1
    for r in range(r, UPDATE_TERMS):
        lhs_ref[r] = jnp.zeros_like(xdt)
        rhs_ref[r] = jnp.concatenate([zeros, zeros], axis=1)
    for g in range(SSM_GROUPS):
        bm = xc[:, SSM_CH + g * SSM_STATE:SSM_CH + (g + 1) * SSM_STATE]
        cm = xc[:, SSM_CH + (SSM_GROUPS + g) * SSM_STATE:SSM_CH + (SSM_GROUPS + g + 1) * SSM_STATE]
        cb = jnp.sum(cm * bm, axis=-1, keepdims=True)
        gl = g * SSM_GROUP_CH
        ydiag_ref[:, gl:gl + SSM_GROUP_CH] = cb * xdt[:, gl:gl + SSM_GROUP_CH]


def _sample_pre(x, mod, w_in, w_dt, conv_w, conv_nw, sconv_w, sconv_b, dtb, alog, cb0, cb1, sb0, sb1, sb2):
    n = x.shape[0]
    args = (x, mod, w_in, w_dt, conv_w, conv_nw, sconv_w, sconv_b, dtb, alog, cb0, cb1, sb0, sb1, sb2)
    f32_shapes = [(n, CONV_CH), (n, CONV_CH), (n, XBC_CH), (n, SSM_CH), (n, SSM_CH), (n, SSM_CH), (n, SSM_CH),
                  (n, SSM_GROUPS * SSM_STATE)]
    f32_shapes += [(UPDATE_TERMS, n, SSM_CH), (UPDATE_TERMS, n, 2 * SSM_STATE)]
    return pl.pallas_call(
        _sample_pre_kernel,
        out_shape=[jax.ShapeDtypeStruct(s, F32) for s in f32_shapes],
        compiler_params=pltpu.CompilerParams(vmem_limit_bytes=VMEM_LIMIT),
        name="sample_pre",
    )(*args)


def _state_update(i, s_ref, lhs_ref, rhs_ref, cm_ref, decx_ref, o_ref, yoff_ref, block):
    rows = UPDATE_TERMS * block
    lhs_t = lhs_ref[...].reshape(rows, SSM_CH).T.astype(BF16)
    rhs_all = rhs_ref[...].reshape(rows, 2 * SSM_STATE)
    token_of = lax.broadcasted_iota(jnp.int32, rhs_all.shape, 0) % block

    def body(k, carry):
        b = i * block + k
        s = s_ref[k]
        upd = _dot(lhs_t, jnp.where(token_of == k, rhs_all, 0.0).astype(BF16))
        o_ref[k] = s * upd[:, SSM_STATE:2 * SSM_STATE] + upd[:, 0:SSM_STATE]
        cm = cm_ref[pl.ds(b, 1), :]
        sums = []
        for c0 in range(0, SSM_CH, LANES):
            g = c0 // SSM_GROUP_CH
            prod = s[c0:c0 + LANES, :] * cm[:, g * SSM_STATE:(g + 1) * SSM_STATE]
            sums.append(jnp.sum(prod.T, axis=0, keepdims=True))
        yoff_ref[pl.ds(b, 1), :] = jnp.concatenate(sums, axis=1) * decx_ref[pl.ds(b, 1), :]
        return carry

    lax.fori_loop(0, block, body, 0, unroll=True)


def _sample_post_kernel(x_ref, mod_ref, yconv_ref, ydiag_ref, yoff_ref, xs_ref, z_ref, dexp_ref, snw_ref,
                        w_out_ref, ln_g_ref, ln_b_ref, x1_ref):
    g1 = mod_ref[:, 2 * D_MODEL:3 * D_MODEL]
    y = ydiag_ref[...] + yoff_ref[...] + xs_ref[...] * dexp_ref[...]
    y = y * _silu(z_ref[...])
    m = _mix_out(yconv_ref[...], _ssm_group_norm(y, snw_ref[...]), w_out_ref)
    x1_ref[...] = _layer_norm(ALPHA * x_ref[...] + (1.0 + g1) * m, ln_g_ref[...], ln_b_ref[...])


def _sample_post(x, mod, yconv, ydiag, yoff, xs, z, dexp, snw, w_out, ln_g, ln_b):
    return pl.pallas_call(
        _sample_post_kernel,
        out_shape=jax.ShapeDtypeStruct(x.shape, F32),
        compiler_params=pltpu.CompilerParams(vmem_limit_bytes=VMEM_LIMIT),
        name="sample_post",
    )(x, mod, yconv, ydiag, yoff, xs, z, dexp, snw, w_out, ln_g, ln_b)


def kernel(x_prompt, x_sample, state_conv, state_ssm_conv, state_ssm, c_prompt, c_sample, w_ada, b_ada, w_in, conv_w, conv_norm_w, ssm_conv_w, ssm_conv_b, dt_bias, a_log, d_skip, ssm_norm_w, w_out, ln1_g, ln1_b, w_up, w_down, ln2_g, ln2_b):
    assert w_ada.shape[0] == 1, "single-layer trunk"
    nb, seq, _ = x_prompt.shape
    ns = x_sample.shape[0]
    row = lambda a: a.reshape(1, -1)
    pad_heads = lambda a: jnp.pad(a.reshape(1, -1), ((0, 0), (0, LANES - SSM_HEADS)))

    w_in_b, w_dt_b = _cast_in_proj(w_in[0])
    w_out_b = w_out[0].astype(BF16)
    conv_nw, sconv_b, snw = row(conv_norm_w[0]), row(ssm_conv_b[0]), row(ssm_norm_w[0])
    dtb, alog = pad_heads(dt_bias[0]), pad_heads(a_log[0])
    dexp = row(jnp.repeat(d_skip[0], SSM_HEAD_DIM))
    g1, b1, g2, b2 = row(ln1_g[0]), row(ln1_b[0]), row(ln2_g[0]), row(ln2_b[0])

    mod_p, mod_s = _ada(c_sample, c_prompt, w_ada[0], row(b_ada[0]))
    mod_p = mod_p.reshape(nb, 1, 6 * D_MODEL)

    x1_p, cst_p, scst_p, sst_p, w_up_b, w_down_b = _mixer_prompt(
        x_prompt, mod_p, w_in_b, w_dt_b, conv_w[0], conv_nw, ssm_conv_w[0], sconv_b, dtb, alog, dexp, snw, w_out_b,
        g1, b1, to_cast=(w_up[0], w_down[0]))

    xs2 = x_sample.reshape(ns, D_MODEL)
    (yconv_s, ch_s, xbc_s, z_s, xs_s, ydiag_s, decx_s, cm_s, lhs_s, rhs_s) = _sample_pre(
        xs2, mod_s, w_in_b, w_dt_b, conv_w[0], conv_nw, ssm_conv_w[0], sconv_b, dtb, alog,
        state_conv[0, :, 0], state_conv[0, :, 1],
        state_ssm_conv[0, :, 0], state_ssm_conv[0, :, 1], state_ssm_conv[0, :, 2])

    y_p, new_state_s, yoff_s = _ffn_and_state(x1_p, mod_p, seq, w_up_b, w_down_b, g2, b2,
                                              state_ssm[0].reshape(ns, SSM_CH, SSM_STATE), lhs_s, rhs_s, cm_s, decx_s)

    x1_s = _sample_post(xs2, mod_s, yconv_s, ydiag_s, yoff_s, xs_s, z_s, dexp, snw, w_out_b, g1, b1)
    y_s = _ffn_stream(x1_s, mod_s, w_up_b, w_down_b, g2, b2)

    return (y_p.reshape(nb, seq, D_MODEL),
            y_s.reshape(ns, 1, D_MODEL),
            cst_p[None],
            scst_p[None],
            sst_p.reshape(1, nb, SSM_HEADS, SSM_HEAD_DIM, SSM_STATE),
            jnp.stack([state_conv[0, :, 1], ch_s], axis=1)[None],
            jnp.stack([state_ssm_conv[0, :, 1], state_ssm_conv[0, :, 2], xbc_s], axis=1)[None],
            new_state_s.reshape(1, ns, SSM_HEADS, SSM_HEAD_DIM, SSM_STATE))
```

```python
import functools

import jax
import jax.numpy as jnp
import numpy as np
from jax import lax
from jax.experimental import pallas as pl
from jax.experimental.pallas import tpu as pltpu

F32 = jnp.float32
BF16 = jnp.bfloat16

D_MODEL = 1024
CONV_CH = 1024
CONV_GROUP = 64
SSM_CH = 1024
SSM_HEADS = 16
SSM_HEAD_DIM = 64
SSM_GROUPS = 2
SSM_GROUP_CH = SSM_CH // SSM_GROUPS
SSM_STATE = 128
SSM_CHUNK = 128
XBC_CH = SSM_CH + 2 * SSM_GROUPS * SSM_STATE
D_FF = 4 * D_MODEL
LANES = 128
SUBLANES = 8
MXU_COLS = 256
COL_GB, COL_GC, COL_HV, COL_Z, COL_XBC = 0, 1024, 2048, 3072, 4096
COL_DT = COL_XBC + XBC_CH
IN_COLS = COL_DT + SSM_HEADS
IN_PAD = COL_DT + LANES
N_PIECES = COL_DT // MXU_COLS + 1
ALPHA = 2.0 ** 0.25
LN_EPS = 1e-5
RMS_EPS = 1e-5
VMEM_LIMIT = 56 * 1024 * 1024


def _dot(a, b):
    return jnp.dot(a, b, preferred_element_type=F32)


def _split(a, terms):
    parts = []
    r = a
    for t in range(terms):
        p = r.astype(BF16)
        parts.append(p)
        if t + 1 < terms:
            r = r - p.astype(F32)
    return parts


def _dot_f32_lhs(a, b_exact, terms=3):
    parts = _split(a, terms)
    out = _dot(parts[0], b_exact)
    for p in parts[1:]:
        out = out + _dot(p, b_exact)
    return out


def _dot_f32_rhs(a_exact, b, terms=3):
    parts = _split(b, terms)
    out = _dot(a_exact, parts[0])
    for p in parts[1:]:
        out = out + _dot(a_exact, p)
    return out


def _head_expand(xp=jnp):
    if xp is np:
        return jnp.asarray(np.arange(SSM_CH)[None, :] // SSM_HEAD_DIM == np.arange(LANES)[:, None], BF16)
    h = lax.broadcasted_iota(jnp.int32, (LANES, SSM_CH), 0)
    c = lax.broadcasted_iota(jnp.int32, (LANES, SSM_CH), 1)
    return (c // SSM_HEAD_DIM == h).astype(BF16)


def _group_reduce(xp=jnp):
    if xp is np:
        return jnp.asarray(np.arange(CONV_CH)[:, None] // CONV_GROUP == np.arange(LANES)[None, :], BF16)
    c = lax.broadcasted_iota(jnp.int32, (CONV_CH, LANES), 0)
    k = lax.broadcasted_iota(jnp.int32, (CONV_CH, LANES), 1)
    return (c // CONV_GROUP == k).astype(BF16)


def _sigmoid(x):
    return 1.0 / (1.0 + jnp.exp(-x))


def _silu(x):
    return x * _sigmoid(x)


def _softplus(x):
    return jnp.maximum(x, 0.0) + jnp.log1p(jnp.exp(-jnp.abs(x)))


def _layer_norm(r, g, b):
    mu = jnp.mean(r, axis=-1, keepdims=True)
    d = r - mu
    var = jnp.mean(d * d, axis=-1, keepdims=True)
    return d * lax.rsqrt(var + LN_EPS) * g + b


def _conv_group_norm(prod, w, expand, reduce):
    ssum = _dot_f32_lhs(prod * prod, reduce, terms=2)
    rstd = lax.rsqrt(ssum * (1.0 / CONV_GROUP) + RMS_EPS)
    return prod * _dot_f32_lhs(rstd, expand, terms=2) * w


def _ssm_group_norm(y, w):
    outs = []
    for g in range(SSM_GROUPS):
        yg = y[:, g * SSM_GROUP_CH:(g + 1) * SSM_GROUP_CH]
        ms = jnp.mean(yg * yg, axis=-1, keepdims=True)
        outs.append((yg * lax.rsqrt(ms + RMS_EPS) * w[:, g * SSM_GROUP_CH:(g + 1) * SSM_GROUP_CH]).astype(BF16))
    return outs


def _mix_out(y_conv, y_ssm_groups, w_out_ref):
    m = _dot(y_conv.astype(BF16), w_out_ref[0:CONV_CH, :])
    for g, yg in enumerate(y_ssm_groups):
        lo = CONV_CH + g * SSM_GROUP_CH
        m = m + _dot(yg, w_out_ref[lo:lo + SSM_GROUP_CH, :])
    return m


def _transpose_to_bf16(wt, lanes):
    if wt.shape[0] < lanes:
        wt = jnp.concatenate([wt, jnp.zeros((lanes - wt.shape[0], wt.shape[1]), wt.dtype)], axis=0)
    return wt.T.astype(BF16)


def _prep_kernel(cs_ref, cp_ref, w_ref, b_ref, wt_ref, wt_dt_ref, op_ref, os_ref, win_ref, wdt_ref):
    c = jnp.concatenate([cs_ref[...], cp_ref[...]], axis=0)
    c_hi = c.astype(BF16)
    c_lo = (c - c_hi.astype(F32)).astype(BF16)
    w_hi = w_ref[...].astype(BF16)
    mod = _dot(c_hi, w_hi) + _dot(c_lo, w_hi) + b_ref[...]
    n_sample = os_ref.shape[0]
    os_ref[...] = mod[0:n_sample, :]
    op_ref[...] = mod[n_sample:, :]
    win_ref[...] = _transpose_to_bf16(wt_ref[...], win_ref.shape[1])
    wdt_ref[...] = _transpose_to_bf16(wt_dt_ref[...], LANES)


def _prep(c_sample, c_prompt, w_ada, b_ada, w_in, steps=4):
    ns, nb = c_sample.shape[0], c_prompt.shape[0]
    n_mod = w_ada.shape[1]
    mod_cols, in_cols = n_mod // steps, COL_DT // steps
    assert mod_cols % LANES == 0 and in_cols % LANES == 0
    wt = w_in.T
    n_dt = w_in.shape[1] - COL_DT
    return pl.pallas_call(
        _prep_kernel,
        grid=(steps,),
        in_specs=[pl.BlockSpec((ns, D_MODEL), lambda i: (0, 0)),
                  pl.BlockSpec((nb, D_MODEL), lambda i: (0, 0)),
                  pl.BlockSpec((D_MODEL, mod_cols), lambda i: (0, i)),
                  pl.BlockSpec((1, mod_cols), lambda i: (0, i)),
                  pl.BlockSpec((in_cols, D_MODEL), lambda i: (i, 0)),
                  pl.BlockSpec((n_dt, D_MODEL), lambda i: (COL_DT // n_dt, 0))],
        out_specs=[pl.BlockSpec((nb, mod_cols), lambda i: (0, i)),
                   pl.BlockSpec((ns, mod_cols), lambda i: (0, i)),
                   pl.BlockSpec((D_MODEL, in_cols), lambda i: (0, i)),
                   pl.BlockSpec((D_MODEL, LANES), lambda i: (0, 0))],
        out_shape=[jax.ShapeDtypeStruct((nb, n_mod), F32), jax.ShapeDtypeStruct((ns, n_mod), F32),
                   jax.ShapeDtypeStruct((D_MODEL, COL_DT), BF16), jax.ShapeDtypeStruct((D_MODEL, LANES), BF16)],
        compiler_params=pltpu.CompilerParams(dimension_semantics=("arbitrary",), vmem_limit_bytes=VMEM_LIMIT),
        name="prep",
    )(c_sample, c_prompt, w_ada, b_ada, wt, wt)


def _mixer_prompt_kernel(xa_ref, moda_ref, modb_ref, w_in_ref, w_dt_ref, expand_ref, reduce_ref,
                         conv_w_ref, conv_nw_ref, sconv_w_ref, sconv_b_ref,
                         dtb_ref, alog_ref, dexp_ref, snw_ref, w_out_ref, ln_g_ref, ln_b_ref, *rest,
                         tile, tiles_per_seq, sched, n_cast, cast_steps):
    cast_in, outs, cast_out, scratch = (rest[:n_cast], rest[n_cast:n_cast + 4],
                                        rest[n_cast + 4:2 * n_cast + 4], rest[2 * n_cast + 4:])

    @pl.when(pl.program_id(0) < cast_steps)
    def _():
        for src, dst in zip(cast_in, cast_out):
            dst[...] = src[...].astype(dst.dtype)

    _mixer_prompt_body(xa_ref, moda_ref, modb_ref, w_in_ref, w_dt_ref, expand_ref, reduce_ref,
                       conv_w_ref, conv_nw_ref, sconv_w_ref, sconv_b_ref,
                       dtb_ref, alog_ref, dexp_ref, snw_ref, w_out_ref, ln_g_ref, ln_b_ref, *outs, *scratch,
                       tile=tile, tiles_per_seq=tiles_per_seq, sched=sched)


def _mixer_prompt_body(xa_ref, moda_ref, modb_ref, w_in_ref, w_dt_ref, expand_ref, reduce_ref,
                       conv_w_ref, conv_nw_ref, sconv_w_ref, sconv_b_ref,
                       dtb_ref, alog_ref, dexp_ref, snw_ref, w_out_ref, ln_g_ref, ln_b_ref,
                       x1_ref, cst_ref, scst_ref, sst_ref,
                       p, xk, cbuf, xbuf, st_ref, xs_ref, bc_ref, dtx_ref, acsx_ref, endx_ref,
                       acst_ref, cb_ref, bmt_ref, y_ref, yc_ref,
                       *, tile, tiles_per_seq, sched):
    s = pl.program_id(0)
    jb = lax.rem(s + (tiles_per_seq - 1), tiles_per_seq)

    @pl.when(s == 0)
    def _():
        p[...] = jnp.zeros_like(p)
        xk[...] = jnp.zeros_like(xk)

    @pl.when((jb == 0) | (s == 0))
    def _():
        cbuf[...] = jnp.zeros_like(cbuf)
        xbuf[...] = jnp.zeros_like(xbuf)
        st_ref[...] = jnp.zeros_like(st_ref)

    def stages():
        xa = xa_ref[...]
        u = (xa * (1.0 + moda_ref[:, D_MODEL:2 * D_MODEL]) + moda_ref[:, 0:D_MODEL]).astype(BF16)
        free = []

        def first_stage(n):
            for _ in range(min(n, len(free))):
                lo = free.pop(0)
                if lo == COL_DT:
                    p[:, COL_DT:IN_PAD] = _dot(u, w_dt_ref[...])
                else:
                    p[:, lo:lo + MXU_COLS] = _dot(u, w_in_ref[:, lo:lo + MXU_COLS])

        expand = expand_ref[...]
        x = xk[...]
        g1 = modb_ref[:, 2 * D_MODEL:3 * D_MODEL]

        def proj(lo, width):
            return p[:, lo:lo + width]

        def delayed(tail_ref, cs, cur, taps):
            seq = jnp.concatenate([tail_ref[:, cs], cur], axis=0)
            tail_ref[:, cs] = cur[tile - SUBLANES:, :]
            return [pltpu.roll(seq, k, axis=0)[SUBLANES:, :] for k in range(1, taps + 1)]

        for k in range(CONV_CH // MXU_COLS):
            c0 = k * MXU_COLS
            cs = slice(c0, c0 + MXU_COLS)
            ch = proj(COL_GC + c0, MXU_COLS) * proj(COL_HV + c0, MXU_COLS)
            ch1, ch2 = delayed(cbuf, cs, ch, 2)
            cv = conv_w_ref[0:1, cs] * ch2 + conv_w_ref[1:2, cs] * ch1 + conv_w_ref[2:3, cs] * ch
            prod = proj(COL_GB + c0, MXU_COLS) * cv
            free.extend((COL_GC + c0, COL_HV + c0, COL_GB + c0))
            first_stage(sched[0])
            ssum = _dot_f32_lhs(prod * prod, reduce_ref[cs, :], terms=1)
            rstd = lax.rsqrt(ssum * (1.0 / CONV_GROUP) + RMS_EPS)
            yc_ref[:, cs] = (prod * _dot_f32_lhs(rstd, expand_ref[:, cs], terms=2)
                             * conv_nw_ref[:, cs]).astype(BF16)

        def pre_conv(c0):
            cs = slice(c0, c0 + MXU_COLS)
            xbc = proj(COL_XBC + c0, MXU_COLS)
            free.append(COL_XBC + c0)
            x1, x2, x3 = delayed(xbuf, cs, xbc, 3)
            return _silu(sconv_w_ref[0:1, cs] * x3 + sconv_w_ref[1:2, cs] * x2 + sconv_w_ref[2:3, cs] * x1
                         + sconv_w_ref[3:4, cs] * xbc + sconv_b_ref[:, cs])

        row = lax.broadcasted_iota(jnp.int32, (SSM_CHUNK, SSM_CHUNK), 0)
        col = lax.broadcasted_iota(jnp.int32, (SSM_CHUNK, SSM_CHUNK), 1)
        causal = row >= col
        tri = causal.astype(BF16)
        groups = SSM_CHUNK // SUBLANES
        causal_bias = jnp.where(causal, 0.0, -jnp.inf).reshape(groups, SUBLANES, SSM_CHUNK)
        first_half = (col < SSM_HEAD_DIM).reshape(groups, SUBLANES, SSM_CHUNK)
        half_rows = col < SSM_HEAD_DIM
        chunks = [slice(c * SSM_CHUNK, (c + 1) * SSM_CHUNK) for c in range(tile // SSM_CHUNK)]

        dt = _softplus(proj(COL_DT, LANES) + dtb_ref[...])
        free.append(COL_DT)
        first_stage(sched[1])
        dta = dt * (-jnp.exp(alog_ref[...]))
        dtx_ref[...] = _dot_f32_lhs(dt, expand, terms=1)
        for c, rows in enumerate(chunks):
            acs = _dot_f32_rhs(tri, dta[rows, :])
            acs_t = acs.T
            for h in range(SSM_HEADS):
                r8 = (c * SSM_HEADS + h) * SUBLANES
                acst_ref[r8:r8 + SUBLANES, :] = jnp.broadcast_to(acs_t[h:h + 1, :], (SUBLANES, SSM_CHUNK))
            acs_x = _dot_f32_lhs(acs, expand, terms=2)
            acsx_ref[rows, :] = acs_x
            endx_ref[c * SUBLANES:(c + 1) * SUBLANES, :] = jnp.broadcast_to(acs_x[SSM_CHUNK - 1:SSM_CHUNK, :],
                                                                             (SUBLANES, SSM_CH))
        for c0 in range(SSM_CH, XBC_CH, MXU_COLS):
            first_stage(sched[2])
            bc_ref[:, c0 - SSM_CH:c0 - SSM_CH + MXU_COLS] = pre_conv(c0)
        for rows in chunks:
            for g in range(SSM_GROUPS):
                gs = slice(g * SSM_STATE, (g + 1) * SSM_STATE)
                bm = bc_ref[rows, gs]
                cm = bc_ref[rows, (SSM_GROUPS + g) * SSM_STATE:(SSM_GROUPS + g + 1) * SSM_STATE]
                cb_ref[rows, gs] = lax.dot_general(cm.astype(BF16), bm.astype(BF16), (((1,), (1,)), ((), ())),
                                                   preferred_element_type=F32)
                bmt_ref[rows, gs] = bm.T.astype(BF16)

        for c0 in range(0, SSM_CH, MXU_COLS):
            first_stage(sched[3])
            cs = slice(c0, c0 + MXU_COLS)
            g = c0 // SSM_GROUP_CH
            gs = slice(g * SSM_STATE, (g + 1) * SSM_STATE)
            xs = pre_conv(c0)
            xs_ref[:, cs] = xs
            xdt = xs * dtx_ref[:, cs]
            for c, rows in enumerate(chunks):
                first_stage(sched[4])
                acs_x = acsx_ref[rows, cs].reshape(groups, SUBLANES, MXU_COLS)
                end_x = endx_ref[c * SUBLANES:(c + 1) * SUBLANES, cs]
                xdt_c = xdt[rows, :]
                xdec = (xdt_c * jnp.exp(end_x[None] - acs_x).reshape(SSM_CHUNK, MXU_COLS)).astype(BF16)
                cm = bc_ref[rows, (SSM_GROUPS + g) * SSM_STATE:(SSM_GROUPS + g + 1) * SSM_STATE].astype(BF16)
                cb = cb_ref[rows, gs]
                st = st_ref[:, cs]
                y_off = _dot(cm, st.astype(BF16)) * jnp.exp(acs_x).reshape(SSM_CHUNK, MXU_COLS)
                st_ref[:, cs] = ((st.reshape(groups, SUBLANES, MXU_COLS) * jnp.exp(end_x)[None])
                                 .reshape(SSM_STATE, MXU_COLS) + _dot(bmt_ref[rows, gs], xdec))
                for lo in range(0, MXU_COLS, LANES):
                    h0 = (c * SSM_HEADS + (c0 + lo) // SSM_HEAD_DIM) * SUBLANES
                    slab = acs_x[:, :, lo:lo + LANES]
                    rolled = pltpu.roll(slab, SSM_HEAD_DIM, axis=2)
                    a0 = jnp.where(first_half, slab, rolled) - acst_ref[h0:h0 + SUBLANES, :][None]
                    a1 = jnp.where(first_half, rolled, slab) - acst_ref[h0 + SUBLANES:h0 + 2 * SUBLANES, :][None]
                    l0 = jnp.exp(a0 + causal_bias).reshape(SSM_CHUNK, SSM_CHUNK)
                    l1 = jnp.exp(a1 + causal_bias).reshape(SSM_CHUNK, SSM_CHUNK)
                    m = jnp.concatenate([(cb * l0).astype(BF16), (cb * l1).astype(BF16)], axis=1)
                    xp = xdt_c[:, lo:lo + LANES]
                    rhs = jnp.concatenate([jnp.where(half_rows, xp, 0.0), jnp.where(half_rows, 0.0, xp)],
                                          axis=0).astype(BF16)
                    y_ref[rows, c0 + lo:c0 + lo + LANES] = _dot(m, rhs) + y_off[:, lo:lo + LANES]

        for k in range(SSM_CH // MXU_COLS):
            first_stage(sched[5])
            c0 = k * MXU_COLS
            cs = slice(c0, c0 + MXU_COLS)
            y_ref[:, cs] = (y_ref[:, cs] + xs_ref[:, cs] * dexp_ref[:, cs]) * _silu(proj(COL_Z + c0, MXU_COLS))
            free.append(COL_Z + c0)
        first_stage(sched[6])
        m = _mix_out(yc_ref[...], _ssm_group_norm(y_ref[...], snw_ref[...]), w_out_ref)
        x1_ref[...] = _layer_norm(ALPHA * x + (1.0 + g1) * m, ln_g_ref[...], ln_b_ref[...])
        first_stage(N_PIECES)
        assert not free
        xk[...] = xa

    stages()

    @pl.when((jb == tiles_per_seq - 1) & (s > 0))
    def _():
        cst_ref[...] = cbuf[SUBLANES - 2:SUBLANES, :]
        scst_ref[...] = xbuf[SUBLANES - 3:SUBLANES, :]
        sst_ref[...] = st_ref[...].T


def _const_spec(shape):
    return pl.BlockSpec(shape, lambda *_: (0,) * len(shape), pipeline_mode=pl.Buffered(1))


def _mixer_prompt(x, mod, w_in, w_dt, conv_w, conv_nw, sconv_w, sconv_b, dtb, alog, dexp, snw, w_out, ln_g, ln_b,
                  to_cast=(), tile=256, sched=(1, 0, 0, 2, 1, 0, 0), cast_steps=16):
    assert CONV_GROUP == SSM_HEAD_DIM and CONV_CH == SSM_CH
    nb, seq, _ = x.shape
    tiles_per_seq = seq // tile
    n_tiles = nb * tiles_per_seq
    kern = functools.partial(_mixer_prompt_kernel, tile=tile, tiles_per_seq=tiles_per_seq, sched=sched,
                             n_cast=len(to_cast), cast_steps=cast_steps)
    cast_block = lambda s: (jnp.minimum(s, cast_steps - 1), 0)
    cast_specs = [pl.BlockSpec((w.shape[0] // cast_steps, w.shape[1]), cast_block) for w in to_cast]
    consts = [w_in, w_dt, _head_expand(np), _group_reduce(np), conv_w, conv_nw, sconv_w, sconv_b, dtb, alog, dexp, snw,
              w_out, ln_g, ln_b]
    first = lambda s: jnp.minimum(s, n_tiles - 1)
    second = lambda s: jnp.maximum(s - 1, 0)
    return pl.pallas_call(
        kern,
        grid=(n_tiles + 1,),
        in_specs=[pl.BlockSpec((tile, D_MODEL), lambda s: (first(s), 0)),
                  pl.BlockSpec((None, 1, 6 * D_MODEL), lambda s: (first(s) // tiles_per_seq, 0, 0)),
                  pl.BlockSpec((None, 1, 6 * D_MODEL), lambda s: (second(s) // tiles_per_seq, 0, 0))]
                 + [_const_spec(a.shape) for a in consts] + cast_specs,
        out_specs=[pl.BlockSpec((tile, D_MODEL), lambda s: (second(s), 0)),
                   pl.BlockSpec((None, 2, CONV_CH), lambda s: (second(s) // tiles_per_seq, 0, 0)),
                   pl.BlockSpec((None, 3, XBC_CH), lambda s: (second(s) // tiles_per_seq, 0, 0)),
                   pl.BlockSpec((None, SSM_CH, SSM_STATE), lambda s: (second(s) // tiles_per_seq, 0, 0))]
                  + cast_specs,
        out_shape=[jax.ShapeDtypeStruct((nb * seq, D_MODEL), F32),
                   jax.ShapeDtypeStruct((nb, 2, CONV_CH), F32),
                   jax.ShapeDtypeStruct((nb, 3, XBC_CH), F32),
                   jax.ShapeDtypeStruct((nb, SSM_CH, SSM_STATE), F32)]
                  + [jax.ShapeDtypeStruct(w.shape, BF16) for w in to_cast],
        scratch_shapes=[pltpu.VMEM((tile, IN_PAD), F32),
                        pltpu.VMEM((tile, D_MODEL), F32),
                        pltpu.VMEM((SUBLANES, CONV_CH), F32),
                        pltpu.VMEM((SUBLANES, XBC_CH), F32),
                        pltpu.VMEM((SSM_STATE, SSM_CH), F32),
                        pltpu.VMEM((tile, SSM_CH), F32),
                        pltpu.VMEM((tile, 2 * SSM_GROUPS * SSM_STATE), F32),
                        pltpu.VMEM((tile, SSM_CH), F32),
                        pltpu.VMEM((tile, SSM_CH), F32),
                        pltpu.VMEM((tile // SSM_CHUNK * SUBLANES, SSM_CH), F32),
                        pltpu.VMEM((tile // SSM_CHUNK * SSM_HEADS * SUBLANES, SSM_CHUNK), F32),
                        pltpu.VMEM((tile, SSM_GROUPS * SSM_STATE), F32),
                        pltpu.VMEM((tile, SSM_GROUPS * SSM_STATE), BF16),
                        pltpu.VMEM((tile, SSM_CH), F32),
                        pltpu.VMEM((tile, CONV_CH), BF16)],
        compiler_params=pltpu.CompilerParams(dimension_semantics=("arbitrary",),
                                             vmem_limit_bytes=VMEM_LIMIT),
        name="mixer_prompt",
    )(x.reshape(nb * seq, D_MODEL), mod, mod, *consts, *to_cast)


def _ffn_state_kernel(x_ref, mod_ref, w_up_ref, w_down_ref, ln_g_ref, ln_b_ref,
                      s_ref, lhs_ref, rhs_ref, cm_ref, decx_ref,
                      o_ref, so_ref, yoff_ref, r_ref, *, ff_tile, state_block, state_steps):
    s = pl.program_id(0)
    n_tiles = pl.num_programs(0) - 1

    @pl.when(s == 0)
    def _():
        r_ref[...] = jnp.zeros_like(r_ref)

    def norm_previous():
        o_ref[...] = _layer_norm(r_ref[...], ln_g_ref[...], ln_b_ref[...])

    @pl.when(s < n_tiles)
    def _():
        norm_previous()
        x = x_ref[...]
        sh2 = mod_ref[:, 3 * D_MODEL:4 * D_MODEL]
        sc2 = mod_ref[:, 4 * D_MODEL:5 * D_MODEL]
        g2 = mod_ref[:, 5 * D_MODEL:6 * D_MODEL]
        v = (x * (1.0 + sc2) + sh2).astype(BF16)
        acc = jnp.zeros(x.shape, F32)
        for k in range(D_FF // ff_tile):
            h = jnp.maximum(_dot(v, w_up_ref[:, k * ff_tile:(k + 1) * ff_tile]), 0.0)
            acc = acc + _dot((h * h).astype(BF16), w_down_ref[k * ff_tile:(k + 1) * ff_tile, :])
        r_ref[...] = ALPHA * x + (1.0 + g2) * acc

    @pl.when(s == n_tiles)
    def _():
        norm_previous()

    @pl.when(s < state_steps)
    def _():
        _state_update(s, s_ref, lhs_ref, rhs_ref, cm_ref, decx_ref, so_ref, yoff_ref, state_block)


def _ffn_and_state(x, mod, rows_per_mod, w_up, w_down, ln_g, ln_b, state, lhs, rhs, cm, decx,
                   tile=512, ff_tile=1024, state_block=8):
    rows = x.shape[0]
    n_tok = state.shape[0]
    tiles_per_mod = rows_per_mod // tile
    n_tiles = rows // tile
    state_steps = n_tok // state_block
    assert state_steps <= n_tiles + 1
    kern = functools.partial(_ffn_state_kernel, ff_tile=ff_tile, state_block=state_block, state_steps=state_steps)
    first = lambda s: jnp.minimum(s, n_tiles - 1)
    second = lambda s: jnp.maximum(s - 1, 0)
    tokens = lambda s: jnp.minimum(s, state_steps - 1)
    return pl.pallas_call(
        kern,
        grid=(n_tiles + 1,),
        in_specs=[pl.BlockSpec((tile, D_MODEL), lambda s: (first(s), 0)),
                  pl.BlockSpec((None, 1, 6 * D_MODEL), lambda s: (first(s) // tiles_per_mod, 0, 0)),
                  _const_spec(w_up.shape), _const_spec(w_down.shape),
                  _const_spec(ln_g.shape), _const_spec(ln_b.shape),
                  pl.BlockSpec((state_block, SSM_CH, SSM_STATE), lambda s: (tokens(s), 0, 0)),
                  pl.BlockSpec((UPDATE_TERMS, state_block, SSM_CH), lambda s: (0, tokens(s), 0)),
                  pl.BlockSpec((UPDATE_TERMS, state_block, 2 * SSM_STATE), lambda s: (0, tokens(s), 0)),
                  _const_spec(cm.shape), _const_spec(decx.shape)],
        out_specs=[pl.BlockSpec((tile, D_MODEL), lambda s: (second(s), 0)),
                   pl.BlockSpec((state_block, SSM_CH, SSM_STATE), lambda s: (tokens(s), 0, 0)),
                   pl.BlockSpec((n_tok, SSM_CH), lambda s: (0, 0))],
        out_shape=[jax.ShapeDtypeStruct((rows, D_MODEL), F32), jax.ShapeDtypeStruct(state.shape, F32),
                   jax.ShapeDtypeStruct((n_tok, SSM_CH), F32)],
        scratch_shapes=[pltpu.VMEM((tile, D_MODEL), F32)],
        compiler_params=pltpu.CompilerParams(dimension_semantics=("arbitrary",),
                                             vmem_limit_bytes=VMEM_LIMIT),
        name="ffn_and_state",
    )(x, mod, w_up, w_down, ln_g, ln_b, state, lhs, rhs, cm, decx)


def _ffn_stream_kernel(x_ref, mod_ref, w_up_ref, w_down_ref, ln_g_ref, ln_b_ref, o_ref, acc_ref):
    k = pl.program_id(0)

    @pl.when(k == 0)
    def _():
        acc_ref[...] = jnp.zeros_like(acc_ref)

    x = x_ref[...]
    sh2 = mod_ref[:, 3 * D_MODEL:4 * D_MODEL]
    sc2 = mod_ref[:, 4 * D_MODEL:5 * D_MODEL]
    v = (x * (1.0 + sc2) + sh2).astype(BF16)
    h = jnp.maximum(_dot(v, w_up_ref[...]), 0.0)
    acc_ref[...] += _dot((h * h).astype(BF16), w_down_ref[...])

    @pl.when(k == pl.num_programs(0) - 1)
    def _():
        g2 = mod_ref[:, 5 * D_MODEL:6 * D_MODEL]
        o_ref[...] = _layer_norm(ALPHA * x + (1.0 + g2) * acc_ref[...], ln_g_ref[...], ln_b_ref[...])


def _ffn_stream(x, mod, w_up, w_down, ln_g, ln_b, ff_tile=1024):
    rows = x.shape[0]
    return pl.pallas_call(
        _ffn_stream_kernel,
        grid=(D_FF // ff_tile,),
        in_specs=[_const_spec(x.shape), _const_spec(mod.shape),
                  pl.BlockSpec((D_MODEL, ff_tile), lambda k: (0, k)),
                  pl.BlockSpec((ff_tile, D_MODEL), lambda k: (k, 0)),
                  _const_spec(ln_g.shape), _const_spec(ln_b.shape)],
        out_specs=pl.BlockSpec((rows, D_MODEL), lambda k: (0, 0)),
        out_shape=jax.ShapeDtypeStruct((rows, D_MODEL), F32),
        scratch_shapes=[pltpu.VMEM((rows, D_MODEL), F32)],
        compiler_params=pltpu.CompilerParams(dimension_semantics=("arbitrary",),
                                             vmem_limit_bytes=VMEM_LIMIT),
        name="ffn_stream",
    )(x, mod, w_up, w_down, ln_g, ln_b)


_PRODUCT_TERMS = ((0, 0), (0, 1), (1, 0), (0, 2), (2, 0), (1, 1))
UPDATE_TERMS = 16


def _sample_pre_kernel(x_ref, mod_ref, w_in_ref, w_dt_ref, conv_w_ref, conv_nw_ref, sconv_w_ref, sconv_b_ref,
                       dtb_ref, alog_ref, cb0_ref, cb1_ref, sb0_ref, sb1_ref, sb2_ref,
                       yconv_ref, ch_ref, xbc_ref, z_ref, xs_ref, ydiag_ref, decx_ref, cm_ref, lhs_ref, rhs_ref):
    expand = _head_expand()
    reduce = _group_reduce()
    x = x_ref[...]
    sh1 = mod_ref[:, 0:D_MODEL]
    sc1 = mod_ref[:, D_MODEL:2 * D_MODEL]
    u = (x * (1.0 + sc1) + sh1).astype(BF16)

    def proj(lo, width):
        return _dot(u, w_in_ref[:, lo:lo + width])

    ch = proj(COL_GC, CONV_CH) * proj(COL_HV, CONV_CH)
    ch_ref[...] = ch
    cw = conv_w_ref[...]
    cv = cw[0:1, :] * cb0_ref[...] + cw[1:2, :] * cb1_ref[...] + cw[2:3, :] * ch
    yconv_ref[...] = _conv_group_norm(proj(COL_GB, CONV_CH) * cv, conv_nw_ref[...], expand, reduce)

    xbc = proj(COL_XBC, XBC_CH)
    xbc_ref[...] = xbc
    sw = sconv_w_ref[...]
    xc = _silu(sw[0:1, :] * sb0_ref[...] + sw[1:2, :] * sb1_ref[...] + sw[2:3, :] * sb2_ref[...]
               + sw[3:4, :] * xbc + sconv_b_ref[...])
    xs = xc[:, 0:SSM_CH]
    xs_ref[...] = xs
    cm_ref[...] = xc[:, SSM_CH + SSM_GROUPS * SSM_STATE:XBC_CH]
    z_ref[...] = proj(COL_Z, SSM_CH)

    dt = _softplus(_dot(u, w_dt_ref[...]) + dtb_ref[...])
    dta = dt * (-jnp.exp(alog_ref[...]))
    xdt = xs * _dot_f32_lhs(dt, expand)
    decx = jnp.exp(_dot_f32_lhs(dta, expand))
    decx_ref[...] = decx
    xdt_t, dec_t = _split(xdt, 3), _split(decx, 3)
    bm_t = _split(xc[:, SSM_CH:SSM_CH + SSM_GROUPS * SSM_STATE], 3)
    group_of = lax.broadcasted_iota(jnp.int32, xdt.shape, 1) // SSM_GROUP_CH
    zeros = jnp.zeros((x.shape[0], SSM_STATE), F32)
    r = 0
    for g in range(SSM_GROUPS):
        for tx, tb in _PRODUCT_TERMS:
            lhs_ref[r] = jnp.where(group_of == g, xdt_t[tx].astype(F32), 0.0)
            rhs_ref[r] = jnp.concatenate([bm_t[tb][:, g * SSM_STATE:(g + 1) * SSM_STATE].astype(F32), zeros], axis=1)
            r += 1
    for t in range(3):
        lhs_ref[r] = dec_t[t].astype(F32)
        rhs_ref[r] = jnp.concatenate([zeros, jnp.ones_like(zeros)], axis=1)
        r += 1
    for r in range(r, UPDATE_TERMS):
        lhs_ref[r] = jnp.zeros_like(xdt)
        rhs_ref[r] = jnp.concatenate([zeros, zeros], axis=1)
    for g in range(SSM_GROUPS):
        bm = xc[:, SSM_CH + g * SSM_STATE:SSM_CH + (g + 1) * SSM_STATE]
        cm = xc[:, SSM_CH + (SSM_GROUPS + g) * SSM_STATE:SSM_CH + (SSM_GROUPS + g + 1) * SSM_STATE]
        cb = jnp.sum(cm * bm, axis=-1, keepdims=True)
        gl = g * SSM_GROUP_CH
        ydiag_ref[:, gl:gl + SSM_GROUP_CH] = cb * xdt[:, gl:gl + SSM_GROUP_CH]


def _sample_pre(x, mod, w_in, w_dt, conv_w, conv_nw, sconv_w, sconv_b, dtb, alog, cb0, cb1, sb0, sb1, sb2):
    n = x.shape[0]
    args = (x, mod, w_in, w_dt, conv_w, conv_nw, sconv_w, sconv_b, dtb, alog, cb0, cb1, sb0, sb1, sb2)
    f32_shapes = [(n, CONV_CH), (n, CONV_CH), (n, XBC_CH), (n, SSM_CH), (n, SSM_CH), (n, SSM_CH), (n, SSM_CH),
                  (n, SSM_GROUPS * SSM_STATE)]
    f32_shapes += [(UPDATE_TERMS, n, SSM_CH), (UPDATE_TERMS, n, 2 * SSM_STATE)]
    return pl.pallas_call(
        _sample_pre_kernel,
        out_shape=[jax.ShapeDtypeStruct(s, F32) for s in f32_shapes],
        compiler_params=pltpu.CompilerParams(vmem_limit_bytes=VMEM_LIMIT),
        name="sample_pre",
    )(*args)


def _state_update(i, s_ref, lhs_ref, rhs_ref, cm_ref, decx_ref, o_ref, yoff_ref, block):
    rows = UPDATE_TERMS * block
    lhs_t = lhs_ref[...].reshape(rows, SSM_CH).T.astype(BF16)
    rhs_all = rhs_ref[...].reshape(rows, 2 * SSM_STATE)
    token_of = lax.broadcasted_iota(jnp.int32, rhs_all.shape, 0) % block

    def body(k, carry):
        b = i * block + k
        s = s_ref[k]
        upd = _dot(lhs_t, jnp.where(token_of == k, rhs_all, 0.0).astype(BF16))
        o_ref[k] = s * upd[:, SSM_STATE:2 * SSM_STATE] + upd[:, 0:SSM_STATE]
        cm = cm_ref[pl.ds(b, 1), :]
        sums = []
        for c0 in range(0, SSM_CH, LANES):
            g = c0 // SSM_GROUP_CH
            prod = s[c0:c0 + LANES, :] * cm[:, g * SSM_STATE:(g + 1) * SSM_STATE]
            sums.append(jnp.sum(prod.T, axis=0, keepdims=True))
        yoff_ref[pl.ds(b, 1), :] = jnp.concatenate(sums, axis=1) * decx_ref[pl.ds(b, 1), :]
        return carry

    lax.fori_loop(0, block, body, 0, unroll=True)


def _sample_post_kernel(x_ref, mod_ref, yconv_ref, ydiag_ref, yoff_ref, xs_ref, z_ref, dexp_ref, snw_ref,
                        w_out_ref, ln_g_ref, ln_b_ref, x1_ref):
    g1 = mod_ref[:, 2 * D_MODEL:3 * D_MODEL]
    y = ydiag_ref[...] + yoff_ref[...] + xs_ref[...] * dexp_ref[...]
    y = y * _silu(z_ref[...])
    m = _mix_out(yconv_ref[...], _ssm_group_norm(y, snw_ref[...]), w_out_ref)
    x1_ref[...] = _layer_norm(ALPHA * x_ref[...] + (1.0 + g1) * m, ln_g_ref[...], ln_b_ref[...])


def _sample_post(x, mod, yconv, ydiag, yoff, xs, z, dexp, snw, w_out, ln_g, ln_b):
    return pl.pallas_call(
        _sample_post_kernel,
        out_shape=jax.ShapeDtypeStruct(x.shape, F32),
        compiler_params=pltpu.CompilerParams(vmem_limit_bytes=VMEM_LIMIT),
        name="sample_post",
    )(x, mod, yconv, ydiag, yoff, xs, z, dexp, snw, w_out, ln_g, ln_b)


def kernel(x_prompt, x_sample, state_conv, state_ssm_conv, state_ssm, c_prompt, c_sample, w_ada, b_ada, w_in, conv_w, conv_norm_w, ssm_conv_w, ssm_conv_b, dt_bias, a_log, d_skip, ssm_norm_w, w_out, ln1_g, ln1_b, w_up, w_down, ln2_g, ln2_b):
    assert w_ada.shape[0] == 1, "single-layer trunk"
    nb, seq, _ = x_prompt.shape
    ns = x_sample.shape[0]
    row = lambda a: a.reshape(1, -1)
    pad_heads = lambda a: jnp.pad(a.reshape(1, -1), ((0, 0), (0, LANES - SSM_HEADS)))

    w_out_b = w_out[0].astype(BF16)
    conv_nw, sconv_b, snw = row(conv_norm_w[0]), row(ssm_conv_b[0]), row(ssm_norm_w[0])
    dtb, alog = pad_heads(dt_bias[0]), pad_heads(a_log[0])
    dexp = row(jnp.repeat(d_skip[0], SSM_HEAD_DIM))
    g1, b1, g2, b2 = row(ln1_g[0]), row(ln1_b[0]), row(ln2_g[0]), row(ln2_b[0])

    mod_p, mod_s, w_in_b, w_dt_b = _prep(c_sample, c_prompt, w_ada[0], row(b_ada[0]), w_in[0])
    mod_p = mod_p.reshape(nb, 1, 6 * D_MODEL)

    x1_p, cst_p, scst_p, sst_p, w_up_b, w_down_b = _mixer_prompt(
        x_prompt, mod_p, w_in_b, w_dt_b, conv_w[0], conv_nw, ssm_conv_w[0], sconv_b, dtb, alog, dexp, snw, w_out_b,
        g1, b1, to_cast=(w_up[0], w_down[0]))

    xs2 = x_sample.reshape(ns, D_MODEL)
    (yconv_s, ch_s, xbc_s, z_s, xs_s, ydiag_s, decx_s, cm_s, lhs_s, rhs_s) = _sample_pre(
        xs2, mod_s, w_in_b, w_dt_b, conv_w[0], conv_nw, ssm_conv_w[0], sconv_b, dtb, alog,
        state_conv[0, :, 0], state_conv[0, :, 1],
        state_ssm_conv[0, :, 0], state_ssm_conv[0, :, 1], state_ssm_conv[0, :, 2])

    y_p, new_state_s, yoff_s = _ffn_and_state(x1_p, mod_p, seq, w_up_b, w_down_b, g2, b2,
                                              state_ssm[0].reshape(ns, SSM_CH, SSM_STATE), lhs_s, rhs_s, cm_s, decx_s)

    x1_s = _sample_post(xs2, mod_s, yconv_s, ydiag_s, yoff_s, xs_s, z_s, dexp, snw, w_out_b, g1, b1)
    y_s = _ffn_stream(x1_s, mod_s, w_up_b, w_down_b, g2, b2)

    return (y_p.reshape(nb, seq, D_MODEL),
            y_s.reshape(ns, 1, D_MODEL),
            cst_p[None],
            scst_p[None],
            sst_p.reshape(1, nb, SSM_HEADS, SSM_HEAD_DIM, SSM_STATE),
            jnp.stack([state_conv[0, :, 1], ch_s], axis=1)[None],
            jnp.stack([state_ssm_conv[0, :, 1], state_ssm_conv[0, :, 2], xbc_s], axis=1)[None],
            new_state_s.reshape(1, ns, SSM_HEADS, SSM_HEAD_DIM, SSM_STATE))
```

```python
import functools

import jax
import jax.numpy as jnp
import numpy as np
from jax import lax
from jax.experimental import pallas as pl
from jax.experimental.pallas import tpu as pltpu

F32 = jnp.float32
BF16 = jnp.bfloat16

D_MODEL = 1024
CONV_CH = 1024
CONV_GROUP = 64
SSM_CH = 1024
SSM_HEADS = 16
SSM_HEAD_DIM = 64
SSM_GROUPS = 2
SSM_GROUP_CH = SSM_CH // SSM_GROUPS
SSM_STATE = 128
SSM_CHUNK = 128
XBC_CH = SSM_CH + 2 * SSM_GROUPS * SSM_STATE
D_FF = 4 * D_MODEL
LANES = 128
SUBLANES = 8
MXU_COLS = 256
COL_GB, COL_GC, COL_HV, COL_Z, COL_XBC = 0, 1024, 2048, 3072, 4096
COL_DT = COL_XBC + XBC_CH
IN_PAD = COL_DT + LANES
N_PIECES = COL_DT // MXU_COLS + 1
ALPHA = 2.0 ** 0.25
LN_EPS = 1e-5
RMS_EPS = 1e-5
VMEM_LIMIT = 56 * 1024 * 1024

PREP_STEPS = 4
MIXER_TILE = 256
MIXER_SCHED = (1, 0, 0, 2, 1, 0, 0)
CAST_STEPS = 16
FFN_TILE = 512
FF_SLAB = 1024
STATE_BLOCK = 8


def _dot(a, b):
    return jnp.dot(a, b, preferred_element_type=F32)


def _split(a, terms):
    parts = []
    r = a
    for t in range(terms):
        p = r.astype(BF16)
        parts.append(p)
        if t + 1 < terms:
            r = r - p.astype(F32)
    return parts


def _dot_f32_lhs(a, b_exact, terms=3):
    parts = _split(a, terms)
    out = _dot(parts[0], b_exact)
    for p in parts[1:]:
        out = out + _dot(p, b_exact)
    return out


def _dot_f32_rhs(a_exact, b, terms=3):
    parts = _split(b, terms)
    out = _dot(a_exact, parts[0])
    for p in parts[1:]:
        out = out + _dot(a_exact, p)
    return out


def _head_expand(xp=jnp):
    if xp is np:
        return jnp.asarray(np.arange(SSM_CH)[None, :] // SSM_HEAD_DIM == np.arange(LANES)[:, None], BF16)
    h = lax.broadcasted_iota(jnp.int32, (LANES, SSM_CH), 0)
    c = lax.broadcasted_iota(jnp.int32, (LANES, SSM_CH), 1)
    return (c // SSM_HEAD_DIM == h).astype(BF16)


def _group_reduce(xp=jnp):
    if xp is np:
        return jnp.asarray(np.arange(CONV_CH)[:, None] // CONV_GROUP == np.arange(LANES)[None, :], BF16)
    c = lax.broadcasted_iota(jnp.int32, (CONV_CH, LANES), 0)
    k = lax.broadcasted_iota(jnp.int32, (CONV_CH, LANES), 1)
    return (c // CONV_GROUP == k).astype(BF16)


def _sigmoid(x):
    return 1.0 / (1.0 + jnp.exp(-x))


def _silu(x):
    return x * _sigmoid(x)


def _softplus(x):
    return jnp.maximum(x, 0.0) + jnp.log1p(jnp.exp(-jnp.abs(x)))


def _layer_norm(r, g, b):
    mu = jnp.mean(r, axis=-1, keepdims=True)
    d = r - mu
    var = jnp.mean(d * d, axis=-1, keepdims=True)
    return d * lax.rsqrt(var + LN_EPS) * g + b


def _conv_group_norm(prod, w, expand, reduce):
    ssum = _dot_f32_lhs(prod * prod, reduce, terms=2)
    rstd = lax.rsqrt(ssum * (1.0 / CONV_GROUP) + RMS_EPS)
    return prod * _dot_f32_lhs(rstd, expand, terms=2) * w


def _ssm_group_norm(y, w):
    outs = []
    for g in range(SSM_GROUPS):
        yg = y[:, g * SSM_GROUP_CH:(g + 1) * SSM_GROUP_CH]
        ms = jnp.mean(yg * yg, axis=-1, keepdims=True)
        outs.append((yg * lax.rsqrt(ms + RMS_EPS) * w[:, g * SSM_GROUP_CH:(g + 1) * SSM_GROUP_CH]).astype(BF16))
    return outs


def _mix_out(y_conv, y_ssm_groups, w_out_ref):
    m = _dot(y_conv.astype(BF16), w_out_ref[0:CONV_CH, :])
    for g, yg in enumerate(y_ssm_groups):
        lo = CONV_CH + g * SSM_GROUP_CH
        m = m + _dot(yg, w_out_ref[lo:lo + SSM_GROUP_CH, :])
    return m


def _transpose_to_bf16(wt, lanes):
    if wt.shape[0] < lanes:
        wt = jnp.concatenate([wt, jnp.zeros((lanes - wt.shape[0], wt.shape[1]), wt.dtype)], axis=0)
    return wt.T.astype(BF16)


def _prep_kernel(cs_ref, cp_ref, w_ref, b_ref, wt_ref, wt_dt_ref, op_ref, os_ref, win_ref, wdt_ref):
    c = jnp.concatenate([cs_ref[...], cp_ref[...]], axis=0)
    c_hi = c.astype(BF16)
    c_lo = (c - c_hi.astype(F32)).astype(BF16)
    w_hi = w_ref[...].astype(BF16)
    mod = _dot(c_hi, w_hi) + _dot(c_lo, w_hi) + b_ref[...]
    n_sample = os_ref.shape[0]
    os_ref[...] = mod[0:n_sample, :]
    op_ref[...] = mod[n_sample:, :]
    win_ref[...] = _transpose_to_bf16(wt_ref[...], win_ref.shape[1])
    wdt_ref[...] = _transpose_to_bf16(wt_dt_ref[...], LANES)


def _prep(c_sample, c_prompt, w_ada, b_ada, w_in, steps=PREP_STEPS):
    ns, nb = c_sample.shape[0], c_prompt.shape[0]
    n_mod = w_ada.shape[1]
    mod_cols, in_cols = n_mod // steps, COL_DT // steps
    assert mod_cols % LANES == 0 and in_cols % LANES == 0
    wt = w_in.T
    n_dt = w_in.shape[1] - COL_DT
    return pl.pallas_call(
        _prep_kernel,
        grid=(steps,),
        in_specs=[pl.BlockSpec((ns, D_MODEL), lambda i: (0, 0)),
                  pl.BlockSpec((nb, D_MODEL), lambda i: (0, 0)),
                  pl.BlockSpec((D_MODEL, mod_cols), lambda i: (0, i)),
                  pl.BlockSpec((1, mod_cols), lambda i: (0, i)),
                  pl.BlockSpec((in_cols, D_MODEL), lambda i: (i, 0)),
                  pl.BlockSpec((n_dt, D_MODEL), lambda i: (COL_DT // n_dt, 0))],
        out_specs=[pl.BlockSpec((nb, mod_cols), lambda i: (0, i)),
                   pl.BlockSpec((ns, mod_cols), lambda i: (0, i)),
                   pl.BlockSpec((D_MODEL, in_cols), lambda i: (0, i)),
                   pl.BlockSpec((D_MODEL, LANES), lambda i: (0, 0))],
        out_shape=[jax.ShapeDtypeStruct((nb, n_mod), F32), jax.ShapeDtypeStruct((ns, n_mod), F32),
                   jax.ShapeDtypeStruct((D_MODEL, COL_DT), BF16), jax.ShapeDtypeStruct((D_MODEL, LANES), BF16)],
        compiler_params=pltpu.CompilerParams(dimension_semantics=("arbitrary",), vmem_limit_bytes=VMEM_LIMIT),
        name="prep",
    )(c_sample, c_prompt, w_ada, b_ada, wt, wt)


def _mixer_prompt_kernel(xa_ref, moda_ref, modb_ref, w_in_ref, w_dt_ref, expand_ref, reduce_ref,
                         conv_w_ref, conv_nw_ref, sconv_w_ref, sconv_b_ref,
                         dtb_ref, alog_ref, dexp_ref, snw_ref, w_out_ref, ln_g_ref, ln_b_ref, *rest,
                         tile, tiles_per_seq, sched, n_cast, cast_steps):
    cast_in, outs, cast_out, scratch = (rest[:n_cast], rest[n_cast:n_cast + 4],
                                        rest[n_cast + 4:2 * n_cast + 4], rest[2 * n_cast + 4:])

    @pl.when(pl.program_id(0) < cast_steps)
    def _():
        for src, dst in zip(cast_in, cast_out):
            dst[...] = src[...].astype(dst.dtype)

    _mixer_prompt_body(xa_ref, moda_ref, modb_ref, w_in_ref, w_dt_ref, expand_ref, reduce_ref,
                       conv_w_ref, conv_nw_ref, sconv_w_ref, sconv_b_ref,
                       dtb_ref, alog_ref, dexp_ref, snw_ref, w_out_ref, ln_g_ref, ln_b_ref, *outs, *scratch,
                       tile=tile, tiles_per_seq=tiles_per_seq, sched=sched)


def _mixer_prompt_body(xa_ref, moda_ref, modb_ref, w_in_ref, w_dt_ref, expand_ref, reduce_ref,
                       conv_w_ref, conv_nw_ref, sconv_w_ref, sconv_b_ref,
                       dtb_ref, alog_ref, dexp_ref, snw_ref, w_out_ref, ln_g_ref, ln_b_ref,
                       x1_ref, cst_ref, scst_ref, sst_ref,
                       p, xk, cbuf, xbuf, st_ref, xs_ref, bc_ref, dtx_ref, acsx_ref, endx_ref,
                       acst_ref, cb_ref, bmt_ref, y_ref, yc_ref,
                       *, tile, tiles_per_seq, sched):
    s = pl.program_id(0)
    jb = lax.rem(s + (tiles_per_seq - 1), tiles_per_seq)

    @pl.when(s == 0)
    def _():
        p[...] = jnp.zeros_like(p)
        xk[...] = jnp.zeros_like(xk)

    @pl.when((jb == 0) | (s == 0))
    def _():
        cbuf[...] = jnp.zeros_like(cbuf)
        xbuf[...] = jnp.zeros_like(xbuf)
        st_ref[...] = jnp.zeros_like(st_ref)

    def stages():
        xa = xa_ref[...]
        u = (xa * (1.0 + moda_ref[:, D_MODEL:2 * D_MODEL]) + moda_ref[:, 0:D_MODEL]).astype(BF16)
        free = []

        def first_stage(n):
            for _ in range(min(n, len(free))):
                lo = free.pop(0)
                if lo == COL_DT:
                    p[:, COL_DT:IN_PAD] = _dot(u, w_dt_ref[...])
                else:
                    p[:, lo:lo + MXU_COLS] = _dot(u, w_in_ref[:, lo:lo + MXU_COLS])

        expand = expand_ref[...]
        x = xk[...]
        g1 = modb_ref[:, 2 * D_MODEL:3 * D_MODEL]

        def proj(lo, width):
            return p[:, lo:lo + width]

        def delayed(tail_ref, cs, cur, taps):
            seq = jnp.concatenate([tail_ref[:, cs], cur], axis=0)
            tail_ref[:, cs] = cur[tile - SUBLANES:, :]
            return [pltpu.roll(seq, k, axis=0)[SUBLANES:, :] for k in range(1, taps + 1)]

        for k in range(CONV_CH // MXU_COLS):
            c0 = k * MXU_COLS
            cs = slice(c0, c0 + MXU_COLS)
            ch = proj(COL_GC + c0, MXU_COLS) * proj(COL_HV + c0, MXU_COLS)
            ch1, ch2 = delayed(cbuf, cs, ch, 2)
            cv = conv_w_ref[0:1, cs] * ch2 + conv_w_ref[1:2, cs] * ch1 + conv_w_ref[2:3, cs] * ch
            prod = proj(COL_GB + c0, MXU_COLS) * cv
            free.extend((COL_GC + c0, COL_HV + c0, COL_GB + c0))
            first_stage(sched[0])
            ssum = _dot_f32_lhs(prod * prod, reduce_ref[cs, :], terms=1)
            rstd = lax.rsqrt(ssum * (1.0 / CONV_GROUP) + RMS_EPS)
            yc_ref[:, cs] = (prod * _dot_f32_lhs(rstd, expand_ref[:, cs], terms=2)
                             * conv_nw_ref[:, cs]).astype(BF16)

        def pre_conv(c0):
            cs = slice(c0, c0 + MXU_COLS)
            xbc = proj(COL_XBC + c0, MXU_COLS)
            free.append(COL_XBC + c0)
            x1, x2, x3 = delayed(xbuf, cs, xbc, 3)
            return _silu(sconv_w_ref[0:1, cs] * x3 + sconv_w_ref[1:2, cs] * x2 + sconv_w_ref[2:3, cs] * x1
                         + sconv_w_ref[3:4, cs] * xbc + sconv_b_ref[:, cs])

        row = lax.broadcasted_iota(jnp.int32, (SSM_CHUNK, SSM_CHUNK), 0)
        col = lax.broadcasted_iota(jnp.int32, (SSM_CHUNK, SSM_CHUNK), 1)
        causal = row >= col
        tri = causal.astype(BF16)
        groups = SSM_CHUNK // SUBLANES
        causal_bias = jnp.where(causal, 0.0, -jnp.inf).reshape(groups, SUBLANES, SSM_CHUNK)
        first_half = (col < SSM_HEAD_DIM).reshape(groups, SUBLANES, SSM_CHUNK)
        half_rows = col < SSM_HEAD_DIM
        chunks = [slice(c * SSM_CHUNK, (c + 1) * SSM_CHUNK) for c in range(tile // SSM_CHUNK)]

        dt = _softplus(proj(COL_DT, LANES) + dtb_ref[...])
        free.append(COL_DT)
        first_stage(sched[1])
        dta = dt * (-jnp.exp(alog_ref[...]))
        dtx_ref[...] = _dot_f32_lhs(dt, expand, terms=1)
        for c, rows in enumerate(chunks):
            acs = _dot_f32_rhs(tri, dta[rows, :])
            acs_t = acs.T
            for h in range(SSM_HEADS):
                r8 = (c * SSM_HEADS + h) * SUBLANES
                acst_ref[r8:r8 + SUBLANES, :] = jnp.broadcast_to(acs_t[h:h + 1, :], (SUBLANES, SSM_CHUNK))
            acs_x = _dot_f32_lhs(acs, expand, terms=2)
            acsx_ref[rows, :] = acs_x
            endx_ref[c * SUBLANES:(c + 1) * SUBLANES, :] = jnp.broadcast_to(acs_x[SSM_CHUNK - 1:SSM_CHUNK, :],
                                                                             (SUBLANES, SSM_CH))
        for c0 in range(SSM_CH, XBC_CH, MXU_COLS):
            first_stage(sched[2])
            bc_ref[:, c0 - SSM_CH:c0 - SSM_CH + MXU_COLS] = pre_conv(c0)
        for rows in chunks:
            for g in range(SSM_GROUPS):
                gs = slice(g * SSM_STATE, (g + 1) * SSM_STATE)
                bm = bc_ref[rows, gs]
                cm = bc_ref[rows, (SSM_GROUPS + g) * SSM_STATE:(SSM_GROUPS + g + 1) * SSM_STATE]
                cb_ref[rows, gs] = lax.dot_general(cm.astype(BF16), bm.astype(BF16), (((1,), (1,)), ((), ())),
                                                   preferred_element_type=F32)
                bmt_ref[rows, gs] = bm.T.astype(BF16)

        for c0 in range(0, SSM_CH, MXU_COLS):
            first_stage(sched[3])
            cs = slice(c0, c0 + MXU_COLS)
            g = c0 // SSM_GROUP_CH
            gs = slice(g * SSM_STATE, (g + 1) * SSM_STATE)
            xs = pre_conv(c0)
            xs_ref[:, cs] = xs
            xdt = xs * dtx_ref[:, cs]
            for c, rows in enumerate(chunks):
                first_stage(sched[4])
                acs_x = acsx_ref[rows, cs].reshape(groups, SUBLANES, MXU_COLS)
                end_x = endx_ref[c * SUBLANES:(c + 1) * SUBLANES, cs]
                xdt_c = xdt[rows, :]
                xdec = (xdt_c * jnp.exp(end_x[None] - acs_x).reshape(SSM_CHUNK, MXU_COLS)).astype(BF16)
                cm = bc_ref[rows, (SSM_GROUPS + g) * SSM_STATE:(SSM_GROUPS + g + 1) * SSM_STATE].astype(BF16)
                cb = cb_ref[rows, gs]
                st = st_ref[:, cs]
                y_off = _dot(cm, st.astype(BF16)) * jnp.exp(acs_x).reshape(SSM_CHUNK, MXU_COLS)
                st_ref[:, cs] = ((st.reshape(groups, SUBLANES, MXU_COLS) * jnp.exp(end_x)[None])
                                 .reshape(SSM_STATE, MXU_COLS) + _dot(bmt_ref[rows, gs], xdec))
                for lo in range(0, MXU_COLS, LANES):
                    h0 = (c * SSM_HEADS + (c0 + lo) // SSM_HEAD_DIM) * SUBLANES
                    slab = acs_x[:, :, lo:lo + LANES]
                    rolled = pltpu.roll(slab, SSM_HEAD_DIM, axis=2)
                    a0 = jnp.where(first_half, slab, rolled) - acst_ref[h0:h0 + SUBLANES, :][None]
                    a1 = jnp.where(first_half, rolled, slab) - acst_ref[h0 + SUBLANES:h0 + 2 * SUBLANES, :][None]
                    l0 = jnp.exp(a0 + causal_bias).reshape(SSM_CHUNK, SSM_CHUNK)
                    l1 = jnp.exp(a1 + causal_bias).reshape(SSM_CHUNK, SSM_CHUNK)
                    m = jnp.concatenate([(cb * l0).astype(BF16), (cb * l1).astype(BF16)], axis=1)
                    xp = xdt_c[:, lo:lo + LANES]
                    rhs = jnp.concatenate([jnp.where(half_rows, xp, 0.0), jnp.where(half_rows, 0.0, xp)],
                                          axis=0).astype(BF16)
                    y_ref[rows, c0 + lo:c0 + lo + LANES] = _dot(m, rhs) + y_off[:, lo:lo + LANES]

        for k in range(SSM_CH // MXU_COLS):
            first_stage(sched[5])
            c0 = k * MXU_COLS
            cs = slice(c0, c0 + MXU_COLS)
            y_ref[:, cs] = (y_ref[:, cs] + xs_ref[:, cs] * dexp_ref[:, cs]) * _silu(proj(COL_Z + c0, MXU_COLS))
            free.append(COL_Z + c0)
        first_stage(sched[6])
        m = _mix_out(yc_ref[...], _ssm_group_norm(y_ref[...], snw_ref[...]), w_out_ref)
        x1_ref[...] = _layer_norm(ALPHA * x + (1.0 + g1) * m, ln_g_ref[...], ln_b_ref[...])
        first_stage(N_PIECES)
        assert not free
        xk[...] = xa

    stages()

    @pl.when((jb == tiles_per_seq - 1) & (s > 0))
    def _():
        cst_ref[...] = cbuf[SUBLANES - 2:SUBLANES, :]
        scst_ref[...] = xbuf[SUBLANES - 3:SUBLANES, :]
        sst_ref[...] = st_ref[...].T


def _const_spec(shape):
    return pl.BlockSpec(shape, lambda *_: (0,) * len(shape), pipeline_mode=pl.Buffered(1))


def _mixer_prompt(x, mod, w_in, w_dt, conv_w, conv_nw, sconv_w, sconv_b, dtb, alog, dexp, snw, w_out, ln_g, ln_b,
                  to_cast=(), tile=MIXER_TILE, sched=MIXER_SCHED, cast_steps=CAST_STEPS):
    assert CONV_GROUP == SSM_HEAD_DIM and CONV_CH == SSM_CH
    nb, seq, _ = x.shape
    tiles_per_seq = seq // tile
    n_tiles = nb * tiles_per_seq
    kern = functools.partial(_mixer_prompt_kernel, tile=tile, tiles_per_seq=tiles_per_seq, sched=sched,
                             n_cast=len(to_cast), cast_steps=cast_steps)
    cast_block = lambda s: (jnp.minimum(s, cast_steps - 1), 0)
    cast_specs = [pl.BlockSpec((w.shape[0] // cast_steps, w.shape[1]), cast_block) for w in to_cast]
    consts = [w_in, w_dt, _head_expand(np), _group_reduce(np), conv_w, conv_nw, sconv_w, sconv_b, dtb, alog, dexp, snw,
              w_out, ln_g, ln_b]
    first = lambda s: jnp.minimum(s, n_tiles - 1)
    second = lambda s: jnp.maximum(s - 1, 0)
    return pl.pallas_call(
        kern,
        grid=(n_tiles + 1,),
        in_specs=[pl.BlockSpec((tile, D_MODEL), lambda s: (first(s), 0)),
                  pl.BlockSpec((None, 1, 6 * D_MODEL), lambda s: (first(s) // tiles_per_seq, 0, 0)),
                  pl.BlockSpec((None, 1, 6 * D_MODEL), lambda s: (second(s) // tiles_per_seq, 0, 0))]
                 + [_const_spec(a.shape) for a in consts] + cast_specs,
        out_specs=[pl.BlockSpec((tile, D_MODEL), lambda s: (second(s), 0)),
                   pl.BlockSpec((None, 2, CONV_CH), lambda s: (second(s) // tiles_per_seq, 0, 0)),
                   pl.BlockSpec((None, 3, XBC_CH), lambda s: (second(s) // tiles_per_seq, 0, 0)),
                   pl.BlockSpec((None, SSM_CH, SSM_STATE), lambda s: (second(s) // tiles_per_seq, 0, 0))]
                  + cast_specs,
        out_shape=[jax.ShapeDtypeStruct((nb * seq, D_MODEL), F32),
                   jax.ShapeDtypeStruct((nb, 2, CONV_CH), F32),
                   jax.ShapeDtypeStruct((nb, 3, XBC_CH), F32),
                   jax.ShapeDtypeStruct((nb, SSM_CH, SSM_STATE), F32)]
                  + [jax.ShapeDtypeStruct(w.shape, BF16) for w in to_cast],
        scratch_shapes=[pltpu.VMEM((tile, IN_PAD), F32),
                        pltpu.VMEM((tile, D_MODEL), F32),
                        pltpu.VMEM((SUBLANES, CONV_CH), F32),
                        pltpu.VMEM((SUBLANES, XBC_CH), F32),
                        pltpu.VMEM((SSM_STATE, SSM_CH), F32),
                        pltpu.VMEM((tile, SSM_CH), F32),
                        pltpu.VMEM((tile, 2 * SSM_GROUPS * SSM_STATE), F32),
                        pltpu.VMEM((tile, SSM_CH), F32),
                        pltpu.VMEM((tile, SSM_CH), F32),
                        pltpu.VMEM((tile // SSM_CHUNK * SUBLANES, SSM_CH), F32),
                        pltpu.VMEM((tile // SSM_CHUNK * SSM_HEADS * SUBLANES, SSM_CHUNK), F32),
                        pltpu.VMEM((tile, SSM_GROUPS * SSM_STATE), F32),
                        pltpu.VMEM((tile, SSM_GROUPS * SSM_STATE), BF16),
                        pltpu.VMEM((tile, SSM_CH), F32),
                        pltpu.VMEM((tile, CONV_CH), BF16)],
        compiler_params=pltpu.CompilerParams(dimension_semantics=("arbitrary",),
                                             vmem_limit_bytes=VMEM_LIMIT),
        name="mixer_prompt",
    )(x.reshape(nb * seq, D_MODEL), mod, mod, *consts, *to_cast)


def _ffn_state_kernel(x_ref, mod_ref, w_up_ref, w_down_ref, ln_g_ref, ln_b_ref,
                      s_ref, lhs_ref, rhs_ref, cm_ref, decx_ref,
                      o_ref, so_ref, yoff_ref, r_ref, *, ff_tile, state_block, state_steps):
    s = pl.program_id(0)
    n_tiles = pl.num_programs(0) - 1

    @pl.when(s == 0)
    def _():
        r_ref[...] = jnp.zeros_like(r_ref)

    def norm_previous():
        o_ref[...] = _layer_norm(r_ref[...], ln_g_ref[...], ln_b_ref[...])

    @pl.when(s < n_tiles)
    def _():
        norm_previous()
        x = x_ref[...]
        sh2 = mod_ref[:, 3 * D_MODEL:4 * D_MODEL]
        sc2 = mod_ref[:, 4 * D_MODEL:5 * D_MODEL]
        g2 = mod_ref[:, 5 * D_MODEL:6 * D_MODEL]
        v = (x * (1.0 + sc2) + sh2).astype(BF16)
        acc = jnp.zeros(x.shape, F32)
        for k in range(D_FF // ff_tile):
            h = jnp.maximum(_dot(v, w_up_ref[:, k * ff_tile:(k + 1) * ff_tile]), 0.0)
            acc = acc + _dot((h * h).astype(BF16), w_down_ref[k * ff_tile:(k + 1) * ff_tile, :])
        r_ref[...] = ALPHA * x + (1.0 + g2) * acc

    @pl.when(s == n_tiles)
    def _():
        norm_previous()

    @pl.when(s < state_steps)
    def _():
        _state_update(s, s_ref, lhs_ref, rhs_ref, cm_ref, decx_ref, so_ref, yoff_ref, state_block)


def _ffn_and_state(x, mod, rows_per_mod, w_up, w_down, ln_g, ln_b, state, lhs, rhs, cm, decx,
                   tile=FFN_TILE, ff_tile=FF_SLAB, state_block=STATE_BLOCK):
    rows = x.shape[0]
    n_tok = state.shape[0]
    tiles_per_mod = rows_per_mod // tile
    n_tiles = rows // tile
    state_steps = n_tok // state_block
    assert state_steps <= n_tiles + 1
    kern = functools.partial(_ffn_state_kernel, ff_tile=ff_tile, state_block=state_block, state_steps=state_steps)
    first = lambda s: jnp.minimum(s, n_tiles - 1)
    second = lambda s: jnp.maximum(s - 1, 0)
    tokens = lambda s: jnp.minimum(s, state_steps - 1)
    return pl.pallas_call(
        kern,
        grid=(n_tiles + 1,),
        in_specs=[pl.BlockSpec((tile, D_MODEL), lambda s: (first(s), 0)),
                  pl.BlockSpec((None, 1, 6 * D_MODEL), lambda s: (first(s) // tiles_per_mod, 0, 0)),
                  _const_spec(w_up.shape), _const_spec(w_down.shape),
                  _const_spec(ln_g.shape), _const_spec(ln_b.shape),
                  pl.BlockSpec((state_block, SSM_CH, SSM_STATE), lambda s: (tokens(s), 0, 0)),
                  pl.BlockSpec((UPDATE_TERMS, state_block, SSM_CH), lambda s: (0, tokens(s), 0)),
                  pl.BlockSpec((UPDATE_TERMS, state_block, 2 * SSM_STATE), lambda s: (0, tokens(s), 0)),
                  _const_spec(cm.shape), _const_spec(decx.shape)],
        out_specs=[pl.BlockSpec((tile, D_MODEL), lambda s: (second(s), 0)),
                   pl.BlockSpec((state_block, SSM_CH, SSM_STATE), lambda s: (tokens(s), 0, 0)),
                   pl.BlockSpec((n_tok, SSM_CH), lambda s: (0, 0))],
        out_shape=[jax.ShapeDtypeStruct((rows, D_MODEL), F32), jax.ShapeDtypeStruct(state.shape, F32),
                   jax.ShapeDtypeStruct((n_tok, SSM_CH), F32)],
        scratch_shapes=[pltpu.VMEM((tile, D_MODEL), F32)],
        compiler_params=pltpu.CompilerParams(dimension_semantics=("arbitrary",),
                                             vmem_limit_bytes=VMEM_LIMIT),
        name="ffn_and_state",
    )(x, mod, w_up, w_down, ln_g, ln_b, state, lhs, rhs, cm, decx)


def _ffn_stream_kernel(x_ref, mod_ref, w_up_ref, w_down_ref, ln_g_ref, ln_b_ref, o_ref, acc_ref):
    k = pl.program_id(0)

    @pl.when(k == 0)
    def _():
        acc_ref[...] = jnp.zeros_like(acc_ref)

    x = x_ref[...]
    sh2 = mod_ref[:, 3 * D_MODEL:4 * D_MODEL]
    sc2 = mod_ref[:, 4 * D_MODEL:5 * D_MODEL]
    v = (x * (1.0 + sc2) + sh2).astype(BF16)
    h = jnp.maximum(_dot(v, w_up_ref[...]), 0.0)
    acc_ref[...] += _dot((h * h).astype(BF16), w_down_ref[...])

    @pl.when(k == pl.num_programs(0) - 1)
    def _():
        g2 = mod_ref[:, 5 * D_MODEL:6 * D_MODEL]
        o_ref[...] = _layer_norm(ALPHA * x + (1.0 + g2) * acc_ref[...], ln_g_ref[...], ln_b_ref[...])


def _ffn_stream(x, mod, w_up, w_down, ln_g, ln_b, ff_tile=FF_SLAB):
    rows = x.shape[0]
    return pl.pallas_call(
        _ffn_stream_kernel,
        grid=(D_FF // ff_tile,),
        in_specs=[_const_spec(x.shape), _const_spec(mod.shape),
                  pl.BlockSpec((D_MODEL, ff_tile), lambda k: (0, k)),
                  pl.BlockSpec((ff_tile, D_MODEL), lambda k: (k, 0)),
                  _const_spec(ln_g.shape), _const_spec(ln_b.shape)],
        out_specs=pl.BlockSpec((rows, D_MODEL), lambda k: (0, 0)),
        out_shape=jax.ShapeDtypeStruct((rows, D_MODEL), F32),
        scratch_shapes=[pltpu.VMEM((rows, D_MODEL), F32)],
        compiler_params=pltpu.CompilerParams(dimension_semantics=("arbitrary",),
                                             vmem_limit_bytes=VMEM_LIMIT),
        name="ffn_stream",
    )(x, mod, w_up, w_down, ln_g, ln_b)


_PRODUCT_TERMS = ((0, 0), (0, 1), (1, 0), (0, 2), (2, 0), (1, 1))
UPDATE_TERMS = 16


def _sample_pre_kernel(x_ref, mod_ref, w_in_ref, w_dt_ref, conv_w_ref, conv_nw_ref, sconv_w_ref, sconv_b_ref,
                       dtb_ref, alog_ref, cb0_ref, cb1_ref, sb0_ref, sb1_ref, sb2_ref,
                       yconv_ref, ch_ref, xbc_ref, z_ref, xs_ref, ydiag_ref, decx_ref, cm_ref, lhs_ref, rhs_ref):
    expand = _head_expand()
    reduce = _group_reduce()
    x = x_ref[...]
    sh1 = mod_ref[:, 0:D_MODEL]
    sc1 = mod_ref[:, D_MODEL:2 * D_MODEL]
    u = (x * (1.0 + sc1) + sh1).astype(BF16)

    def proj(lo, width):
        return _dot(u, w_in_ref[:, lo:lo + width])

    ch = proj(COL_GC, CONV_CH) * proj(COL_HV, CONV_CH)
    ch_ref[...] = ch
    cw = conv_w_ref[...]
    cv = cw[0:1, :] * cb0_ref[...] + cw[1:2, :] * cb1_ref[...] + cw[2:3, :] * ch
    yconv_ref[...] = _conv_group_norm(proj(COL_GB, CONV_CH) * cv, conv_nw_ref[...], expand, reduce)

    xbc = proj(COL_XBC, XBC_CH)
    xbc_ref[...] = xbc
    sw = sconv_w_ref[...]
    xc = _silu(sw[0:1, :] * sb0_ref[...] + sw[1:2, :] * sb1_ref[...] + sw[2:3, :] * sb2_ref[...]
               + sw[3:4, :] * xbc + sconv_b_ref[...])
    xs = xc[:, 0:SSM_CH]
    xs_ref[...] = xs
    cm_ref[...] = xc[:, SSM_CH + SSM_GROUPS * SSM_STATE:XBC_CH]
    z_ref[...] = proj(COL_Z, SSM_CH)

    dt = _softplus(_dot(u, w_dt_ref[...]) + dtb_ref[...])
    dta = dt * (-jnp.exp(alog_ref[...]))
    xdt = xs * _dot_f32_lhs(dt, expand)
    decx = jnp.exp(_dot_f32_lhs(dta, expand))
    decx_ref[...] = decx
    xdt_t, dec_t = _split(xdt, 3), _split(decx, 3)
    bm_t = _split(xc[:, SSM_CH:SSM_CH + SSM_GROUPS * SSM_STATE], 3)
    group_of = lax.broadcasted_iota(jnp.int32, xdt.shape, 1) // SSM_GROUP_CH
    zeros = jnp.zeros((x.shape[0], SSM_STATE), F32)
    r = 0
    for g in range(SSM_GROUPS):
        for tx, tb in _PRODUCT_TERMS:
            lhs_ref[r] = jnp.where(group_of == g, xdt_t[tx].astype(F32), 0.0)
            rhs_ref[r] = jnp.concatenate([bm_t[tb][:, g * SSM_STATE:(g + 1) * SSM_STATE].astype(F32), zeros], axis=1)
            r += 1
    for t in range(3):
        lhs_ref[r] = dec_t[t].astype(F32)
        rhs_ref[r] = jnp.concatenate([zeros, jnp.ones_like(zeros)], axis=1)
        r += 1
    for r in range(r, UPDATE_TERMS):
        lhs_ref[r] = jnp.zeros_like(xdt)
        rhs_ref[r] = jnp.concatenate([zeros, zeros], axis=1)
    for g in range(SSM_GROUPS):
        bm = xc[:, SSM_CH + g * SSM_STATE:SSM_CH + (g + 1) * SSM_STATE]
        cm = xc[:, SSM_CH + (SSM_GROUPS + g) * SSM_STATE:SSM_CH + (SSM_GROUPS + g + 1) * SSM_STATE]
        cb = jnp.sum(cm * bm, axis=-1, keepdims=True)
        gl = g * SSM_GROUP_CH
        ydiag_ref[:, gl:gl + SSM_GROUP_CH] = cb * xdt[:, gl:gl + SSM_GROUP_CH]


def _sample_pre(x, mod, w_in, w_dt, conv_w, conv_nw, sconv_w, sconv_b, dtb, alog, cb0, cb1, sb0, sb1, sb2):
    n = x.shape[0]
    args = (x, mod, w_in, w_dt, conv_w, conv_nw, sconv_w, sconv_b, dtb, alog, cb0, cb1, sb0, sb1, sb2)
    f32_shapes = [(n, CONV_CH), (n, CONV_CH), (n, XBC_CH), (n, SSM_CH), (n, SSM_CH), (n, SSM_CH), (n, SSM_CH),
                  (n, SSM_GROUPS * SSM_STATE)]
    f32_shapes += [(UPDATE_TERMS, n, SSM_CH), (UPDATE_TERMS, n, 2 * SSM_STATE)]
    return pl.pallas_call(
        _sample_pre_kernel,
        out_shape=[jax.ShapeDtypeStruct(s, F32) for s in f32_shapes],
        compiler_params=pltpu.CompilerParams(vmem_limit_bytes=VMEM_LIMIT),
        name="sample_pre",
    )(*args)


def _state_update(i, s_ref, lhs_ref, rhs_ref, cm_ref, decx_ref, o_ref, yoff_ref, block):
    rows = UPDATE_TERMS * block
    lhs_t = lhs_ref[...].reshape(rows, SSM_CH).T.astype(BF16)
    rhs_all = rhs_ref[...].reshape(rows, 2 * SSM_STATE)
    token_of = lax.broadcasted_iota(jnp.int32, rhs_all.shape, 0) % block

    def body(k, carry):
        b = i * block + k
        s = s_ref[k]
        upd = _dot(lhs_t, jnp.where(token_of == k, rhs_all, 0.0).astype(BF16))
        o_ref[k] = s * upd[:, SSM_STATE:2 * SSM_STATE] + upd[:, 0:SSM_STATE]
        cm = cm_ref[pl.ds(b, 1), :]
        sums = []
        for c0 in range(0, SSM_CH, LANES):
            g = c0 // SSM_GROUP_CH
            prod = s[c0:c0 + LANES, :] * cm[:, g * SSM_STATE:(g + 1) * SSM_STATE]
            sums.append(jnp.sum(prod.T, axis=0, keepdims=True))
        yoff_ref[pl.ds(b, 1), :] = jnp.concatenate(sums, axis=1) * decx_ref[pl.ds(b, 1), :]
        return carry

    lax.fori_loop(0, block, body, 0, unroll=True)


def _sample_post_kernel(x_ref, mod_ref, yconv_ref, ydiag_ref, yoff_ref, xs_ref, z_ref, dexp_ref, snw_ref,
                        w_out_ref, ln_g_ref, ln_b_ref, x1_ref):
    g1 = mod_ref[:, 2 * D_MODEL:3 * D_MODEL]
    y = ydiag_ref[...] + yoff_ref[...] + xs_ref[...] * dexp_ref[...]
    y = y * _silu(z_ref[...])
    m = _mix_out(yconv_ref[...], _ssm_group_norm(y, snw_ref[...]), w_out_ref)
    x1_ref[...] = _layer_norm(ALPHA * x_ref[...] + (1.0 + g1) * m, ln_g_ref[...], ln_b_ref[...])


def _sample_post(x, mod, yconv, ydiag, yoff, xs, z, dexp, snw, w_out, ln_g, ln_b):
    return pl.pallas_call(
        _sample_post_kernel,
        out_shape=jax.ShapeDtypeStruct(x.shape, F32),
        compiler_params=pltpu.CompilerParams(vmem_limit_bytes=VMEM_LIMIT),
        name="sample_post",
    )(x, mod, yconv, ydiag, yoff, xs, z, dexp, snw, w_out, ln_g, ln_b)


def kernel(x_prompt, x_sample, state_conv, state_ssm_conv, state_ssm, c_prompt, c_sample, w_ada, b_ada, w_in, conv_w, conv_norm_w, ssm_conv_w, ssm_conv_b, dt_bias, a_log, d_skip, ssm_norm_w, w_out, ln1_g, ln1_b, w_up, w_down, ln2_g, ln2_b):
    assert w_ada.shape[0] == 1, "single-layer trunk"
    nb, seq, _ = x_prompt.shape
    ns = x_sample.shape[0]
    row = lambda a: a.reshape(1, -1)
    pad_heads = lambda a: jnp.pad(a.reshape(1, -1), ((0, 0), (0, LANES - SSM_HEADS)))

    w_out_b = w_out[0].astype(BF16)
    conv_nw, sconv_b, snw = row(conv_norm_w[0]), row(ssm_conv_b[0]), row(ssm_norm_w[0])
    dtb, alog = pad_heads(dt_bias[0]), pad_heads(a_log[0])
    dexp = row(jnp.repeat(d_skip[0], SSM_HEAD_DIM))
    g1, b1, g2, b2 = row(ln1_g[0]), row(ln1_b[0]), row(ln2_g[0]), row(ln2_b[0])

    mod_p, mod_s, w_in_b, w_dt_b = _prep(c_sample, c_prompt, w_ada[0], row(b_ada[0]), w_in[0])
    mod_p = mod_p.reshape(nb, 1, 6 * D_MODEL)

    x1_p, cst_p, scst_p, sst_p, w_up_b, w_down_b = _mixer_prompt(
        x_prompt, mod_p, w_in_b, w_dt_b, conv_w[0], conv_nw, ssm_conv_w[0], sconv_b, dtb, alog, dexp, snw, w_out_b,
        g1, b1, to_cast=(w_up[0], w_down[0]))

    xs2 = x_sample.reshape(ns, D_MODEL)
    (yconv_s, ch_s, xbc_s, z_s, xs_s, ydiag_s, decx_s, cm_s, lhs_s, rhs_s) = _sample_pre(
        xs2, mod_s, w_in_b, w_dt_b, conv_w[0], conv_nw, ssm_conv_w[0], sconv_b, dtb, alog,
        state_conv[0, :, 0], state_conv[0, :, 1],
        state_ssm_conv[0, :, 0], state_ssm_conv[0, :, 1], state_ssm_conv[0, :, 2])

    y_p, new_state_s, yoff_s = _ffn_and_state(x1_p, mod_p, seq, w_up_b, w_down_b, g2, b2,
                                              state_ssm[0].reshape(ns, SSM_CH, SSM_STATE), lhs_s, rhs_s, cm_s, decx_s)

    x1_s = _sample_post(xs2, mod_s, yconv_s, ydiag_s, yoff_s, xs_s, z_s, dexp, snw, w_out_b, g1, b1)
    y_s = _ffn_stream(x1_s, mod_s, w_up_b, w_down_b, g2, b2)

    return (y_p.reshape(nb, seq, D_MODEL),
            y_s.reshape(ns, 1, D_MODEL),
            cst_p[None],
            scst_p[None],
            sst_p.reshape(1, nb, SSM_HEADS, SSM_HEAD_DIM, SSM_STATE),
            jnp.stack([state_conv[0, :, 1], ch_s], axis=1)[None],
            jnp.stack([state_ssm_conv[0, :, 1], state_ssm_conv[0, :, 2], xbc_s], axis=1)[None],
            new_state_s.reshape(1, ns, SSM_HEADS, SSM_HEAD_DIM, SSM_STATE))
```

```python
import functools

import jax
import jax.numpy as jnp
import numpy as np
from jax import lax
from jax.experimental import pallas as pl
from jax.experimental.pallas import tpu as pltpu

F32 = jnp.float32
BF16 = jnp.bfloat16

D_MODEL = 1024
CONV_CH = 1024
CONV_GROUP = 64
SSM_CH = 1024
SSM_HEADS = 16
SSM_HEAD_DIM = 64
SSM_GROUPS = 2
SSM_GROUP_CH = SSM_CH // SSM_GROUPS
SSM_STATE = 128
SSM_CHUNK = 128
XBC_CH = SSM_CH + 2 * SSM_GROUPS * SSM_STATE
D_FF = 4 * D_MODEL
LANES = 128
SUBLANES = 8
MXU_COLS = 256
COL_GB, COL_GC, COL_HV, COL_Z, COL_XBC = 0, 1024, 2048, 3072, 4096
COL_DT = COL_XBC + XBC_CH
IN_PAD = COL_DT + LANES
N_PIECES = COL_DT // MXU_COLS + 1
ALPHA = 2.0 ** 0.25
LN_EPS = 1e-5
RMS_EPS = 1e-5
VMEM_LIMIT = 60 * 1024 * 1024

PREP_STEPS = 4
MIXER_TILE = 256
MIXER_SCHED = (1, 0, 0, 2, 1, 0, 0)
CAST_STEPS = 16
FFN_TILE = 512
FF_SLAB = 1024
STATE_BLOCK = 8


def _dot(a, b):
    return jnp.dot(a, b, preferred_element_type=F32)


def _split(a, terms):
    parts = []
    r = a
    for t in range(terms):
        p = r.astype(BF16)
        parts.append(p)
        if t + 1 < terms:
            r = r - p.astype(F32)
    return parts


def _dot_f32_lhs(a, b_exact, terms=3):
    parts = _split(a, terms)
    out = _dot(parts[0], b_exact)
    for p in parts[1:]:
        out = out + _dot(p, b_exact)
    return out


def _dot_f32_rhs(a_exact, b, terms=3):
    parts = _split(b, terms)
    out = _dot(a_exact, parts[0])
    for p in parts[1:]:
        out = out + _dot(a_exact, p)
    return out


def _head_expand(xp=jnp):
    if xp is np:
        return jnp.asarray(np.arange(SSM_CH)[None, :] // SSM_HEAD_DIM == np.arange(LANES)[:, None], BF16)
    h = lax.broadcasted_iota(jnp.int32, (LANES, SSM_CH), 0)
    c = lax.broadcasted_iota(jnp.int32, (LANES, SSM_CH), 1)
    return (c // SSM_HEAD_DIM == h).astype(BF16)


def _group_reduce(xp=jnp):
    if xp is np:
        return jnp.asarray(np.arange(CONV_CH)[:, None] // CONV_GROUP == np.arange(LANES)[None, :], BF16)
    c = lax.broadcasted_iota(jnp.int32, (CONV_CH, LANES), 0)
    k = lax.broadcasted_iota(jnp.int32, (CONV_CH, LANES), 1)
    return (c // CONV_GROUP == k).astype(BF16)


def _sigmoid(x):
    return 1.0 / (1.0 + jnp.exp(-x))


def _silu(x):
    return x * _sigmoid(x)


def _softplus(x):
    return jnp.maximum(x, 0.0) + jnp.log1p(jnp.exp(-jnp.abs(x)))


def _layer_norm(r, g, b):
    mu = jnp.mean(r, axis=-1, keepdims=True)
    d = r - mu
    var = jnp.mean(d * d, axis=-1, keepdims=True)
    return d * lax.rsqrt(var + LN_EPS) * g + b


def _conv_group_norm(prod, w, expand, reduce):
    ssum = _dot_f32_lhs(prod * prod, reduce, terms=2)
    rstd = lax.rsqrt(ssum * (1.0 / CONV_GROUP) + RMS_EPS)
    return prod * _dot_f32_lhs(rstd, expand, terms=2) * w


def _ssm_group_norm(y, w):
    outs = []
    for g in range(SSM_GROUPS):
        yg = y[:, g * SSM_GROUP_CH:(g + 1) * SSM_GROUP_CH]
        ms = jnp.mean(yg * yg, axis=-1, keepdims=True)
        outs.append((yg * lax.rsqrt(ms + RMS_EPS) * w[:, g * SSM_GROUP_CH:(g + 1) * SSM_GROUP_CH]).astype(BF16))
    return outs


def _mix_out(y_conv, y_ssm_groups, w_out_ref):
    m = _dot(y_conv.astype(BF16), w_out_ref[0:CONV_CH, :])
    for g, yg in enumerate(y_ssm_groups):
        lo = CONV_CH + g * SSM_GROUP_CH
        m = m + _dot(yg, w_out_ref[lo:lo + SSM_GROUP_CH, :])
    return m


def _transpose_to_bf16(wt, lanes):
    if wt.shape[0] < lanes:
        wt = jnp.concatenate([wt, jnp.zeros((lanes - wt.shape[0], wt.shape[1]), wt.dtype)], axis=0)
    return wt.T.astype(BF16)


def _prep_kernel(cs_ref, cp_ref, w_ref, b_ref, wt_ref, wt_dt_ref, op_ref, os_ref, win_ref, wdt_ref):
    c = jnp.concatenate([cs_ref[...], cp_ref[...]], axis=0)
    c_hi = c.astype(BF16)
    c_lo = (c - c_hi.astype(F32)).astype(BF16)
    w_hi = w_ref[...].astype(BF16)
    mod = _dot(c_hi, w_hi) + _dot(c_lo, w_hi) + b_ref[...]
    n_sample = os_ref.shape[0]
    os_ref[...] = mod[0:n_sample, :]
    op_ref[...] = mod[n_sample:, :]
    win_ref[...] = _transpose_to_bf16(wt_ref[...], win_ref.shape[1])
    wdt_ref[...] = _transpose_to_bf16(wt_dt_ref[...], LANES)


def _prep(c_sample, c_prompt, w_ada, b_ada, w_in, steps=PREP_STEPS):
    ns, nb = c_sample.shape[0], c_prompt.shape[0]
    n_mod = w_ada.shape[1]
    mod_cols, in_cols = n_mod // steps, COL_DT // steps
    assert mod_cols % LANES == 0 and in_cols % LANES == 0
    wt = w_in.T
    n_dt = w_in.shape[1] - COL_DT
    return pl.pallas_call(
        _prep_kernel,
        grid=(steps,),
        in_specs=[pl.BlockSpec((ns, D_MODEL), lambda i: (0, 0)),
                  pl.BlockSpec((nb, D_MODEL), lambda i: (0, 0)),
                  pl.BlockSpec((D_MODEL, mod_cols), lambda i: (0, i)),
                  pl.BlockSpec((1, mod_cols), lambda i: (0, i)),
                  pl.BlockSpec((in_cols, D_MODEL), lambda i: (i, 0)),
                  pl.BlockSpec((n_dt, D_MODEL), lambda i: (COL_DT // n_dt, 0))],
        out_specs=[pl.BlockSpec((nb, mod_cols), lambda i: (0, i)),
                   pl.BlockSpec((ns, mod_cols), lambda i: (0, i)),
                   pl.BlockSpec((D_MODEL, in_cols), lambda i: (0, i)),
                   pl.BlockSpec((D_MODEL, LANES), lambda i: (0, 0))],
        out_shape=[jax.ShapeDtypeStruct((nb, n_mod), F32), jax.ShapeDtypeStruct((ns, n_mod), F32),
                   jax.ShapeDtypeStruct((D_MODEL, COL_DT), BF16), jax.ShapeDtypeStruct((D_MODEL, LANES), BF16)],
        compiler_params=pltpu.CompilerParams(dimension_semantics=("arbitrary",), vmem_limit_bytes=VMEM_LIMIT),
        name="prep",
    )(c_sample, c_prompt, w_ada, b_ada, wt, wt)


def _mixer_prompt_kernel(xa_ref, moda_ref, modb_ref, w_in_ref, w_dt_ref, expand_ref, reduce_ref,
                         conv_w_ref, conv_nw_ref, sconv_w_ref, sconv_b_ref,
                         dtb_ref, alog_ref, dexp_ref, snw_ref, w_out_ref, ln_g_ref, ln_b_ref, *rest,
                         tile, tiles_per_seq, sched, n_cast, cast_steps):
    cast_in, outs, cast_out, scratch = (rest[:n_cast], rest[n_cast:n_cast + 4],
                                        rest[n_cast + 4:2 * n_cast + 4], rest[2 * n_cast + 4:])

    @pl.when(pl.program_id(0) < cast_steps)
    def _():
        for src, dst in zip(cast_in, cast_out):
            dst[...] = src[...].astype(dst.dtype)

    _mixer_prompt_body(xa_ref, moda_ref, modb_ref, w_in_ref, w_dt_ref, expand_ref, reduce_ref,
                       conv_w_ref, conv_nw_ref, sconv_w_ref, sconv_b_ref,
                       dtb_ref, alog_ref, dexp_ref, snw_ref, w_out_ref, ln_g_ref, ln_b_ref, *outs, *scratch,
                       tile=tile, tiles_per_seq=tiles_per_seq, sched=sched)


def _mixer_prompt_body(xa_ref, moda_ref, modb_ref, w_in_ref, w_dt_ref, expand_ref, reduce_ref,
                       conv_w_ref, conv_nw_ref, sconv_w_ref, sconv_b_ref,
                       dtb_ref, alog_ref, dexp_ref, snw_ref, w_out_ref, ln_g_ref, ln_b_ref,
                       x1_ref, cst_ref, scst_ref, sst_ref,
                       p, xk, cbuf, xbuf, st_ref, xs_ref, bc_ref, dtx_ref, acsx_ref, endx_ref,
                       acst_ref, cb_ref, bmt_ref, y_ref, yc_ref,
                       *, tile, tiles_per_seq, sched):
    s = pl.program_id(0)
    jb = lax.rem(s + (tiles_per_seq - 1), tiles_per_seq)

    @pl.when(s == 0)
    def _():
        p[...] = jnp.zeros_like(p)
        xk[...] = jnp.zeros_like(xk)

    @pl.when((jb == 0) | (s == 0))
    def _():
        cbuf[...] = jnp.zeros_like(cbuf)
        xbuf[...] = jnp.zeros_like(xbuf)
        st_ref[...] = jnp.zeros_like(st_ref)

    def stages():
        xa = xa_ref[...]
        u = (xa * (1.0 + moda_ref[:, D_MODEL:2 * D_MODEL]) + moda_ref[:, 0:D_MODEL]).astype(BF16)
        free = []

        def first_stage(n):
            for _ in range(min(n, len(free))):
                lo = free.pop(0)
                if lo == COL_DT:
                    p[:, COL_DT:IN_PAD] = _dot(u, w_dt_ref[...])
                else:
                    p[:, lo:lo + MXU_COLS] = _dot(u, w_in_ref[:, lo:lo + MXU_COLS])

        expand = expand_ref[...]
        x = xk[...]
        g1 = modb_ref[:, 2 * D_MODEL:3 * D_MODEL]

        def proj(lo, width):
            return p[:, lo:lo + width]

        def delayed(tail_ref, cs, cur, taps):
            seq = jnp.concatenate([tail_ref[:, cs], cur], axis=0)
            tail_ref[:, cs] = cur[tile - SUBLANES:, :]
            return [pltpu.roll(seq, k, axis=0)[SUBLANES:, :] for k in range(1, taps + 1)]

        for k in range(CONV_CH // MXU_COLS):
            c0 = k * MXU_COLS
            cs = slice(c0, c0 + MXU_COLS)
            ch = proj(COL_GC + c0, MXU_COLS) * proj(COL_HV + c0, MXU_COLS)
            ch1, ch2 = delayed(cbuf, cs, ch, 2)
            cv = conv_w_ref[0:1, cs] * ch2 + conv_w_ref[1:2, cs] * ch1 + conv_w_ref[2:3, cs] * ch
            prod = proj(COL_GB + c0, MXU_COLS) * cv
            free.extend((COL_GC + c0, COL_HV + c0, COL_GB + c0))
            first_stage(sched[0])
            ssum = _dot_f32_lhs(prod * prod, reduce_ref[cs, :], terms=1)
            rstd = lax.rsqrt(ssum * (1.0 / CONV_GROUP) + RMS_EPS)
            yc_ref[:, cs] = (prod * _dot_f32_lhs(rstd, expand_ref[:, cs], terms=2)
                             * conv_nw_ref[:, cs]).astype(BF16)

        def pre_conv(c0):
            cs = slice(c0, c0 + MXU_COLS)
            xbc = proj(COL_XBC + c0, MXU_COLS)
            free.append(COL_XBC + c0)
            x1, x2, x3 = delayed(xbuf, cs, xbc, 3)
            return _silu(sconv_w_ref[0:1, cs] * x3 + sconv_w_ref[1:2, cs] * x2 + sconv_w_ref[2:3, cs] * x1
                         + sconv_w_ref[3:4, cs] * xbc + sconv_b_ref[:, cs])

        row = lax.broadcasted_iota(jnp.int32, (SSM_CHUNK, SSM_CHUNK), 0)
        col = lax.broadcasted_iota(jnp.int32, (SSM_CHUNK, SSM_CHUNK), 1)
        causal = row >= col
        tri = causal.astype(BF16)
        groups = SSM_CHUNK // SUBLANES
        causal_bias = jnp.where(causal, 0.0, -jnp.inf).reshape(groups, SUBLANES, SSM_CHUNK)
        first_half = (col < SSM_HEAD_DIM).reshape(groups, SUBLANES, SSM_CHUNK)
        half_rows = col < SSM_HEAD_DIM
        chunks = [slice(c * SSM_CHUNK, (c + 1) * SSM_CHUNK) for c in range(tile // SSM_CHUNK)]

        dt = _softplus(proj(COL_DT, LANES) + dtb_ref[...])
        free.append(COL_DT)
        first_stage(sched[1])
        dta = dt * (-jnp.exp(alog_ref[...]))
        dtx_ref[...] = _dot_f32_lhs(dt, expand, terms=1)
        for c, rows in enumerate(chunks):
            acs = _dot_f32_rhs(tri, dta[rows, :])
            acs_t = acs.T
            for h in range(SSM_HEADS):
                r8 = (c * SSM_HEADS + h) * SUBLANES
                acst_ref[r8:r8 + SUBLANES, :] = jnp.broadcast_to(acs_t[h:h + 1, :], (SUBLANES, SSM_CHUNK))
            acs_x = _dot_f32_lhs(acs, expand, terms=2)
            acsx_ref[rows, :] = acs_x
            endx_ref[c * SUBLANES:(c + 1) * SUBLANES, :] = jnp.broadcast_to(acs_x[SSM_CHUNK - 1:SSM_CHUNK, :],
                                                                             (SUBLANES, SSM_CH))
        for c0 in range(SSM_CH, XBC_CH, MXU_COLS):
            first_stage(sched[2])
            bc_ref[:, c0 - SSM_CH:c0 - SSM_CH + MXU_COLS] = pre_conv(c0)
        for rows in chunks:
            for g in range(SSM_GROUPS):
                gs = slice(g * SSM_STATE, (g + 1) * SSM_STATE)
                bm = bc_ref[rows, gs]
                cm = bc_ref[rows, (SSM_GROUPS + g) * SSM_STATE:(SSM_GROUPS + g + 1) * SSM_STATE]
                cb_ref[rows, gs] = lax.dot_general(cm.astype(BF16), bm.astype(BF16), (((1,), (1,)), ((), ())),
                                                   preferred_element_type=F32)
                bmt_ref[rows, gs] = bm.T.astype(BF16)

        for c0 in range(0, SSM_CH, MXU_COLS):
            first_stage(sched[3])
            cs = slice(c0, c0 + MXU_COLS)
            g = c0 // SSM_GROUP_CH
            gs = slice(g * SSM_STATE, (g + 1) * SSM_STATE)
            xs = pre_conv(c0)
            xs_ref[:, cs] = xs
            xdt = xs * dtx_ref[:, cs]
            for c, rows in enumerate(chunks):
                first_stage(sched[4])
                acs_x = acsx_ref[rows, cs].reshape(groups, SUBLANES, MXU_COLS)
                end_x = endx_ref[c * SUBLANES:(c + 1) * SUBLANES, cs]
                xdt_c = xdt[rows, :]
                xdec = (xdt_c * jnp.exp(end_x[None] - acs_x).reshape(SSM_CHUNK, MXU_COLS)).astype(BF16)
                cm = bc_ref[rows, (SSM_GROUPS + g) * SSM_STATE:(SSM_GROUPS + g + 1) * SSM_STATE].astype(BF16)
                cb = cb_ref[rows, gs]
                st = st_ref[:, cs]
                y_off = _dot(cm, st.astype(BF16)) * jnp.exp(acs_x).reshape(SSM_CHUNK, MXU_COLS)
                st_ref[:, cs] = ((st.reshape(groups, SUBLANES, MXU_COLS) * jnp.exp(end_x)[None])
                                 .reshape(SSM_STATE, MXU_COLS) + _dot(bmt_ref[rows, gs], xdec))
                for lo in range(0, MXU_COLS, LANES):
                    h0 = (c * SSM_HEADS + (c0 + lo) // SSM_HEAD_DIM) * SUBLANES
                    slab = acs_x[:, :, lo:lo + LANES]
                    rolled = pltpu.roll(slab, SSM_HEAD_DIM, axis=2)
                    a0 = jnp.where(first_half, slab, rolled) - acst_ref[h0:h0 + SUBLANES, :][None]
                    a1 = jnp.where(first_half, rolled, slab) - acst_ref[h0 + SUBLANES:h0 + 2 * SUBLANES, :][None]
                    l0 = jnp.exp(a0 + causal_bias).reshape(SSM_CHUNK, SSM_CHUNK)
                    l1 = jnp.exp(a1 + causal_bias).reshape(SSM_CHUNK, SSM_CHUNK)
                    m = jnp.concatenate([(cb * l0).astype(BF16), (cb * l1).astype(BF16)], axis=1)
                    xp = xdt_c[:, lo:lo + LANES]
                    rhs = jnp.concatenate([jnp.where(half_rows, xp, 0.0), jnp.where(half_rows, 0.0, xp)],
                                          axis=0).astype(BF16)
                    y_ref[rows, c0 + lo:c0 + lo + LANES] = _dot(m, rhs) + y_off[:, lo:lo + LANES]

        for k in range(SSM_CH // MXU_COLS):
            first_stage(sched[5])
            c0 = k * MXU_COLS
            cs = slice(c0, c0 + MXU_COLS)
            y_ref[:, cs] = (y_ref[:, cs] + xs_ref[:, cs] * dexp_ref[:, cs]) * _silu(proj(COL_Z + c0, MXU_COLS))
            free.append(COL_Z + c0)
        first_stage(sched[6])
        m = _mix_out(yc_ref[...], _ssm_group_norm(y_ref[...], snw_ref[...]), w_out_ref)
        x1_ref[...] = _layer_norm(ALPHA * x + (1.0 + g1) * m, ln_g_ref[...], ln_b_ref[...])
        first_stage(N_PIECES)
        assert not free
        xk[...] = xa

    stages()

    @pl.when((jb == tiles_per_seq - 1) & (s > 0))
    def _():
        cst_ref[...] = cbuf[SUBLANES - 2:SUBLANES, :]
        scst_ref[...] = xbuf[SUBLANES - 3:SUBLANES, :]
        sst_ref[...] = st_ref[...].T


def _const_spec(shape):
    return pl.BlockSpec(shape, lambda *_: (0,) * len(shape), pipeline_mode=pl.Buffered(1))


def _mixer_prompt(x, mod, w_in, w_dt, conv_w, conv_nw, sconv_w, sconv_b, dtb, alog, dexp, snw, w_out, ln_g, ln_b,
                  to_cast=(), tile=MIXER_TILE, sched=MIXER_SCHED, cast_steps=CAST_STEPS):
    assert CONV_GROUP == SSM_HEAD_DIM and CONV_CH == SSM_CH
    nb, seq, _ = x.shape
    tiles_per_seq = seq // tile
    n_tiles = nb * tiles_per_seq
    kern = functools.partial(_mixer_prompt_kernel, tile=tile, tiles_per_seq=tiles_per_seq, sched=sched,
                             n_cast=len(to_cast), cast_steps=cast_steps)
    cast_block = lambda s: (jnp.minimum(s, cast_steps - 1), 0)
    cast_specs = [pl.BlockSpec((w.shape[0] // cast_steps, w.shape[1]), cast_block) for w in to_cast]
    consts = [w_in, w_dt, _head_expand(np), _group_reduce(np), conv_w, conv_nw, sconv_w, sconv_b, dtb, alog, dexp, snw,
              w_out, ln_g, ln_b]
    first = lambda s: jnp.minimum(s, n_tiles - 1)
    second = lambda s: jnp.maximum(s - 1, 0)
    return pl.pallas_call(
        kern,
        grid=(n_tiles + 1,),
        in_specs=[pl.BlockSpec((tile, D_MODEL), lambda s: (first(s), 0)),
                  pl.BlockSpec((None, 1, 6 * D_MODEL), lambda s: (first(s) // tiles_per_seq, 0, 0)),
                  pl.BlockSpec((None, 1, 6 * D_MODEL), lambda s: (second(s) // tiles_per_seq, 0, 0))]
                 + [_const_spec(a.shape) for a in consts] + cast_specs,
        out_specs=[pl.BlockSpec((tile, D_MODEL), lambda s: (second(s), 0)),
                   pl.BlockSpec((None, 2, CONV_CH), lambda s: (second(s) // tiles_per_seq, 0, 0)),
                   pl.BlockSpec((None, 3, XBC_CH), lambda s: (second(s) // tiles_per_seq, 0, 0)),
                   pl.BlockSpec((None, SSM_CH, SSM_STATE), lambda s: (second(s) // tiles_per_seq, 0, 0))]
                  + cast_specs,
        out_shape=[jax.ShapeDtypeStruct((nb * seq, D_MODEL), F32),
                   jax.ShapeDtypeStruct((nb, 2, CONV_CH), F32),
                   jax.ShapeDtypeStruct((nb, 3, XBC_CH), F32),
                   jax.ShapeDtypeStruct((nb, SSM_CH, SSM_STATE), F32)]
                  + [jax.ShapeDtypeStruct(w.shape, BF16) for w in to_cast],
        scratch_shapes=[pltpu.VMEM((tile, IN_PAD), F32),
                        pltpu.VMEM((tile, D_MODEL), F32),
                        pltpu.VMEM((SUBLANES, CONV_CH), F32),
                        pltpu.VMEM((SUBLANES, XBC_CH), F32),
                        pltpu.VMEM((SSM_STATE, SSM_CH), F32),
                        pltpu.VMEM((tile, SSM_CH), F32),
                        pltpu.VMEM((tile, 2 * SSM_GROUPS * SSM_STATE), F32),
                        pltpu.VMEM((tile, SSM_CH), F32),
                        pltpu.VMEM((tile, SSM_CH), F32),
                        pltpu.VMEM((tile // SSM_CHUNK * SUBLANES, SSM_CH), F32),
                        pltpu.VMEM((tile // SSM_CHUNK * SSM_HEADS * SUBLANES, SSM_CHUNK), F32),
                        pltpu.VMEM((tile, SSM_GROUPS * SSM_STATE), F32),
                        pltpu.VMEM((tile, SSM_GROUPS * SSM_STATE), BF16),
                        pltpu.VMEM((tile, SSM_CH), F32),
                        pltpu.VMEM((tile, CONV_CH), BF16)],
        compiler_params=pltpu.CompilerParams(dimension_semantics=("arbitrary",),
                                             vmem_limit_bytes=VMEM_LIMIT),
        name="mixer_prompt",
    )(x.reshape(nb * seq, D_MODEL), mod, mod, *consts, *to_cast)


def _ffn_residual(x, mod_ref, w_up_ref, w_down_ref, ff_tile):
    sh2 = mod_ref[:, 3 * D_MODEL:4 * D_MODEL]
    sc2 = mod_ref[:, 4 * D_MODEL:5 * D_MODEL]
    g2 = mod_ref[:, 5 * D_MODEL:6 * D_MODEL]
    v = (x * (1.0 + sc2) + sh2).astype(BF16)
    acc = jnp.zeros(x.shape, F32)
    for k in range(D_FF // ff_tile):
        h = jnp.maximum(_dot(v, w_up_ref[:, k * ff_tile:(k + 1) * ff_tile]), 0.0)
        acc = acc + _dot((h * h).astype(BF16), w_down_ref[k * ff_tile:(k + 1) * ff_tile, :])
    return ALPHA * x + (1.0 + g2) * acc


def _ffn_state_kernel(x_ref, mod_ref, w_up_ref, w_down_ref, ln_g_ref, ln_b_ref,
                      s_ref, lhs_ref, rhs_ref, cm_ref, decx_ref,
                      xs_in_ref, mods_ref, yconv_ref, ydiag_ref, xs_ref, z_ref, dexp_ref, snw_ref, w_out_ref,
                      ln1_g_ref, ln1_b_ref,
                      o_ref, so_ref, yoff_ref, ys_ref, r_ref, *, ff_tile, state_block, state_steps):
    s = pl.program_id(0)
    n_tiles = pl.num_programs(0) - 1

    @pl.when(s == 0)
    def _():
        r_ref[...] = jnp.zeros_like(r_ref)

    def norm_previous():
        o_ref[...] = _layer_norm(r_ref[...], ln_g_ref[...], ln_b_ref[...])

    @pl.when(s < n_tiles)
    def _():
        norm_previous()
        r_ref[...] = _ffn_residual(x_ref[...], mod_ref, w_up_ref, w_down_ref, ff_tile)

    @pl.when(s == n_tiles)
    def _():
        norm_previous()
        g1 = mods_ref[:, 2 * D_MODEL:3 * D_MODEL]
        y = ydiag_ref[...] + yoff_ref[...] + xs_ref[...] * dexp_ref[...]
        y = y * _silu(z_ref[...])
        m = _mix_out(yconv_ref[...], _ssm_group_norm(y, snw_ref[...]), w_out_ref)
        x1 = _layer_norm(ALPHA * xs_in_ref[...] + (1.0 + g1) * m, ln1_g_ref[...], ln1_b_ref[...])
        ys_ref[...] = _layer_norm(_ffn_residual(x1, mods_ref, w_up_ref, w_down_ref, ff_tile),
                                  ln_g_ref[...], ln_b_ref[...])

    @pl.when(s < state_steps)
    def _():
        _state_update(s, s_ref, lhs_ref, rhs_ref, cm_ref, decx_ref, so_ref, yoff_ref, state_block)


def _ffn_and_state(x, mod, rows_per_mod, w_up, w_down, ln_g, ln_b, state, lhs, rhs, cm, decx, sample_rest,
                   tile=FFN_TILE, ff_tile=FF_SLAB, state_block=STATE_BLOCK):
    rows = x.shape[0]
    n_tok = state.shape[0]
    tiles_per_mod = rows_per_mod // tile
    n_tiles = rows // tile
    state_steps = n_tok // state_block
    assert state_steps <= n_tiles
    kern = functools.partial(_ffn_state_kernel, ff_tile=ff_tile, state_block=state_block, state_steps=state_steps)
    first = lambda s: jnp.minimum(s, n_tiles - 1)
    second = lambda s: jnp.maximum(s - 1, 0)
    tokens = lambda s: jnp.minimum(s, state_steps - 1)
    y, new_state, _, y_sample = pl.pallas_call(
        kern,
        grid=(n_tiles + 1,),
        in_specs=[pl.BlockSpec((tile, D_MODEL), lambda s: (first(s), 0)),
                  pl.BlockSpec((None, 1, 6 * D_MODEL), lambda s: (first(s) // tiles_per_mod, 0, 0)),
                  _const_spec(w_up.shape), _const_spec(w_down.shape),
                  _const_spec(ln_g.shape), _const_spec(ln_b.shape),
                  pl.BlockSpec((state_block, SSM_CH, SSM_STATE), lambda s: (tokens(s), 0, 0)),
                  pl.BlockSpec((UPDATE_TERMS, state_block, SSM_CH), lambda s: (0, tokens(s), 0)),
                  pl.BlockSpec((UPDATE_TERMS, state_block, 2 * SSM_STATE), lambda s: (0, tokens(s), 0)),
                  _const_spec(cm.shape), _const_spec(decx.shape)]
                 + [_const_spec(a.shape) for a in sample_rest],
        out_specs=[pl.BlockSpec((tile, D_MODEL), lambda s: (second(s), 0)),
                   pl.BlockSpec((state_block, SSM_CH, SSM_STATE), lambda s: (tokens(s), 0, 0)),
                   pl.BlockSpec((n_tok, SSM_CH), lambda s: (0, 0)),
                   pl.BlockSpec((n_tok, D_MODEL), lambda s: (0, 0))],
        out_shape=[jax.ShapeDtypeStruct((rows, D_MODEL), F32), jax.ShapeDtypeStruct(state.shape, F32),
                   jax.ShapeDtypeStruct((n_tok, SSM_CH), F32), jax.ShapeDtypeStruct((n_tok, D_MODEL), F32)],
        scratch_shapes=[pltpu.VMEM((tile, D_MODEL), F32)],
        compiler_params=pltpu.CompilerParams(dimension_semantics=("arbitrary",),
                                             vmem_limit_bytes=VMEM_LIMIT),
        name="ffn_and_state",
    )(x, mod, w_up, w_down, ln_g, ln_b, state, lhs, rhs, cm, decx, *sample_rest)
    return y, new_state, y_sample


_PRODUCT_TERMS = ((0, 0), (0, 1), (1, 0), (0, 2), (2, 0), (1, 1))
UPDATE_TERMS = 16


def _sample_pre_kernel(x_ref, mod_ref, w_in_ref, w_dt_ref, conv_w_ref, conv_nw_ref, sconv_w_ref, sconv_b_ref,
                       dtb_ref, alog_ref, cb0_ref, cb1_ref, sb0_ref, sb1_ref, sb2_ref,
                       yconv_ref, ch_ref, xbc_ref, z_ref, xs_ref, ydiag_ref, decx_ref, cm_ref, lhs_ref, rhs_ref):
    expand = _head_expand()
    reduce = _group_reduce()
    x = x_ref[...]
    sh1 = mod_ref[:, 0:D_MODEL]
    sc1 = mod_ref[:, D_MODEL:2 * D_MODEL]
    u = (x * (1.0 + sc1) + sh1).astype(BF16)

    def proj(lo, width):
        return _dot(u, w_in_ref[:, lo:lo + width])

    ch = proj(COL_GC, CONV_CH) * proj(COL_HV, CONV_CH)
    ch_ref[...] = ch
    cw = conv_w_ref[...]
    cv = cw[0:1, :] * cb0_ref[...] + cw[1:2, :] * cb1_ref[...] + cw[2:3, :] * ch
    yconv_ref[...] = _conv_group_norm(proj(COL_GB, CONV_CH) * cv, conv_nw_ref[...], expand, reduce)

    xbc = proj(COL_XBC, XBC_CH)
    xbc_ref[...] = xbc
    sw = sconv_w_ref[...]
    xc = _silu(sw[0:1, :] * sb0_ref[...] + sw[1:2, :] * sb1_ref[...] + sw[2:3, :] * sb2_ref[...]
               + sw[3:4, :] * xbc + sconv_b_ref[...])
    xs = xc[:, 0:SSM_CH]
    xs_ref[...] = xs
    cm_ref[...] = xc[:, SSM_CH + SSM_GROUPS * SSM_STATE:XBC_CH]
    z_ref[...] = proj(COL_Z, SSM_CH)

    dt = _softplus(_dot(u, w_dt_ref[...]) + dtb_ref[...])
    dta = dt * (-jnp.exp(alog_ref[...]))
    xdt = xs * _dot_f32_lhs(dt, expand)
    decx = jnp.exp(_dot_f32_lhs(dta, expand))
    decx_ref[...] = decx
    xdt_t, dec_t = _split(xdt, 3), _split(decx, 3)
    bm_t = _split(xc[:, SSM_CH:SSM_CH + SSM_GROUPS * SSM_STATE], 3)
    group_of = lax.broadcasted_iota(jnp.int32, xdt.shape, 1) // SSM_GROUP_CH
    zeros = jnp.zeros((x.shape[0], SSM_STATE), F32)
    r = 0
    for g in range(SSM_GROUPS):
        for tx, tb in _PRODUCT_TERMS:
            lhs_ref[r] = jnp.where(group_of == g, xdt_t[tx].astype(F32), 0.0)
            rhs_ref[r] = jnp.concatenate([bm_t[tb][:, g * SSM_STATE:(g + 1) * SSM_STATE].astype(F32), zeros], axis=1)
            r += 1
    for t in range(3):
        lhs_ref[r] = dec_t[t].astype(F32)
        rhs_ref[r] = jnp.concatenate([zeros, jnp.ones_like(zeros)], axis=1)
        r += 1
    for r in range(r, UPDATE_TERMS):
        lhs_ref[r] = jnp.zeros_like(xdt)
        rhs_ref[r] = jnp.concatenate([zeros, zeros], axis=1)
    for g in range(SSM_GROUPS):
        bm = xc[:, SSM_CH + g * SSM_STATE:SSM_CH + (g + 1) * SSM_STATE]
        cm = xc[:, SSM_CH + (SSM_GROUPS + g) * SSM_STATE:SSM_CH + (SSM_GROUPS + g + 1) * SSM_STATE]
        cb = jnp.sum(cm * bm, axis=-1, keepdims=True)
        gl = g * SSM_GROUP_CH
        ydiag_ref[:, gl:gl + SSM_GROUP_CH] = cb * xdt[:, gl:gl + SSM_GROUP_CH]


def _sample_pre(x, mod, w_in, w_dt, conv_w, conv_nw, sconv_w, sconv_b, dtb, alog, cb0, cb1, sb0, sb1, sb2):
    n = x.shape[0]
    args = (x, mod, w_in, w_dt, conv_w, conv_nw, sconv_w, sconv_b, dtb, alog, cb0, cb1, sb0, sb1, sb2)
    f32_shapes = [(n, CONV_CH), (n, CONV_CH), (n, XBC_CH), (n, SSM_CH), (n, SSM_CH), (n, SSM_CH), (n, SSM_CH),
                  (n, SSM_GROUPS * SSM_STATE)]
    f32_shapes += [(UPDATE_TERMS, n, SSM_CH), (UPDATE_TERMS, n, 2 * SSM_STATE)]
    return pl.pallas_call(
        _sample_pre_kernel,
        out_shape=[jax.ShapeDtypeStruct(s, F32) for s in f32_shapes],
        compiler_params=pltpu.CompilerParams(vmem_limit_bytes=VMEM_LIMIT),
        name="sample_pre",
    )(*args)


def _state_update(i, s_ref, lhs_ref, rhs_ref, cm_ref, decx_ref, o_ref, yoff_ref, block):
    rows = UPDATE_TERMS * block
    lhs_t = lhs_ref[...].reshape(rows, SSM_CH).T.astype(BF16)
    rhs_all = rhs_ref[...].reshape(rows, 2 * SSM_STATE)
    token_of = lax.broadcasted_iota(jnp.int32, rhs_all.shape, 0) % block

    def body(k, carry):
        b = i * block + k
        s = s_ref[k]
        upd = _dot(lhs_t, jnp.where(token_of == k, rhs_all, 0.0).astype(BF16))
        o_ref[k] = s * upd[:, SSM_STATE:2 * SSM_STATE] + upd[:, 0:SSM_STATE]
        cm = cm_ref[pl.ds(b, 1), :]
        sums = []
        for c0 in range(0, SSM_CH, LANES):
            g = c0 // SSM_GROUP_CH
            prod = s[c0:c0 + LANES, :] * cm[:, g * SSM_STATE:(g + 1) * SSM_STATE]
            sums.append(jnp.sum(prod.T, axis=0, keepdims=True))
        yoff_ref[pl.ds(b, 1), :] = jnp.concatenate(sums, axis=1) * decx_ref[pl.ds(b, 1), :]
        return carry

    lax.fori_loop(0, block, body, 0, unroll=True)


def kernel(x_prompt, x_sample, state_conv, state_ssm_conv, state_ssm, c_prompt, c_sample, w_ada, b_ada, w_in, conv_w, conv_norm_w, ssm_conv_w, ssm_conv_b, dt_bias, a_log, d_skip, ssm_norm_w, w_out, ln1_g, ln1_b, w_up, w_down, ln2_g, ln2_b):
    assert w_ada.shape[0] == 1, "single-layer trunk"
    nb, seq, _ = x_prompt.shape
    ns = x_sample.shape[0]
    row = lambda a: a.reshape(1, -1)
    pad_heads = lambda a: jnp.pad(a.reshape(1, -1), ((0, 0), (0, LANES - SSM_HEADS)))

    w_out_b = w_out[0].astype(BF16)
    conv_nw, sconv_b, snw = row(conv_norm_w[0]), row(ssm_conv_b[0]), row(ssm_norm_w[0])
    dtb, alog = pad_heads(dt_bias[0]), pad_heads(a_log[0])
    dexp = row(jnp.repeat(d_skip[0], SSM_HEAD_DIM))
    g1, b1, g2, b2 = row(ln1_g[0]), row(ln1_b[0]), row(ln2_g[0]), row(ln2_b[0])

    mod_p, mod_s, w_in_b, w_dt_b = _prep(c_sample, c_prompt, w_ada[0], row(b_ada[0]), w_in[0])
    mod_p = mod_p.reshape(nb, 1, 6 * D_MODEL)

    x1_p, cst_p, scst_p, sst_p, w_up_b, w_down_b = _mixer_prompt(
        x_prompt, mod_p, w_in_b, w_dt_b, conv_w[0], conv_nw, ssm_conv_w[0], sconv_b, dtb, alog, dexp, snw, w_out_b,
        g1, b1, to_cast=(w_up[0], w_down[0]))

    xs2 = x_sample.reshape(ns, D_MODEL)
    (yconv_s, ch_s, xbc_s, z_s, xs_s, ydiag_s, decx_s, cm_s, lhs_s, rhs_s) = _sample_pre(
        xs2, mod_s, w_in_b, w_dt_b, conv_w[0], conv_nw, ssm_conv_w[0], sconv_b, dtb, alog,
        state_conv[0, :, 0], state_conv[0, :, 1],
        state_ssm_conv[0, :, 0], state_ssm_conv[0, :, 1], state_ssm_conv[0, :, 2])

    y_p, new_state_s, y_s = _ffn_and_state(
        x1_p, mod_p, seq, w_up_b, w_down_b, g2, b2,
        state_ssm[0].reshape(ns, SSM_CH, SSM_STATE), lhs_s, rhs_s, cm_s, decx_s,
        sample_rest=(xs2, mod_s, yconv_s, ydiag_s, xs_s, z_s, dexp, snw, w_out_b, g1, b1))

    return (y_p.reshape(nb, seq, D_MODEL),
            y_s.reshape(ns, 1, D_MODEL),
            cst_p[None],
            scst_p[None],
            sst_p.reshape(1, nb, SSM_HEADS, SSM_HEAD_DIM, SSM_STATE),
            jnp.stack([state_conv[0, :, 1], ch_s], axis=1)[None],
            jnp.stack([state_ssm_conv[0, :, 1], state_ssm_conv[0, :, 2], xbc_s], axis=1)[None],
            new_state_s.reshape(1, ns, SSM_HEADS, SSM_HEAD_DIM, SSM_STATE))
```

```python
import functools

import jax
import jax.numpy as jnp
import numpy as np
from jax import lax
from jax.experimental import pallas as pl
from jax.experimental.pallas import tpu as pltpu

F32 = jnp.float32
BF16 = jnp.bfloat16

D_MODEL = 1024
CONV_CH = 1024
CONV_GROUP = 64
SSM_CH = 1024
SSM_HEADS = 16
SSM_HEAD_DIM = 64
SSM_GROUPS = 2
SSM_GROUP_CH = SSM_CH // SSM_GROUPS
SSM_STATE = 128
SSM_CHUNK = 128
XBC_CH = SSM_CH + 2 * SSM_GROUPS * SSM_STATE
D_FF = 4 * D_MODEL
LANES = 128
SUBLANES = 8
MXU_COLS = 256
COL_GB, COL_GC, COL_HV, COL_Z, COL_XBC = 0, 1024, 2048, 3072, 4096
COL_DT = COL_XBC + XBC_CH
IN_PAD = COL_DT + LANES
N_PIECES = COL_DT // MXU_COLS + 1
ALPHA = 2.0 ** 0.25
LN_EPS = 1e-5
RMS_EPS = 1e-5
VMEM_LIMIT = 60 * 1024 * 1024

PREP_STEPS = 4
MIXER_TILE = 256
MIXER_SCHED = (1, 0, 0, 2, 1, 0, 0)
CAST_STEPS = 16
FFN_TILE = 512
FF_SLAB = 1024
STATE_BLOCK = 8


def _dot(a, b):
    return jnp.dot(a, b, preferred_element_type=F32)


def _split(a, terms):
    parts = []
    r = a
    for t in range(terms):
        p = r.astype(BF16)
        parts.append(p)
        if t + 1 < terms:
            r = r - p.astype(F32)
    return parts


def _dot_f32_lhs(a, b_exact, terms=3):
    parts = _split(a, terms)
    out = _dot(parts[0], b_exact)
    for p in parts[1:]:
        out = out + _dot(p, b_exact)
    return out


def _dot_f32_rhs(a_exact, b, terms=3):
    parts = _split(b, terms)
    out = _dot(a_exact, parts[0])
    for p in parts[1:]:
        out = out + _dot(a_exact, p)
    return out


def _head_expand(xp=jnp):
    if xp is np:
        return jnp.asarray(np.arange(SSM_CH)[None, :] // SSM_HEAD_DIM == np.arange(LANES)[:, None], BF16)
    h = lax.broadcasted_iota(jnp.int32, (LANES, SSM_CH), 0)
    c = lax.broadcasted_iota(jnp.int32, (LANES, SSM_CH), 1)
    return (c // SSM_HEAD_DIM == h).astype(BF16)


def _group_reduce(xp=jnp):
    if xp is np:
        return jnp.asarray(np.arange(CONV_CH)[:, None] // CONV_GROUP == np.arange(LANES)[None, :], BF16)
    c = lax.broadcasted_iota(jnp.int32, (CONV_CH, LANES), 0)
    k = lax.broadcasted_iota(jnp.int32, (CONV_CH, LANES), 1)
    return (c // CONV_GROUP == k).astype(BF16)


def _sigmoid(x):
    return 1.0 / (1.0 + jnp.exp(-x))


def _silu(x):
    return x * _sigmoid(x)


def _softplus(x):
    return jnp.maximum(x, 0.0) + jnp.log1p(jnp.exp(-jnp.abs(x)))


def _layer_norm(r, g, b):
    mu = jnp.mean(r, axis=-1, keepdims=True)
    d = r - mu
    var = jnp.mean(d * d, axis=-1, keepdims=True)
    return d * lax.rsqrt(var + LN_EPS) * g + b


def _conv_group_norm(prod, w, expand, reduce):
    ssum = _dot_f32_lhs(prod * prod, reduce, terms=2)
    rstd = lax.rsqrt(ssum * (1.0 / CONV_GROUP) + RMS_EPS)
    return prod * _dot_f32_lhs(rstd, expand, terms=2) * w


def _ssm_group_norm(y, w):
    outs = []
    for g in range(SSM_GROUPS):
        yg = y[:, g * SSM_GROUP_CH:(g + 1) * SSM_GROUP_CH]
        ms = jnp.mean(yg * yg, axis=-1, keepdims=True)
        outs.append((yg * lax.rsqrt(ms + RMS_EPS) * w[:, g * SSM_GROUP_CH:(g + 1) * SSM_GROUP_CH]).astype(BF16))
    return outs


def _mix_out(y_conv, y_ssm_groups, w_out_ref):
    m = _dot(y_conv.astype(BF16), w_out_ref[0:CONV_CH, :])
    for g, yg in enumerate(y_ssm_groups):
        lo = CONV_CH + g * SSM_GROUP_CH
        m = m + _dot(yg, w_out_ref[lo:lo + SSM_GROUP_CH, :])
    return m


def _transpose_to_bf16(wt, lanes):
    if wt.shape[0] < lanes:
        wt = jnp.concatenate([wt, jnp.zeros((lanes - wt.shape[0], wt.shape[1]), wt.dtype)], axis=0)
    return wt.T.astype(BF16)


def _prep_kernel(cs_ref, cp_ref, w_ref, b_ref, wt_ref, wt_dt_ref, op_ref, os_ref, win_ref, wdt_ref):
    c = jnp.concatenate([cs_ref[...], cp_ref[...]], axis=0)
    c_hi = c.astype(BF16)
    c_lo = (c - c_hi.astype(F32)).astype(BF16)
    w_hi = w_ref[...].astype(BF16)
    mod = _dot(c_hi, w_hi) + _dot(c_lo, w_hi) + b_ref[...]
    n_sample = os_ref.shape[0]
    os_ref[...] = mod[0:n_sample, :]
    op_ref[...] = mod[n_sample:, :]
    win_ref[...] = _transpose_to_bf16(wt_ref[...], win_ref.shape[1])
    wdt_ref[...] = _transpose_to_bf16(wt_dt_ref[...], LANES)


def _prep(c_sample, c_prompt, w_ada, b_ada, w_in, steps=PREP_STEPS):
    ns, nb = c_sample.shape[0], c_prompt.shape[0]
    n_mod = w_ada.shape[1]
    mod_cols, in_cols = n_mod // steps, COL_DT // steps
    assert mod_cols % LANES == 0 and in_cols % LANES == 0
    wt = w_in.T
    n_dt = w_in.shape[1] - COL_DT
    return pl.pallas_call(
        _prep_kernel,
        grid=(steps,),
        in_specs=[pl.BlockSpec((ns, D_MODEL), lambda i: (0, 0)),
                  pl.BlockSpec((nb, D_MODEL), lambda i: (0, 0)),
                  pl.BlockSpec((D_MODEL, mod_cols), lambda i: (0, i)),
                  pl.BlockSpec((1, mod_cols), lambda i: (0, i)),
                  pl.BlockSpec((in_cols, D_MODEL), lambda i: (i, 0)),
                  pl.BlockSpec((n_dt, D_MODEL), lambda i: (COL_DT // n_dt, 0))],
        out_specs=[pl.BlockSpec((nb, mod_cols), lambda i: (0, i)),
                   pl.BlockSpec((ns, mod_cols), lambda i: (0, i)),
                   pl.BlockSpec((D_MODEL, in_cols), lambda i: (0, i)),
                   pl.BlockSpec((D_MODEL, LANES), lambda i: (0, 0))],
        out_shape=[jax.ShapeDtypeStruct((nb, n_mod), F32), jax.ShapeDtypeStruct((ns, n_mod), F32),
                   jax.ShapeDtypeStruct((D_MODEL, COL_DT), BF16), jax.ShapeDtypeStruct((D_MODEL, LANES), BF16)],
        compiler_params=pltpu.CompilerParams(dimension_semantics=("arbitrary",), vmem_limit_bytes=VMEM_LIMIT),
        name="prep",
    )(c_sample, c_prompt, w_ada, b_ada, wt, wt)


def _mixer_prompt_kernel(xa_ref, moda_ref, modb_ref, w_in_ref, w_dt_ref, expand_ref, reduce_ref,
                         conv_w_ref, conv_nw_ref, sconv_w_ref, sconv_b_ref,
                         dtb_ref, alog_ref, dexp_ref, snw_ref, w_out_ref, ln_g_ref, ln_b_ref, *rest,
                         tile, tiles_per_seq, sched, n_cast, cast_steps):
    cast_in, outs, cast_out, scratch = (rest[:n_cast], rest[n_cast:n_cast + 4],
                                        rest[n_cast + 4:2 * n_cast + 4], rest[2 * n_cast + 4:])

    @pl.when(pl.program_id(0) < cast_steps)
    def _():
        for src, dst in zip(cast_in, cast_out):
            dst[...] = src[...].astype(dst.dtype)

    _mixer_prompt_body(xa_ref, moda_ref, modb_ref, w_in_ref, w_dt_ref, expand_ref, reduce_ref,
                       conv_w_ref, conv_nw_ref, sconv_w_ref, sconv_b_ref,
                       dtb_ref, alog_ref, dexp_ref, snw_ref, w_out_ref, ln_g_ref, ln_b_ref, *outs, *scratch,
                       tile=tile, tiles_per_seq=tiles_per_seq, sched=sched)


def _mixer_prompt_body(xa_ref, moda_ref, modb_ref, w_in_ref, w_dt_ref, expand_ref, reduce_ref,
                       conv_w_ref, conv_nw_ref, sconv_w_ref, sconv_b_ref,
                       dtb_ref, alog_ref, dexp_ref, snw_ref, w_out_ref, ln_g_ref, ln_b_ref,
                       x1_ref, cst_ref, scst_ref, sst_ref,
                       p, xk, cbuf, xbuf, st_ref, xs_ref, bc_ref, dtx_ref, acsx_ref, endx_ref,
                       acst_ref, cb_ref, bmt_ref, y_ref, yc_ref,
                       *, tile, tiles_per_seq, sched):
    s = pl.program_id(0)
    jb = lax.rem(s + (tiles_per_seq - 1), tiles_per_seq)

    @pl.when(s == 0)
    def _():
        p[...] = jnp.zeros_like(p)
        xk[...] = jnp.zeros_like(xk)

    @pl.when((jb == 0) | (s == 0))
    def _():
        cbuf[...] = jnp.zeros_like(cbuf)
        xbuf[...] = jnp.zeros_like(xbuf)
        st_ref[...] = jnp.zeros_like(st_ref)

    def stages():
        xa = xa_ref[...]
        u = (xa * (1.0 + moda_ref[:, D_MODEL:2 * D_MODEL]) + moda_ref[:, 0:D_MODEL]).astype(BF16)
        free = []

        def first_stage(n):
            for _ in range(min(n, len(free))):
                lo = free.pop(0)
                if lo == COL_DT:
                    p[:, COL_DT:IN_PAD] = _dot(u, w_dt_ref[...])
                else:
                    p[:, lo:lo + MXU_COLS] = _dot(u, w_in_ref[:, lo:lo + MXU_COLS])

        expand = expand_ref[...]
        x = xk[...]
        g1 = modb_ref[:, 2 * D_MODEL:3 * D_MODEL]

        def proj(lo, width):
            return p[:, lo:lo + width]

        def delayed(tail_ref, cs, cur, taps):
            seq = jnp.concatenate([tail_ref[:, cs], cur], axis=0)
            tail_ref[:, cs] = cur[tile - SUBLANES:, :]
            return [pltpu.roll(seq, k, axis=0)[SUBLANES:, :] for k in range(1, taps + 1)]

        for k in range(CONV_CH // MXU_COLS):
            c0 = k * MXU_COLS
            cs = slice(c0, c0 + MXU_COLS)
            ch = proj(COL_GC + c0, MXU_COLS) * proj(COL_HV + c0, MXU_COLS)
            ch1, ch2 = delayed(cbuf, cs, ch, 2)
            cv = conv_w_ref[0:1, cs] * ch2 + conv_w_ref[1:2, cs] * ch1 + conv_w_ref[2:3, cs] * ch
            prod = proj(COL_GB + c0, MXU_COLS) * cv
            free.extend((COL_GC + c0, COL_HV + c0, COL_GB + c0))
            first_stage(sched[0])
            ssum = _dot_f32_lhs(prod * prod, reduce_ref[cs, :], terms=1)
            rstd = lax.rsqrt(ssum * (1.0 / CONV_GROUP) + RMS_EPS)
            yc_ref[:, cs] = (prod * _dot_f32_lhs(rstd, expand_ref[:, cs], terms=2)
                             * conv_nw_ref[:, cs]).astype(BF16)

        def pre_conv(c0):
            cs = slice(c0, c0 + MXU_COLS)
            xbc = proj(COL_XBC + c0, MXU_COLS)
            free.append(COL_XBC + c0)
            x1, x2, x3 = delayed(xbuf, cs, xbc, 3)
            return _silu(sconv_w_ref[0:1, cs] * x3 + sconv_w_ref[1:2, cs] * x2 + sconv_w_ref[2:3, cs] * x1
                         + sconv_w_ref[3:4, cs] * xbc + sconv_b_ref[:, cs])

        row = lax.broadcasted_iota(jnp.int32, (SSM_CHUNK, SSM_CHUNK), 0)
        col = lax.broadcasted_iota(jnp.int32, (SSM_CHUNK, SSM_CHUNK), 1)
        causal = row >= col
        tri = causal.astype(BF16)
        groups = SSM_CHUNK // SUBLANES
        causal_bias = jnp.where(causal, 0.0, -jnp.inf).reshape(groups, SUBLANES, SSM_CHUNK)
        first_half = (col < SSM_HEAD_DIM).reshape(groups, SUBLANES, SSM_CHUNK)
        half_rows = col < SSM_HEAD_DIM
        chunks = [slice(c * SSM_CHUNK, (c + 1) * SSM_CHUNK) for c in range(tile // SSM_CHUNK)]

        dt = _softplus(proj(COL_DT, LANES) + dtb_ref[...])
        free.append(COL_DT)
        first_stage(sched[1])
        dta = dt * (-jnp.exp(alog_ref[...]))
        dtx_ref[...] = _dot_f32_lhs(dt, expand, terms=1)
        for c, rows in enumerate(chunks):
            acs = _dot_f32_rhs(tri, dta[rows, :])
            acs_t = acs.T
            for h in range(SSM_HEADS):
                r8 = (c * SSM_HEADS + h) * SUBLANES
                acst_ref[r8:r8 + SUBLANES, :] = jnp.broadcast_to(acs_t[h:h + 1, :], (SUBLANES, SSM_CHUNK))
            acs_x = _dot_f32_lhs(acs, expand, terms=2)
            acsx_ref[rows, :] = acs_x
            endx_ref[c * SUBLANES:(c + 1) * SUBLANES, :] = jnp.broadcast_to(acs_x[SSM_CHUNK - 1:SSM_CHUNK, :],
                                                                             (SUBLANES, SSM_CH))
        for c0 in range(SSM_CH, XBC_CH, MXU_COLS):
            first_stage(sched[2])
            bc_ref[:, c0 - SSM_CH:c0 - SSM_CH + MXU_COLS] = pre_conv(c0)
        for rows in chunks:
            for g in range(SSM_GROUPS):
                gs = slice(g * SSM_STATE, (g + 1) * SSM_STATE)
                bm = bc_ref[rows, gs]
                cm = bc_ref[rows, (SSM_GROUPS + g) * SSM_STATE:(SSM_GROUPS + g + 1) * SSM_STATE]
                cb_ref[rows, gs] = lax.dot_general(cm.astype(BF16), bm.astype(BF16), (((1,), (1,)), ((), ())),
                                                   preferred_element_type=F32)
                bmt_ref[rows, gs] = bm.T.astype(BF16)

        for c0 in range(0, SSM_CH, MXU_COLS):
            first_stage(sched[3])
            cs = slice(c0, c0 + MXU_COLS)
            g = c0 // SSM_GROUP_CH
            gs = slice(g * SSM_STATE, (g + 1) * SSM_STATE)
            xs = pre_conv(c0)
            xs_ref[:, cs] = xs
            xdt = xs * dtx_ref[:, cs]
            for c, rows in enumerate(chunks):
                first_stage(sched[4])
                acs_x = acsx_ref[rows, cs].reshape(groups, SUBLANES, MXU_COLS)
                end_x = endx_ref[c * SUBLANES:(c + 1) * SUBLANES, cs]
                xdt_c = xdt[rows, :]
                xdec = (xdt_c * jnp.exp(end_x[None] - acs_x).reshape(SSM_CHUNK, MXU_COLS)).astype(BF16)
                cm = bc_ref[rows, (SSM_GROUPS + g) * SSM_STATE:(SSM_GROUPS + g + 1) * SSM_STATE].astype(BF16)
                cb = cb_ref[rows, gs]
                st = st_ref[:, cs]
                y_off = _dot(cm, st.astype(BF16)) * jnp.exp(acs_x).reshape(SSM_CHUNK, MXU_COLS)
                st_ref[:, cs] = ((st.reshape(groups, SUBLANES, MXU_COLS) * jnp.exp(end_x)[None])
                                 .reshape(SSM_STATE, MXU_COLS) + _dot(bmt_ref[rows, gs], xdec))
                for lo in range(0, MXU_COLS, LANES):
                    h0 = (c * SSM_HEADS + (c0 + lo) // SSM_HEAD_DIM) * SUBLANES
                    slab = acs_x[:, :, lo:lo + LANES]
                    rolled = pltpu.roll(slab, SSM_HEAD_DIM, axis=2)
                    a0 = jnp.where(first_half, slab, rolled) - acst_ref[h0:h0 + SUBLANES, :][None]
                    a1 = jnp.where(first_half, rolled, slab) - acst_ref[h0 + SUBLANES:h0 + 2 * SUBLANES, :][None]
                    l0 = jnp.exp(a0 + causal_bias).reshape(SSM_CHUNK, SSM_CHUNK)
                    l1 = jnp.exp(a1 + causal_bias).reshape(SSM_CHUNK, SSM_CHUNK)
                    m = jnp.concatenate([(cb * l0).astype(BF16), (cb * l1).astype(BF16)], axis=1)
                    xp = xdt_c[:, lo:lo + LANES]
                    rhs = jnp.concatenate([jnp.where(half_rows, xp, 0.0), jnp.where(half_rows, 0.0, xp)],
                                          axis=0).astype(BF16)
                    y_ref[rows, c0 + lo:c0 + lo + LANES] = _dot(m, rhs) + y_off[:, lo:lo + LANES]

        for k in range(SSM_CH // MXU_COLS):
            first_stage(sched[5])
            c0 = k * MXU_COLS
            cs = slice(c0, c0 + MXU_COLS)
            y_ref[:, cs] = (y_ref[:, cs] + xs_ref[:, cs] * dexp_ref[:, cs]) * _silu(proj(COL_Z + c0, MXU_COLS))
            free.append(COL_Z + c0)
        first_stage(sched[6])
        m = _mix_out(yc_ref[...], _ssm_group_norm(y_ref[...], snw_ref[...]), w_out_ref)
        x1_ref[...] = _layer_norm(ALPHA * x + (1.0 + g1) * m, ln_g_ref[...], ln_b_ref[...])
        first_stage(N_PIECES)
        assert not free
        xk[...] = xa

    stages()

    @pl.when((jb == tiles_per_seq - 1) & (s > 0))
    def _():
        cst_ref[...] = cbuf[SUBLANES - 2:SUBLANES, :]
        scst_ref[...] = xbuf[SUBLANES - 3:SUBLANES, :]
        sst_ref[...] = st_ref[...].T


def _const_spec(shape):
    return pl.BlockSpec(shape, lambda *_: (0,) * len(shape), pipeline_mode=pl.Buffered(1))


def _mixer_prompt(x, mod, w_in, w_dt, conv_w, conv_nw, sconv_w, sconv_b, dtb, alog, dexp, snw, w_out, ln_g, ln_b,
                  to_cast=(), tile=MIXER_TILE, sched=MIXER_SCHED, cast_steps=CAST_STEPS):
    assert CONV_GROUP == SSM_HEAD_DIM and CONV_CH == SSM_CH
    nb, seq, _ = x.shape
    tiles_per_seq = seq // tile
    n_tiles = nb * tiles_per_seq
    kern = functools.partial(_mixer_prompt_kernel, tile=tile, tiles_per_seq=tiles_per_seq, sched=sched,
                             n_cast=len(to_cast), cast_steps=cast_steps)
    cast_block = lambda s: (jnp.minimum(s, cast_steps - 1), 0)
    cast_specs = [pl.BlockSpec((w.shape[0] // cast_steps, w.shape[1]), cast_block) for w in to_cast]
    consts = [w_in, w_dt, _head_expand(np), _group_reduce(np), conv_w, conv_nw, sconv_w, sconv_b, dtb, alog, dexp, snw,
              w_out, ln_g, ln_b]
    first = lambda s: jnp.minimum(s, n_tiles - 1)
    second = lambda s: jnp.maximum(s - 1, 0)
    return pl.pallas_call(
        kern,
        grid=(n_tiles + 1,),
        in_specs=[pl.BlockSpec((tile, D_MODEL), lambda s: (first(s), 0)),
                  pl.BlockSpec((None, 1, 6 * D_MODEL), lambda s: (first(s) // tiles_per_seq, 0, 0)),
                  pl.BlockSpec((None, 1, 6 * D_MODEL), lambda s: (second(s) // tiles_per_seq, 0, 0))]
                 + [_const_spec(a.shape) for a in consts] + cast_specs,
        out_specs=[pl.BlockSpec((tile, D_MODEL), lambda s: (second(s), 0)),
                   pl.BlockSpec((None, 2, CONV_CH), lambda s: (second(s) // tiles_per_seq, 0, 0)),
                   pl.BlockSpec((None, 3, XBC_CH), lambda s: (second(s) // tiles_per_seq, 0, 0)),
                   pl.BlockSpec((None, SSM_CH, SSM_STATE), lambda s: (second(s) // tiles_per_seq, 0, 0))]
                  + cast_specs,
        out_shape=[jax.ShapeDtypeStruct((nb * seq, D_MODEL), F32),
                   jax.ShapeDtypeStruct((nb, 2, CONV_CH), F32),
                   jax.ShapeDtypeStruct((nb, 3, XBC_CH), F32),
                   jax.ShapeDtypeStruct((nb, SSM_CH, SSM_STATE), F32)]
                  + [jax.ShapeDtypeStruct(w.shape, BF16) for w in to_cast],
        scratch_shapes=[pltpu.VMEM((tile, IN_PAD), F32),
                        pltpu.VMEM((tile, D_MODEL), F32),
                        pltpu.VMEM((SUBLANES, CONV_CH), F32),
                        pltpu.VMEM((SUBLANES, XBC_CH), F32),
                        pltpu.VMEM((SSM_STATE, SSM_CH), F32),
                        pltpu.VMEM((tile, SSM_CH), F32),
                        pltpu.VMEM((tile, 2 * SSM_GROUPS * SSM_STATE), F32),
                        pltpu.VMEM((tile, SSM_CH), F32),
                        pltpu.VMEM((tile, SSM_CH), F32),
                        pltpu.VMEM((tile // SSM_CHUNK * SUBLANES, SSM_CH), F32),
                        pltpu.VMEM((tile // SSM_CHUNK * SSM_HEADS * SUBLANES, SSM_CHUNK), F32),
                        pltpu.VMEM((tile, SSM_GROUPS * SSM_STATE), F32),
                        pltpu.VMEM((tile, SSM_GROUPS * SSM_STATE), BF16),
                        pltpu.VMEM((tile, SSM_CH), F32),
                        pltpu.VMEM((tile, CONV_CH), BF16)],
        compiler_params=pltpu.CompilerParams(dimension_semantics=("arbitrary",),
                                             vmem_limit_bytes=VMEM_LIMIT),
        name="mixer_prompt",
    )(x.reshape(nb * seq, D_MODEL), mod, mod, *consts, *to_cast)


def _ffn_residual(x, mod_ref, w_up_ref, w_down_ref, ff_tile):
    sh2 = mod_ref[:, 3 * D_MODEL:4 * D_MODEL]
    sc2 = mod_ref[:, 4 * D_MODEL:5 * D_MODEL]
    g2 = mod_ref[:, 5 * D_MODEL:6 * D_MODEL]
    v = (x * (1.0 + sc2) + sh2).astype(BF16)
    acc = jnp.zeros(x.shape, F32)
    for k in range(D_FF // ff_tile):
        h = jnp.maximum(_dot(v, w_up_ref[:, k * ff_tile:(k + 1) * ff_tile]), 0.0)
        acc = acc + _dot((h * h).astype(BF16), w_down_ref[k * ff_tile:(k + 1) * ff_tile, :])
    return ALPHA * x + (1.0 + g2) * acc


def _ffn_state_kernel(x_ref, mod_ref, w_up_ref, w_down_ref, ln_g_ref, ln_b_ref,
                      s_ref, xdt_ref, bm_ref, cm_ref, decx_ref,
                      xs_in_ref, mods_ref, yconv_ref, ydiag_ref, xs_ref, z_ref, dexp_ref, snw_ref, w_out_ref,
                      ln1_g_ref, ln1_b_ref,
                      o_ref, so_ref, yoff_ref, ys_ref, r_ref, *, ff_tile, state_block, state_steps):
    s = pl.program_id(0)
    n_tiles = pl.num_programs(0) - 1

    @pl.when(s == 0)
    def _():
        r_ref[...] = jnp.zeros_like(r_ref)

    def norm_previous():
        o_ref[...] = _layer_norm(r_ref[...], ln_g_ref[...], ln_b_ref[...])

    @pl.when(s < n_tiles)
    def _():
        norm_previous()
        r_ref[...] = _ffn_residual(x_ref[...], mod_ref, w_up_ref, w_down_ref, ff_tile)

    @pl.when(s == n_tiles)
    def _():
        norm_previous()
        g1 = mods_ref[:, 2 * D_MODEL:3 * D_MODEL]
        y = ydiag_ref[...] + yoff_ref[...] + xs_ref[...] * dexp_ref[...]
        y = y * _silu(z_ref[...])
        m = _mix_out(yconv_ref[...], _ssm_group_norm(y, snw_ref[...]), w_out_ref)
        x1 = _layer_norm(ALPHA * xs_in_ref[...] + (1.0 + g1) * m, ln1_g_ref[...], ln1_b_ref[...])
        ys_ref[...] = _layer_norm(_ffn_residual(x1, mods_ref, w_up_ref, w_down_ref, ff_tile),
                                  ln_g_ref[...], ln_b_ref[...])

    @pl.when(s < state_steps)
    def _():
        _state_update(s, s_ref, xdt_ref, bm_ref, cm_ref, decx_ref, so_ref, yoff_ref, state_block)


def _ffn_and_state(x, mod, rows_per_mod, w_up, w_down, ln_g, ln_b, state, xdt, bm, cm, decx, sample_rest,
                   tile=FFN_TILE, ff_tile=FF_SLAB, state_block=STATE_BLOCK):
    rows = x.shape[0]
    n_tok = state.shape[0]
    tiles_per_mod = rows_per_mod // tile
    n_tiles = rows // tile
    state_steps = n_tok // state_block
    assert state_steps <= n_tiles
    kern = functools.partial(_ffn_state_kernel, ff_tile=ff_tile, state_block=state_block, state_steps=state_steps)
    first = lambda s: jnp.minimum(s, n_tiles - 1)
    second = lambda s: jnp.maximum(s - 1, 0)
    tokens = lambda s: jnp.minimum(s, state_steps - 1)
    y, new_state, _, y_sample = pl.pallas_call(
        kern,
        grid=(n_tiles + 1,),
        in_specs=[pl.BlockSpec((tile, D_MODEL), lambda s: (first(s), 0)),
                  pl.BlockSpec((None, 1, 6 * D_MODEL), lambda s: (first(s) // tiles_per_mod, 0, 0)),
                  _const_spec(w_up.shape), _const_spec(w_down.shape),
                  _const_spec(ln_g.shape), _const_spec(ln_b.shape),
                  pl.BlockSpec((state_block, SSM_CH, SSM_STATE), lambda s: (tokens(s), 0, 0)),
                  _const_spec(xdt.shape), _const_spec(bm.shape), _const_spec(cm.shape), _const_spec(decx.shape)]
                 + [_const_spec(a.shape) for a in sample_rest],
        out_specs=[pl.BlockSpec((tile, D_MODEL), lambda s: (second(s), 0)),
                   pl.BlockSpec((state_block, SSM_CH, SSM_STATE), lambda s: (tokens(s), 0, 0)),
                   pl.BlockSpec((n_tok, SSM_CH), lambda s: (0, 0)),
                   pl.BlockSpec((n_tok, D_MODEL), lambda s: (0, 0))],
        out_shape=[jax.ShapeDtypeStruct((rows, D_MODEL), F32), jax.ShapeDtypeStruct(state.shape, F32),
                   jax.ShapeDtypeStruct((n_tok, SSM_CH), F32), jax.ShapeDtypeStruct((n_tok, D_MODEL), F32)],
        scratch_shapes=[pltpu.VMEM((tile, D_MODEL), F32)],
        compiler_params=pltpu.CompilerParams(dimension_semantics=("arbitrary",),
                                             vmem_limit_bytes=VMEM_LIMIT),
        name="ffn_and_state",
    )(x, mod, w_up, w_down, ln_g, ln_b, state, xdt, bm, cm, decx, *sample_rest)
    return y, new_state, y_sample


_PRODUCT_TERMS = ((0, 0), (0, 1), (1, 0), (0, 2), (2, 0), (1, 1))
UPDATE_TERMS = 16


def _sample_pre_kernel(x_ref, mod_ref, w_in_ref, w_dt_ref, conv_w_ref, conv_nw_ref, sconv_w_ref, sconv_b_ref,
                       dtb_ref, alog_ref, cb0_ref, cb1_ref, sb0_ref, sb1_ref, sb2_ref,
                       yconv_ref, ch_ref, xbc_ref, z_ref, xs_ref, ydiag_ref, decx_ref, cm_ref, xdt_ref, bm_ref):
    expand = _head_expand()
    reduce = _group_reduce()
    x = x_ref[...]
    sh1 = mod_ref[:, 0:D_MODEL]
    sc1 = mod_ref[:, D_MODEL:2 * D_MODEL]
    u = (x * (1.0 + sc1) + sh1).astype(BF16)

    def proj(lo, width):
        return _dot(u, w_in_ref[:, lo:lo + width])

    ch = proj(COL_GC, CONV_CH) * proj(COL_HV, CONV_CH)
    ch_ref[...] = ch
    cw = conv_w_ref[...]
    cv = cw[0:1, :] * cb0_ref[...] + cw[1:2, :] * cb1_ref[...] + cw[2:3, :] * ch
    yconv_ref[...] = _conv_group_norm(proj(COL_GB, CONV_CH) * cv, conv_nw_ref[...], expand, reduce)

    xbc = proj(COL_XBC, XBC_CH)
    xbc_ref[...] = xbc
    sw = sconv_w_ref[...]
    xc = _silu(sw[0:1, :] * sb0_ref[...] + sw[1:2, :] * sb1_ref[...] + sw[2:3, :] * sb2_ref[...]
               + sw[3:4, :] * xbc + sconv_b_ref[...])
    xs = xc[:, 0:SSM_CH]
    xs_ref[...] = xs
    cm_ref[...] = xc[:, SSM_CH + SSM_GROUPS * SSM_STATE:XBC_CH]
    z_ref[...] = proj(COL_Z, SSM_CH)

    dt = _softplus(_dot(u, w_dt_ref[...]) + dtb_ref[...])
    dta = dt * (-jnp.exp(alog_ref[...]))
    xdt = xs * _dot_f32_lhs(dt, expand)
    decx = jnp.exp(_dot_f32_lhs(dta, expand))
    decx_ref[...] = decx
    xdt_ref[...] = xdt
    bm_ref[...] = xc[:, SSM_CH:SSM_CH + SSM_GROUPS * SSM_STATE]
    for g in range(SSM_GROUPS):
        bm = xc[:, SSM_CH + g * SSM_STATE:SSM_CH + (g + 1) * SSM_STATE]
        cm = xc[:, SSM_CH + (SSM_GROUPS + g) * SSM_STATE:SSM_CH + (SSM_GROUPS + g + 1) * SSM_STATE]
        cb = jnp.sum(cm * bm, axis=-1, keepdims=True)
        gl = g * SSM_GROUP_CH
        ydiag_ref[:, gl:gl + SSM_GROUP_CH] = cb * xdt[:, gl:gl + SSM_GROUP_CH]


def _sample_pre(x, mod, w_in, w_dt, conv_w, conv_nw, sconv_w, sconv_b, dtb, alog, cb0, cb1, sb0, sb1, sb2):
    n = x.shape[0]
    args = (x, mod, w_in, w_dt, conv_w, conv_nw, sconv_w, sconv_b, dtb, alog, cb0, cb1, sb0, sb1, sb2)
    f32_shapes = [(n, CONV_CH), (n, CONV_CH), (n, XBC_CH), (n, SSM_CH), (n, SSM_CH), (n, SSM_CH), (n, SSM_CH),
                  (n, SSM_GROUPS * SSM_STATE)]
    f32_shapes += [(n, SSM_CH), (n, SSM_GROUPS * SSM_STATE)]
    return pl.pallas_call(
        _sample_pre_kernel,
        out_shape=[jax.ShapeDtypeStruct(s, F32) for s in f32_shapes],
        compiler_params=pltpu.CompilerParams(vmem_limit_bytes=VMEM_LIMIT),
        name="sample_pre",
    )(*args)


def _state_update(i, s_ref, xdt_ref, bm_ref, cm_ref, decx_ref, o_ref, yoff_ref, block):
    tok = pl.ds(pl.multiple_of(i * block, block), block)
    xdt_t = [t.astype(F32) for t in _split(xdt_ref[tok, :], 3)]
    dec_t = [t.astype(F32) for t in _split(decx_ref[tok, :], 3)]
    bm_t = [t.astype(F32) for t in _split(bm_ref[tok, :], 3)]
    group_of = lax.broadcasted_iota(jnp.int32, (block, SSM_CH), 1) // SSM_GROUP_CH
    zeros = jnp.zeros((block, SSM_STATE), F32)
    lhs, rhs = [], []
    for g in range(SSM_GROUPS):
        for tx, tb in _PRODUCT_TERMS:
            lhs.append(jnp.where(group_of == g, xdt_t[tx], 0.0))
            rhs.append(jnp.concatenate([bm_t[tb][:, g * SSM_STATE:(g + 1) * SSM_STATE], zeros], axis=1))
    for t in range(3):
        lhs.append(dec_t[t])
        rhs.append(jnp.concatenate([zeros, zeros + 1.0], axis=1))
    while len(lhs) < UPDATE_TERMS:
        lhs.append(jnp.zeros((block, SSM_CH), F32))
        rhs.append(jnp.concatenate([zeros, zeros], axis=1))
    lhs_t = jnp.concatenate(lhs, axis=0).T.astype(BF16)
    rhs_all = jnp.concatenate(rhs, axis=0)
    token_of = lax.broadcasted_iota(jnp.int32, rhs_all.shape, 0) % block

    def body(k, carry):
        b = i * block + k
        s = s_ref[k]
        upd = _dot(lhs_t, jnp.where(token_of == k, rhs_all, 0.0).astype(BF16))
        o_ref[k] = s * upd[:, SSM_STATE:2 * SSM_STATE] + upd[:, 0:SSM_STATE]
        cm = cm_ref[pl.ds(b, 1), :]
        sums = []
        for c0 in range(0, SSM_CH, LANES):
            g = c0 // SSM_GROUP_CH
            prod = s[c0:c0 + LANES, :] * cm[:, g * SSM_STATE:(g + 1) * SSM_STATE]
            sums.append(jnp.sum(prod.T, axis=0, keepdims=True))
        yoff_ref[pl.ds(b, 1), :] = jnp.concatenate(sums, axis=1) * decx_ref[pl.ds(b, 1), :]
        return carry

    lax.fori_loop(0, block, body, 0, unroll=True)


def kernel(x_prompt, x_sample, state_conv, state_ssm_conv, state_ssm, c_prompt, c_sample, w_ada, b_ada, w_in, conv_w, conv_norm_w, ssm_conv_w, ssm_conv_b, dt_bias, a_log, d_skip, ssm_norm_w, w_out, ln1_g, ln1_b, w_up, w_down, ln2_g, ln2_b):
    assert w_ada.shape[0] == 1, "single-layer trunk"
    nb, seq, _ = x_prompt.shape
    ns = x_sample.shape[0]
    row = lambda a: a.reshape(1, -1)
    pad_heads = lambda a: jnp.pad(a.reshape(1, -1), ((0, 0), (0, LANES - SSM_HEADS)))

    w_out_b = w_out[0].astype(BF16)
    conv_nw, sconv_b, snw = row(conv_norm_w[0]), row(ssm_conv_b[0]), row(ssm_norm_w[0])
    dtb, alog = pad_heads(dt_bias[0]), pad_heads(a_log[0])
    dexp = row(jnp.repeat(d_skip[0], SSM_HEAD_DIM))
    g1, b1, g2, b2 = row(ln1_g[0]), row(ln1_b[0]), row(ln2_g[0]), row(ln2_b[0])

    mod_p, mod_s, w_in_b, w_dt_b = _prep(c_sample, c_prompt, w_ada[0], row(b_ada[0]), w_in[0])
    mod_p = mod_p.reshape(nb, 1, 6 * D_MODEL)

    x1_p, cst_p, scst_p, sst_p, w_up_b, w_down_b = _mixer_prompt(
        x_prompt, mod_p, w_in_b, w_dt_b, conv_w[0], conv_nw, ssm_conv_w[0], sconv_b, dtb, alog, dexp, snw, w_out_b,
        g1, b1, to_cast=(w_up[0], w_down[0]))

    xs2 = x_sample.reshape(ns, D_MODEL)
    (yconv_s, ch_s, xbc_s, z_s, xs_s, ydiag_s, decx_s, cm_s, xdt_s, bm_s) = _sample_pre(
        xs2, mod_s, w_in_b, w_dt_b, conv_w[0], conv_nw, ssm_conv_w[0], sconv_b, dtb, alog,
        state_conv[0, :, 0], state_conv[0, :, 1],
        state_ssm_conv[0, :, 0], state_ssm_conv[0, :, 1], state_ssm_conv[0, :, 2])

    y_p, new_state_s, y_s = _ffn_and_state(
        x1_p, mod_p, seq, w_up_b, w_down_b, g2, b2,
        state_ssm[0].reshape(ns, SSM_CH, SSM_STATE), xdt_s, bm_s, cm_s, decx_s,
        sample_rest=(xs2, mod_s, yconv_s, ydiag_s, xs_s, z_s, dexp, snw, w_out_b, g1, b1))

    return (y_p.reshape(nb, seq, D_MODEL),
            y_s.reshape(ns, 1, D_MODEL),
            cst_p[None],
            scst_p[None],
            sst_p.reshape(1, nb, SSM_HEADS, SSM_HEAD_DIM, SSM_STATE),
            jnp.stack([state_conv[0, :, 1], ch_s], axis=1)[None],
            jnp.stack([state_ssm_conv[0, :, 1], state_ssm_conv[0, :, 2], xbc_s], axis=1)[None],
            new_state_s.reshape(1, ns, SSM_HEADS, SSM_HEAD_DIM, SSM_STATE))
```

```python
import functools

import jax
import jax.numpy as jnp
import numpy as np
from jax import lax
from jax.experimental import pallas as pl
from jax.experimental.pallas import tpu as pltpu

F32 = jnp.float32
BF16 = jnp.bfloat16

D_MODEL = 1024
CONV_CH = 1024
CONV_GROUP = 64
SSM_CH = 1024
SSM_HEADS = 16
SSM_HEAD_DIM = 64
SSM_GROUPS = 2
SSM_GROUP_CH = SSM_CH // SSM_GROUPS
SSM_STATE = 128
SSM_CHUNK = 128
XBC_CH = SSM_CH + 2 * SSM_GROUPS * SSM_STATE
D_FF = 4 * D_MODEL
LANES = 128
SUBLANES = 8
MXU_COLS = 256
COL_GB, COL_GC, COL_HV, COL_Z, COL_XBC = 0, 1024, 2048, 3072, 4096
COL_DT = COL_XBC + XBC_CH
IN_PAD = COL_DT + LANES
N_PIECES = COL_DT // MXU_COLS + 1
ALPHA = 2.0 ** 0.25
LN_EPS = 1e-5
RMS_EPS = 1e-5
VMEM_LIMIT = 60 * 1024 * 1024

PREP_STEPS = 4
MIXER_TILE = 256
MIXER_SCHED = (1, 0, 0, 2, 1, 0, 0)
CAST_STEPS = 64
FFN_TILE = 512
FF_SLAB = 1024
STATE_BLOCK = 8


def _dot(a, b):
    return jnp.dot(a, b, preferred_element_type=F32)


def _split(a, terms):
    parts = []
    r = a
    for t in range(terms):
        p = r.astype(BF16)
        parts.append(p)
        if t + 1 < terms:
            r = r - p.astype(F32)
    return parts


def _dot_f32_lhs(a, b_exact, terms=3):
    parts = _split(a, terms)
    out = _dot(parts[0], b_exact)
    for p in parts[1:]:
        out = out + _dot(p, b_exact)
    return out


def _dot_f32_rhs(a_exact, b, terms=3):
    parts = _split(b, terms)
    out = _dot(a_exact, parts[0])
    for p in parts[1:]:
        out = out + _dot(a_exact, p)
    return out


def _head_expand(xp=jnp):
    if xp is np:
        return jnp.asarray(np.arange(SSM_CH)[None, :] // SSM_HEAD_DIM == np.arange(LANES)[:, None], BF16)
    h = lax.broadcasted_iota(jnp.int32, (LANES, SSM_CH), 0)
    c = lax.broadcasted_iota(jnp.int32, (LANES, SSM_CH), 1)
    return (c // SSM_HEAD_DIM == h).astype(BF16)


def _group_reduce(xp=jnp):
    if xp is np:
        return jnp.asarray(np.arange(CONV_CH)[:, None] // CONV_GROUP == np.arange(LANES)[None, :], BF16)
    c = lax.broadcasted_iota(jnp.int32, (CONV_CH, LANES), 0)
    k = lax.broadcasted_iota(jnp.int32, (CONV_CH, LANES), 1)
    return (c // CONV_GROUP == k).astype(BF16)


def _sigmoid(x):
    return 1.0 / (1.0 + jnp.exp(-x))


def _silu(x):
    return x * _sigmoid(x)


def _softplus(x):
    return jnp.maximum(x, 0.0) + jnp.log1p(jnp.exp(-jnp.abs(x)))


def _layer_norm(r, g, b):
    mu = jnp.mean(r, axis=-1, keepdims=True)
    d = r - mu
    var = jnp.mean(d * d, axis=-1, keepdims=True)
    return d * lax.rsqrt(var + LN_EPS) * g + b


def _conv_group_norm(prod, w, expand, reduce):
    ssum = _dot_f32_lhs(prod * prod, reduce, terms=2)
    rstd = lax.rsqrt(ssum * (1.0 / CONV_GROUP) + RMS_EPS)
    return prod * _dot_f32_lhs(rstd, expand, terms=2) * w


def _ssm_group_norm(y, w):
    outs = []
    for g in range(SSM_GROUPS):
        yg = y[:, g * SSM_GROUP_CH:(g + 1) * SSM_GROUP_CH]
        ms = jnp.mean(yg * yg, axis=-1, keepdims=True)
        outs.append((yg * lax.rsqrt(ms + RMS_EPS) * w[:, g * SSM_GROUP_CH:(g + 1) * SSM_GROUP_CH]).astype(BF16))
    return outs


def _mix_out(y_conv, y_ssm_groups, w_out_ref):
    m = _dot(y_conv.astype(BF16), w_out_ref[0:CONV_CH, :])
    for g, yg in enumerate(y_ssm_groups):
        lo = CONV_CH + g * SSM_GROUP_CH
        m = m + _dot(yg, w_out_ref[lo:lo + SSM_GROUP_CH, :])
    return m


def _transpose_to_bf16(wt, lanes):
    if wt.shape[0] < lanes:
        wt = jnp.concatenate([wt, jnp.zeros((lanes - wt.shape[0], wt.shape[1]), wt.dtype)], axis=0)
    return wt.T.astype(BF16)


def _prep_kernel(cs_ref, cp_ref, w_ref, b_ref, wt_ref, wt_dt_ref, op_ref, os_ref, win_ref, wdt_ref):
    c = jnp.concatenate([cs_ref[...], cp_ref[...]], axis=0)
    c_hi = c.astype(BF16)
    c_lo = (c - c_hi.astype(F32)).astype(BF16)
    w_hi = w_ref[...].astype(BF16)
    mod = _dot(c_hi, w_hi) + _dot(c_lo, w_hi) + b_ref[...]
    n_sample = os_ref.shape[0]
    os_ref[...] = mod[0:n_sample, :]
    op_ref[...] = mod[n_sample:, :]
    win_ref[...] = _transpose_to_bf16(wt_ref[...], win_ref.shape[1])
    wdt_ref[...] = _transpose_to_bf16(wt_dt_ref[...], LANES)


def _prep(c_sample, c_prompt, w_ada, b_ada, w_in, steps=PREP_STEPS):
    ns, nb = c_sample.shape[0], c_prompt.shape[0]
    n_mod = w_ada.shape[1]
    mod_cols, in_cols = n_mod // steps, COL_DT // steps
    assert mod_cols % LANES == 0 and in_cols % LANES == 0
    wt = w_in.T
    n_dt = w_in.shape[1] - COL_DT
    return pl.pallas_call(
        _prep_kernel,
        grid=(steps,),
        in_specs=[pl.BlockSpec((ns, D_MODEL), lambda i: (0, 0)),
                  pl.BlockSpec((nb, D_MODEL), lambda i: (0, 0)),
                  pl.BlockSpec((D_MODEL, mod_cols), lambda i: (0, i)),
                  pl.BlockSpec((1, mod_cols), lambda i: (0, i)),
                  pl.BlockSpec((in_cols, D_MODEL), lambda i: (i, 0)),
                  pl.BlockSpec((n_dt, D_MODEL), lambda i: (COL_DT // n_dt, 0))],
        out_specs=[pl.BlockSpec((nb, mod_cols), lambda i: (0, i)),
                   pl.BlockSpec((ns, mod_cols), lambda i: (0, i)),
                   pl.BlockSpec((D_MODEL, in_cols), lambda i: (0, i)),
                   pl.BlockSpec((D_MODEL, LANES), lambda i: (0, 0))],
        out_shape=[jax.ShapeDtypeStruct((nb, n_mod), F32), jax.ShapeDtypeStruct((ns, n_mod), F32),
                   jax.ShapeDtypeStruct((D_MODEL, COL_DT), BF16), jax.ShapeDtypeStruct((D_MODEL, LANES), BF16)],
        compiler_params=pltpu.CompilerParams(dimension_semantics=("arbitrary",), vmem_limit_bytes=VMEM_LIMIT),
        name="prep",
    )(c_sample, c_prompt, w_ada, b_ada, wt, wt)


def _mixer_prompt_kernel(xa_ref, moda_ref, modb_ref, w_in_ref, w_dt_ref, expand_ref, reduce_ref,
                         conv_w_ref, conv_nw_ref, sconv_w_ref, sconv_b_ref,
                         dtb_ref, alog_ref, dexp_ref, snw_ref, w_out_ref, ln_g_ref, ln_b_ref, *rest,
                         tile, tiles_per_seq, sched, n_cast, cast_steps):
    cast_in, outs, cast_out, scratch = (rest[:n_cast], rest[n_cast:n_cast + 4],
                                        rest[n_cast + 4:2 * n_cast + 4], rest[2 * n_cast + 4:])

    for src, dst in zip(cast_in, cast_out):
        dst[...] = src[...].astype(dst.dtype)

    _mixer_prompt_body(xa_ref, moda_ref, modb_ref, w_in_ref, w_dt_ref, expand_ref, reduce_ref,
                       conv_w_ref, conv_nw_ref, sconv_w_ref, sconv_b_ref,
                       dtb_ref, alog_ref, dexp_ref, snw_ref, w_out_ref, ln_g_ref, ln_b_ref, *outs, *scratch,
                       tile=tile, tiles_per_seq=tiles_per_seq, sched=sched)


def _mixer_prompt_body(xa_ref, moda_ref, modb_ref, w_in_ref, w_dt_ref, expand_ref, reduce_ref,
                       conv_w_ref, conv_nw_ref, sconv_w_ref, sconv_b_ref,
                       dtb_ref, alog_ref, dexp_ref, snw_ref, w_out_ref, ln_g_ref, ln_b_ref,
                       x1_ref, cst_ref, scst_ref, sst_ref,
                       p, xk, cbuf, xbuf, st_ref, xs_ref, bc_ref, dtx_ref, acsx_ref, endx_ref,
                       acst_ref, cb_ref, bmt_ref, y_ref, yc_ref,
                       *, tile, tiles_per_seq, sched):
    s = pl.program_id(0)
    jb = lax.rem(s + (tiles_per_seq - 1), tiles_per_seq)

    @pl.when(s == 0)
    def _():
        p[...] = jnp.zeros_like(p)
        xk[...] = jnp.zeros_like(xk)

    @pl.when((jb == 0) | (s == 0))
    def _():
        cbuf[...] = jnp.zeros_like(cbuf)
        xbuf[...] = jnp.zeros_like(xbuf)
        st_ref[...] = jnp.zeros_like(st_ref)

    def stages():
        xa = xa_ref[...]
        u = (xa * (1.0 + moda_ref[:, D_MODEL:2 * D_MODEL]) + moda_ref[:, 0:D_MODEL]).astype(BF16)
        free = []

        def first_stage(n):
            for _ in range(min(n, len(free))):
                lo = free.pop(0)
                if lo == COL_DT:
                    p[:, COL_DT:IN_PAD] = _dot(u, w_dt_ref[...])
                else:
                    p[:, lo:lo + MXU_COLS] = _dot(u, w_in_ref[:, lo:lo + MXU_COLS])

        expand = expand_ref[...]
        x = xk[...]
        g1 = modb_ref[:, 2 * D_MODEL:3 * D_MODEL]

        def proj(lo, width):
            return p[:, lo:lo + width]

        def delayed(tail_ref, cs, cur, taps):
            seq = jnp.concatenate([tail_ref[:, cs], cur], axis=0)
            tail_ref[:, cs] = cur[tile - SUBLANES:, :]
            return [pltpu.roll(seq, k, axis=0)[SUBLANES:, :] for k in range(1, taps + 1)]

        for k in range(CONV_CH // MXU_COLS):
            c0 = k * MXU_COLS
            cs = slice(c0, c0 + MXU_COLS)
            ch = proj(COL_GC + c0, MXU_COLS) * proj(COL_HV + c0, MXU_COLS)
            ch1, ch2 = delayed(cbuf, cs, ch, 2)
            cv = conv_w_ref[0:1, cs] * ch2 + conv_w_ref[1:2, cs] * ch1 + conv_w_ref[2:3, cs] * ch
            prod = proj(COL_GB + c0, MXU_COLS) * cv
            free.extend((COL_GC + c0, COL_HV + c0, COL_GB + c0))
            first_stage(sched[0])
            ssum = _dot_f32_lhs(prod * prod, reduce_ref[cs, :], terms=1)
            rstd = lax.rsqrt(ssum * (1.0 / CONV_GROUP) + RMS_EPS)
            yc_ref[:, cs] = (prod * _dot_f32_lhs(rstd, expand_ref[:, cs], terms=2)
                             * conv_nw_ref[:, cs]).astype(BF16)

        def pre_conv(c0):
            cs = slice(c0, c0 + MXU_COLS)
            xbc = proj(COL_XBC + c0, MXU_COLS)
            free.append(COL_XBC + c0)
            x1, x2, x3 = delayed(xbuf, cs, xbc, 3)
            return _silu(sconv_w_ref[0:1, cs] * x3 + sconv_w_ref[1:2, cs] * x2 + sconv_w_ref[2:3, cs] * x1
                         + sconv_w_ref[3:4, cs] * xbc + sconv_b_ref[:, cs])

        row = lax.broadcasted_iota(jnp.int32, (SSM_CHUNK, SSM_CHUNK), 0)
        col = lax.broadcasted_iota(jnp.int32, (SSM_CHUNK, SSM_CHUNK), 1)
        causal = row >= col
        tri = causal.astype(BF16)
        groups = SSM_CHUNK // SUBLANES
        causal_bias = jnp.where(causal, 0.0, -jnp.inf).reshape(groups, SUBLANES, SSM_CHUNK)
        first_half = (col < SSM_HEAD_DIM).reshape(groups, SUBLANES, SSM_CHUNK)
        half_rows = col < SSM_HEAD_DIM
        chunks = [slice(c * SSM_CHUNK, (c + 1) * SSM_CHUNK) for c in range(tile // SSM_CHUNK)]

        dt = _softplus(proj(COL_DT, LANES) + dtb_ref[...])
        free.append(COL_DT)
        first_stage(sched[1])
        dta = dt * (-jnp.exp(alog_ref[...]))
        dtx_ref[...] = _dot_f32_lhs(dt, expand, terms=1)
        for c, rows in enumerate(chunks):
            acs = _dot_f32_rhs(tri, dta[rows, :])
            acs_t = acs.T
            for h in range(SSM_HEADS):
                r8 = (c * SSM_HEADS + h) * SUBLANES
                acst_ref[r8:r8 + SUBLANES, :] = jnp.broadcast_to(acs_t[h:h + 1, :], (SUBLANES, SSM_CHUNK))
            acs_x = _dot_f32_lhs(acs, expand, terms=2)
            acsx_ref[rows, :] = acs_x
            endx_ref[c * SUBLANES:(c + 1) * SUBLANES, :] = jnp.broadcast_to(acs_x[SSM_CHUNK - 1:SSM_CHUNK, :],
                                                                             (SUBLANES, SSM_CH))
        for c0 in range(SSM_CH, XBC_CH, MXU_COLS):
            first_stage(sched[2])
            bc_ref[:, c0 - SSM_CH:c0 - SSM_CH + MXU_COLS] = pre_conv(c0)
        for rows in chunks:
            for g in range(SSM_GROUPS):
                gs = slice(g * SSM_STATE, (g + 1) * SSM_STATE)
                bm = bc_ref[rows, gs]
                cm = bc_ref[rows, (SSM_GROUPS + g) * SSM_STATE:(SSM_GROUPS + g + 1) * SSM_STATE]
                cb_ref[rows, gs] = lax.dot_general(cm.astype(BF16), bm.astype(BF16), (((1,), (1,)), ((), ())),
                                                   preferred_element_type=F32)
                bmt_ref[rows, gs] = bm.T.astype(BF16)

        for c0 in range(0, SSM_CH, MXU_COLS):
            first_stage(sched[3])
            cs = slice(c0, c0 + MXU_COLS)
            g = c0 // SSM_GROUP_CH
            gs = slice(g * SSM_STATE, (g + 1) * SSM_STATE)
            xs = pre_conv(c0)
            xs_ref[:, cs] = xs
            xdt = xs * dtx_ref[:, cs]
            for c, rows in enumerate(chunks):
                first_stage(sched[4])
                acs_x = acsx_ref[rows, cs].reshape(groups, SUBLANES, MXU_COLS)
                end_x = endx_ref[c * SUBLANES:(c + 1) * SUBLANES, cs]
                xdt_c = xdt[rows, :]
                xdec = (xdt_c * jnp.exp(end_x[None] - acs_x).reshape(SSM_CHUNK, MXU_COLS)).astype(BF16)
                cm = bc_ref[rows, (SSM_GROUPS + g) * SSM_STATE:(SSM_GROUPS + g + 1) * SSM_STATE].astype(BF16)
                cb = cb_ref[rows, gs]
                st = st_ref[:, cs]
                y_off = _dot(cm, st.astype(BF16)) * jnp.exp(acs_x).reshape(SSM_CHUNK, MXU_COLS)
                st_ref[:, cs] = ((st.reshape(groups, SUBLANES, MXU_COLS) * jnp.exp(end_x)[None])
                                 .reshape(SSM_STATE, MXU_COLS) + _dot(bmt_ref[rows, gs], xdec))
                for lo in range(0, MXU_COLS, LANES):
                    h0 = (c * SSM_HEADS + (c0 + lo) // SSM_HEAD_DIM) * SUBLANES
                    slab = acs_x[:, :, lo:lo + LANES]
                    rolled = pltpu.roll(slab, SSM_HEAD_DIM, axis=2)
                    a0 = jnp.where(first_half, slab, rolled) - acst_ref[h0:h0 + SUBLANES, :][None]
                    a1 = jnp.where(first_half, rolled, slab) - acst_ref[h0 + SUBLANES:h0 + 2 * SUBLANES, :][None]
                    l0 = jnp.exp(a0 + causal_bias).reshape(SSM_CHUNK, SSM_CHUNK)
                    l1 = jnp.exp(a1 + causal_bias).reshape(SSM_CHUNK, SSM_CHUNK)
                    m = jnp.concatenate([(cb * l0).astype(BF16), (cb * l1).astype(BF16)], axis=1)
                    xp = xdt_c[:, lo:lo + LANES]
                    rhs = jnp.concatenate([jnp.where(half_rows, xp, 0.0), jnp.where(half_rows, 0.0, xp)],
                                          axis=0).astype(BF16)
                    y_ref[rows, c0 + lo:c0 + lo + LANES] = _dot(m, rhs) + y_off[:, lo:lo + LANES]

        for k in range(SSM_CH // MXU_COLS):
            first_stage(sched[5])
            c0 = k * MXU_COLS
            cs = slice(c0, c0 + MXU_COLS)
            y_ref[:, cs] = (y_ref[:, cs] + xs_ref[:, cs] * dexp_ref[:, cs]) * _silu(proj(COL_Z + c0, MXU_COLS))
            free.append(COL_Z + c0)
        first_stage(sched[6])
        m = _mix_out(yc_ref[...], _ssm_group_norm(y_ref[...], snw_ref[...]), w_out_ref)
        x1_ref[...] = _layer_norm(ALPHA * x + (1.0 + g1) * m, ln_g_ref[...], ln_b_ref[...])
        first_stage(N_PIECES)
        assert not free
        xk[...] = xa

    stages()

    @pl.when((jb == tiles_per_seq - 1) & (s > 0))
    def _():
        cst_ref[...] = cbuf[SUBLANES - 2:SUBLANES, :]
        scst_ref[...] = xbuf[SUBLANES - 3:SUBLANES, :]
        sst_ref[...] = st_ref[...].T


def _const_spec(shape):
    return pl.BlockSpec(shape, lambda *_: (0,) * len(shape), pipeline_mode=pl.Buffered(1))


def _mixer_prompt(x, mod, w_in, w_dt, conv_w, conv_nw, sconv_w, sconv_b, dtb, alog, dexp, snw, w_out, ln_g, ln_b,
                  to_cast=(), tile=MIXER_TILE, sched=MIXER_SCHED, cast_steps=CAST_STEPS):
    assert CONV_GROUP == SSM_HEAD_DIM and CONV_CH == SSM_CH
    nb, seq, _ = x.shape
    tiles_per_seq = seq // tile
    n_tiles = nb * tiles_per_seq
    kern = functools.partial(_mixer_prompt_kernel, tile=tile, tiles_per_seq=tiles_per_seq, sched=sched,
                             n_cast=len(to_cast), cast_steps=cast_steps)
    cast_block = lambda s: (jnp.minimum(s, cast_steps - 1), 0)
    cast_specs = [pl.BlockSpec((w.shape[0] // cast_steps, w.shape[1]), cast_block) for w in to_cast]
    consts = [w_in, w_dt, _head_expand(np), _group_reduce(np), conv_w, conv_nw, sconv_w, sconv_b, dtb, alog, dexp, snw,
              w_out, ln_g, ln_b]
    first = lambda s: jnp.minimum(s, n_tiles - 1)
    second = lambda s: jnp.maximum(s - 1, 0)
    return pl.pallas_call(
        kern,
        grid=(n_tiles + 1,),
        in_specs=[pl.BlockSpec((tile, D_MODEL), lambda s: (first(s), 0)),
                  pl.BlockSpec((None, 1, 6 * D_MODEL), lambda s: (first(s) // tiles_per_seq, 0, 0)),
                  pl.BlockSpec((None, 1, 6 * D_MODEL), lambda s: (second(s) // tiles_per_seq, 0, 0))]
                 + [_const_spec(a.shape) for a in consts] + cast_specs,
        out_specs=[pl.BlockSpec((tile, D_MODEL), lambda s: (second(s), 0)),
                   pl.BlockSpec((None, 2, CONV_CH), lambda s: (second(s) // tiles_per_seq, 0, 0)),
                   pl.BlockSpec((None, 3, XBC_CH), lambda s: (second(s) // tiles_per_seq, 0, 0)),
                   pl.BlockSpec((None, SSM_CH, SSM_STATE), lambda s: (second(s) // tiles_per_seq, 0, 0))]
                  + cast_specs,
        out_shape=[jax.ShapeDtypeStruct((nb * seq, D_MODEL), F32),
                   jax.ShapeDtypeStruct((nb, 2, CONV_CH), F32),
                   jax.ShapeDtypeStruct((nb, 3, XBC_CH), F32),
                   jax.ShapeDtypeStruct((nb, SSM_CH, SSM_STATE), F32)]
                  + [jax.ShapeDtypeStruct(w.shape, BF16) for w in to_cast],
        scratch_shapes=[pltpu.VMEM((tile, IN_PAD), F32),
                        pltpu.VMEM((tile, D_MODEL), F32),
                        pltpu.VMEM((SUBLANES, CONV_CH), F32),
                        pltpu.VMEM((SUBLANES, XBC_CH), F32),
                        pltpu.VMEM((SSM_STATE, SSM_CH), F32),
                        pltpu.VMEM((tile, SSM_CH), F32),
                        pltpu.VMEM((tile, 2 * SSM_GROUPS * SSM_STATE), F32),
                        pltpu.VMEM((tile, SSM_CH), F32),
                        pltpu.VMEM((tile, SSM_CH), F32),
                        pltpu.VMEM((tile // SSM_CHUNK * SUBLANES, SSM_CH), F32),
                        pltpu.VMEM((tile // SSM_CHUNK * SSM_HEADS * SUBLANES, SSM_CHUNK), F32),
                        pltpu.VMEM((tile, SSM_GROUPS * SSM_STATE), F32),
                        pltpu.VMEM((tile, SSM_GROUPS * SSM_STATE), BF16),
                        pltpu.VMEM((tile, SSM_CH), F32),
                        pltpu.VMEM((tile, CONV_CH), BF16)],
        compiler_params=pltpu.CompilerParams(dimension_semantics=("arbitrary",),
                                             vmem_limit_bytes=VMEM_LIMIT),
        name="mixer_prompt",
    )(x.reshape(nb * seq, D_MODEL), mod, mod, *consts, *to_cast)


def _ffn_residual(x, mod_ref, w_up_ref, w_down_ref, ff_tile):
    sh2 = mod_ref[:, 3 * D_MODEL:4 * D_MODEL]
    sc2 = mod_ref[:, 4 * D_MODEL:5 * D_MODEL]
    g2 = mod_ref[:, 5 * D_MODEL:6 * D_MODEL]
    v = (x * (1.0 + sc2) + sh2).astype(BF16)
    acc = jnp.zeros(x.shape, F32)
    for k in range(D_FF // ff_tile):
        h = jnp.maximum(_dot(v, w_up_ref[:, k * ff_tile:(k + 1) * ff_tile]), 0.0)
        acc = acc + _dot((h * h).astype(BF16), w_down_ref[k * ff_tile:(k + 1) * ff_tile, :])
    return ALPHA * x + (1.0 + g2) * acc


def _ffn_state_kernel(x_ref, mod_ref, w_up_ref, w_down_ref, ln_g_ref, ln_b_ref,
                      s_ref, xdt_ref, bm_ref, cm_ref, decx_ref,
                      xs_in_ref, mods_ref, yconv_ref, ydiag_ref, xs_ref, z_ref, dexp_ref, snw_ref, w_out_ref,
                      ln1_g_ref, ln1_b_ref,
                      o_ref, so_ref, yoff_ref, ys_ref, r_ref, *, ff_tile, state_block, state_steps):
    s = pl.program_id(0)
    n_tiles = pl.num_programs(0) - 1

    @pl.when(s == 0)
    def _():
        r_ref[...] = jnp.zeros_like(r_ref)

    def norm_previous():
        o_ref[...] = _layer_norm(r_ref[...], ln_g_ref[...], ln_b_ref[...])

    @pl.when(s < n_tiles)
    def _():
        norm_previous()
        r_ref[...] = _ffn_residual(x_ref[...], mod_ref, w_up_ref, w_down_ref, ff_tile)

    @pl.when(s == n_tiles)
    def _():
        norm_previous()
        g1 = mods_ref[:, 2 * D_MODEL:3 * D_MODEL]
        y = ydiag_ref[...] + yoff_ref[...] + xs_ref[...] * dexp_ref[...]
        y = y * _silu(z_ref[...])
        m = _mix_out(yconv_ref[...], _ssm_group_norm(y, snw_ref[...]), w_out_ref)
        x1 = _layer_norm(ALPHA * xs_in_ref[...] + (1.0 + g1) * m, ln1_g_ref[...], ln1_b_ref[...])
        ys_ref[...] = _layer_norm(_ffn_residual(x1, mods_ref, w_up_ref, w_down_ref, ff_tile),
                                  ln_g_ref[...], ln_b_ref[...])

    @pl.when(s < state_steps)
    def _():
        _state_update(s, s_ref, xdt_ref, bm_ref, cm_ref, decx_ref, so_ref, yoff_ref, state_block)


def _ffn_and_state(x, mod, rows_per_mod, w_up, w_down, ln_g, ln_b, state, xdt, bm, cm, decx, sample_rest,
                   tile=FFN_TILE, ff_tile=FF_SLAB, state_block=STATE_BLOCK):
    rows = x.shape[0]
    n_tok = state.shape[0]
    tiles_per_mod = rows_per_mod // tile
    n_tiles = rows // tile
    state_steps = n_tok // state_block
    assert state_steps <= n_tiles
    kern = functools.partial(_ffn_state_kernel, ff_tile=ff_tile, state_block=state_block, state_steps=state_steps)
    first = lambda s: jnp.minimum(s, n_tiles - 1)
    second = lambda s: jnp.maximum(s - 1, 0)
    tokens = lambda s: jnp.minimum(s, state_steps - 1)
    y, new_state, _, y_sample = pl.pallas_call(
        kern,
        grid=(n_tiles + 1,),
        in_specs=[pl.BlockSpec((tile, D_MODEL), lambda s: (first(s), 0)),
                  pl.BlockSpec((None, 1, 6 * D_MODEL), lambda s: (first(s) // tiles_per_mod, 0, 0)),
                  _const_spec(w_up.shape), _const_spec(w_down.shape),
                  _const_spec(ln_g.shape), _const_spec(ln_b.shape),
                  pl.BlockSpec((state_block, SSM_CH, SSM_STATE), lambda s: (tokens(s), 0, 0)),
                  _const_spec(xdt.shape), _const_spec(bm.shape), _const_spec(cm.shape), _const_spec(decx.shape)]
                 + [_const_spec(a.shape) for a in sample_rest],
        out_specs=[pl.BlockSpec((tile, D_MODEL), lambda s: (second(s), 0)),
                   pl.BlockSpec((state_block, SSM_CH, SSM_STATE), lambda s: (tokens(s), 0, 0)),
                   pl.BlockSpec((n_tok, SSM_CH), lambda s: (0, 0)),
                   pl.BlockSpec((n_tok, D_MODEL), lambda s: (0, 0))],
        out_shape=[jax.ShapeDtypeStruct((rows, D_MODEL), F32), jax.ShapeDtypeStruct(state.shape, F32),
                   jax.ShapeDtypeStruct((n_tok, SSM_CH), F32), jax.ShapeDtypeStruct((n_tok, D_MODEL), F32)],
        scratch_shapes=[pltpu.VMEM((tile, D_MODEL), F32)],
        compiler_params=pltpu.CompilerParams(dimension_semantics=("arbitrary",),
                                             vmem_limit_bytes=VMEM_LIMIT),
        name="ffn_and_state",
    )(x, mod, w_up, w_down, ln_g, ln_b, state, xdt, bm, cm, decx, *sample_rest)
    return y, new_state, y_sample


_PRODUCT_TERMS = ((0, 0), (0, 1), (1, 0), (0, 2), (2, 0), (1, 1))
UPDATE_TERMS = 16


def _sample_pre_kernel(x_ref, mod_ref, w_in_ref, w_dt_ref, conv_w_ref, conv_nw_ref, sconv_w_ref, sconv_b_ref,
                       dtb_ref, alog_ref, cb0_ref, cb1_ref, sb0_ref, sb1_ref, sb2_ref,
                       yconv_ref, ch_ref, xbc_ref, z_ref, xs_ref, ydiag_ref, decx_ref, cm_ref, xdt_ref, bm_ref):
    expand = _head_expand()
    reduce = _group_reduce()
    x = x_ref[...]
    sh1 = mod_ref[:, 0:D_MODEL]
    sc1 = mod_ref[:, D_MODEL:2 * D_MODEL]
    u = (x * (1.0 + sc1) + sh1).astype(BF16)

    def proj(lo, width):
        return _dot(u, w_in_ref[:, lo:lo + width])

    ch = proj(COL_GC, CONV_CH) * proj(COL_HV, CONV_CH)
    ch_ref[...] = ch
    cw = conv_w_ref[...]
    cv = cw[0:1, :] * cb0_ref[...] + cw[1:2, :] * cb1_ref[...] + cw[2:3, :] * ch
    yconv_ref[...] = _conv_group_norm(proj(COL_GB, CONV_CH) * cv, conv_nw_ref[...], expand, reduce)

    xbc = proj(COL_XBC, XBC_CH)
    xbc_ref[...] = xbc
    sw = sconv_w_ref[...]
    xc = _silu(sw[0:1, :] * sb0_ref[...] + sw[1:2, :] * sb1_ref[...] + sw[2:3, :] * sb2_ref[...]
               + sw[3:4, :] * xbc + sconv_b_ref[...])
    xs = xc[:, 0:SSM_CH]
    xs_ref[...] = xs
    cm_ref[...] = xc[:, SSM_CH + SSM_GROUPS * SSM_STATE:XBC_CH]
    z_ref[...] = proj(COL_Z, SSM_CH)

    dt = _softplus(_dot(u, w_dt_ref[...]) + dtb_ref[...])
    dta = dt * (-jnp.exp(alog_ref[...]))
    xdt = xs * _dot_f32_lhs(dt, expand)
    decx = jnp.exp(_dot_f32_lhs(dta, expand))
    decx_ref[...] = decx
    xdt_ref[...] = xdt
    bm_ref[...] = xc[:, SSM_CH:SSM_CH + SSM_GROUPS * SSM_STATE]
    for g in range(SSM_GROUPS):
        bm = xc[:, SSM_CH + g * SSM_STATE:SSM_CH + (g + 1) * SSM_STATE]
        cm = xc[:, SSM_CH + (SSM_GROUPS + g) * SSM_STATE:SSM_CH + (SSM_GROUPS + g + 1) * SSM_STATE]
        cb = jnp.sum(cm * bm, axis=-1, keepdims=True)
        gl = g * SSM_GROUP_CH
        ydiag_ref[:, gl:gl + SSM_GROUP_CH] = cb * xdt[:, gl:gl + SSM_GROUP_CH]


def _sample_pre(x, mod, w_in, w_dt, conv_w, conv_nw, sconv_w, sconv_b, dtb, alog, cb0, cb1, sb0, sb1, sb2):
    n = x.shape[0]
    args = (x, mod, w_in, w_dt, conv_w, conv_nw, sconv_w, sconv_b, dtb, alog, cb0, cb1, sb0, sb1, sb2)
    f32_shapes = [(n, CONV_CH), (n, CONV_CH), (n, XBC_CH), (n, SSM_CH), (n, SSM_CH), (n, SSM_CH), (n, SSM_CH),
                  (n, SSM_GROUPS * SSM_STATE)]
    f32_shapes += [(n, SSM_CH), (n, SSM_GROUPS * SSM_STATE)]
    return pl.pallas_call(
        _sample_pre_kernel,
        out_shape=[jax.ShapeDtypeStruct(s, F32) for s in f32_shapes],
        compiler_params=pltpu.CompilerParams(vmem_limit_bytes=VMEM_LIMIT),
        name="sample_pre",
    )(*args)


def _state_update(i, s_ref, xdt_ref, bm_ref, cm_ref, decx_ref, o_ref, yoff_ref, block):
    tok = pl.ds(pl.multiple_of(i * block, block), block)
    xdt_t = [t.astype(F32) for t in _split(xdt_ref[tok, :], 3)]
    dec_t = [t.astype(F32) for t in _split(decx_ref[tok, :], 3)]
    bm_t = [t.astype(F32) for t in _split(bm_ref[tok, :], 3)]
    group_of = lax.broadcasted_iota(jnp.int32, (block, SSM_CH), 1) // SSM_GROUP_CH
    zeros = jnp.zeros((block, SSM_STATE), F32)
    lhs, rhs = [], []
    for g in range(SSM_GROUPS):
        for tx, tb in _PRODUCT_TERMS:
            lhs.append(jnp.where(group_of == g, xdt_t[tx], 0.0))
            rhs.append(jnp.concatenate([bm_t[tb][:, g * SSM_STATE:(g + 1) * SSM_STATE], zeros], axis=1))
    for t in range(3):
        lhs.append(dec_t[t])
        rhs.append(jnp.concatenate([zeros, zeros + 1.0], axis=1))
    while len(lhs) < UPDATE_TERMS:
        lhs.append(jnp.zeros((block, SSM_CH), F32))
        rhs.append(jnp.concatenate([zeros, zeros], axis=1))
    lhs_t = jnp.concatenate(lhs, axis=0).T.astype(BF16)
    rhs_all = jnp.concatenate(rhs, axis=0)
    token_of = lax.broadcasted_iota(jnp.int32, rhs_all.shape, 0) % block

    def body(k, carry):
        b = i * block + k
        s = s_ref[k]
        upd = _dot(lhs_t, jnp.where(token_of == k, rhs_all, 0.0).astype(BF16))
        o_ref[k] = s * upd[:, SSM_STATE:2 * SSM_STATE] + upd[:, 0:SSM_STATE]
        cm = cm_ref[pl.ds(b, 1), :]
        sums = []
        for c0 in range(0, SSM_CH, LANES):
            g = c0 // SSM_GROUP_CH
            prod = s[c0:c0 + LANES, :] * cm[:, g * SSM_STATE:(g + 1) * SSM_STATE]
            sums.append(jnp.sum(prod.T, axis=0, keepdims=True))
        yoff_ref[pl.ds(b, 1), :] = jnp.concatenate(sums, axis=1) * decx_ref[pl.ds(b, 1), :]
        return carry

    lax.fori_loop(0, block, body, 0, unroll=True)


def kernel(x_prompt, x_sample, state_conv, state_ssm_conv, state_ssm, c_prompt, c_sample, w_ada, b_ada, w_in, conv_w, conv_norm_w, ssm_conv_w, ssm_conv_b, dt_bias, a_log, d_skip, ssm_norm_w, w_out, ln1_g, ln1_b, w_up, w_down, ln2_g, ln2_b):
    assert w_ada.shape[0] == 1, "single-layer trunk"
    nb, seq, _ = x_prompt.shape
    ns = x_sample.shape[0]
    row = lambda a: a.reshape(1, -1)
    pad_heads = lambda a: jnp.pad(a.reshape(1, -1), ((0, 0), (0, LANES - SSM_HEADS)))

    w_out_b = w_out[0].astype(BF16)
    conv_nw, sconv_b, snw = row(conv_norm_w[0]), row(ssm_conv_b[0]), row(ssm_norm_w[0])
    dtb, alog = pad_heads(dt_bias[0]), pad_heads(a_log[0])
    dexp = row(jnp.repeat(d_skip[0], SSM_HEAD_DIM))
    g1, b1, g2, b2 = row(ln1_g[0]), row(ln1_b[0]), row(ln2_g[0]), row(ln2_b[0])

    mod_p, mod_s, w_in_b, w_dt_b = _prep(c_sample, c_prompt, w_ada[0], row(b_ada[0]), w_in[0])
    mod_p = mod_p.reshape(nb, 1, 6 * D_MODEL)

    x1_p, cst_p, scst_p, sst_p, w_up_b, w_down_b = _mixer_prompt(
        x_prompt, mod_p, w_in_b, w_dt_b, conv_w[0], conv_nw, ssm_conv_w[0], sconv_b, dtb, alog, dexp, snw, w_out_b,
        g1, b1, to_cast=(w_up[0], w_down[0]))

    xs2 = x_sample.reshape(ns, D_MODEL)
    (yconv_s, ch_s, xbc_s, z_s, xs_s, ydiag_s, decx_s, cm_s, xdt_s, bm_s) = _sample_pre(
        xs2, mod_s, w_in_b, w_dt_b, conv_w[0], conv_nw, ssm_conv_w[0], sconv_b, dtb, alog,
        state_conv[0, :, 0], state_conv[0, :, 1],
        state_ssm_conv[0, :, 0], state_ssm_conv[0, :, 1], state_ssm_conv[0, :, 2])

    y_p, new_state_s, y_s = _ffn_and_state(
        x1_p, mod_p, seq, w_up_b, w_down_b, g2, b2,
        state_ssm[0].reshape(ns, SSM_CH, SSM_STATE), xdt_s, bm_s, cm_s, decx_s,
        sample_rest=(xs2, mod_s, yconv_s, ydiag_s, xs_s, z_s, dexp, snw, w_out_b, g1, b1))

    return (y_p.reshape(nb, seq, D_MODEL),
            y_s.reshape(ns, 1, D_MODEL),
            cst_p[None],
            scst_p[None],
            sst_p.reshape(1, nb, SSM_HEADS, SSM_HEAD_DIM, SSM_STATE),
            jnp.stack([state_conv[0, :, 1], ch_s], axis=1)[None],
            jnp.stack([state_ssm_conv[0, :, 1], state_ssm_conv[0, :, 2], xbc_s], axis=1)[None],
            new_state_s.reshape(1, ns, SSM_HEADS, SSM_HEAD_DIM, SSM_STATE))
```

```python
import functools

import jax
import jax.numpy as jnp
import numpy as np
from jax import lax
from jax.experimental import pallas as pl
from jax.experimental.pallas import tpu as pltpu

F32 = jnp.float32
BF16 = jnp.bfloat16

D_MODEL = 1024
CONV_CH = 1024
CONV_GROUP = 64
SSM_CH = 1024
SSM_HEADS = 16
SSM_HEAD_DIM = 64
SSM_GROUPS = 2
SSM_GROUP_CH = SSM_CH // SSM_GROUPS
SSM_STATE = 128
SSM_CHUNK = 128
XBC_CH = SSM_CH + 2 * SSM_GROUPS * SSM_STATE
D_FF = 4 * D_MODEL
LANES = 128
SUBLANES = 8
MXU_COLS = 256
COL_GB, COL_GC, COL_HV, COL_Z, COL_XBC = 0, 1024, 2048, 3072, 4096
COL_DT = COL_XBC + XBC_CH
IN_PAD = COL_DT + LANES
N_PIECES = COL_DT // MXU_COLS + 1
ALPHA = 2.0 ** 0.25
LN_EPS = 1e-5
RMS_EPS = 1e-5
VMEM_LIMIT = 60 * 1024 * 1024

PREP_STEPS = 4
MIXER_TILE = 256
MIXER_SCHED = (1, 0, 0, 2, 1, 0, 0)
CAST_STEPS = 16
FFN_TILE = 512
FF_SLAB = 1024
STATE_BLOCK = 8


def _dot(a, b):
    return jnp.dot(a, b, preferred_element_type=F32)


def _split(a, terms):
    parts = []
    r = a
    for t in range(terms):
        p = r.astype(BF16)
        parts.append(p)
        if t + 1 < terms:
            r = r - p.astype(F32)
    return parts


def _dot_f32_lhs(a, b_exact, terms=3):
    parts = _split(a, terms)
    out = _dot(parts[0], b_exact)
    for p in parts[1:]:
        out = out + _dot(p, b_exact)
    return out


def _dot_f32_rhs(a_exact, b, terms=3):
    parts = _split(b, terms)
    out = _dot(a_exact, parts[0])
    for p in parts[1:]:
        out = out + _dot(a_exact, p)
    return out


def _head_expand(xp=jnp):
    if xp is np:
        return jnp.asarray(np.arange(SSM_CH)[None, :] // SSM_HEAD_DIM == np.arange(LANES)[:, None], BF16)
    h = lax.broadcasted_iota(jnp.int32, (LANES, SSM_CH), 0)
    c = lax.broadcasted_iota(jnp.int32, (LANES, SSM_CH), 1)
    return (c // SSM_HEAD_DIM == h).astype(BF16)


def _group_reduce(xp=jnp):
    if xp is np:
        return jnp.asarray(np.arange(CONV_CH)[:, None] // CONV_GROUP == np.arange(LANES)[None, :], BF16)
    c = lax.broadcasted_iota(jnp.int32, (CONV_CH, LANES), 0)
    k = lax.broadcasted_iota(jnp.int32, (CONV_CH, LANES), 1)
    return (c // CONV_GROUP == k).astype(BF16)


LOG2_E = 1.4426950408889634


def _sigmoid(x):
    return 1.0 / (1.0 + jnp.exp2(x * (-LOG2_E)))


def _silu(x):
    return x * _sigmoid(x)


def _softplus(x):
    return jnp.maximum(x, 0.0) + jnp.log1p(jnp.exp(-jnp.abs(x)))


def _layer_norm(r, g, b):
    mu = jnp.mean(r, axis=-1, keepdims=True)
    d = r - mu
    var = jnp.mean(d * d, axis=-1, keepdims=True)
    return d * lax.rsqrt(var + LN_EPS) * g + b


def _conv_group_norm(prod, w, expand, reduce):
    ssum = _dot_f32_lhs(prod * prod, reduce, terms=2)
    rstd = lax.rsqrt(ssum * (1.0 / CONV_GROUP) + RMS_EPS)
    return prod * _dot_f32_lhs(rstd, expand, terms=2) * w


def _ssm_group_norm(y, w):
    outs = []
    for g in range(SSM_GROUPS):
        yg = y[:, g * SSM_GROUP_CH:(g + 1) * SSM_GROUP_CH]
        ms = jnp.mean(yg * yg, axis=-1, keepdims=True)
        outs.append((yg * lax.rsqrt(ms + RMS_EPS) * w[:, g * SSM_GROUP_CH:(g + 1) * SSM_GROUP_CH]).astype(BF16))
    return outs


def _mix_out(y_conv, y_ssm_groups, w_out_ref):
    m = _dot(y_conv.astype(BF16), w_out_ref[0:CONV_CH, :])
    for g, yg in enumerate(y_ssm_groups):
        lo = CONV_CH + g * SSM_GROUP_CH
        m = m + _dot(yg, w_out_ref[lo:lo + SSM_GROUP_CH, :])
    return m


def _transpose_to_bf16(wt, lanes):
    if wt.shape[0] < lanes:
        wt = jnp.concatenate([wt, jnp.zeros((lanes - wt.shape[0], wt.shape[1]), wt.dtype)], axis=0)
    return wt.T.astype(BF16)


def _prep_kernel(cs_ref, cp_ref, w_ref, b_ref, wt_ref, wt_dt_ref, op_ref, os_ref, win_ref, wdt_ref):
    c = jnp.concatenate([cs_ref[...], cp_ref[...]], axis=0)
    c_hi = c.astype(BF16)
    c_lo = (c - c_hi.astype(F32)).astype(BF16)
    w_hi = w_ref[...].astype(BF16)
    mod = _dot(c_hi, w_hi) + _dot(c_lo, w_hi) + b_ref[...]
    n_sample = os_ref.shape[0]
    os_ref[...] = mod[0:n_sample, :]
    op_ref[...] = mod[n_sample:, :]
    win_ref[...] = _transpose_to_bf16(wt_ref[...], win_ref.shape[1])
    wdt_ref[...] = _transpose_to_bf16(wt_dt_ref[...], LANES)


def _prep(c_sample, c_prompt, w_ada, b_ada, w_in, steps=PREP_STEPS):
    ns, nb = c_sample.shape[0], c_prompt.shape[0]
    n_mod = w_ada.shape[1]
    mod_cols, in_cols = n_mod // steps, COL_DT // steps
    assert mod_cols % LANES == 0 and in_cols % LANES == 0
    wt = w_in.T
    n_dt = w_in.shape[1] - COL_DT
    return pl.pallas_call(
        _prep_kernel,
        grid=(steps,),
        in_specs=[pl.BlockSpec((ns, D_MODEL), lambda i: (0, 0)),
                  pl.BlockSpec((nb, D_MODEL), lambda i: (0, 0)),
                  pl.BlockSpec((D_MODEL, mod_cols), lambda i: (0, i)),
                  pl.BlockSpec((1, mod_cols), lambda i: (0, i)),
                  pl.BlockSpec((in_cols, D_MODEL), lambda i: (i, 0)),
                  pl.BlockSpec((n_dt, D_MODEL), lambda i: (COL_DT // n_dt, 0))],
        out_specs=[pl.BlockSpec((nb, mod_cols), lambda i: (0, i)),
                   pl.BlockSpec((ns, mod_cols), lambda i: (0, i)),
                   pl.BlockSpec((D_MODEL, in_cols), lambda i: (0, i)),
                   pl.BlockSpec((D_MODEL, LANES), lambda i: (0, 0))],
        out_shape=[jax.ShapeDtypeStruct((nb, n_mod), F32), jax.ShapeDtypeStruct((ns, n_mod), F32),
                   jax.ShapeDtypeStruct((D_MODEL, COL_DT), BF16), jax.ShapeDtypeStruct((D_MODEL, LANES), BF16)],
        compiler_params=pltpu.CompilerParams(dimension_semantics=("arbitrary",), vmem_limit_bytes=VMEM_LIMIT),
        name="prep",
    )(c_sample, c_prompt, w_ada, b_ada, wt, wt)


def _mixer_prompt_kernel(xa_ref, moda_ref, modb_ref, w_in_ref, w_dt_ref, expand_ref, reduce_ref,
                         conv_w_ref, conv_nw_ref, sconv_w_ref, sconv_b_ref,
                         dtb_ref, alog_ref, dexp_ref, snw_ref, w_out_ref, ln_g_ref, ln_b_ref, *rest,
                         tile, tiles_per_seq, sched, n_cast, cast_steps):
    cast_in, outs, cast_out, scratch = (rest[:n_cast], rest[n_cast:n_cast + 4],
                                        rest[n_cast + 4:2 * n_cast + 4], rest[2 * n_cast + 4:])

    @pl.when(pl.program_id(0) < cast_steps)
    def _():
        for src, dst in zip(cast_in, cast_out):
            dst[...] = src[...].astype(dst.dtype)

    _mixer_prompt_body(xa_ref, moda_ref, modb_ref, w_in_ref, w_dt_ref, expand_ref, reduce_ref,
                       conv_w_ref, conv_nw_ref, sconv_w_ref, sconv_b_ref,
                       dtb_ref, alog_ref, dexp_ref, snw_ref, w_out_ref, ln_g_ref, ln_b_ref, *outs, *scratch,
                       tile=tile, tiles_per_seq=tiles_per_seq, sched=sched)


def _mixer_prompt_body(xa_ref, moda_ref, modb_ref, w_in_ref, w_dt_ref, expand_ref, reduce_ref,
                       conv_w_ref, conv_nw_ref, sconv_w_ref, sconv_b_ref,
                       dtb_ref, alog_ref, dexp_ref, snw_ref, w_out_ref, ln_g_ref, ln_b_ref,
                       x1_ref, cst_ref, scst_ref, sst_ref,
                       p, xk, cbuf, xbuf, st_ref, xs_ref, bc_ref, dtx_ref, acsx_ref, endx_ref,
                       acst_ref, cb_ref, bmt_ref, y_ref, yc_ref,
                       *, tile, tiles_per_seq, sched):
    s = pl.program_id(0)
    jb = lax.rem(s + (tiles_per_seq - 1), tiles_per_seq)

    @pl.when(s == 0)
    def _():
        p[...] = jnp.zeros_like(p)
        xk[...] = jnp.zeros_like(xk)

    @pl.when((jb == 0) | (s == 0))
    def _():
        cbuf[...] = jnp.zeros_like(cbuf)
        xbuf[...] = jnp.zeros_like(xbuf)
        st_ref[...] = jnp.zeros_like(st_ref)

    def stages():
        xa = xa_ref[...]
        u = (xa * (1.0 + moda_ref[:, D_MODEL:2 * D_MODEL]) + moda_ref[:, 0:D_MODEL]).astype(BF16)
        free = []

        def first_stage(n):
            for _ in range(min(n, len(free))):
                lo = free.pop(0)
                if lo == COL_DT:
                    p[:, COL_DT:IN_PAD] = _dot(u, w_dt_ref[...])
                else:
                    p[:, lo:lo + MXU_COLS] = _dot(u, w_in_ref[:, lo:lo + MXU_COLS])

        expand = expand_ref[...]
        x = xk[...]
        g1 = modb_ref[:, 2 * D_MODEL:3 * D_MODEL]

        def proj(lo, width):
            return p[:, lo:lo + width]

        def delayed(tail_ref, cs, cur, taps):
            seq = jnp.concatenate([tail_ref[:, cs], cur], axis=0)
            tail_ref[:, cs] = cur[tile - SUBLANES:, :]
            return [pltpu.roll(seq, k, axis=0)[SUBLANES:, :] for k in range(1, taps + 1)]

        for k in range(CONV_CH // MXU_COLS):
            c0 = k * MXU_COLS
            cs = slice(c0, c0 + MXU_COLS)
            ch = proj(COL_GC + c0, MXU_COLS) * proj(COL_HV + c0, MXU_COLS)
            ch1, ch2 = delayed(cbuf, cs, ch, 2)
            cv = conv_w_ref[0:1, cs] * ch2 + conv_w_ref[1:2, cs] * ch1 + conv_w_ref[2:3, cs] * ch
            prod = proj(COL_GB + c0, MXU_COLS) * cv
            free.extend((COL_GC + c0, COL_HV + c0, COL_GB + c0))
            first_stage(sched[0])
            ssum = _dot_f32_lhs(prod * prod, reduce_ref[cs, :], terms=1)
            rstd = lax.rsqrt(ssum * (1.0 / CONV_GROUP) + RMS_EPS)
            yc_ref[:, cs] = (prod * _dot_f32_lhs(rstd, expand_ref[:, cs], terms=2)
                             * conv_nw_ref[:, cs]).astype(BF16)

        def pre_conv(c0):
            cs = slice(c0, c0 + MXU_COLS)
            xbc = proj(COL_XBC + c0, MXU_COLS)
            free.append(COL_XBC + c0)
            x1, x2, x3 = delayed(xbuf, cs, xbc, 3)
            return _silu(sconv_w_ref[0:1, cs] * x3 + sconv_w_ref[1:2, cs] * x2 + sconv_w_ref[2:3, cs] * x1
                         + sconv_w_ref[3:4, cs] * xbc + sconv_b_ref[:, cs])

        row = lax.broadcasted_iota(jnp.int32, (SSM_CHUNK, SSM_CHUNK), 0)
        col = lax.broadcasted_iota(jnp.int32, (SSM_CHUNK, SSM_CHUNK), 1)
        causal = row >= col
        tri = causal.astype(BF16)
        groups = SSM_CHUNK // SUBLANES
        causal_bias = jnp.where(causal, 0.0, -jnp.inf).reshape(groups, SUBLANES, SSM_CHUNK)
        first_half = (col < SSM_HEAD_DIM).reshape(groups, SUBLANES, SSM_CHUNK)
        half_rows = col < SSM_HEAD_DIM
        chunks = [slice(c * SSM_CHUNK, (c + 1) * SSM_CHUNK) for c in range(tile // SSM_CHUNK)]

        dt = _softplus(proj(COL_DT, LANES) + dtb_ref[...])
        free.append(COL_DT)
        first_stage(sched[1])
        dta = dt * (-jnp.exp(alog_ref[...]))
        dtx_ref[...] = _dot_f32_lhs(dt, expand, terms=1)
        for c, rows in enumerate(chunks):
            acs = _dot_f32_rhs(tri, dta[rows, :])
            acs_t = acs.T
            for h in range(SSM_HEADS):
                r8 = (c * SSM_HEADS + h) * SUBLANES
                acst_ref[r8:r8 + SUBLANES, :] = jnp.broadcast_to(acs_t[h:h + 1, :], (SUBLANES, SSM_CHUNK))
            acs_x = _dot_f32_lhs(acs, expand, terms=2)
            acsx_ref[rows, :] = acs_x
            endx_ref[c * SUBLANES:(c + 1) * SUBLANES, :] = jnp.broadcast_to(acs_x[SSM_CHUNK - 1:SSM_CHUNK, :],
                                                                             (SUBLANES, SSM_CH))
        for c0 in range(SSM_CH, XBC_CH, MXU_COLS):
            first_stage(sched[2])
            bc_ref[:, c0 - SSM_CH:c0 - SSM_CH + MXU_COLS] = pre_conv(c0)
        for rows in chunks:
            for g in range(SSM_GROUPS):
                gs = slice(g * SSM_STATE, (g + 1) * SSM_STATE)
                bm = bc_ref[rows, gs]
                cm = bc_ref[rows, (SSM_GROUPS + g) * SSM_STATE:(SSM_GROUPS + g + 1) * SSM_STATE]
                cb_ref[rows, gs] = lax.dot_general(cm.astype(BF16), bm.astype(BF16), (((1,), (1,)), ((), ())),
                                                   preferred_element_type=F32)
                bmt_ref[rows, gs] = bm.T.astype(BF16)

        for c0 in range(0, SSM_CH, MXU_COLS):
            first_stage(sched[3])
            cs = slice(c0, c0 + MXU_COLS)
            g = c0 // SSM_GROUP_CH
            gs = slice(g * SSM_STATE, (g + 1) * SSM_STATE)
            xs = pre_conv(c0)
            xs_ref[:, cs] = xs
            xdt = xs * dtx_ref[:, cs]
            for c, rows in enumerate(chunks):
                first_stage(sched[4])
                acs_x = acsx_ref[rows, cs].reshape(groups, SUBLANES, MXU_COLS)
                end_x = endx_ref[c * SUBLANES:(c + 1) * SUBLANES, cs]
                xdt_c = xdt[rows, :]
                xdec = (xdt_c * jnp.exp(end_x[None] - acs_x).reshape(SSM_CHUNK, MXU_COLS)).astype(BF16)
                cm = bc_ref[rows, (SSM_GROUPS + g) * SSM_STATE:(SSM_GROUPS + g + 1) * SSM_STATE].astype(BF16)
                cb = cb_ref[rows, gs]
                st = st_ref[:, cs]
                y_off = _dot(cm, st.astype(BF16)) * jnp.exp(acs_x).reshape(SSM_CHUNK, MXU_COLS)
                st_ref[:, cs] = ((st.reshape(groups, SUBLANES, MXU_COLS) * jnp.exp(end_x)[None])
                                 .reshape(SSM_STATE, MXU_COLS) + _dot(bmt_ref[rows, gs], xdec))
                for lo in range(0, MXU_COLS, LANES):
                    h0 = (c * SSM_HEADS + (c0 + lo) // SSM_HEAD_DIM) * SUBLANES
                    slab = acs_x[:, :, lo:lo + LANES]
                    rolled = pltpu.roll(slab, SSM_HEAD_DIM, axis=2)
                    a0 = jnp.where(first_half, slab, rolled) - acst_ref[h0:h0 + SUBLANES, :][None]
                    a1 = jnp.where(first_half, rolled, slab) - acst_ref[h0 + SUBLANES:h0 + 2 * SUBLANES, :][None]
                    l0 = jnp.exp(a0 + causal_bias).reshape(SSM_CHUNK, SSM_CHUNK)
                    l1 = jnp.exp(a1 + causal_bias).reshape(SSM_CHUNK, SSM_CHUNK)
                    m = jnp.concatenate([(cb * l0).astype(BF16), (cb * l1).astype(BF16)], axis=1)
                    xp = xdt_c[:, lo:lo + LANES]
                    rhs = jnp.concatenate([jnp.where(half_rows, xp, 0.0), jnp.where(half_rows, 0.0, xp)],
                                          axis=0).astype(BF16)
                    y_ref[rows, c0 + lo:c0 + lo + LANES] = _dot(m, rhs) + y_off[:, lo:lo + LANES]

        for k in range(SSM_CH // MXU_COLS):
            first_stage(sched[5])
            c0 = k * MXU_COLS
            cs = slice(c0, c0 + MXU_COLS)
            y_ref[:, cs] = (y_ref[:, cs] + xs_ref[:, cs] * dexp_ref[:, cs]) * _silu(proj(COL_Z + c0, MXU_COLS))
            free.append(COL_Z + c0)
        first_stage(sched[6])
        m = _mix_out(yc_ref[...], _ssm_group_norm(y_ref[...], snw_ref[...]), w_out_ref)
        x1_ref[...] = _layer_norm(ALPHA * x + (1.0 + g1) * m, ln_g_ref[...], ln_b_ref[...])
        first_stage(N_PIECES)
        assert not free
        xk[...] = xa

    stages()

    @pl.when((jb == tiles_per_seq - 1) & (s > 0))
    def _():
        cst_ref[...] = cbuf[SUBLANES - 2:SUBLANES, :]
        scst_ref[...] = xbuf[SUBLANES - 3:SUBLANES, :]
        sst_ref[...] = st_ref[...].T


def _const_spec(shape):
    return pl.BlockSpec(shape, lambda *_: (0,) * len(shape), pipeline_mode=pl.Buffered(1))


def _mixer_prompt(x, mod, w_in, w_dt, conv_w, conv_nw, sconv_w, sconv_b, dtb, alog, dexp, snw, w_out, ln_g, ln_b,
                  to_cast=(), tile=MIXER_TILE, sched=MIXER_SCHED, cast_steps=CAST_STEPS):
    assert CONV_GROUP == SSM_HEAD_DIM and CONV_CH == SSM_CH
    nb, seq, _ = x.shape
    tiles_per_seq = seq // tile
    n_tiles = nb * tiles_per_seq
    kern = functools.partial(_mixer_prompt_kernel, tile=tile, tiles_per_seq=tiles_per_seq, sched=sched,
                             n_cast=len(to_cast), cast_steps=cast_steps)
    cast_block = lambda s: (jnp.minimum(s, cast_steps - 1), 0)
    cast_specs = [pl.BlockSpec((w.shape[0] // cast_steps, w.shape[1]), cast_block) for w in to_cast]
    consts = [w_in, w_dt, _head_expand(np), _group_reduce(np), conv_w, conv_nw, sconv_w, sconv_b, dtb, alog, dexp, snw,
              w_out, ln_g, ln_b]
    first = lambda s: jnp.minimum(s, n_tiles - 1)
    second = lambda s: jnp.maximum(s - 1, 0)
    return pl.pallas_call(
        kern,
        grid=(n_tiles + 1,),
        in_specs=[pl.BlockSpec((tile, D_MODEL), lambda s: (first(s), 0)),
                  pl.BlockSpec((None, 1, 6 * D_MODEL), lambda s: (first(s) // tiles_per_seq, 0, 0)),
                  pl.BlockSpec((None, 1, 6 * D_MODEL), lambda s: (second(s) // tiles_per_seq, 0, 0))]
                 + [_const_spec(a.shape) for a in consts] + cast_specs,
        out_specs=[pl.BlockSpec((tile, D_MODEL), lambda s: (second(s), 0)),
                   pl.BlockSpec((None, 2, CONV_CH), lambda s: (second(s) // tiles_per_seq, 0, 0)),
                   pl.BlockSpec((None, 3, XBC_CH), lambda s: (second(s) // tiles_per_seq, 0, 0)),
                   pl.BlockSpec((None, SSM_CH, SSM_STATE), lambda s: (second(s) // tiles_per_seq, 0, 0))]
                  + cast_specs,
        out_shape=[jax.ShapeDtypeStruct((nb * seq, D_MODEL), F32),
                   jax.ShapeDtypeStruct((nb, 2, CONV_CH), F32),
                   jax.ShapeDtypeStruct((nb, 3, XBC_CH), F32),
                   jax.ShapeDtypeStruct((nb, SSM_CH, SSM_STATE), F32)]
                  + [jax.ShapeDtypeStruct(w.shape, BF16) for w in to_cast],
        scratch_shapes=[pltpu.VMEM((tile, IN_PAD), F32),
                        pltpu.VMEM((tile, D_MODEL), F32),
                        pltpu.VMEM((SUBLANES, CONV_CH), F32),
                        pltpu.VMEM((SUBLANES, XBC_CH), F32),
                        pltpu.VMEM((SSM_STATE, SSM_CH), F32),
                        pltpu.VMEM((tile, SSM_CH), F32),
                        pltpu.VMEM((tile, 2 * SSM_GROUPS * SSM_STATE), F32),
                        pltpu.VMEM((tile, SSM_CH), F32),
                        pltpu.VMEM((tile, SSM_CH), F32),
                        pltpu.VMEM((tile // SSM_CHUNK * SUBLANES, SSM_CH), F32),
                        pltpu.VMEM((tile // SSM_CHUNK * SSM_HEADS * SUBLANES, SSM_CHUNK), F32),
                        pltpu.VMEM((tile, SSM_GROUPS * SSM_STATE), F32),
                        pltpu.VMEM((tile, SSM_GROUPS * SSM_STATE), BF16),
                        pltpu.VMEM((tile, SSM_CH), F32),
                        pltpu.VMEM((tile, CONV_CH), BF16)],
        compiler_params=pltpu.CompilerParams(dimension_semantics=("arbitrary",),
                                             vmem_limit_bytes=VMEM_LIMIT),
        name="mixer_prompt",
    )(x.reshape(nb * seq, D_MODEL), mod, mod, *consts, *to_cast)


def _ffn_residual(x, mod_ref, w_up_ref, w_down_ref, ff_tile):
    sh2 = mod_ref[:, 3 * D_MODEL:4 * D_MODEL]
    sc2 = mod_ref[:, 4 * D_MODEL:5 * D_MODEL]
    g2 = mod_ref[:, 5 * D_MODEL:6 * D_MODEL]
    v = (x * (1.0 + sc2) + sh2).astype(BF16)
    acc = jnp.zeros(x.shape, F32)
    for k in range(D_FF // ff_tile):
        h = jnp.maximum(_dot(v, w_up_ref[:, k * ff_tile:(k + 1) * ff_tile]), 0.0)
        acc = acc + _dot((h * h).astype(BF16), w_down_ref[k * ff_tile:(k + 1) * ff_tile, :])
    return ALPHA * x + (1.0 + g2) * acc


def _ffn_state_kernel(x_ref, mod_ref, w_up_ref, w_down_ref, ln_g_ref, ln_b_ref,
                      s_ref, xdt_ref, bm_ref, cm_ref, decx_ref,
                      xs_in_ref, mods_ref, yconv_ref, ydiag_ref, xs_ref, z_ref, dexp_ref, snw_ref, w_out_ref,
                      ln1_g_ref, ln1_b_ref,
                      o_ref, so_ref, yoff_ref, ys_ref, r_ref, *, ff_tile, state_block, state_steps):
    s = pl.program_id(0)
    n_tiles = pl.num_programs(0) - 1

    @pl.when(s == 0)
    def _():
        r_ref[...] = jnp.zeros_like(r_ref)

    def norm_previous():
        o_ref[...] = _layer_norm(r_ref[...], ln_g_ref[...], ln_b_ref[...])

    @pl.when(s < n_tiles)
    def _():
        norm_previous()
        r_ref[...] = _ffn_residual(x_ref[...], mod_ref, w_up_ref, w_down_ref, ff_tile)

    @pl.when(s == n_tiles)
    def _():
        norm_previous()
        g1 = mods_ref[:, 2 * D_MODEL:3 * D_MODEL]
        y = ydiag_ref[...] + yoff_ref[...] + xs_ref[...] * dexp_ref[...]
        y = y * _silu(z_ref[...])
        m = _mix_out(yconv_ref[...], _ssm_group_norm(y, snw_ref[...]), w_out_ref)
        x1 = _layer_norm(ALPHA * xs_in_ref[...] + (1.0 + g1) * m, ln1_g_ref[...], ln1_b_ref[...])
        ys_ref[...] = _layer_norm(_ffn_residual(x1, mods_ref, w_up_ref, w_down_ref, ff_tile),
                                  ln_g_ref[...], ln_b_ref[...])

    @pl.when(s < state_steps)
    def _():
        _state_update(s, s_ref, xdt_ref, bm_ref, cm_ref, decx_ref, so_ref, yoff_ref, state_block)


def _ffn_and_state(x, mod, rows_per_mod, w_up, w_down, ln_g, ln_b, state, xdt, bm, cm, decx, sample_rest,
                   tile=FFN_TILE, ff_tile=FF_SLAB, state_block=STATE_BLOCK):
    rows = x.shape[0]
    n_tok = state.shape[0]
    tiles_per_mod = rows_per_mod // tile
    n_tiles = rows // tile
    state_steps = n_tok // state_block
    assert state_steps <= n_tiles
    kern = functools.partial(_ffn_state_kernel, ff_tile=ff_tile, state_block=state_block, state_steps=state_steps)
    first = lambda s: jnp.minimum(s, n_tiles - 1)
    second = lambda s: jnp.maximum(s - 1, 0)
    tokens = lambda s: jnp.minimum(s, state_steps - 1)
    y, new_state, _, y_sample = pl.pallas_call(
        kern,
        grid=(n_tiles + 1,),
        in_specs=[pl.BlockSpec((tile, D_MODEL), lambda s: (first(s), 0)),
                  pl.BlockSpec((None, 1, 6 * D_MODEL), lambda s: (first(s) // tiles_per_mod, 0, 0)),
                  _const_spec(w_up.shape), _const_spec(w_down.shape),
                  _const_spec(ln_g.shape), _const_spec(ln_b.shape),
                  pl.BlockSpec((state_block, SSM_CH, SSM_STATE), lambda s: (tokens(s), 0, 0)),
                  _const_spec(xdt.shape), _const_spec(bm.shape), _const_spec(cm.shape), _const_spec(decx.shape)]
                 + [_const_spec(a.shape) for a in sample_rest],
        out_specs=[pl.BlockSpec((tile, D_MODEL), lambda s: (second(s), 0)),
                   pl.BlockSpec((state_block, SSM_CH, SSM_STATE), lambda s: (tokens(s), 0, 0)),
                   pl.BlockSpec((n_tok, SSM_CH), lambda s: (0, 0)),
                   pl.BlockSpec((n_tok, D_MODEL), lambda s: (0, 0))],
        out_shape=[jax.ShapeDtypeStruct((rows, D_MODEL), F32), jax.ShapeDtypeStruct(state.shape, F32),
                   jax.ShapeDtypeStruct((n_tok, SSM_CH), F32), jax.ShapeDtypeStruct((n_tok, D_MODEL), F32)],
        scratch_shapes=[pltpu.VMEM((tile, D_MODEL), F32)],
        compiler_params=pltpu.CompilerParams(dimension_semantics=("arbitrary",),
                                             vmem_limit_bytes=VMEM_LIMIT),
        name="ffn_and_state",
    )(x, mod, w_up, w_down, ln_g, ln_b, state, xdt, bm, cm, decx, *sample_rest)
    return y, new_state, y_sample


_PRODUCT_TERMS = ((0, 0), (0, 1), (1, 0), (0, 2), (2, 0), (1, 1))
UPDATE_TERMS = 16


def _sample_pre_kernel(x_ref, mod_ref, w_in_ref, w_dt_ref, conv_w_ref, conv_nw_ref, sconv_w_ref, sconv_b_ref,
                       dtb_ref, alog_ref, cb0_ref, cb1_ref, sb0_ref, sb1_ref, sb2_ref,
                       yconv_ref, ch_ref, xbc_ref, z_ref, xs_ref, ydiag_ref, decx_ref, cm_ref, xdt_ref, bm_ref):
    expand = _head_expand()
    reduce = _group_reduce()
    x = x_ref[...]
    sh1 = mod_ref[:, 0:D_MODEL]
    sc1 = mod_ref[:, D_MODEL:2 * D_MODEL]
    u = (x * (1.0 + sc1) + sh1).astype(BF16)

    def proj(lo, width):
        return _dot(u, w_in_ref[:, lo:lo + width])

    ch = proj(COL_GC, CONV_CH) * proj(COL_HV, CONV_CH)
    ch_ref[...] = ch
    cw = conv_w_ref[...]
    cv = cw[0:1, :] * cb0_ref[...] + cw[1:2, :] * cb1_ref[...] + cw[2:3, :] * ch
    yconv_ref[...] = _conv_group_norm(proj(COL_GB, CONV_CH) * cv, conv_nw_ref[...], expand, reduce)

    xbc = proj(COL_XBC, XBC_CH)
    xbc_ref[...] = xbc
    sw = sconv_w_ref[...]
    xc = _silu(sw[0:1, :] * sb0_ref[...] + sw[1:2, :] * sb1_ref[...] + sw[2:3, :] * sb2_ref[...]
               + sw[3:4, :] * xbc + sconv_b_ref[...])
    xs = xc[:, 0:SSM_CH]
    xs_ref[...] = xs
    cm_ref[...] = xc[:, SSM_CH + SSM_GROUPS * SSM_STATE:XBC_CH]
    z_ref[...] = proj(COL_Z, SSM_CH)

    dt = _softplus(_dot(u, w_dt_ref[...]) + dtb_ref[...])
    dta = dt * (-jnp.exp(alog_ref[...]))
    xdt = xs * _dot_f32_lhs(dt, expand)
    decx = jnp.exp(_dot_f32_lhs(dta, expand))
    decx_ref[...] = decx
    xdt_ref[...] = xdt
    bm_ref[...] = xc[:, SSM_CH:SSM_CH + SSM_GROUPS * SSM_STATE]
    for g in range(SSM_GROUPS):
        bm = xc[:, SSM_CH + g * SSM_STATE:SSM_CH + (g + 1) * SSM_STATE]
        cm = xc[:, SSM_CH + (SSM_GROUPS + g) * SSM_STATE:SSM_CH + (SSM_GROUPS + g + 1) * SSM_STATE]
        cb = jnp.sum(cm * bm, axis=-1, keepdims=True)
        gl = g * SSM_GROUP_CH
        ydiag_ref[:, gl:gl + SSM_GROUP_CH] = cb * xdt[:, gl:gl + SSM_GROUP_CH]


def _sample_pre(x, mod, w_in, w_dt, conv_w, conv_nw, sconv_w, sconv_b, dtb, alog, cb0, cb1, sb0, sb1, sb2):
    n = x.shape[0]
    args = (x, mod, w_in, w_dt, conv_w, conv_nw, sconv_w, sconv_b, dtb, alog, cb0, cb1, sb0, sb1, sb2)
    f32_shapes = [(n, CONV_CH), (n, CONV_CH), (n, XBC_CH), (n, SSM_CH), (n, SSM_CH), (n, SSM_CH), (n, SSM_CH),
                  (n, SSM_GROUPS * SSM_STATE)]
    f32_shapes += [(n, SSM_CH), (n, SSM_GROUPS * SSM_STATE)]
    return pl.pallas_call(
        _sample_pre_kernel,
        out_shape=[jax.ShapeDtypeStruct(s, F32) for s in f32_shapes],
        compiler_params=pltpu.CompilerParams(vmem_limit_bytes=VMEM_LIMIT),
        name="sample_pre",
    )(*args)


def _state_update(i, s_ref, xdt_ref, bm_ref, cm_ref, decx_ref, o_ref, yoff_ref, block):
    tok = pl.ds(pl.multiple_of(i * block, block), block)
    xdt_t = [t.astype(F32) for t in _split(xdt_ref[tok, :], 3)]
    dec_t = [t.astype(F32) for t in _split(decx_ref[tok, :], 3)]
    bm_t = [t.astype(F32) for t in _split(bm_ref[tok, :], 3)]
    group_of = lax.broadcasted_iota(jnp.int32, (block, SSM_CH), 1) // SSM_GROUP_CH
    zeros = jnp.zeros((block, SSM_STATE), F32)
    lhs, rhs = [], []
    for g in range(SSM_GROUPS):
        for tx, tb in _PRODUCT_TERMS:
            lhs.append(jnp.where(group_of == g, xdt_t[tx], 0.0))
            rhs.append(jnp.concatenate([bm_t[tb][:, g * SSM_STATE:(g + 1) * SSM_STATE], zeros], axis=1))
    for t in range(3):
        lhs.append(dec_t[t])
        rhs.append(jnp.concatenate([zeros, zeros + 1.0], axis=1))
    while len(lhs) < UPDATE_TERMS:
        lhs.append(jnp.zeros((block, SSM_CH), F32))
        rhs.append(jnp.concatenate([zeros, zeros], axis=1))
    lhs_t = jnp.concatenate(lhs, axis=0).T.astype(BF16)
    rhs_all = jnp.concatenate(rhs, axis=0)
    token_of = lax.broadcasted_iota(jnp.int32, rhs_all.shape, 0) % block

    def body(k, carry):
        b = i * block + k
        s = s_ref[k]
        upd = _dot(lhs_t, jnp.where(token_of == k, rhs_all, 0.0).astype(BF16))
        o_ref[k] = s * upd[:, SSM_STATE:2 * SSM_STATE] + upd[:, 0:SSM_STATE]
        cm = cm_ref[pl.ds(b, 1), :]
        sums = []
        for c0 in range(0, SSM_CH, LANES):
            g = c0 // SSM_GROUP_CH
            prod = s[c0:c0 + LANES, :] * cm[:, g * SSM_STATE:(g + 1) * SSM_STATE]
            sums.append(jnp.sum(prod.T, axis=0, keepdims=True))
        yoff_ref[pl.ds(b, 1), :] = jnp.concatenate(sums, axis=1) * decx_ref[pl.ds(b, 1), :]
        return carry

    lax.fori_loop(0, block, body, 0, unroll=True)


def kernel(x_prompt, x_sample, state_conv, state_ssm_conv, state_ssm, c_prompt, c_sample, w_ada, b_ada, w_in, conv_w, conv_norm_w, ssm_conv_w, ssm_conv_b, dt_bias, a_log, d_skip, ssm_norm_w, w_out, ln1_g, ln1_b, w_up, w_down, ln2_g, ln2_b):
    assert w_ada.shape[0] == 1, "single-layer trunk"
    nb, seq, _ = x_prompt.shape
    ns = x_sample.shape[0]
    row = lambda a: a.reshape(1, -1)
    pad_heads = lambda a: jnp.pad(a.reshape(1, -1), ((0, 0), (0, LANES - SSM_HEADS)))

    w_out_b = w_out[0].astype(BF16)
    conv_nw, sconv_b, snw = row(conv_norm_w[0]), row(ssm_conv_b[0]), row(ssm_norm_w[0])
    dtb, alog = pad_heads(dt_bias[0]), pad_heads(a_log[0])
    dexp = row(jnp.repeat(d_skip[0], SSM_HEAD_DIM))
    g1, b1, g2, b2 = row(ln1_g[0]), row(ln1_b[0]), row(ln2_g[0]), row(ln2_b[0])

    mod_p, mod_s, w_in_b, w_dt_b = _prep(c_sample, c_prompt, w_ada[0], row(b_ada[0]), w_in[0])
    mod_p = mod_p.reshape(nb, 1, 6 * D_MODEL)

    x1_p, cst_p, scst_p, sst_p, w_up_b, w_down_b = _mixer_prompt(
        x_prompt, mod_p, w_in_b, w_dt_b, conv_w[0], conv_nw, ssm_conv_w[0], sconv_b, dtb, alog, dexp, snw, w_out_b,
        g1, b1, to_cast=(w_up[0], w_down[0]))

    xs2 = x_sample.reshape(ns, D_MODEL)
    (yconv_s, ch_s, xbc_s, z_s, xs_s, ydiag_s, decx_s, cm_s, xdt_s, bm_s) = _sample_pre(
        xs2, mod_s, w_in_b, w_dt_b, conv_w[0], conv_nw, ssm_conv_w[0], sconv_b, dtb, alog,
        state_conv[0, :, 0], state_conv[0, :, 1],
        state_ssm_conv[0, :, 0], state_ssm_conv[0, :, 1], state_ssm_conv[0, :, 2])

    y_p, new_state_s, y_s = _ffn_and_state(
        x1_p, mod_p, seq, w_up_b, w_down_b, g2, b2,
        state_ssm[0].reshape(ns, SSM_CH, SSM_STATE), xdt_s, bm_s, cm_s, decx_s,
        sample_rest=(xs2, mod_s, yconv_s, ydiag_s, xs_s, z_s, dexp, snw, w_out_b, g1, b1))

    return (y_p.reshape(nb, seq, D_MODEL),
            y_s.reshape(ns, 1, D_MODEL),
            cst_p[None],
            scst_p[None],
            sst_p.reshape(1, nb, SSM_HEADS, SSM_HEAD_DIM, SSM_STATE),
            jnp.stack([state_conv[0, :, 1], ch_s], axis=1)[None],
            jnp.stack([state_ssm_conv[0, :, 1], state_ssm_conv[0, :, 2], xbc_s], axis=1)[None],
            new_state_s.reshape(1, ns, SSM_HEADS, SSM_HEAD_DIM, SSM_STATE))
```
